```python
import jax, jax.numpy as jnp
from jax import lax
import numpy as np

D_MODEL = 1024
BATCH = 2
SEQ = 8192
DEPTH = 1

N_ATTN_HEADS = 8
ATTN_HEAD_DIM = 64
ATTN_WIDTH = N_ATTN_HEADS * ATTN_HEAD_DIM
N_SGU_GROUPS = 8
SGU_WIDTH = D_MODEL // 2
SGU_GROUP_DIM = SGU_WIDTH // N_SGU_GROUPS
CHUNK = 128
Q_BLOCK = 128
D_FF = 4 * D_MODEL
N_BRANCH = 2
EPS = 1e-6
IN_SPLITS = (2 * SGU_WIDTH, ATTN_WIDTH, ATTN_WIDTH, ATTN_WIDTH, N_ATTN_HEADS, N_BRANCH * D_MODEL)
IN_WIDTH = sum(IN_SPLITS)
IN_OFFSETS = tuple(int(o) for o in np.cumsum(IN_SPLITS)[:-1])

kernel_name = "hybrid_gmlp_fox_gated_block"


def rmsnorm(x, g):
    x32 = x.astype(jnp.float32)
    y = x32 * lax.rsqrt(jnp.mean(x32 * x32, axis=-1, keepdims=True) + EPS)
    return y.astype(x.dtype) * g


def layernorm(x, g, b):
    x32 = x.astype(jnp.float32)
    mu = jnp.mean(x32, axis=-1, keepdims=True)
    xc = x32 - mu
    y = xc * lax.rsqrt(jnp.mean(xc * xc, axis=-1, keepdims=True) + EPS)
    return y.astype(x.dtype) * g + b


def chunked_sgu(z, g_sgu, b_sgu, w_spatial, b_spatial):
    B, S, _ = z.shape
    u, v = z[..., :SGU_WIDTH], z[..., SGU_WIDTH:]
    v = layernorm(v, g_sgu, b_sgu)
    vc = v.reshape(B, S // CHUNK, CHUNK, N_SGU_GROUPS, SGU_GROUP_DIM)
    causal = jnp.tril(jnp.ones((CHUNK, CHUNK), dtype=bool))
    ws = jnp.where(causal[None], w_spatial, jnp.zeros_like(w_spatial))
    s = jnp.einsum('gts,bcsgd->bctgd', ws, vc)
    s = s + b_spatial.T[None, None, :, :, None]
    return u * s.reshape(B, S, SGU_WIDTH)


def forgetting_attention(q, k, v, cum):
    B, H, S, d = q.shape
    nb = S // Q_BLOCK
    qb = q.reshape(B, H, nb, Q_BLOCK, d).transpose(2, 0, 1, 3, 4)
    cb = cum.reshape(B, H, nb, Q_BLOCK).transpose(2, 0, 1, 3)
    key_pos = jnp.arange(S)
    scale = d ** -0.5

    def one_block(args):
        q_blk, c_blk, i = args
        s = jnp.einsum('bhqd,bhkd->bhqk', q_blk, k).astype(jnp.float32) * scale
        s = s + c_blk[..., :, None] - cum[..., None, :]
        q_pos = i * Q_BLOCK + jnp.arange(Q_BLOCK)
        mask = key_pos[None, :] <= q_pos[:, None]
        s = jnp.where(mask, s, -jnp.inf)
        p = jax.nn.softmax(s, axis=-1)
        return jnp.einsum('bhqk,bhkd->bhqd', p.astype(v.dtype), v)

    out = lax.map(one_block, (qb, cb, jnp.arange(nb)))
    return out.transpose(1, 2, 0, 3, 4).reshape(B, H, S, d)


def setup_inputs(seed: int = 0) -> dict:
    key = jax.random.key(seed)
    ks = jax.random.split(key, 20)
    L = DEPTH

    def nrm(k, shape, scale):
        return jax.random.normal(k, shape, jnp.float32) * scale

    def gain(k, shape):
        return 1.0 + 0.05 * jax.random.normal(k, shape, jnp.float32)

    return {
        "x": jax.random.normal(ks[0], (BATCH, SEQ, D_MODEL), jnp.float32),
        "g_mix_pre": gain(ks[1], (L, D_MODEL)),
        "w_in": nrm(ks[2], (L, D_MODEL, IN_WIDTH), D_MODEL ** -0.5),
        "b_forget": 2.0 + 0.5 * jax.random.normal(ks[3], (L, N_ATTN_HEADS), jnp.float32),
        "g_sgu": gain(ks[4], (L, SGU_WIDTH)),
        "b_sgu": nrm(ks[5], (L, SGU_WIDTH), 0.02),
        "w_spatial": nrm(ks[6], (L, N_SGU_GROUPS, CHUNK, CHUNK), CHUNK ** -0.5),
        "b_spatial": 1.0 + 0.1 * jax.random.normal(ks[7], (L, N_SGU_GROUPS, CHUNK), jnp.float32),
        "w_branch_sgu": nrm(ks[8], (L, SGU_WIDTH, D_MODEL), SGU_WIDTH ** -0.5),
        "w_branch_attn": nrm(ks[9], (L, ATTN_WIDTH, D_MODEL), ATTN_WIDTH ** -0.5),
        "w_out": nrm(ks[10], (L, D_MODEL, D_MODEL), D_MODEL ** -0.5),
        "g_mix_post": gain(ks[11], (L, D_MODEL)),
        "g_ffn_pre": gain(ks[12], (L, D_MODEL)),
        "w_up": nrm(ks[13], (L, D_MODEL, D_FF), D_MODEL ** -0.5),
        "w_down": nrm(ks[14], (L, D_FF, D_MODEL), D_FF ** -0.5),
        "g_ffn_post": gain(ks[15], (L, D_MODEL)),
    }


def reference(x, g_mix_pre, w_in, b_forget, g_sgu, b_sgu, w_spatial, b_spatial,
              w_branch_sgu, w_branch_attn, w_out, g_mix_post, g_ffn_pre, w_up, w_down,
              g_ffn_post):
    B, S, _ = x.shape
    h = x
    for l in range(DEPTH):
        xn = rmsnorm(h, g_mix_pre[l])
        proj = xn @ w_in[l]
        z_sgu, q, k, v, f_logit, gate_logit = jnp.split(proj, IN_OFFSETS, axis=-1)

        y_sgu = chunked_sgu(jax.nn.gelu(z_sgu), g_sgu[l], b_sgu[l], w_spatial[l], b_spatial[l])

        def heads(t):
            return t.reshape(B, S, N_ATTN_HEADS, ATTN_HEAD_DIM).transpose(0, 2, 1, 3)
        log_f = jax.nn.log_sigmoid((f_logit + b_forget[l]).astype(jnp.float32))
        cum = jnp.cumsum(log_f, axis=1).transpose(0, 2, 1)
        y_attn = forgetting_attention(heads(q), heads(k), heads(v), cum)
        y_attn = y_attn.transpose(0, 2, 1, 3).reshape(B, S, ATTN_WIDTH)

        gates = jax.nn.sigmoid(gate_logit)
        merged = (gates[..., :D_MODEL] * (y_sgu @ w_branch_sgu[l])
                  + gates[..., D_MODEL:] * (y_attn @ w_branch_attn[l]))
        h = h + rmsnorm(merged @ w_out[l], g_mix_post[l])

        xn2 = rmsnorm(h, g_ffn_pre[l])
        hid = jnp.square(jax.nn.relu(xn2 @ w_up[l]))
        h = h + rmsnorm(hid @ w_down[l], g_ffn_post[l])
    return h
```

```python
import functools

import jax
import jax.numpy as jnp
import numpy as np
from jax import lax
from jax.experimental import pallas as pl
from jax.experimental.pallas import tpu as pltpu

D_MODEL = 1024
N_HEADS = 8
HEAD_DIM = 64
ATTN_WIDTH = N_HEADS * HEAD_DIM
N_GROUPS = 8
SGU_WIDTH = D_MODEL // 2
CHUNK = 128
D_FF = 4 * D_MODEL
EPS = 1e-6

LANES = 128
N_SPLIT = 3
BIAS_COLS = N_HEADS * N_SPLIT
PAIR = 2 * HEAD_DIM
N_PAIRS = N_HEADS // 2

ROW_TILE = 512
FF_TILE = 1024
TQ = 256
TK = 256
VMEM_LIMIT = 56 * 1024 * 1024

_BF16 = jnp.bfloat16
_F32 = jnp.float32


def _split_bf16(x):
    parts = []
    r = x
    for _ in range(N_SPLIT):
        p = r.astype(_BF16)
        parts.append(p)
        r = r - p.astype(_F32)
    return parts


def _dot(a, b):
    return jnp.dot(a, b, preferred_element_type=_F32)


def _dot_nt(a, b):
    return lax.dot_general(a, b, (((1,), (1,)), ((), ())), preferred_element_type=_F32)


def _inproj_kernel(x_ref, gpre_ref, wz_ref, wq_ref, wk_ref, wvt_ref, wf_ref, wg_ref,
                   bf_ref, gsgu_ref, bsgu_ref, expand_ref,
                   u_ref, vn_ref, q_ref, k_ref, vt_ref, gates_ref, qa_ref, ka_ref,
                   carry_ref):
    @pl.when(pl.program_id(1) == 0)
    def _():
        carry_ref[...] = jnp.zeros_like(carry_ref)

    x = x_ref[0]
    tm = x.shape[0]
    ms = jnp.mean(x * x, axis=-1, keepdims=True)
    xb = ((x * lax.rsqrt(ms + EPS)) * gpre_ref[...]).astype(_BF16)

    z = jax.nn.gelu(_dot(xb, wz_ref[...]), approximate=True)
    u_ref[0] = z[:, :SGU_WIDTH].astype(_BF16)
    v = z[:, SGU_WIDTH:]
    mu = jnp.mean(v, axis=-1, keepdims=True)
    vc = v - mu
    var = jnp.mean(vc * vc, axis=-1, keepdims=True)
    vn_ref[0] = ((vc * lax.rsqrt(var + EPS)) * gsgu_ref[...] + bsgu_ref[...]).astype(_BF16)

    q_ref[0] = (_dot(xb, wq_ref[...]) * (HEAD_DIM ** -0.5)).astype(_BF16)
    k_ref[0] = _dot(xb, wk_ref[...]).astype(_BF16)
    vt_ref[0] = _dot_nt(wvt_ref[...], xb).astype(_BF16)

    gates_ref[0] = jax.nn.sigmoid(_dot(xb, wg_ref[...])).astype(_BF16)

    f = _dot(xb, wf_ref[...]) + bf_ref[...]
    log_f = jnp.minimum(f, 0.0) - jnp.log(1.0 + jnp.exp(-jnp.abs(f)))
    row = lax.broadcasted_iota(jnp.int32, (tm, tm), 0)
    col = lax.broadcasted_iota(jnp.int32, (tm, tm), 1)
    tri = jnp.where(row >= col, 1.0, 0.0).astype(_BF16)
    cum = carry_ref[0:1, :]
    for part in _split_bf16(log_f):
        cum = cum + _dot(tri, part)
    carry_ref[0:1, :] = cum[tm - 1:tm, :]

    spread = jnp.zeros((tm, LANES), _F32)
    for i, part in enumerate(_split_bf16(cum)):
        spread = spread + _dot(part, expand_ref[i])
    lane = lax.broadcasted_iota(jnp.int32, (tm, LANES), 1)
    in_a = lane < BIAS_COLS
    in_b = jnp.logical_and(lane >= BIAS_COLS, lane < 2 * BIAS_COLS)
    ka_ref[0] = jnp.where(in_a, -spread, jnp.where(in_b, 1.0, 0.0)).astype(_BF16)
    qa_ref[0] = jnp.where(in_a, 1.0, jnp.where(in_b, spread, 0.0)).astype(_BF16)


def _expand_matrices():
    e = np.zeros((N_SPLIT, LANES, LANES), np.float32)
    for h in range(N_HEADS):
        for i in range(N_SPLIT):
            e[i, h, N_SPLIT * h + i] = 1.0
            e[i, h, BIAS_COLS + N_SPLIT * h + i] = 1.0
    return jnp.asarray(e, _BF16)


def _in_projection(x, g_pre, w_in, b_forget, g_sgu, b_sgu):
    B, S, D = x.shape
    tm = ROW_TILE
    o = np.cumsum((0, 2 * SGU_WIDTH, ATTN_WIDTH, ATTN_WIDTH, ATTN_WIDTH, N_HEADS, 2 * D_MODEL))
    wz = w_in[:, o[0]:o[1]].astype(_BF16)
    wq = w_in[:, o[1]:o[2]].astype(_BF16)
    wk = w_in[:, o[2]:o[3]].astype(_BF16)
    wvt = w_in[:, o[3]:o[4]].T.astype(_BF16)
    wf = jnp.pad(w_in[:, o[4]:o[5]], ((0, 0), (0, LANES - N_HEADS))).astype(_BF16)
    wg = w_in[:, o[5]:o[6]].astype(_BF16)
    bf = jnp.pad(b_forget.reshape(1, N_HEADS), ((0, 0), (0, LANES - N_HEADS)))

    def const(shape):
        return pl.BlockSpec(shape, lambda b, i: (0,) * len(shape))

    def rows(width):
        return pl.BlockSpec((1, tm, width), lambda b, i: (b, i, 0))

    out_shape = (
        jax.ShapeDtypeStruct((B, S, SGU_WIDTH), _BF16),
        jax.ShapeDtypeStruct((B, S, SGU_WIDTH), _BF16),
        jax.ShapeDtypeStruct((B, S, ATTN_WIDTH), _BF16),
        jax.ShapeDtypeStruct((B, S, ATTN_WIDTH), _BF16),
        jax.ShapeDtypeStruct((B, ATTN_WIDTH, S), _BF16),
        jax.ShapeDtypeStruct((B, S, 2 * D_MODEL), _BF16),
        jax.ShapeDtypeStruct((B, S, LANES), _BF16),
        jax.ShapeDtypeStruct((B, S, LANES), _BF16),
    )
    out_specs = (
        rows(SGU_WIDTH), rows(SGU_WIDTH), rows(ATTN_WIDTH), rows(ATTN_WIDTH),
        pl.BlockSpec((1, ATTN_WIDTH, tm), lambda b, i: (b, 0, i)),
        rows(2 * D_MODEL), rows(LANES), rows(LANES),
    )
    return pl.pallas_call(
        _inproj_kernel,
        grid=(B, S // tm),
        in_specs=[
            rows(D), const((1, D)),
            const(wz.shape), const(wq.shape), const(wk.shape), const(wvt.shape),
            const(wf.shape), const(wg.shape),
            const((1, LANES)), const((1, SGU_WIDTH)), const((1, SGU_WIDTH)),
            const((N_SPLIT, LANES, LANES)),
        ],
        out_specs=out_specs,
        out_shape=out_shape,
        scratch_shapes=[pltpu.VMEM((8, LANES), _F32)],
        compiler_params=pltpu.CompilerParams(
            dimension_semantics=("arbitrary", "arbitrary"),
            vmem_limit_bytes=VMEM_LIMIT),
        name="in_projection",
    )(x, g_pre.reshape(1, D), wz, wq, wk, wvt, wf, wg, bf,
      g_sgu.reshape(1, SGU_WIDTH), b_sgu.reshape(1, SGU_WIDTH), _expand_matrices())


def _attn_kernel(q_ref, k_ref, vt_ref, qa_ref, ka_ref, o_ref):
    pair = pl.program_id(1)
    S = q_ref.shape[1]
    lane = lax.broadcasted_iota(jnp.int32, (1, LANES), 1)
    key_i = lax.broadcasted_iota(jnp.int32, (TK, TQ), 0)
    qry_i = lax.broadcasted_iota(jnp.int32, (TK, TQ), 1)
    out_row = lax.broadcasted_iota(jnp.int32, (PAIR, TQ), 0)

    def block(qf, k0, carry, masked):
        m, l, acc = carry
        kf = jnp.concatenate([k_ref[0, pl.ds(k0, TK), :], ka_ref[0, pl.ds(k0, TK), :]], axis=1)
        st = _dot_nt(kf, qf)
        if masked:
            st = jnp.where(key_i <= qry_i, st, -jnp.inf)
        m_new = jnp.maximum(m, jnp.max(st, axis=0, keepdims=True))
        p = jnp.exp(st - m_new)
        alpha = jnp.exp(m - m_new)
        l = alpha * l + jnp.sum(p, axis=0, keepdims=True)
        pv = _dot(vt_ref[0, :, pl.ds(k0, TK)], p.astype(_BF16))
        return m_new, l, alpha * acc + pv

    def q_tile(qi, _):
        q0 = pl.multiple_of(qi * TQ, TQ)
        qp = q_ref[0, pl.ds(q0, TQ), :]
        qa = qa_ref[0, pl.ds(q0, TQ), :]
        outs = []
        for e in range(2):
            head = 2 * pair + e
            a0 = N_SPLIT * head
            q_mask = jnp.logical_and(lane >= HEAD_DIM * e, lane < HEAD_DIM * (e + 1))
            a_mask = jnp.logical_or(
                jnp.logical_and(lane >= a0, lane < a0 + N_SPLIT),
                jnp.logical_and(lane >= BIAS_COLS + a0, lane < BIAS_COLS + a0 + N_SPLIT))
            zero = jnp.zeros((), _BF16)
            qf = jnp.concatenate([jnp.where(q_mask, qp, zero), jnp.where(a_mask, qa, zero)], axis=1)
            init = (jnp.full((1, TQ), -jnp.inf, _F32), jnp.zeros((1, TQ), _F32),
                    jnp.zeros((PAIR, TQ), _F32))
            carry = lax.fori_loop(
                0, qi,
                lambda kb, c: block(qf, pl.multiple_of(kb * TK, TK), c, False), init)
            m, l, acc = block(qf, q0, carry, True)
            outs.append(acc / l)
        ot = jnp.where(out_row < HEAD_DIM, outs[0], outs[1])
        o_ref[0, pl.ds(q0, TQ), :] = ot.T.astype(_BF16)
        return 0

    lax.fori_loop(0, S // TQ, q_tile, 0)


def _attention(q, k, vt, qa, ka):
    B, S, _ = q.shape
    assert TQ == TK and S % TQ == 0
    pair_rows = pl.BlockSpec((1, S, PAIR), lambda b, j: (b, 0, j))
    bias_rows = pl.BlockSpec((1, S, LANES), lambda b, j: (b, 0, 0))
    return pl.pallas_call(
        _attn_kernel,
        grid=(B, N_PAIRS),
        in_specs=[pair_rows, pair_rows,
                  pl.BlockSpec((1, PAIR, S), lambda b, j: (b, j, 0)),
                  bias_rows, bias_rows],
        out_specs=pair_rows,
        out_shape=jax.ShapeDtypeStruct((B, S, ATTN_WIDTH), _BF16),
        compiler_params=pltpu.CompilerParams(
            dimension_semantics=("parallel", "parallel"),
            vmem_limit_bytes=VMEM_LIMIT),
        name="forgetting_attention",
    )(q, k, vt, qa, ka)


def _merge_kernel(x_ref, u_ref, vn_ref, ya_ref, gates_ref, wsp_ref, bsp_ref,
                  wbs_ref, wba_ref, wout_ref, gpost_ref, o_ref, ysgu_ref):
    tm = x_ref.shape[0]
    row = lax.broadcasted_iota(jnp.int32, (CHUNK, CHUNK), 0)
    col = lax.broadcasted_iota(jnp.int32, (CHUNK, CHUNK), 1)
    lane = lax.broadcasted_iota(jnp.int32, (CHUNK, PAIR), 1)
    ws = [jnp.where(row >= col, wsp_ref[g], 0.0).astype(_BF16) for g in range(N_GROUPS)]
    for c in range(tm // CHUNK):
        r = slice(c * CHUNK, (c + 1) * CHUNK)
        for j in range(N_GROUPS // 2):
            cs = slice(j * PAIR, (j + 1) * PAIR)
            vp = vn_ref[r, cs]
            s = jnp.where(lane < HEAD_DIM, _dot(ws[2 * j], vp), _dot(ws[2 * j + 1], vp))
            s = s + bsp_ref[:, cs]
            ysgu_ref[r, cs] = (u_ref[r, cs].astype(_F32) * s).astype(_BF16)

    a = _dot(ysgu_ref[...], wbs_ref[...])
    b = _dot(ya_ref[...], wba_ref[...])
    merged = (gates_ref[:, :D_MODEL].astype(_F32) * a
              + gates_ref[:, D_MODEL:].astype(_F32) * b).astype(_BF16)
    o = _dot(merged, wout_ref[...])
    ms = jnp.mean(o * o, axis=-1, keepdims=True)
    o_ref[...] = x_ref[...] + (o * lax.rsqrt(ms + EPS)) * gpost_ref[...]


def _merge(x2, u2, vn2, ya2, gates2, w_spatial, b_spatial, w_bs, w_ba, w_out, g_post):
    R, D = x2.shape
    tm = ROW_TILE
    bsp = jnp.repeat(b_spatial.T, SGU_WIDTH // N_GROUPS, axis=1)

    def const(shape):
        return pl.BlockSpec(shape, lambda i: (0,) * len(shape))

    def rows(width):
        return pl.BlockSpec((tm, width), lambda i: (i, 0))

    return pl.pallas_call(
        _merge_kernel,
        grid=(R // tm,),
        in_specs=[rows(D), rows(SGU_WIDTH), rows(SGU_WIDTH), rows(ATTN_WIDTH), rows(2 * D),
                  const(w_spatial.shape), const(bsp.shape),
                  const(w_bs.shape), const(w_ba.shape), const(w_out.shape), const((1, D))],
        out_specs=rows(D),
        out_shape=jax.ShapeDtypeStruct((R, D), _F32),
        scratch_shapes=[pltpu.VMEM((tm, SGU_WIDTH), _BF16)],
        compiler_params=pltpu.CompilerParams(
            dimension_semantics=("parallel",), vmem_limit_bytes=VMEM_LIMIT),
        name="gated_merge",
    )(x2, u2, vn2, ya2, gates2, w_spatial, bsp,
      w_bs.astype(_BF16), w_ba.astype(_BF16), w_out.astype(_BF16), g_post.reshape(1, D))


def _mlp_kernel(h_ref, gpre_ref, wup_ref, wdown_ref, gpost_ref, o_ref):
    h = h_ref[...]
    ms = jnp.mean(h * h, axis=-1, keepdims=True)
    xb = ((h * lax.rsqrt(ms + EPS)) * gpre_ref[...]).astype(_BF16)
    acc = jnp.zeros(h.shape, _F32)
    for c in range(D_FF // FF_TILE):
        cs = slice(c * FF_TILE, (c + 1) * FF_TILE)
        hid = jnp.square(jnp.maximum(_dot(xb, wup_ref[:, cs]), 0.0)).astype(_BF16)
        acc = acc + _dot(hid, wdown_ref[cs, :])
    ms2 = jnp.mean(acc * acc, axis=-1, keepdims=True)
    o_ref[...] = h + (acc * lax.rsqrt(ms2 + EPS)) * gpost_ref[...]


def _mlp(h2, g_pre, w_up, w_down, g_post):
    R, D = h2.shape
    tm = ROW_TILE

    def const(shape):
        return pl.BlockSpec(shape, lambda i: (0,) * len(shape))

    rows = pl.BlockSpec((tm, D), lambda i: (i, 0))
    return pl.pallas_call(
        _mlp_kernel,
        grid=(R // tm,),
        in_specs=[rows, const((1, D)), const(w_up.shape), const(w_down.shape), const((1, D))],
        out_specs=rows,
        out_shape=jax.ShapeDtypeStruct((R, D), _F32),
        compiler_params=pltpu.CompilerParams(
            dimension_semantics=("parallel",), vmem_limit_bytes=VMEM_LIMIT),
        name="relu2_mlp",
    )(h2, g_pre.reshape(1, D), w_up.astype(_BF16), w_down.astype(_BF16), g_post.reshape(1, D))


def kernel(x, g_mix_pre, w_in, b_forget, g_sgu, b_sgu, w_spatial, b_spatial, w_branch_sgu,
           w_branch_attn, w_out, g_mix_post, g_ffn_pre, w_up, w_down, g_ffn_post):
    B, S, D = x.shape
    h = x
    for l in range(g_mix_pre.shape[0]):
        u, vn, q, k, vt, gates, qa, ka = _in_projection(
            h, g_mix_pre[l], w_in[l], b_forget[l], g_sgu[l], b_sgu[l])
        y_attn = _attention(q, k, vt, qa, ka)
        h1 = _merge(h.reshape(B * S, D), u.reshape(B * S, -1), vn.reshape(B * S, -1),
                    y_attn.reshape(B * S, -1), gates.reshape(B * S, -1),
                    w_spatial[l], b_spatial[l], w_branch_sgu[l], w_branch_attn[l], w_out[l],
                    g_mix_post[l])
        h = _mlp(h1, g_ffn_pre[l], w_up[l], w_down[l], g_ffn_post[l]).reshape(B, S, D)
    return h
```

```python
import functools

import jax
import jax.numpy as jnp
import numpy as np
from jax import lax
from jax.experimental import pallas as pl
from jax.experimental.pallas import tpu as pltpu

D_MODEL = 1024
N_HEADS = 8
HEAD_DIM = 64
ATTN_WIDTH = N_HEADS * HEAD_DIM
N_GROUPS = 8
SGU_WIDTH = D_MODEL // 2
CHUNK = 128
D_FF = 4 * D_MODEL
EPS = 1e-6
LOG2E = 1.4426950408889634

LANES = 128
N_SPLIT = 3
BIAS_COLS = N_HEADS * N_SPLIT
PAIR = 2 * HEAD_DIM
N_PAIRS = N_HEADS // 2

ROW_TILE = 512
FF_TILE = 1024
TQ = 256
TK = 256
NQ = 4
QG = NQ * TQ
VMEM_LIMIT = 56 * 1024 * 1024

_BF16 = jnp.bfloat16
_F32 = jnp.float32


def _split_bf16(x):
    parts = []
    r = x
    for _ in range(N_SPLIT):
        p = r.astype(_BF16)
        parts.append(p)
        r = r - p.astype(_F32)
    return parts


def _dot(a, b):
    return jnp.dot(a, b, preferred_element_type=_F32)


def _dot_nt(a, b):
    return lax.dot_general(a, b, (((1,), (1,)), ((), ())), preferred_element_type=_F32)


def _inproj_kernel(x_ref, gpre_ref, wz_ref, wqt_ref, wk_ref, wvt_ref, wf_ref, wg_ref,
                   bf_ref, gsgu_ref, bsgu_ref, expand_ref, expand_t_ref,
                   u_ref, vn_ref, qt_ref, k_ref, vt_ref, gates_ref, qat_ref, ka_ref,
                   carry_ref):
    @pl.when(pl.program_id(1) == 0)
    def _():
        carry_ref[...] = jnp.zeros_like(carry_ref)

    x = x_ref[0]
    tm = x.shape[0]
    ms = jnp.mean(x * x, axis=-1, keepdims=True)
    xb = ((x * lax.rsqrt(ms + EPS)) * gpre_ref[...]).astype(_BF16)

    z = jax.nn.gelu(_dot(xb, wz_ref[...]), approximate=True)
    u_ref[0] = z[:, :SGU_WIDTH].astype(_BF16)
    v = z[:, SGU_WIDTH:]
    mu = jnp.mean(v, axis=-1, keepdims=True)
    vc = v - mu
    var = jnp.mean(vc * vc, axis=-1, keepdims=True)
    vn_ref[0] = ((vc * lax.rsqrt(var + EPS)) * gsgu_ref[...] + bsgu_ref[...]).astype(_BF16)

    qt_ref[0] = (_dot_nt(wqt_ref[...], xb) * (HEAD_DIM ** -0.5 * LOG2E)).astype(_BF16)
    k_ref[0] = _dot(xb, wk_ref[...]).astype(_BF16)
    vt = _dot_nt(wvt_ref[...], xb).astype(_BF16)
    for h in range(N_HEADS):
        vt_ref[0, h, :HEAD_DIM, :] = vt[h * HEAD_DIM:(h + 1) * HEAD_DIM, :]
        vt_ref[0, h, HEAD_DIM:, :] = jnp.ones((PAIR - HEAD_DIM, tm), _BF16)

    gates_ref[0] = jax.nn.sigmoid(_dot(xb, wg_ref[...])).astype(_BF16)

    f = _dot(xb, wf_ref[...]) + bf_ref[...]
    log_f = jnp.minimum(f, 0.0) - jnp.log(1.0 + jnp.exp(-jnp.abs(f)))
    row = lax.broadcasted_iota(jnp.int32, (tm, tm), 0)
    col = lax.broadcasted_iota(jnp.int32, (tm, tm), 1)
    tri = jnp.where(row >= col, 1.0, 0.0).astype(_BF16)
    cum = carry_ref[0:1, :]
    for part in _split_bf16(log_f):
        cum = cum + _dot(tri, part)
    carry_ref[0:1, :] = cum[tm - 1:tm, :]

    spread = jnp.zeros((tm, LANES), _F32)
    spread_t = jnp.zeros((LANES, tm), _F32)
    for i, part in enumerate(_split_bf16(cum * LOG2E)):
        spread = spread + _dot(part, expand_ref[i])
        spread_t = spread_t + _dot_nt(expand_t_ref[i], part)
    lane = lax.broadcasted_iota(jnp.int32, (tm, LANES), 1)
    in_a = lane < BIAS_COLS
    in_b = jnp.logical_and(lane >= BIAS_COLS, lane < 2 * BIAS_COLS)
    ka_ref[0] = jnp.where(in_a, -spread, jnp.where(in_b, 1.0, 0.0)).astype(_BF16)
    row = lax.broadcasted_iota(jnp.int32, (LANES, tm), 0)
    in_a = row < BIAS_COLS
    in_b = jnp.logical_and(row >= BIAS_COLS, row < 2 * BIAS_COLS)
    qat_ref[0] = jnp.where(in_a, 1.0, jnp.where(in_b, spread_t, 0.0)).astype(_BF16)


def _expand_matrices():
    e = np.zeros((N_SPLIT, LANES, LANES), np.float32)
    for h in range(N_HEADS):
        for i in range(N_SPLIT):
            e[i, h, N_SPLIT * h + i] = 1.0
            e[i, h, BIAS_COLS + N_SPLIT * h + i] = 1.0
    return jnp.asarray(e, _BF16), jnp.asarray(e.transpose(0, 2, 1), _BF16)


def _in_projection(x, g_pre, w_in, b_forget, g_sgu, b_sgu):
    B, S, D = x.shape
    tm = ROW_TILE
    o = np.cumsum((0, 2 * SGU_WIDTH, ATTN_WIDTH, ATTN_WIDTH, ATTN_WIDTH, N_HEADS, 2 * D_MODEL))
    wz = w_in[:, o[0]:o[1]].astype(_BF16)
    wqt = w_in[:, o[1]:o[2]].T.astype(_BF16)
    wk = w_in[:, o[2]:o[3]].astype(_BF16)
    wvt = w_in[:, o[3]:o[4]].T.astype(_BF16)
    wf = jnp.pad(w_in[:, o[4]:o[5]], ((0, 0), (0, LANES - N_HEADS))).astype(_BF16)
    wg = w_in[:, o[5]:o[6]].astype(_BF16)
    bf = jnp.pad(b_forget.reshape(1, N_HEADS), ((0, 0), (0, LANES - N_HEADS)))

    def const(shape):
        return pl.BlockSpec(shape, lambda b, i: (0,) * len(shape))

    def rows(width):
        return pl.BlockSpec((1, tm, width), lambda b, i: (b, i, 0))

    out_shape = (
        jax.ShapeDtypeStruct((B, S, SGU_WIDTH), _BF16),
        jax.ShapeDtypeStruct((B, S, SGU_WIDTH), _BF16),
        jax.ShapeDtypeStruct((B, ATTN_WIDTH, S), _BF16),
        jax.ShapeDtypeStruct((B, S, ATTN_WIDTH), _BF16),
        jax.ShapeDtypeStruct((B, N_HEADS, PAIR, S), _BF16),
        jax.ShapeDtypeStruct((B, S, 2 * D_MODEL), _BF16),
        jax.ShapeDtypeStruct((B, LANES, S), _BF16),
        jax.ShapeDtypeStruct((B, S, LANES), _BF16),
    )

    def cols(height):
        return pl.BlockSpec((1, height, tm), lambda b, i: (b, 0, i))

    out_specs = (
        rows(SGU_WIDTH), rows(SGU_WIDTH), cols(ATTN_WIDTH), rows(ATTN_WIDTH),
        pl.BlockSpec((1, N_HEADS, PAIR, tm), lambda b, i: (b, 0, 0, i)),
        rows(2 * D_MODEL), cols(LANES), rows(LANES),
    )
    expand, expand_t = _expand_matrices()
    return pl.pallas_call(
        _inproj_kernel,
        grid=(B, S // tm),
        in_specs=[
            rows(D), const((1, D)),
            const(wz.shape), const(wqt.shape), const(wk.shape), const(wvt.shape),
            const(wf.shape), const(wg.shape),
            const((1, LANES)), const((1, SGU_WIDTH)), const((1, SGU_WIDTH)),
            const(expand.shape), const(expand_t.shape),
        ],
        out_specs=out_specs,
        out_shape=out_shape,
        scratch_shapes=[pltpu.VMEM((8, LANES), _F32)],
        compiler_params=pltpu.CompilerParams(
            dimension_semantics=("arbitrary", "arbitrary"),
            vmem_limit_bytes=VMEM_LIMIT),
        name="in_projection",
    )(x, g_pre.reshape(1, D), wz, wqt, wk, wvt, wf, wg, bf,
      g_sgu.reshape(1, SGU_WIDTH), b_sgu.reshape(1, SGU_WIDTH), expand, expand_t)


def _attn_kernel(k_ref, ka_ref, vt_ref, qt_ref, qat_ref, o_ref,
                 qf_ref, st_a, st_b, m_ref, acc_ref):
    pair = pl.program_id(1)
    S = k_ref.shape[1]
    row = lax.broadcasted_iota(jnp.int32, (PAIR, 1), 0)
    all_tiles = [(e, t) for t in range(NQ) for e in range(2)]

    def key_tile(k0):
        return jnp.concatenate([k_ref[0, pl.ds(k0, TK), :], ka_ref[0, pl.ds(k0, TK), :]], axis=1)

    def values(k0):
        return [vt_ref[0, e, :, pl.ds(k0, TK)] for e in range(2)]

    def scores(kf, e, t):
        return _dot(kf, qf_ref[e, :, t * TQ:(t + 1) * TQ])

    def update(e, t, st, vts, masked):
        cs = slice(t * TQ, (t + 1) * TQ)
        if masked:
            key_i = lax.broadcasted_iota(jnp.int32, (TK, TQ), 0)
            qry_i = lax.broadcasted_iota(jnp.int32, (TK, TQ), 1)
            st = jnp.where(key_i <= qry_i, st, -jnp.inf)
        m = m_ref[e, :, cs]
        m_new = jnp.maximum(m, jnp.max(st, axis=0, keepdims=True))
        p = jnp.exp2(st - m_new)
        alpha = jnp.exp2(m - m_new)
        m_ref[e, :, cs] = m_new
        acc_ref[e, :, cs] = alpha * acc_ref[e, :, cs] + _dot(vts[e], p.astype(_BF16))

    def step(cur, cur_tiles, vts, nxt, nxt_tiles, kf_next):
        for n in range(max(len(cur_tiles), len(nxt_tiles))):
            if n < len(nxt_tiles):
                ne, nt = nxt_tiles[n]
                new = scores(kf_next, ne, nt)
            if n < len(cur_tiles):
                e, t, masked = cur_tiles[n]
                update(e, t, cur[e, t], vts, masked)
            if n < len(nxt_tiles):
                nxt[ne, nt] = new

    def q_group(g, _):
        q0 = pl.multiple_of(g * QG, QG)
        qt = qt_ref[0, :, pl.ds(q0, QG)].astype(_F32)
        qat = qat_ref[0, :, pl.ds(q0, QG)].astype(_F32)
        for e in range(2):
            a0 = N_SPLIT * (2 * pair + e)
            q_mask = jnp.logical_and(row >= HEAD_DIM * e, row < HEAD_DIM * (e + 1))
            a_mask = jnp.logical_or(
                jnp.logical_and(row >= a0, row < a0 + N_SPLIT),
                jnp.logical_and(row >= BIAS_COLS + a0, row < BIAS_COLS + a0 + N_SPLIT))
            qf_ref[e, :PAIR, :] = jnp.where(q_mask, qt, 0.0).astype(_BF16)
            qf_ref[e, PAIR:, :] = jnp.where(a_mask, qat, 0.0).astype(_BF16)
        m_ref[...] = jnp.full(m_ref.shape, -jnp.inf, _F32)
        acc_ref[...] = jnp.zeros(acc_ref.shape, _F32)

        kf = key_tile(0)
        for e, t in all_tiles:
            st_a[e, t] = scores(kf, e, t)

        full = [(e, t, False) for e, t in all_tiles]

        def two_key_tiles(j, _):
            k0 = pl.multiple_of(2 * j * TK, 2 * TK)
            step(st_a, full, values(k0), st_b, all_tiles, key_tile(k0 + TK))
            step(st_b, full, values(k0 + TK), st_a, all_tiles, key_tile(k0 + 2 * TK))
            return 0

        lax.fori_loop(0, g * (NQ // 2), two_key_tiles, 0)

        bufs = (st_a, st_b)
        for i in range(NQ):
            cur_tiles = [(e, t, t == i) for t in range(i, NQ) for e in range(2)]
            nxt_tiles = [(e, t) for t in range(i + 1, NQ) for e in range(2)]
            kf_next = key_tile(q0 + (i + 1) * TK) if nxt_tiles else None
            step(bufs[i % 2], cur_tiles, values(q0 + i * TK), bufs[(i + 1) % 2], nxt_tiles, kf_next)

        num = jnp.concatenate([acc_ref[0, :HEAD_DIM, :], acc_ref[1, :HEAD_DIM, :]], axis=0)
        den = jnp.where(row < HEAD_DIM, acc_ref[0, HEAD_DIM:HEAD_DIM + 1, :],
                        acc_ref[1, HEAD_DIM:HEAD_DIM + 1, :])
        o_ref[0, pl.ds(q0, QG), :] = (num / den).T.astype(_BF16)
        return 0

    lax.fori_loop(0, S // QG, q_group, 0)


def _attention(k, ka, vt, qt, qat):
    B, S, _ = k.shape
    assert TQ == TK and NQ % 2 == 0 and S % QG == 0
    pair_rows = pl.BlockSpec((1, S, PAIR), lambda b, j: (b, 0, j))
    return pl.pallas_call(
        _attn_kernel,
        grid=(B, N_PAIRS),
        in_specs=[pair_rows, pl.BlockSpec((1, S, LANES), lambda b, j: (b, 0, 0)),
                  pl.BlockSpec((1, 2, PAIR, S), lambda b, j: (b, j, 0, 0)),
                  pl.BlockSpec((1, PAIR, S), lambda b, j: (b, j, 0)),
                  pl.BlockSpec((1, LANES, S), lambda b, j: (b, 0, 0))],
        out_specs=pair_rows,
        out_shape=jax.ShapeDtypeStruct((B, S, ATTN_WIDTH), _BF16),
        scratch_shapes=[pltpu.VMEM((2, PAIR + LANES, QG), _BF16),
                        pltpu.VMEM((2, NQ, TK, TQ), _F32), pltpu.VMEM((2, NQ, TK, TQ), _F32),
                        pltpu.VMEM((2, 1, QG), _F32), pltpu.VMEM((2, PAIR, QG), _F32)],
        compiler_params=pltpu.CompilerParams(
            dimension_semantics=("parallel", "parallel"),
            vmem_limit_bytes=VMEM_LIMIT),
        name="forgetting_attention",
    )(k, ka, vt, qt, qat)


def _merge_kernel(x_ref, u_ref, vn_ref, ya_ref, gates_ref, wsp_ref, bsp_ref,
                  wbs_ref, wba_ref, wout_ref, gpost_ref, o_ref, ysgu_ref):
    tm = x_ref.shape[0]
    row = lax.broadcasted_iota(jnp.int32, (CHUNK, CHUNK), 0)
    col = lax.broadcasted_iota(jnp.int32, (CHUNK, CHUNK), 1)
    lane = lax.broadcasted_iota(jnp.int32, (CHUNK, PAIR), 1)
    ws = [jnp.where(row >= col, wsp_ref[g], 0.0).astype(_BF16) for g in range(N_GROUPS)]
    for c in range(tm // CHUNK):
        r = slice(c * CHUNK, (c + 1) * CHUNK)
        for j in range(N_GROUPS // 2):
            cs = slice(j * PAIR, (j + 1) * PAIR)
            vp = vn_ref[r, cs]
            s = jnp.where(lane < HEAD_DIM, _dot(ws[2 * j], vp), _dot(ws[2 * j + 1], vp))
            s = s + bsp_ref[:, cs]
            ysgu_ref[r, cs] = (u_ref[r, cs].astype(_F32) * s).astype(_BF16)

    a = _dot(ysgu_ref[...], wbs_ref[...])
    b = _dot(ya_ref[...], wba_ref[...])
    merged = (gates_ref[:, :D_MODEL].astype(_F32) * a
              + gates_ref[:, D_MODEL:].astype(_F32) * b).astype(_BF16)
    o = _dot(merged, wout_ref[...])
    ms = jnp.mean(o * o, axis=-1, keepdims=True)
    o_ref[...] = x_ref[...] + (o * lax.rsqrt(ms + EPS)) * gpost_ref[...]


def _merge(x2, u2, vn2, ya2, gates2, w_spatial, b_spatial, w_bs, w_ba, w_out, g_post):
    R, D = x2.shape
    tm = ROW_TILE
    bsp = jnp.repeat(b_spatial.T, SGU_WIDTH // N_GROUPS, axis=1)

    def const(shape):
        return pl.BlockSpec(shape, lambda i: (0,) * len(shape))

    def rows(width):
        return pl.BlockSpec((tm, width), lambda i: (i, 0))

    return pl.pallas_call(
        _merge_kernel,
        grid=(R // tm,),
        in_specs=[rows(D), rows(SGU_WIDTH), rows(SGU_WIDTH), rows(ATTN_WIDTH), rows(2 * D),
                  const(w_spatial.shape), const(bsp.shape),
                  const(w_bs.shape), const(w_ba.shape), const(w_out.shape), const((1, D))],
        out_specs=rows(D),
        out_shape=jax.ShapeDtypeStruct((R, D), _F32),
        scratch_shapes=[pltpu.VMEM((tm, SGU_WIDTH), _BF16)],
        compiler_params=pltpu.CompilerParams(
            dimension_semantics=("parallel",), vmem_limit_bytes=VMEM_LIMIT),
        name="gated_merge",
    )(x2, u2, vn2, ya2, gates2, w_spatial, bsp,
      w_bs.astype(_BF16), w_ba.astype(_BF16), w_out.astype(_BF16), g_post.reshape(1, D))


def _mlp_kernel(h_ref, gpre_ref, wup_ref, wdown_ref, gpost_ref, o_ref):
    h = h_ref[...]
    ms = jnp.mean(h * h, axis=-1, keepdims=True)
    xb = ((h * lax.rsqrt(ms + EPS)) * gpre_ref[...]).astype(_BF16)
    acc = jnp.zeros(h.shape, _F32)
    for c in range(D_FF // FF_TILE):
        cs = slice(c * FF_TILE, (c + 1) * FF_TILE)
        hid = jnp.square(jnp.maximum(_dot(xb, wup_ref[:, cs]), 0.0)).astype(_BF16)
        acc = acc + _dot(hid, wdown_ref[cs, :])
    ms2 = jnp.mean(acc * acc, axis=-1, keepdims=True)
    o_ref[...] = h + (acc * lax.rsqrt(ms2 + EPS)) * gpost_ref[...]


def _mlp(h2, g_pre, w_up, w_down, g_post):
    R, D = h2.shape
    tm = ROW_TILE

    def const(shape):
        return pl.BlockSpec(shape, lambda i: (0,) * len(shape))

    rows = pl.BlockSpec((tm, D), lambda i: (i, 0))
    return pl.pallas_call(
        _mlp_kernel,
        grid=(R // tm,),
        in_specs=[rows, const((1, D)), const(w_up.shape), const(w_down.shape), const((1, D))],
        out_specs=rows,
        out_shape=jax.ShapeDtypeStruct((R, D), _F32),
        compiler_params=pltpu.CompilerParams(
            dimension_semantics=("parallel",), vmem_limit_bytes=VMEM_LIMIT),
        name="relu2_mlp",
    )(h2, g_pre.reshape(1, D), w_up.astype(_BF16), w_down.astype(_BF16), g_post.reshape(1, D))


def kernel(x, g_mix_pre, w_in, b_forget, g_sgu, b_sgu, w_spatial, b_spatial, w_branch_sgu,
           w_branch_attn, w_out, g_mix_post, g_ffn_pre, w_up, w_down, g_ffn_post):
    B, S, D = x.shape
    h = x
    for l in range(g_mix_pre.shape[0]):
        u, vn, qt, k, vt, gates, qat, ka = _in_projection(
            h, g_mix_pre[l], w_in[l], b_forget[l], g_sgu[l], b_sgu[l])
        y_attn = _attention(k, ka, vt, qt, qat)
        h1 = _merge(h.reshape(B * S, D), u.reshape(B * S, -1), vn.reshape(B * S, -1),
                    y_attn.reshape(B * S, -1), gates.reshape(B * S, -1),
                    w_spatial[l], b_spatial[l], w_branch_sgu[l], w_branch_attn[l], w_out[l],
                    g_mix_post[l])
        h = _mlp(h1, g_ffn_pre[l], w_up[l], w_down[l], g_ffn_post[l]).reshape(B, S, D)
    return h
```

```python
import functools

import jax
import jax.numpy as jnp
import numpy as np
from jax import lax
from jax.experimental import pallas as pl
from jax.experimental.pallas import tpu as pltpu

D_MODEL = 1024
N_HEADS = 8
HEAD_DIM = 64
ATTN_WIDTH = N_HEADS * HEAD_DIM
N_GROUPS = 8
SGU_WIDTH = D_MODEL // 2
CHUNK = 128
D_FF = 4 * D_MODEL
EPS = 1e-6
LOG2E = 1.4426950408889634

LANES = 128
N_SPLIT = 3
BIAS_COLS = N_HEADS * N_SPLIT
PAIR = 2 * HEAD_DIM
N_PAIRS = N_HEADS // 2

ROW_TILE = 512
FF_TILE = 1024
TQ = 256
TK = 256
NQ = 4
QG = NQ * TQ
KU = 4
VMEM_LIMIT = 56 * 1024 * 1024

_BF16 = jnp.bfloat16
_F32 = jnp.float32


def _split_bf16(x):
    parts = []
    r = x
    for _ in range(N_SPLIT):
        p = r.astype(_BF16)
        parts.append(p)
        r = r - p.astype(_F32)
    return parts


def _dot(a, b):
    return jnp.dot(a, b, preferred_element_type=_F32)


def _dot_nt(a, b):
    return lax.dot_general(a, b, (((1,), (1,)), ((), ())), preferred_element_type=_F32)


def _inproj_kernel(x_ref, gpre_ref, wz_ref, wqt_ref, wk_ref, wvt_ref, wf_ref, wg_ref,
                   bf_ref, gsgu_ref, bsgu_ref, tri_ref, expand_ref, expand_t_ref,
                   u_ref, vn_ref, qt_ref, k_ref, vt_ref, gates_ref, qat_ref, ka_ref,
                   carry_ref):
    @pl.when(pl.program_id(1) == 0)
    def _():
        carry_ref[...] = jnp.zeros_like(carry_ref)

    x = x_ref[0]
    tm = x.shape[0]
    ms = jnp.mean(x * x, axis=-1, keepdims=True)
    xb = ((x * lax.rsqrt(ms + EPS)) * gpre_ref[...]).astype(_BF16)

    lane = lax.broadcasted_iota(jnp.int32, (tm, LANES), 1)

    def pack_pieces(v):
        hi, mid, lo = _split_bf16(v)
        zero = jnp.zeros((), _BF16)
        return jnp.where(lane < N_HEADS, hi,
                         jnp.where(lane < 2 * N_HEADS, mid,
                                   jnp.where(lane < N_SPLIT * N_HEADS, lo, zero)))

    f = _dot(xb, wf_ref[...]) + bf_ref[...]

    z = jax.nn.gelu(_dot(xb, wz_ref[...]), approximate=True)
    u_ref[0] = z[:, :SGU_WIDTH].astype(_BF16)
    v = z[:, SGU_WIDTH:]
    mu = jnp.mean(v, axis=-1, keepdims=True)
    vc = v - mu
    var = jnp.mean(vc * vc, axis=-1, keepdims=True)
    vn_ref[0] = ((vc * lax.rsqrt(var + EPS)) * gsgu_ref[...] + bsgu_ref[...]).astype(_BF16)

    log_f = jnp.minimum(f, 0.0) - jnp.log(1.0 + jnp.exp(-jnp.abs(f)))
    sums = _dot(tri_ref[...], pack_pieces(log_f))

    qt_ref[0] = (_dot_nt(wqt_ref[...], xb) * (HEAD_DIM ** -0.5 * LOG2E)).astype(_BF16)
    k_ref[0] = _dot(xb, wk_ref[...]).astype(_BF16)

    total = sums
    for shift in (N_HEADS, 2 * N_HEADS, LANES - N_HEADS, LANES - 2 * N_HEADS):
        total = total + pltpu.roll(sums, shift, 1)
    cum = carry_ref[0:1, :] + total
    carry_ref[0:1, :] = cum[tm - 1:tm, :]
    pieces = pack_pieces(cum * LOG2E)
    spread = _dot(pieces, expand_ref[...])
    spread_t = _dot_nt(expand_t_ref[...], pieces)

    vt = _dot_nt(wvt_ref[...], xb).astype(_BF16)
    for h in range(N_HEADS):
        vt_ref[0, h, :HEAD_DIM, :] = vt[h * HEAD_DIM:(h + 1) * HEAD_DIM, :]
        vt_ref[0, h, HEAD_DIM:, :] = jnp.ones((PAIR - HEAD_DIM, tm), _BF16)

    gates_ref[0] = jax.nn.sigmoid(_dot(xb, wg_ref[...])).astype(_BF16)

    in_a = lane < BIAS_COLS
    in_b = jnp.logical_and(lane >= BIAS_COLS, lane < 2 * BIAS_COLS)
    ka_ref[0] = jnp.where(in_a, -spread, jnp.where(in_b, 1.0, 0.0)).astype(_BF16)
    row = lax.broadcasted_iota(jnp.int32, (LANES, tm), 0)
    in_a = row < BIAS_COLS
    in_b = jnp.logical_and(row >= BIAS_COLS, row < 2 * BIAS_COLS)
    qat_ref[0] = jnp.where(in_a, 1.0, jnp.where(in_b, spread_t, 0.0)).astype(_BF16)


def _expand_matrix():
    e = np.zeros((LANES, LANES), np.float32)
    for h in range(N_HEADS):
        for i in range(N_SPLIT):
            e[N_HEADS * i + h, N_SPLIT * h + i] = 1.0
            e[N_HEADS * i + h, BIAS_COLS + N_SPLIT * h + i] = 1.0
    return e


def _in_projection(x, g_pre, w_in, b_forget, g_sgu, b_sgu):
    B, S, D = x.shape
    tm = ROW_TILE
    o = np.cumsum((0, 2 * SGU_WIDTH, ATTN_WIDTH, ATTN_WIDTH, ATTN_WIDTH, N_HEADS, 2 * D_MODEL))
    wz = w_in[:, o[0]:o[1]].astype(_BF16)
    wqt = w_in[:, o[1]:o[2]].T.astype(_BF16)
    wk = w_in[:, o[2]:o[3]].astype(_BF16)
    wvt = w_in[:, o[3]:o[4]].T.astype(_BF16)
    lane_pad = ((0, 0), (0, LANES - N_SPLIT * N_HEADS))
    wf = jnp.pad(jnp.tile(w_in[:, o[4]:o[5]], (1, N_SPLIT)), lane_pad).astype(_BF16)
    wg = w_in[:, o[5]:o[6]].astype(_BF16)
    bf = jnp.pad(jnp.tile(b_forget.reshape(1, N_HEADS), (1, N_SPLIT)), lane_pad)
    tri = jnp.asarray(np.tril(np.ones((tm, tm), np.float32)), _BF16)
    expand = jnp.asarray(_expand_matrix(), _BF16)
    expand_t = jnp.asarray(_expand_matrix().T, _BF16)

    def const(shape):
        return pl.BlockSpec(shape, lambda b, i: (0,) * len(shape))

    def rows(width):
        return pl.BlockSpec((1, tm, width), lambda b, i: (b, i, 0))

    out_shape = (
        jax.ShapeDtypeStruct((B, S, SGU_WIDTH), _BF16),
        jax.ShapeDtypeStruct((B, S, SGU_WIDTH), _BF16),
        jax.ShapeDtypeStruct((B, ATTN_WIDTH, S), _BF16),
        jax.ShapeDtypeStruct((B, S, ATTN_WIDTH), _BF16),
        jax.ShapeDtypeStruct((B, N_HEADS, PAIR, S), _BF16),
        jax.ShapeDtypeStruct((B, S, 2 * D_MODEL), _BF16),
        jax.ShapeDtypeStruct((B, LANES, S), _BF16),
        jax.ShapeDtypeStruct((B, S, LANES), _BF16),
    )

    def cols(height):
        return pl.BlockSpec((1, height, tm), lambda b, i: (b, 0, i))

    out_specs = (
        rows(SGU_WIDTH), rows(SGU_WIDTH), cols(ATTN_WIDTH), rows(ATTN_WIDTH),
        pl.BlockSpec((1, N_HEADS, PAIR, tm), lambda b, i: (b, 0, 0, i)),
        rows(2 * D_MODEL), cols(LANES), rows(LANES),
    )
    return pl.pallas_call(
        _inproj_kernel,
        grid=(B, S // tm),
        in_specs=[
            rows(D), const((1, D)),
            const(wz.shape), const(wqt.shape), const(wk.shape), const(wvt.shape),
            const(wf.shape), const(wg.shape),
            const((1, LANES)), const((1, SGU_WIDTH)), const((1, SGU_WIDTH)),
            const(tri.shape), const(expand.shape), const(expand_t.shape),
        ],
        out_specs=out_specs,
        out_shape=out_shape,
        scratch_shapes=[pltpu.VMEM((8, LANES), _F32)],
        compiler_params=pltpu.CompilerParams(
            dimension_semantics=("arbitrary", "arbitrary"),
            vmem_limit_bytes=VMEM_LIMIT),
        name="in_projection",
    )(x, g_pre.reshape(1, D), wz, wqt, wk, wvt, wf, wg, bf,
      g_sgu.reshape(1, SGU_WIDTH), b_sgu.reshape(1, SGU_WIDTH), tri, expand, expand_t)


def _attn_kernel(k_ref, ka_ref, vt_ref, qt_ref, qat_ref, o_ref,
                 qf_ref, st_a, st_b, m_ref, acc_ref):
    pair = pl.program_id(1)
    S = k_ref.shape[1]
    row = lax.broadcasted_iota(jnp.int32, (PAIR, 1), 0)
    all_tiles = [(e, t) for t in range(NQ) for e in range(2)]

    def key_tile(k0):
        return jnp.concatenate([k_ref[0, pl.ds(k0, TK), :], ka_ref[0, pl.ds(k0, TK), :]], axis=1)

    def values(k0):
        return [vt_ref[0, e, :, pl.ds(k0, TK)] for e in range(2)]

    def scores(kf, e, t):
        return _dot(kf, qf_ref[e, :, t * TQ:(t + 1) * TQ])

    def update(e, t, st, vts, masked):
        cs = slice(t * TQ, (t + 1) * TQ)
        if masked:
            key_i = lax.broadcasted_iota(jnp.int32, (TK, TQ), 0)
            qry_i = lax.broadcasted_iota(jnp.int32, (TK, TQ), 1)
            st = jnp.where(key_i <= qry_i, st, -jnp.inf)
        m = m_ref[e, :, cs]
        m_new = jnp.maximum(m, jnp.max(st, axis=0, keepdims=True))
        p = jnp.exp2(st - m_new)
        alpha = jnp.exp2(m - m_new)
        m_ref[e, :, cs] = m_new
        acc_ref[e, :, cs] = alpha * acc_ref[e, :, cs] + _dot(vts[e], p.astype(_BF16))

    def step(cur, cur_tiles, vts, nxt, nxt_tiles, kf_next):
        for n in range(max(len(cur_tiles), len(nxt_tiles))):
            if n < len(nxt_tiles):
                ne, nt = nxt_tiles[n]
                new = scores(kf_next, ne, nt)
            if n < len(cur_tiles):
                e, t, masked = cur_tiles[n]
                update(e, t, cur[e, t], vts, masked)
            if n < len(nxt_tiles):
                nxt[ne, nt] = new

    def load_queries(g):
        q0 = pl.multiple_of(g * QG, QG)
        qt = qt_ref[0, :, pl.ds(q0, QG)].astype(_F32)
        qat = qat_ref[0, :, pl.ds(q0, QG)].astype(_F32)
        for e in range(2):
            a0 = N_SPLIT * (2 * pair + e)
            q_mask = jnp.logical_and(row >= HEAD_DIM * e, row < HEAD_DIM * (e + 1))
            a_mask = jnp.logical_or(
                jnp.logical_and(row >= a0, row < a0 + N_SPLIT),
                jnp.logical_and(row >= BIAS_COLS + a0, row < BIAS_COLS + a0 + N_SPLIT))
            qf_ref[e, :PAIR, :] = jnp.where(q_mask, qt, 0.0).astype(_BF16)
            qf_ref[e, PAIR:, :] = jnp.where(a_mask, qat, 0.0).astype(_BF16)

    def reset_state():
        m_ref[...] = jnp.full(m_ref.shape, -jnp.inf, _F32)
        acc_ref[...] = jnp.zeros(acc_ref.shape, _F32)

    full = [(e, t, False) for e, t in all_tiles]
    bufs = (st_a, st_b)
    n_groups = S // QG

    load_queries(0)
    reset_state()
    step(None, [], None, st_a, all_tiles, key_tile(0))

    def q_group(g, _):
        q0 = pl.multiple_of(g * QG, QG)

        def key_tiles(j, _):
            k0 = pl.multiple_of(j * KU * TK, KU * TK)
            for u in range(KU):
                step(bufs[u % 2], full, values(k0 + u * TK), bufs[(u + 1) % 2], all_tiles,
                     key_tile(k0 + (u + 1) * TK))
            return 0

        lax.fori_loop(0, g * (NQ // KU), key_tiles, 0)

        for i in range(NQ):
            cur_tiles = [(e, t, t == i) for t in range(i, NQ) for e in range(2)]
            if i + 1 < NQ:
                nxt_tiles = [(e, t) for t in range(i + 1, NQ) for e in range(2)]
                kf_next = key_tile(q0 + (i + 1) * TK)
            else:
                load_queries(jnp.minimum(g + 1, n_groups - 1))
                nxt_tiles, kf_next = all_tiles, key_tile(0)
            step(bufs[i % 2], cur_tiles, values(q0 + i * TK), bufs[(i + 1) % 2], nxt_tiles, kf_next)

        out = [acc_ref[e, :HEAD_DIM, :] * (1.0 / acc_ref[e, HEAD_DIM:HEAD_DIM + 1, :])
               for e in range(2)]
        o_ref[0, pl.ds(q0, QG), :] = jnp.concatenate(out, axis=0).T.astype(_BF16)
        reset_state()
        return 0

    lax.fori_loop(0, n_groups, q_group, 0)


def _attention(k, ka, vt, qt, qat):
    B, S, _ = k.shape
    assert TQ == TK and KU % 2 == 0 and NQ % KU == 0 and S % QG == 0
    pair_rows = pl.BlockSpec((1, S, PAIR), lambda b, j: (b, 0, j))
    return pl.pallas_call(
        _attn_kernel,
        grid=(B, N_PAIRS),
        in_specs=[pair_rows, pl.BlockSpec((1, S, LANES), lambda b, j: (b, 0, 0)),
                  pl.BlockSpec((1, 2, PAIR, S), lambda b, j: (b, j, 0, 0)),
                  pl.BlockSpec((1, PAIR, S), lambda b, j: (b, j, 0)),
                  pl.BlockSpec((1, LANES, S), lambda b, j: (b, 0, 0))],
        out_specs=pair_rows,
        out_shape=jax.ShapeDtypeStruct((B, S, ATTN_WIDTH), _BF16),
        scratch_shapes=[pltpu.VMEM((2, PAIR + LANES, QG), _BF16),
                        pltpu.VMEM((2, NQ, TK, TQ), _F32), pltpu.VMEM((2, NQ, TK, TQ), _F32),
                        pltpu.VMEM((2, 1, QG), _F32), pltpu.VMEM((2, PAIR, QG), _F32)],
        compiler_params=pltpu.CompilerParams(
            dimension_semantics=("parallel", "parallel"),
            vmem_limit_bytes=VMEM_LIMIT),
        name="forgetting_attention",
    )(k, ka, vt, qt, qat)


def _merge_kernel(x_ref, u_ref, vn_ref, ya_ref, gates_ref, wsp_ref, bsp_ref,
                  wbs_ref, wba_ref, wout_ref, gpost_ref, o_ref, ysgu_ref):
    tm = x_ref.shape[0]
    row = lax.broadcasted_iota(jnp.int32, (CHUNK, CHUNK), 0)
    col = lax.broadcasted_iota(jnp.int32, (CHUNK, CHUNK), 1)
    lane = lax.broadcasted_iota(jnp.int32, (CHUNK, PAIR), 1)
    ws = [jnp.where(row >= col, wsp_ref[g], 0.0).astype(_BF16) for g in range(N_GROUPS)]
    for c in range(tm // CHUNK):
        r = slice(c * CHUNK, (c + 1) * CHUNK)
        for j in range(N_GROUPS // 2):
            cs = slice(j * PAIR, (j + 1) * PAIR)
            vp = vn_ref[r, cs]
            s = jnp.where(lane < HEAD_DIM, _dot(ws[2 * j], vp), _dot(ws[2 * j + 1], vp))
            s = s + bsp_ref[:, cs]
            ysgu_ref[r, cs] = (u_ref[r, cs].astype(_F32) * s).astype(_BF16)

    a = _dot(ysgu_ref[...], wbs_ref[...])
    b = _dot(ya_ref[...], wba_ref[...])
    merged = (gates_ref[:, :D_MODEL].astype(_F32) * a
              + gates_ref[:, D_MODEL:].astype(_F32) * b).astype(_BF16)
    o = _dot(merged, wout_ref[...])
    ms = jnp.mean(o * o, axis=-1, keepdims=True)
    o_ref[...] = x_ref[...] + (o * lax.rsqrt(ms + EPS)) * gpost_ref[...]


def _merge(x2, u2, vn2, ya2, gates2, w_spatial, b_spatial, w_bs, w_ba, w_out, g_post):
    R, D = x2.shape
    tm = ROW_TILE
    bsp = jnp.repeat(b_spatial.T, SGU_WIDTH // N_GROUPS, axis=1)

    def const(shape):
        return pl.BlockSpec(shape, lambda i: (0,) * len(shape))

    def rows(width):
        return pl.BlockSpec((tm, width), lambda i: (i, 0))

    return pl.pallas_call(
        _merge_kernel,
        grid=(R // tm,),
        in_specs=[rows(D), rows(SGU_WIDTH), rows(SGU_WIDTH), rows(ATTN_WIDTH), rows(2 * D),
                  const(w_spatial.shape), const(bsp.shape),
                  const(w_bs.shape), const(w_ba.shape), const(w_out.shape), const((1, D))],
        out_specs=rows(D),
        out_shape=jax.ShapeDtypeStruct((R, D), _F32),
        scratch_shapes=[pltpu.VMEM((tm, SGU_WIDTH), _BF16)],
        compiler_params=pltpu.CompilerParams(
            dimension_semantics=("parallel",), vmem_limit_bytes=VMEM_LIMIT),
        name="gated_merge",
    )(x2, u2, vn2, ya2, gates2, w_spatial, bsp,
      w_bs.astype(_BF16), w_ba.astype(_BF16), w_out.astype(_BF16), g_post.reshape(1, D))


def _mlp_kernel(h_ref, gpre_ref, wup_ref, wdown_ref, gpost_ref, o_ref):
    h = h_ref[...]
    ms = jnp.mean(h * h, axis=-1, keepdims=True)
    xb = ((h * lax.rsqrt(ms + EPS)) * gpre_ref[...]).astype(_BF16)
    acc = jnp.zeros(h.shape, _F32)
    for c in range(D_FF // FF_TILE):
        cs = slice(c * FF_TILE, (c + 1) * FF_TILE)
        hid = jnp.square(jnp.maximum(_dot(xb, wup_ref[:, cs]), 0.0)).astype(_BF16)
        acc = acc + _dot(hid, wdown_ref[cs, :])
    ms2 = jnp.mean(acc * acc, axis=-1, keepdims=True)
    o_ref[...] = h + (acc * lax.rsqrt(ms2 + EPS)) * gpost_ref[...]


def _mlp(h2, g_pre, w_up, w_down, g_post):
    R, D = h2.shape
    tm = ROW_TILE

    def const(shape):
        return pl.BlockSpec(shape, lambda i: (0,) * len(shape))

    rows = pl.BlockSpec((tm, D), lambda i: (i, 0))
    return pl.pallas_call(
        _mlp_kernel,
        grid=(R // tm,),
        in_specs=[rows, const((1, D)), const(w_up.shape), const(w_down.shape), const((1, D))],
        out_specs=rows,
        out_shape=jax.ShapeDtypeStruct((R, D), _F32),
        compiler_params=pltpu.CompilerParams(
            dimension_semantics=("parallel",), vmem_limit_bytes=VMEM_LIMIT),
        name="relu2_mlp",
    )(h2, g_pre.reshape(1, D), w_up.astype(_BF16), w_down.astype(_BF16), g_post.reshape(1, D))


def kernel(x, g_mix_pre, w_in, b_forget, g_sgu, b_sgu, w_spatial, b_spatial, w_branch_sgu,
           w_branch_attn, w_out, g_mix_post, g_ffn_pre, w_up, w_down, g_ffn_post):
    B, S, D = x.shape
    h = x
    for l in range(g_mix_pre.shape[0]):
        u, vn, qt, k, vt, gates, qat, ka = _in_projection(
            h, g_mix_pre[l], w_in[l], b_forget[l], g_sgu[l], b_sgu[l])
        y_attn = _attention(k, ka, vt, qt, qat)
        h1 = _merge(h.reshape(B * S, D), u.reshape(B * S, -1), vn.reshape(B * S, -1),
                    y_attn.reshape(B * S, -1), gates.reshape(B * S, -1),
                    w_spatial[l], b_spatial[l], w_branch_sgu[l], w_branch_attn[l], w_out[l],
                    g_mix_post[l])
        h = _mlp(h1, g_ffn_pre[l], w_up[l], w_down[l], g_ffn_post[l]).reshape(B, S, D)
    return h
```

```python
import functools

import jax
import jax.numpy as jnp
import numpy as np
from jax import lax
from jax.experimental import pallas as pl
from jax.experimental.pallas import tpu as pltpu

D_MODEL = 1024
N_HEADS = 8
HEAD_DIM = 64
ATTN_WIDTH = N_HEADS * HEAD_DIM
N_GROUPS = 8
SGU_WIDTH = D_MODEL // 2
CHUNK = 128
D_FF = 4 * D_MODEL
EPS = 1e-6
LOG2E = 1.4426950408889634

LANES = 128
N_SPLIT = 3
BIAS_COLS = N_HEADS * N_SPLIT
PAIR = 2 * HEAD_DIM
N_PAIRS = N_HEADS // 2

ROW_TILE = 512
FF_TILE = 1024
TQ = 256
TK = 256
NQ = 4
QG = NQ * TQ
KU = 2
N_STATS = 4
ZERO_EXP2 = 160.0
NORM_SLACK = 2.05
VMEM_LIMIT = 56 * 1024 * 1024

_BF16 = jnp.bfloat16
_F32 = jnp.float32


def _split_bf16(x):
    parts = []
    r = x
    for _ in range(N_SPLIT):
        p = r.astype(_BF16)
        parts.append(p)
        r = r - p.astype(_F32)
    return parts


def _dot(a, b):
    return jnp.dot(a, b, preferred_element_type=_F32)


def _dot_nt(a, b):
    return lax.dot_general(a, b, (((1,), (1,)), ((), ())), preferred_element_type=_F32)


def _inproj_kernel(x_ref, gpre_ref, wz_ref, wqt_ref, wk_ref, wvt_ref, wf_ref, wg_ref,
                   bf_ref, gsgu_ref, bsgu_ref, tri_ref, expand_ref, expand_t_ref,
                   head_rows_ref, piece_rows_ref,
                   u_ref, vn_ref, qt_ref, k_ref, vt_ref, gates_ref, qat_ref, ka_ref, stats_ref,
                   carry_ref):
    @pl.when(pl.program_id(1) == 0)
    def _():
        carry_ref[...] = jnp.zeros_like(carry_ref)
        stats_ref[...] = jnp.zeros_like(stats_ref)

    x = x_ref[0]
    tm = x.shape[0]
    ms = jnp.mean(x * x, axis=-1, keepdims=True)
    xb = ((x * lax.rsqrt(ms + EPS)) * gpre_ref[...]).astype(_BF16)

    lane = lax.broadcasted_iota(jnp.int32, (tm, LANES), 1)

    def pack_pieces(v):
        hi, mid, lo = _split_bf16(v)
        zero = jnp.zeros((), _BF16)
        return jnp.where(lane < N_HEADS, hi,
                         jnp.where(lane < 2 * N_HEADS, mid,
                                   jnp.where(lane < N_SPLIT * N_HEADS, lo, zero)))

    f = _dot(xb, wf_ref[...]) + bf_ref[...]

    z = jax.nn.gelu(_dot(xb, wz_ref[...]), approximate=True)
    u_ref[0] = z[:, :SGU_WIDTH].astype(_BF16)
    v = z[:, SGU_WIDTH:]
    mu = jnp.mean(v, axis=-1, keepdims=True)
    vc = v - mu
    var = jnp.mean(vc * vc, axis=-1, keepdims=True)
    vn_ref[0] = ((vc * lax.rsqrt(var + EPS)) * gsgu_ref[...] + bsgu_ref[...]).astype(_BF16)

    log_f = jnp.minimum(f, 0.0) - jnp.log(1.0 + jnp.exp(-jnp.abs(f)))
    sums = _dot(tri_ref[...], pack_pieces(log_f))

    qt = _dot_nt(wqt_ref[...], xb) * (HEAD_DIM ** -0.5 * LOG2E)
    qt_ref[0] = qt.astype(_BF16)
    k = _dot(xb, wk_ref[...])
    k_ref[0] = k.astype(_BF16)
    qn2 = _dot(head_rows_ref[...], (qt * qt).astype(_BF16))
    kn2 = _dot_nt(head_rows_ref[...], (k * k).astype(_BF16))

    total = sums
    for shift in (N_HEADS, 2 * N_HEADS, LANES - N_HEADS, LANES - 2 * N_HEADS):
        total = total + pltpu.roll(sums, shift, 1)
    cum = carry_ref[0:1, :] + total
    carry_ref[0:1, :] = cum[tm - 1:tm, :]
    pieces = pack_pieces(cum * LOG2E)
    spread = _dot(pieces, expand_ref[...])
    spread_t = _dot_nt(expand_t_ref[...], pieces)

    cum_t = _dot_nt(piece_rows_ref[...], pieces)
    tile_lane = lax.broadcasted_iota(jnp.int32, (2 * N_HEADS, LANES), 1)
    first_tile = pl.program_id(1) * (tm // TK)
    stats = [stats_ref[0, s] for s in range(N_STATS)]
    for j in range(tm // TK):
        lo, hi = j * TK, (j + 1) * TK
        cols = (cum_t[:, lo:lo + 1], cum_t[:, hi - 1:hi],
                jnp.max(qn2[:, lo:hi], axis=1, keepdims=True),
                jnp.max(kn2[:, lo:hi], axis=1, keepdims=True))
        stats = [jnp.where(tile_lane == first_tile + j, c, s) for c, s in zip(cols, stats)]
    for s in range(N_STATS):
        stats_ref[0, s] = stats[s]

    vt = _dot_nt(wvt_ref[...], xb).astype(_BF16)
    for h in range(N_HEADS):
        vt_ref[0, h, :HEAD_DIM, :] = vt[h * HEAD_DIM:(h + 1) * HEAD_DIM, :]
        vt_ref[0, h, HEAD_DIM:, :] = jnp.ones((PAIR - HEAD_DIM, tm), _BF16)

    gates_ref[0] = jax.nn.sigmoid(_dot(xb, wg_ref[...])).astype(_BF16)

    in_a = lane < BIAS_COLS
    in_b = jnp.logical_and(lane >= BIAS_COLS, lane < 2 * BIAS_COLS)
    ka_ref[0] = jnp.where(in_a, -spread, jnp.where(in_b, 1.0, 0.0)).astype(_BF16)
    row = lax.broadcasted_iota(jnp.int32, (LANES, tm), 0)
    in_a = row < BIAS_COLS
    in_b = jnp.logical_and(row >= BIAS_COLS, row < 2 * BIAS_COLS)
    qat_ref[0] = jnp.where(in_a, 1.0, jnp.where(in_b, spread_t, 0.0)).astype(_BF16)


def _head_rows():
    r = np.zeros((2 * N_HEADS, ATTN_WIDTH), np.float32)
    for h in range(N_HEADS):
        r[h, h * HEAD_DIM:(h + 1) * HEAD_DIM] = 1.0
    return r


def _piece_rows():
    r = np.zeros((2 * N_HEADS, LANES), np.float32)
    for h in range(N_HEADS):
        for i in range(N_SPLIT):
            r[h, N_HEADS * i + h] = 1.0
    return r


def _expand_matrix():
    e = np.zeros((LANES, LANES), np.float32)
    for h in range(N_HEADS):
        for i in range(N_SPLIT):
            e[N_HEADS * i + h, N_SPLIT * h + i] = 1.0
            e[N_HEADS * i + h, BIAS_COLS + N_SPLIT * h + i] = 1.0
    return e


def _in_projection(x, g_pre, w_in, b_forget, g_sgu, b_sgu):
    B, S, D = x.shape
    tm = ROW_TILE
    o = np.cumsum((0, 2 * SGU_WIDTH, ATTN_WIDTH, ATTN_WIDTH, ATTN_WIDTH, N_HEADS, 2 * D_MODEL))
    wz = w_in[:, o[0]:o[1]].astype(_BF16)
    wqt = w_in[:, o[1]:o[2]].T.astype(_BF16)
    wk = w_in[:, o[2]:o[3]].astype(_BF16)
    wvt = w_in[:, o[3]:o[4]].T.astype(_BF16)
    lane_pad = ((0, 0), (0, LANES - N_SPLIT * N_HEADS))
    wf = jnp.pad(jnp.tile(w_in[:, o[4]:o[5]], (1, N_SPLIT)), lane_pad).astype(_BF16)
    wg = w_in[:, o[5]:o[6]].astype(_BF16)
    bf = jnp.pad(jnp.tile(b_forget.reshape(1, N_HEADS), (1, N_SPLIT)), lane_pad)
    tri = jnp.asarray(np.tril(np.ones((tm, tm), np.float32)), _BF16)
    expand = jnp.asarray(_expand_matrix(), _BF16)
    expand_t = jnp.asarray(_expand_matrix().T, _BF16)
    head_rows = jnp.asarray(_head_rows(), _BF16)
    piece_rows = jnp.asarray(_piece_rows(), _BF16)

    def const(shape):
        return pl.BlockSpec(shape, lambda b, i: (0,) * len(shape))

    def rows(width):
        return pl.BlockSpec((1, tm, width), lambda b, i: (b, i, 0))

    out_shape = (
        jax.ShapeDtypeStruct((B, S, SGU_WIDTH), _BF16),
        jax.ShapeDtypeStruct((B, S, SGU_WIDTH), _BF16),
        jax.ShapeDtypeStruct((B, ATTN_WIDTH, S), _BF16),
        jax.ShapeDtypeStruct((B, S, ATTN_WIDTH), _BF16),
        jax.ShapeDtypeStruct((B, N_HEADS, PAIR, S), _BF16),
        jax.ShapeDtypeStruct((B, S, 2 * D_MODEL), _BF16),
        jax.ShapeDtypeStruct((B, LANES, S), _BF16),
        jax.ShapeDtypeStruct((B, S, LANES), _BF16),
        jax.ShapeDtypeStruct((B, N_STATS, 2 * N_HEADS, LANES), _F32),
    )

    def cols(height):
        return pl.BlockSpec((1, height, tm), lambda b, i: (b, 0, i))

    out_specs = (
        rows(SGU_WIDTH), rows(SGU_WIDTH), cols(ATTN_WIDTH), rows(ATTN_WIDTH),
        pl.BlockSpec((1, N_HEADS, PAIR, tm), lambda b, i: (b, 0, 0, i)),
        rows(2 * D_MODEL), cols(LANES), rows(LANES),
        pl.BlockSpec((1, N_STATS, 2 * N_HEADS, LANES), lambda b, i: (b, 0, 0, 0)),
    )
    return pl.pallas_call(
        _inproj_kernel,
        grid=(B, S // tm),
        in_specs=[
            rows(D), const((1, D)),
            const(wz.shape), const(wqt.shape), const(wk.shape), const(wvt.shape),
            const(wf.shape), const(wg.shape),
            const((1, LANES)), const((1, SGU_WIDTH)), const((1, SGU_WIDTH)),
            const(tri.shape), const(expand.shape), const(expand_t.shape),
            const(head_rows.shape), const(piece_rows.shape),
        ],
        out_specs=out_specs,
        out_shape=out_shape,
        scratch_shapes=[pltpu.VMEM((8, LANES), _F32)],
        compiler_params=pltpu.CompilerParams(
            dimension_semantics=("arbitrary", "arbitrary"),
            vmem_limit_bytes=VMEM_LIMIT),
        name="in_projection",
    )(x, g_pre.reshape(1, D), wz, wqt, wk, wvt, wf, wg, bf,
      g_sgu.reshape(1, SGU_WIDTH), b_sgu.reshape(1, SGU_WIDTH), tri, expand, expand_t,
      head_rows, piece_rows)


def _attn_kernel(nback_ref, k_ref, ka_ref, vt_ref, qt_ref, qat_ref, o_ref,
                 qf_ref, st_a, st_b, m_ref, acc_ref):
    batch = pl.program_id(0)
    pair = pl.program_id(1)
    S = k_ref.shape[1]
    row = lax.broadcasted_iota(jnp.int32, (PAIR, 1), 0)
    all_tiles = [(e, t) for t in range(NQ) for e in range(2)]

    def key_tile(k0):
        return jnp.concatenate([k_ref[0, pl.ds(k0, TK), :], ka_ref[0, pl.ds(k0, TK), :]], axis=1)

    def values(k0):
        return [vt_ref[0, e, :, pl.ds(k0, TK)] for e in range(2)]

    def scores(kf, e, t):
        return _dot(kf, qf_ref[e, :, t * TQ:(t + 1) * TQ])

    def update(e, t, st, vts, masked):
        cs = slice(t * TQ, (t + 1) * TQ)
        if masked:
            key_i = lax.broadcasted_iota(jnp.int32, (TK, TQ), 0)
            qry_i = lax.broadcasted_iota(jnp.int32, (TK, TQ), 1)
            st = jnp.where(key_i <= qry_i, st, -jnp.inf)
        m = m_ref[e, :, cs]
        m_new = jnp.maximum(m, jnp.max(st, axis=0, keepdims=True))
        p = jnp.exp2(st - m_new)
        alpha = jnp.exp2(m - m_new)
        m_ref[e, :, cs] = m_new
        acc_ref[e, :, cs] = alpha * acc_ref[e, :, cs] + _dot(vts[e], p.astype(_BF16))

    def step(cur, cur_tiles, vts, nxt, nxt_tiles, kf_next):
        for n in range(max(len(cur_tiles), len(nxt_tiles))):
            if n < len(nxt_tiles):
                e, t = nxt_tiles[n]
                nxt[e, t] = scores(kf_next, e, t)
            if n < len(cur_tiles):
                e, t, masked = cur_tiles[n]
                update(e, t, cur[e, t], vts, masked)

    def load_queries(g):
        q0 = pl.multiple_of(g * QG, QG)
        qt = qt_ref[0, :, pl.ds(q0, QG)].astype(_F32)
        qat = qat_ref[0, :, pl.ds(q0, QG)].astype(_F32)
        for e in range(2):
            a0 = N_SPLIT * (2 * pair + e)
            q_mask = jnp.logical_and(row >= HEAD_DIM * e, row < HEAD_DIM * (e + 1))
            a_mask = jnp.logical_or(
                jnp.logical_and(row >= a0, row < a0 + N_SPLIT),
                jnp.logical_and(row >= BIAS_COLS + a0, row < BIAS_COLS + a0 + N_SPLIT))
            qf_ref[e, :PAIR, :] = jnp.where(q_mask, qt, 0.0).astype(_BF16)
            qf_ref[e, PAIR:, :] = jnp.where(a_mask, qat, 0.0).astype(_BF16)

    def reset_state():
        m_ref[...] = jnp.full(m_ref.shape, -jnp.inf, _F32)
        acc_ref[...] = jnp.zeros(acc_ref.shape, _F32)

    full = [(e, t, False) for e, t in all_tiles]
    bufs = (st_a, st_b)
    n_groups = S // QG

    def first_key(g):
        return pl.multiple_of((g * NQ - nback_ref[batch, pair, g]) * TK, TK)

    load_queries(0)
    reset_state()
    step(None, [], None, st_a, all_tiles, key_tile(first_key(0)))

    def q_group(g, _):
        q0 = pl.multiple_of(g * QG, QG)
        k_first = first_key(g)

        def key_tiles(j, _):
            k0 = pl.multiple_of(k_first + j * KU * TK, TK)
            for u in range(KU):
                step(bufs[u % 2], full, values(k0 + u * TK), bufs[(u + 1) % 2], all_tiles,
                     key_tile(k0 + (u + 1) * TK))
            return 0

        lax.fori_loop(0, nback_ref[batch, pair, g] // KU, key_tiles, 0)

        for i in range(NQ):
            cur_tiles = [(e, t, t == i) for t in range(i, NQ) for e in range(2)]
            if i + 1 < NQ:
                nxt_tiles = [(e, t) for t in range(i + 1, NQ) for e in range(2)]
                kf_next = key_tile(q0 + (i + 1) * TK)
            else:
                g_next = jnp.minimum(g + 1, n_groups - 1)
                load_queries(g_next)
                nxt_tiles, kf_next = all_tiles, key_tile(first_key(g_next))
            step(bufs[i % 2], cur_tiles, values(q0 + i * TK), bufs[(i + 1) % 2], nxt_tiles, kf_next)

        out = [acc_ref[e, :HEAD_DIM, :] * (1.0 / acc_ref[e, HEAD_DIM:HEAD_DIM + 1, :])
               for e in range(2)]
        o_ref[0, pl.ds(q0, QG), :] = jnp.concatenate(out, axis=0).T.astype(_BF16)
        reset_state()
        return 0

    lax.fori_loop(0, n_groups, q_group, 0)


def _tiles_to_visit(stats, n_tiles):
    cum_first, cum_last, qn2, kn2 = (stats[:, s, :N_HEADS, :n_tiles] for s in range(N_STATS))
    B = stats.shape[0]
    n_groups = n_tiles // NQ
    q_norm = jnp.sqrt(jnp.max(qn2.reshape(B, N_HEADS, n_groups, NQ), axis=-1))
    k_norm = jnp.sqrt(jnp.max(kn2, axis=-1, keepdims=True))
    cum_group = cum_first[:, :, ::NQ]
    bound = (NORM_SLACK * q_norm * k_norm + cum_group)[..., None] - cum_last[:, :, None, :]
    tile = jnp.arange(n_tiles)[None, None, None, :]
    group_start = (jnp.arange(n_groups) * NQ)[None, None, :, None]
    visit = jnp.logical_and(tile < group_start, bound >= -ZERO_EXP2)
    n = jnp.sum(visit, axis=-1).astype(jnp.int32)
    n = jnp.max(n.reshape(B, N_PAIRS, 2, n_groups), axis=2)
    n = ((n + KU - 1) // KU) * KU
    return jnp.minimum(n, (jnp.arange(n_groups) * NQ)[None, None, :]).astype(jnp.int32)


def _attention(k, ka, vt, qt, qat, stats):
    B, S, _ = k.shape
    assert TQ == TK and KU % 2 == 0 and NQ % KU == 0 and S % QG == 0
    nback = _tiles_to_visit(stats, S // TK)
    pair_rows = pl.BlockSpec((1, S, PAIR), lambda b, j, nb: (b, 0, j))
    return pl.pallas_call(
        _attn_kernel,
        grid_spec=pltpu.PrefetchScalarGridSpec(
            num_scalar_prefetch=1,
            grid=(B, N_PAIRS),
            in_specs=[pair_rows, pl.BlockSpec((1, S, LANES), lambda b, j, nb: (b, 0, 0)),
                      pl.BlockSpec((1, 2, PAIR, S), lambda b, j, nb: (b, j, 0, 0)),
                      pl.BlockSpec((1, PAIR, S), lambda b, j, nb: (b, j, 0)),
                      pl.BlockSpec((1, LANES, S), lambda b, j, nb: (b, 0, 0))],
            out_specs=pair_rows,
            scratch_shapes=[pltpu.VMEM((2, PAIR + LANES, QG), _BF16),
                            pltpu.VMEM((2, NQ, TK, TQ), _F32), pltpu.VMEM((2, NQ, TK, TQ), _F32),
                            pltpu.VMEM((2, 1, QG), _F32), pltpu.VMEM((2, PAIR, QG), _F32)]),
        out_shape=jax.ShapeDtypeStruct((B, S, ATTN_WIDTH), _BF16),
        compiler_params=pltpu.CompilerParams(
            dimension_semantics=("parallel", "parallel"),
            vmem_limit_bytes=VMEM_LIMIT),
        name="forgetting_attention",
    )(nback, k, ka, vt, qt, qat)


def _merge_kernel(x_ref, u_ref, vn_ref, ya_ref, gates_ref, wsp_ref, bsp_ref,
                  wbs_ref, wba_ref, wout_ref, gpost_ref, o_ref, ysgu_ref):
    tm = x_ref.shape[0]
    row = lax.broadcasted_iota(jnp.int32, (CHUNK, CHUNK), 0)
    col = lax.broadcasted_iota(jnp.int32, (CHUNK, CHUNK), 1)
    lane = lax.broadcasted_iota(jnp.int32, (CHUNK, PAIR), 1)
    ws = [jnp.where(row >= col, wsp_ref[g], 0.0).astype(_BF16) for g in range(N_GROUPS)]
    for c in range(tm // CHUNK):
        r = slice(c * CHUNK, (c + 1) * CHUNK)
        for j in range(N_GROUPS // 2):
            cs = slice(j * PAIR, (j + 1) * PAIR)
            vp = vn_ref[r, cs]
            s = jnp.where(lane < HEAD_DIM, _dot(ws[2 * j], vp), _dot(ws[2 * j + 1], vp))
            s = s + bsp_ref[:, cs]
            ysgu_ref[r, cs] = (u_ref[r, cs].astype(_F32) * s).astype(_BF16)

    a = _dot(ysgu_ref[...], wbs_ref[...])
    b = _dot(ya_ref[...], wba_ref[...])
    merged = (gates_ref[:, :D_MODEL].astype(_F32) * a
              + gates_ref[:, D_MODEL:].astype(_F32) * b).astype(_BF16)
    o = _dot(merged, wout_ref[...])
    ms = jnp.mean(o * o, axis=-1, keepdims=True)
    o_ref[...] = x_ref[...] + (o * lax.rsqrt(ms + EPS)) * gpost_ref[...]


def _merge(x2, u2, vn2, ya2, gates2, w_spatial, b_spatial, w_bs, w_ba, w_out, g_post):
    R, D = x2.shape
    tm = ROW_TILE
    bsp = jnp.repeat(b_spatial.T, SGU_WIDTH // N_GROUPS, axis=1)

    def const(shape):
        return pl.BlockSpec(shape, lambda i: (0,) * len(shape))

    def rows(width):
        return pl.BlockSpec((tm, width), lambda i: (i, 0))

    return pl.pallas_call(
        _merge_kernel,
        grid=(R // tm,),
        in_specs=[rows(D), rows(SGU_WIDTH), rows(SGU_WIDTH), rows(ATTN_WIDTH), rows(2 * D),
                  const(w_spatial.shape), const(bsp.shape),
                  const(w_bs.shape), const(w_ba.shape), const(w_out.shape), const((1, D))],
        out_specs=rows(D),
        out_shape=jax.ShapeDtypeStruct((R, D), _F32),
        scratch_shapes=[pltpu.VMEM((tm, SGU_WIDTH), _BF16)],
        compiler_params=pltpu.CompilerParams(
            dimension_semantics=("parallel",), vmem_limit_bytes=VMEM_LIMIT),
        name="gated_merge",
    )(x2, u2, vn2, ya2, gates2, w_spatial, bsp,
      w_bs.astype(_BF16), w_ba.astype(_BF16), w_out.astype(_BF16), g_post.reshape(1, D))


def _mlp_kernel(h_ref, gpre_ref, wup_ref, wdown_ref, gpost_ref, o_ref):
    h = h_ref[...]
    ms = jnp.mean(h * h, axis=-1, keepdims=True)
    xb = ((h * lax.rsqrt(ms + EPS)) * gpre_ref[...]).astype(_BF16)
    acc = jnp.zeros(h.shape, _F32)
    for c in range(D_FF // FF_TILE):
        cs = slice(c * FF_TILE, (c + 1) * FF_TILE)
        hid = jnp.square(jnp.maximum(_dot(xb, wup_ref[:, cs]), 0.0)).astype(_BF16)
        acc = acc + _dot(hid, wdown_ref[cs, :])
    ms2 = jnp.mean(acc * acc, axis=-1, keepdims=True)
    o_ref[...] = h + (acc * lax.rsqrt(ms2 + EPS)) * gpost_ref[...]


def _mlp(h2, g_pre, w_up, w_down, g_post):
    R, D = h2.shape
    tm = ROW_TILE

    def const(shape):
        return pl.BlockSpec(shape, lambda i: (0,) * len(shape))

    rows = pl.BlockSpec((tm, D), lambda i: (i, 0))
    return pl.pallas_call(
        _mlp_kernel,
        grid=(R // tm,),
        in_specs=[rows, const((1, D)), const(w_up.shape), const(w_down.shape), const((1, D))],
        out_specs=rows,
        out_shape=jax.ShapeDtypeStruct((R, D), _F32),
        compiler_params=pltpu.CompilerParams(
            dimension_semantics=("parallel",), vmem_limit_bytes=VMEM_LIMIT),
        name="relu2_mlp",
    )(h2, g_pre.reshape(1, D), w_up.astype(_BF16), w_down.astype(_BF16), g_post.reshape(1, D))


def kernel(x, g_mix_pre, w_in, b_forget, g_sgu, b_sgu, w_spatial, b_spatial, w_branch_sgu,
           w_branch_attn, w_out, g_mix_post, g_ffn_pre, w_up, w_down, g_ffn_post):
    B, S, D = x.shape
    h = x
    for l in range(g_mix_pre.shape[0]):
        u, vn, qt, k, vt, gates, qat, ka, stats = _in_projection(
            h, g_mix_pre[l], w_in[l], b_forget[l], g_sgu[l], b_sgu[l])
        y_attn = _attention(k, ka, vt, qt, qat, stats)
        h1 = _merge(h.reshape(B * S, D), u.reshape(B * S, -1), vn.reshape(B * S, -1),
                    y_attn.reshape(B * S, -1), gates.reshape(B * S, -1),
                    w_spatial[l], b_spatial[l], w_branch_sgu[l], w_branch_attn[l], w_out[l],
                    g_mix_post[l])
        h = _mlp(h1, g_ffn_pre[l], w_up[l], w_down[l], g_ffn_post[l]).reshape(B, S, D)
    return h
```

```python
import functools

import jax
import jax.numpy as jnp
import numpy as np
from jax import lax
from jax.experimental import pallas as pl
from jax.experimental.pallas import tpu as pltpu

D_MODEL = 1024
N_HEADS = 8
HEAD_DIM = 64
ATTN_WIDTH = N_HEADS * HEAD_DIM
N_GROUPS = 8
SGU_WIDTH = D_MODEL // 2
CHUNK = 128
D_FF = 4 * D_MODEL
EPS = 1e-6
LOG2E = 1.4426950408889634

LANES = 128
N_SPLIT = 3
BIAS_COLS = N_HEADS * N_SPLIT
PAIR = 2 * HEAD_DIM
N_PAIRS = N_HEADS // 2

ROW_TILE = 512
FF_TILE = 1024
N_STREAMS = 2
TQ = 256
TK = 256
NQ = 4
QG = NQ * TQ
KU = 2
N_STATS = 4
ZERO_EXP2 = 160.0
NORM_SLACK = 2.05
VMEM_LIMIT = 56 * 1024 * 1024

_BF16 = jnp.bfloat16
_F32 = jnp.float32


def _split_bf16(x):
    parts = []
    r = x
    for _ in range(N_SPLIT):
        p = r.astype(_BF16)
        parts.append(p)
        r = r - p.astype(_F32)
    return parts


def _dot(a, b):
    return jnp.dot(a, b, preferred_element_type=_F32)


def _dot_nt(a, b):
    return lax.dot_general(a, b, (((1,), (1,)), ((), ())), preferred_element_type=_F32)


def _inproj_kernel(x_ref, gpre_ref, wz_ref, wqt_ref, wk_ref, wvt_ref, wf_ref, wg_ref,
                   bf_ref, gsgu_ref, bsgu_ref, tri_ref, expand_ref, expand_t_ref,
                   head_rows_ref, piece_rows_ref,
                   u_ref, vn_ref, qt_ref, k_ref, vt_ref, gates_ref, qat_ref, ka_ref, stats_ref,
                   carry_ref):
    @pl.when(pl.program_id(1) == 0)
    def _():
        carry_ref[...] = jnp.zeros_like(carry_ref)
        stats_ref[...] = jnp.zeros_like(stats_ref)

    x = x_ref[0]
    tm = x.shape[0]
    ms = jnp.mean(x * x, axis=-1, keepdims=True)
    xb = ((x * lax.rsqrt(ms + EPS)) * gpre_ref[...]).astype(_BF16)

    lane = lax.broadcasted_iota(jnp.int32, (tm, LANES), 1)

    def pack_pieces(v):
        hi, mid, lo = _split_bf16(v)
        zero = jnp.zeros((), _BF16)
        return jnp.where(lane < N_HEADS, hi,
                         jnp.where(lane < 2 * N_HEADS, mid,
                                   jnp.where(lane < N_SPLIT * N_HEADS, lo, zero)))

    f = _dot(xb, wf_ref[...]) + bf_ref[...]

    z = jax.nn.gelu(_dot(xb, wz_ref[...]), approximate=True)
    u_ref[0] = z[:, :SGU_WIDTH].astype(_BF16)
    v = z[:, SGU_WIDTH:]
    mu = jnp.mean(v, axis=-1, keepdims=True)
    vc = v - mu
    var = jnp.mean(vc * vc, axis=-1, keepdims=True)
    vn_ref[0] = ((vc * lax.rsqrt(var + EPS)) * gsgu_ref[...] + bsgu_ref[...]).astype(_BF16)

    log_f = jnp.minimum(f, 0.0) - jnp.log(1.0 + jnp.exp(-jnp.abs(f)))
    sums = _dot(tri_ref[...], pack_pieces(log_f))

    qt = _dot_nt(wqt_ref[...], xb) * (HEAD_DIM ** -0.5 * LOG2E)
    qt_ref[0] = qt.astype(_BF16)
    k = _dot(xb, wk_ref[...])
    k_ref[0] = k.astype(_BF16)
    qn2 = _dot(head_rows_ref[...], (qt * qt).astype(_BF16))
    kn2 = _dot_nt(head_rows_ref[...], (k * k).astype(_BF16))

    total = sums
    for shift in (N_HEADS, 2 * N_HEADS, LANES - N_HEADS, LANES - 2 * N_HEADS):
        total = total + pltpu.roll(sums, shift, 1)
    cum = carry_ref[0:1, :] + total
    carry_ref[0:1, :] = cum[tm - 1:tm, :]
    pieces = pack_pieces(cum * LOG2E)
    spread = _dot(pieces, expand_ref[...])
    spread_t = _dot_nt(expand_t_ref[...], pieces)

    cum_t = _dot_nt(piece_rows_ref[...], pieces)
    tile_lane = lax.broadcasted_iota(jnp.int32, (2 * N_HEADS, LANES), 1)
    first_tile = pl.program_id(1) * (tm // TK)
    stats = [stats_ref[0, s] for s in range(N_STATS)]
    for j in range(tm // TK):
        lo, hi = j * TK, (j + 1) * TK
        cols = (cum_t[:, lo:lo + 1], cum_t[:, hi - 1:hi],
                jnp.max(qn2[:, lo:hi], axis=1, keepdims=True),
                jnp.max(kn2[:, lo:hi], axis=1, keepdims=True))
        stats = [jnp.where(tile_lane == first_tile + j, c, s) for c, s in zip(cols, stats)]
    for s in range(N_STATS):
        stats_ref[0, s] = stats[s]

    vt = _dot_nt(wvt_ref[...], xb).astype(_BF16)
    for h in range(N_HEADS):
        vt_ref[0, h, :HEAD_DIM, :] = vt[h * HEAD_DIM:(h + 1) * HEAD_DIM, :]
        vt_ref[0, h, HEAD_DIM:, :] = jnp.ones((PAIR - HEAD_DIM, tm), _BF16)

    gates_ref[0] = jax.nn.sigmoid(_dot(xb, wg_ref[...])).astype(_BF16)

    in_a = lane < BIAS_COLS
    in_b = jnp.logical_and(lane >= BIAS_COLS, lane < 2 * BIAS_COLS)
    ka_ref[0] = jnp.where(in_a, -spread, jnp.where(in_b, 1.0, 0.0)).astype(_BF16)
    row = lax.broadcasted_iota(jnp.int32, (LANES, tm), 0)
    in_a = row < BIAS_COLS
    in_b = jnp.logical_and(row >= BIAS_COLS, row < 2 * BIAS_COLS)
    qat_ref[0] = jnp.where(in_a, 1.0, jnp.where(in_b, spread_t, 0.0)).astype(_BF16)


def _head_rows():
    r = np.zeros((2 * N_HEADS, ATTN_WIDTH), np.float32)
    for h in range(N_HEADS):
        r[h, h * HEAD_DIM:(h + 1) * HEAD_DIM] = 1.0
    return r


def _piece_rows():
    r = np.zeros((2 * N_HEADS, LANES), np.float32)
    for h in range(N_HEADS):
        for i in range(N_SPLIT):
            r[h, N_HEADS * i + h] = 1.0
    return r


def _expand_matrix():
    e = np.zeros((LANES, LANES), np.float32)
    for h in range(N_HEADS):
        for i in range(N_SPLIT):
            e[N_HEADS * i + h, N_SPLIT * h + i] = 1.0
            e[N_HEADS * i + h, BIAS_COLS + N_SPLIT * h + i] = 1.0
    return e


def _in_projection(x, g_pre, w_in, b_forget, g_sgu, b_sgu):
    B, S, D = x.shape
    tm = ROW_TILE
    o = np.cumsum((0, 2 * SGU_WIDTH, ATTN_WIDTH, ATTN_WIDTH, ATTN_WIDTH, N_HEADS, 2 * D_MODEL))
    wz = w_in[:, o[0]:o[1]].astype(_BF16)
    wqt = w_in[:, o[1]:o[2]].T.astype(_BF16)
    wk = w_in[:, o[2]:o[3]].astype(_BF16)
    wvt = w_in[:, o[3]:o[4]].T.astype(_BF16)
    lane_pad = ((0, 0), (0, LANES - N_SPLIT * N_HEADS))
    wf = jnp.pad(jnp.tile(w_in[:, o[4]:o[5]], (1, N_SPLIT)), lane_pad).astype(_BF16)
    wg = w_in[:, o[5]:o[6]].astype(_BF16)
    bf = jnp.pad(jnp.tile(b_forget.reshape(1, N_HEADS), (1, N_SPLIT)), lane_pad)
    tri = jnp.asarray(np.tril(np.ones((tm, tm), np.float32)), _BF16)
    expand = jnp.asarray(_expand_matrix(), _BF16)
    expand_t = jnp.asarray(_expand_matrix().T, _BF16)
    head_rows = jnp.asarray(_head_rows(), _BF16)
    piece_rows = jnp.asarray(_piece_rows(), _BF16)

    def const(shape):
        return pl.BlockSpec(shape, lambda b, i: (0,) * len(shape))

    def rows(width):
        return pl.BlockSpec((1, tm, width), lambda b, i: (b, i, 0))

    out_shape = (
        jax.ShapeDtypeStruct((B, S, SGU_WIDTH), _BF16),
        jax.ShapeDtypeStruct((B, S, SGU_WIDTH), _BF16),
        jax.ShapeDtypeStruct((B, ATTN_WIDTH, S), _BF16),
        jax.ShapeDtypeStruct((B, S, ATTN_WIDTH), _BF16),
        jax.ShapeDtypeStruct((B, N_HEADS, PAIR, S), _BF16),
        jax.ShapeDtypeStruct((B, S, 2 * D_MODEL), _BF16),
        jax.ShapeDtypeStruct((B, LANES, S), _BF16),
        jax.ShapeDtypeStruct((B, S, LANES), _BF16),
        jax.ShapeDtypeStruct((B, N_STATS, 2 * N_HEADS, LANES), _F32),
    )

    def cols(height):
        return pl.BlockSpec((1, height, tm), lambda b, i: (b, 0, i))

    out_specs = (
        rows(SGU_WIDTH), rows(SGU_WIDTH), cols(ATTN_WIDTH), rows(ATTN_WIDTH),
        pl.BlockSpec((1, N_HEADS, PAIR, tm), lambda b, i: (b, 0, 0, i)),
        rows(2 * D_MODEL), cols(LANES), rows(LANES),
        pl.BlockSpec((1, N_STATS, 2 * N_HEADS, LANES), lambda b, i: (b, 0, 0, 0)),
    )
    return pl.pallas_call(
        _inproj_kernel,
        grid=(B, S // tm),
        in_specs=[
            rows(D), const((1, D)),
            const(wz.shape), const(wqt.shape), const(wk.shape), const(wvt.shape),
            const(wf.shape), const(wg.shape),
            const((1, LANES)), const((1, SGU_WIDTH)), const((1, SGU_WIDTH)),
            const(tri.shape), const(expand.shape), const(expand_t.shape),
            const(head_rows.shape), const(piece_rows.shape),
        ],
        out_specs=out_specs,
        out_shape=out_shape,
        scratch_shapes=[pltpu.VMEM((8, LANES), _F32)],
        compiler_params=pltpu.CompilerParams(
            dimension_semantics=("arbitrary", "arbitrary"),
            vmem_limit_bytes=VMEM_LIMIT),
        name="in_projection",
    )(x, g_pre.reshape(1, D), wz, wqt, wk, wvt, wf, wg, bf,
      g_sgu.reshape(1, SGU_WIDTH), b_sgu.reshape(1, SGU_WIDTH), tri, expand, expand_t,
      head_rows, piece_rows)


def _attn_kernel(nback_ref, k_ref, ka_ref, vt_ref, qt_ref, qat_ref, o_ref,
                 qf_ref, st_a, st_b, m_ref, acc_ref):
    batch = pl.program_id(0)
    pair = pl.program_id(1)
    S = k_ref.shape[1]
    row = lax.broadcasted_iota(jnp.int32, (PAIR, 1), 0)
    all_tiles = [(e, t) for t in range(NQ) for e in range(2)]

    def key_tile(k0):
        return jnp.concatenate([k_ref[0, pl.ds(k0, TK), :], ka_ref[0, pl.ds(k0, TK), :]], axis=1)

    def values(k0):
        return [vt_ref[0, e, :, pl.ds(k0, TK)] for e in range(2)]

    def scores(kf, e, t):
        return _dot(kf, qf_ref[e, :, t * TQ:(t + 1) * TQ])

    def update(e, t, st, vts, masked):
        cs = slice(t * TQ, (t + 1) * TQ)
        if masked:
            key_i = lax.broadcasted_iota(jnp.int32, (TK, TQ), 0)
            qry_i = lax.broadcasted_iota(jnp.int32, (TK, TQ), 1)
            st = jnp.where(key_i <= qry_i, st, -jnp.inf)
        m = m_ref[e, :, cs]
        m_new = jnp.maximum(m, jnp.max(st, axis=0, keepdims=True))
        p = jnp.exp2(st - m_new)
        alpha = jnp.exp2(m - m_new)
        m_ref[e, :, cs] = m_new
        acc_ref[e, :, cs] = alpha * acc_ref[e, :, cs] + _dot(vts[e], p.astype(_BF16))

    def step(cur, cur_tiles, vts, nxt, nxt_tiles, kf_next):
        for n in range(max(len(cur_tiles), len(nxt_tiles))):
            if n < len(nxt_tiles):
                e, t = nxt_tiles[n]
                nxt[e, t] = scores(kf_next, e, t)
            if n < len(cur_tiles):
                e, t, masked = cur_tiles[n]
                update(e, t, cur[e, t], vts, masked)

    def load_queries(g):
        q0 = pl.multiple_of(g * QG, QG)
        qt = qt_ref[0, :, pl.ds(q0, QG)].astype(_F32)
        qat = qat_ref[0, :, pl.ds(q0, QG)].astype(_F32)
        for e in range(2):
            a0 = N_SPLIT * (2 * pair + e)
            q_mask = jnp.logical_and(row >= HEAD_DIM * e, row < HEAD_DIM * (e + 1))
            a_mask = jnp.logical_or(
                jnp.logical_and(row >= a0, row < a0 + N_SPLIT),
                jnp.logical_and(row >= BIAS_COLS + a0, row < BIAS_COLS + a0 + N_SPLIT))
            qf_ref[e, :PAIR, :] = jnp.where(q_mask, qt, 0.0).astype(_BF16)
            qf_ref[e, PAIR:, :] = jnp.where(a_mask, qat, 0.0).astype(_BF16)

    def reset_state():
        m_ref[...] = jnp.full(m_ref.shape, -jnp.inf, _F32)
        acc_ref[...] = jnp.zeros(acc_ref.shape, _F32)

    full = [(e, t, False) for e, t in all_tiles]
    bufs = (st_a, st_b)
    n_groups = S // QG

    def first_key(g):
        return pl.multiple_of((g * NQ - nback_ref[batch, pair, g]) * TK, TK)

    load_queries(0)
    reset_state()
    step(None, [], None, st_a, all_tiles, key_tile(first_key(0)))

    def q_group(g, _):
        q0 = pl.multiple_of(g * QG, QG)
        k_first = first_key(g)

        def key_tiles(j, _):
            k0 = pl.multiple_of(k_first + j * KU * TK, TK)
            for u in range(KU):
                step(bufs[u % 2], full, values(k0 + u * TK), bufs[(u + 1) % 2], all_tiles,
                     key_tile(k0 + (u + 1) * TK))
            return 0

        lax.fori_loop(0, nback_ref[batch, pair, g] // KU, key_tiles, 0)

        for i in range(NQ):
            cur_tiles = [(e, t, t == i) for t in range(i, NQ) for e in range(2)]
            if i + 1 < NQ:
                nxt_tiles = [(e, t) for t in range(i + 1, NQ) for e in range(2)]
                kf_next = key_tile(q0 + (i + 1) * TK)
            else:
                g_next = jnp.minimum(g + 1, n_groups - 1)
                load_queries(g_next)
                nxt_tiles, kf_next = all_tiles, key_tile(first_key(g_next))
            step(bufs[i % 2], cur_tiles, values(q0 + i * TK), bufs[(i + 1) % 2], nxt_tiles, kf_next)

        out = [acc_ref[e, :HEAD_DIM, :] * (1.0 / acc_ref[e, HEAD_DIM:HEAD_DIM + 1, :])
               for e in range(2)]
        o_ref[0, pl.ds(q0, QG), :] = jnp.concatenate(out, axis=0).T.astype(_BF16)
        reset_state()
        return 0

    lax.fori_loop(0, n_groups, q_group, 0)


def _tiles_to_visit(stats, n_tiles):
    cum_first, cum_last, qn2, kn2 = (stats[:, s, :N_HEADS, :n_tiles] for s in range(N_STATS))
    B = stats.shape[0]
    n_groups = n_tiles // NQ
    q_norm = jnp.sqrt(jnp.max(qn2.reshape(B, N_HEADS, n_groups, NQ), axis=-1))
    k_norm = jnp.sqrt(jnp.max(kn2, axis=-1, keepdims=True))
    cum_group = cum_first[:, :, ::NQ]
    bound = (NORM_SLACK * q_norm * k_norm + cum_group)[..., None] - cum_last[:, :, None, :]
    tile = jnp.arange(n_tiles)[None, None, None, :]
    group_start = (jnp.arange(n_groups) * NQ)[None, None, :, None]
    visit = jnp.logical_and(tile < group_start, bound >= -ZERO_EXP2)
    n = jnp.sum(visit, axis=-1).astype(jnp.int32)
    n = jnp.max(n.reshape(B, N_PAIRS, 2, n_groups), axis=2)
    n = ((n + KU - 1) // KU) * KU
    return jnp.minimum(n, (jnp.arange(n_groups) * NQ)[None, None, :]).astype(jnp.int32)


def _attention(k, ka, vt, qt, qat, stats):
    B, S, _ = k.shape
    assert TQ == TK and KU % 2 == 0 and NQ % KU == 0 and S % QG == 0
    nback = _tiles_to_visit(stats, S // TK)
    pair_rows = pl.BlockSpec((1, S, PAIR), lambda b, j, nb: (b, 0, j))
    return pl.pallas_call(
        _attn_kernel,
        grid_spec=pltpu.PrefetchScalarGridSpec(
            num_scalar_prefetch=1,
            grid=(B, N_PAIRS),
            in_specs=[pair_rows, pl.BlockSpec((1, S, LANES), lambda b, j, nb: (b, 0, 0)),
                      pl.BlockSpec((1, 2, PAIR, S), lambda b, j, nb: (b, j, 0, 0)),
                      pl.BlockSpec((1, PAIR, S), lambda b, j, nb: (b, j, 0)),
                      pl.BlockSpec((1, LANES, S), lambda b, j, nb: (b, 0, 0))],
            out_specs=pair_rows,
            scratch_shapes=[pltpu.VMEM((2, PAIR + LANES, QG), _BF16),
                            pltpu.VMEM((2, NQ, TK, TQ), _F32), pltpu.VMEM((2, NQ, TK, TQ), _F32),
                            pltpu.VMEM((2, 1, QG), _F32), pltpu.VMEM((2, PAIR, QG), _F32)]),
        out_shape=jax.ShapeDtypeStruct((B, S, ATTN_WIDTH), _BF16),
        compiler_params=pltpu.CompilerParams(
            dimension_semantics=("parallel", "parallel"),
            vmem_limit_bytes=VMEM_LIMIT),
        name="forgetting_attention",
    )(nback, k, ka, vt, qt, qat)


def _rms_scale(v, g):
    ms = jnp.mean(v * v, axis=-1, keepdims=True)
    return (v * lax.rsqrt(ms + EPS)) * g


def _merge_mlp_kernel(x_ref, u_ref, vn_ref, ya_ref, gates_ref, wsp_ref, bsp_ref,
                      wbs_ref, wba_ref, wout_ref, gpost_ref,
                      gpre2_ref, wup_ref, wdown_ref, gpost2_ref, o_ref, ysgu_ref):
    tm = x_ref.shape[0]
    row = lax.broadcasted_iota(jnp.int32, (CHUNK, CHUNK), 0)
    col = lax.broadcasted_iota(jnp.int32, (CHUNK, CHUNK), 1)
    lane = lax.broadcasted_iota(jnp.int32, (CHUNK, PAIR), 1)
    ws = [jnp.where(row >= col, wsp_ref[g], 0.0).astype(_BF16) for g in range(N_GROUPS)]
    for c in range(tm // CHUNK):
        r = slice(c * CHUNK, (c + 1) * CHUNK)
        for j in range(N_GROUPS // 2):
            cs = slice(j * PAIR, (j + 1) * PAIR)
            vp = vn_ref[r, cs]
            s = jnp.where(lane < HEAD_DIM, _dot(ws[2 * j], vp), _dot(ws[2 * j + 1], vp))
            s = s + bsp_ref[:, cs]
            ysgu_ref[r, cs] = (u_ref[r, cs].astype(_F32) * s).astype(_BF16)

    blocks = [slice(n * tm // N_STREAMS, (n + 1) * tm // N_STREAMS) for n in range(N_STREAMS)]

    def mix(r):
        a = _dot(ysgu_ref[r, :], wbs_ref[...])
        b = _dot(ya_ref[r, :], wba_ref[...])
        merged = (gates_ref[r, :D_MODEL].astype(_F32) * a
                  + gates_ref[r, D_MODEL:].astype(_F32) * b).astype(_BF16)
        return _dot(merged, wout_ref[...])

    def mlp(xb):
        acc = jnp.zeros(xb.shape, _F32)
        for c in range(D_FF // FF_TILE):
            cs = slice(c * FF_TILE, (c + 1) * FF_TILE)
            hid = jnp.square(jnp.maximum(_dot(xb, wup_ref[:, cs]), 0.0)).astype(_BF16)
            acc = acc + _dot(hid, wdown_ref[cs, :])
        return acc

    mixed = [mix(r) for r in blocks]
    h1 = [x_ref[r, :] + _rms_scale(o, gpost_ref[...]) for r, o in zip(blocks, mixed)]
    xb = [_rms_scale(h, gpre2_ref[...]).astype(_BF16) for h in h1]
    ff = [mlp(v) for v in xb]
    for r, h, f in zip(blocks, h1, ff):
        o_ref[r, :] = h + _rms_scale(f, gpost2_ref[...])


def _merge_mlp(x2, u2, vn2, ya2, gates2, w_spatial, b_spatial, w_bs, w_ba, w_out, g_post,
               g_pre2, w_up, w_down, g_post2):
    R, D = x2.shape
    tm = ROW_TILE
    bsp = jnp.repeat(b_spatial.T, SGU_WIDTH // N_GROUPS, axis=1)

    def const(shape):
        return pl.BlockSpec(shape, lambda i: (0,) * len(shape), pipeline_mode=pl.Buffered(1))

    def rows(width):
        return pl.BlockSpec((tm, width), lambda i: (i, 0))

    return pl.pallas_call(
        _merge_mlp_kernel,
        grid=(R // tm,),
        in_specs=[rows(D), rows(SGU_WIDTH), rows(SGU_WIDTH), rows(ATTN_WIDTH), rows(2 * D),
                  const(w_spatial.shape), const(bsp.shape),
                  const(w_bs.shape), const(w_ba.shape), const(w_out.shape), const((1, D)),
                  const((1, D)), const(w_up.shape), const(w_down.shape), const((1, D))],
        out_specs=rows(D),
        out_shape=jax.ShapeDtypeStruct((R, D), _F32),
        scratch_shapes=[pltpu.VMEM((tm, SGU_WIDTH), _BF16)],
        compiler_params=pltpu.CompilerParams(
            dimension_semantics=("parallel",), vmem_limit_bytes=VMEM_LIMIT),
        name="merge_mlp",
    )(x2, u2, vn2, ya2, gates2, w_spatial, bsp,
      w_bs.astype(_BF16), w_ba.astype(_BF16), w_out.astype(_BF16), g_post.reshape(1, D),
      g_pre2.reshape(1, D), w_up.astype(_BF16), w_down.astype(_BF16), g_post2.reshape(1, D))


def kernel(x, g_mix_pre, w_in, b_forget, g_sgu, b_sgu, w_spatial, b_spatial, w_branch_sgu,
           w_branch_attn, w_out, g_mix_post, g_ffn_pre, w_up, w_down, g_ffn_post):
    B, S, D = x.shape
    h = x
    for l in range(g_mix_pre.shape[0]):
        u, vn, qt, k, vt, gates, qat, ka, stats = _in_projection(
            h, g_mix_pre[l], w_in[l], b_forget[l], g_sgu[l], b_sgu[l])
        y_attn = _attention(k, ka, vt, qt, qat, stats)
        h = _merge_mlp(h.reshape(B * S, D), u.reshape(B * S, -1), vn.reshape(B * S, -1),
                       y_attn.reshape(B * S, -1), gates.reshape(B * S, -1),
                       w_spatial[l], b_spatial[l], w_branch_sgu[l], w_branch_attn[l], w_out[l],
                       g_mix_post[l], g_ffn_pre[l], w_up[l], w_down[l],
                       g_ffn_post[l]).reshape(B, S, D)
    return h
```

```python
import functools

import jax
import jax.numpy as jnp
import numpy as np
from jax import lax
from jax.experimental import pallas as pl
from jax.experimental.pallas import tpu as pltpu

D_MODEL = 1024
N_HEADS = 8
HEAD_DIM = 64
ATTN_WIDTH = N_HEADS * HEAD_DIM
N_GROUPS = 8
SGU_WIDTH = D_MODEL // 2
CHUNK = 128
D_FF = 4 * D_MODEL
EPS = 1e-6
LOG2E = 1.4426950408889634

LANES = 128
N_SPLIT = 3
BIAS_COLS = N_HEADS * N_SPLIT
PAIR = 2 * HEAD_DIM
N_PAIRS = N_HEADS // 2

ROW_TILE = 512
FF_TILE = 1024
N_STREAMS = 2
TQ = 256
TK = 256
NQ = 4
QG = NQ * TQ
KU = 2
N_STATS = 4
ZERO_EXP2 = 160.0
NORM_SLACK = 2.05
VMEM_LIMIT = 56 * 1024 * 1024

_BF16 = jnp.bfloat16
_F32 = jnp.float32


def _split_bf16(x):
    parts = []
    r = x
    for _ in range(N_SPLIT):
        p = r.astype(_BF16)
        parts.append(p)
        r = r - p.astype(_F32)
    return parts


def _dot(a, b):
    return jnp.dot(a, b, preferred_element_type=_F32)


def _dot_nt(a, b):
    return lax.dot_general(a, b, (((1,), (1,)), ((), ())), preferred_element_type=_F32)


def _inproj_kernel(x_ref, gpre_ref, wz_ref, wqt_ref, wk_ref, wvt_ref, wf_ref, wg_ref,
                   bf_ref, gsgu_ref, bsgu_ref, tri_ref, expand_ref, expand_t_ref,
                   head_rows_ref, piece_rows_ref,
                   u_ref, vn_ref, qt_ref, k_ref, vt_ref, gates_ref, qat_ref, ka_ref, stats_ref,
                   carry_ref):
    @pl.when(pl.program_id(1) == 0)
    def _():
        carry_ref[...] = jnp.zeros_like(carry_ref)
        stats_ref[...] = jnp.zeros_like(stats_ref)

    x = x_ref[0]
    tm = x.shape[0]
    ms = jnp.mean(x * x, axis=-1, keepdims=True)
    xb = ((x * lax.rsqrt(ms + EPS)) * gpre_ref[...]).astype(_BF16)

    lane = lax.broadcasted_iota(jnp.int32, (tm, LANES), 1)

    def pack_pieces(v):
        hi, mid, lo = _split_bf16(v)
        zero = jnp.zeros((), _BF16)
        return jnp.where(lane < N_HEADS, hi,
                         jnp.where(lane < 2 * N_HEADS, mid,
                                   jnp.where(lane < N_SPLIT * N_HEADS, lo, zero)))

    f = _dot(xb, wf_ref[...]) + bf_ref[...]

    z = jax.nn.gelu(_dot(xb, wz_ref[...]), approximate=True)
    u_ref[0] = z[:, :SGU_WIDTH].astype(_BF16)
    v = z[:, SGU_WIDTH:]
    mu = jnp.mean(v, axis=-1, keepdims=True)
    vc = v - mu
    var = jnp.mean(vc * vc, axis=-1, keepdims=True)
    vn_ref[0] = ((vc * lax.rsqrt(var + EPS)) * gsgu_ref[...] + bsgu_ref[...]).astype(_BF16)

    log_f = jnp.minimum(f, 0.0) - jnp.log(1.0 + jnp.exp(-jnp.abs(f)))
    sums = _dot(tri_ref[...], pack_pieces(log_f))

    qt = _dot_nt(wqt_ref[...], xb) * (HEAD_DIM ** -0.5 * LOG2E)
    qt_ref[0] = qt.astype(_BF16)
    k = _dot(xb, wk_ref[...])
    k_ref[0] = k.astype(_BF16)
    qn2 = _dot(head_rows_ref[...], (qt * qt).astype(_BF16))
    kn2 = _dot_nt(head_rows_ref[...], (k * k).astype(_BF16))

    total = sums
    for shift in (N_HEADS, 2 * N_HEADS, LANES - N_HEADS, LANES - 2 * N_HEADS):
        total = total + pltpu.roll(sums, shift, 1)
    cum = carry_ref[0:1, :] + total
    carry_ref[0:1, :] = cum[tm - 1:tm, :]
    pieces = pack_pieces(cum * LOG2E)
    spread = _dot(pieces, expand_ref[...])
    spread_t = _dot_nt(expand_t_ref[...], pieces)

    cum_t = _dot_nt(piece_rows_ref[...], pieces)
    tile_lane = lax.broadcasted_iota(jnp.int32, (2 * N_HEADS, LANES), 1)
    first_tile = pl.program_id(1) * (tm // TK)
    stats = [stats_ref[0, s] for s in range(N_STATS)]
    for j in range(tm // TK):
        lo, hi = j * TK, (j + 1) * TK
        cols = (cum_t[:, lo:lo + 1], cum_t[:, hi - 1:hi],
                jnp.max(qn2[:, lo:hi], axis=1, keepdims=True),
                jnp.max(kn2[:, lo:hi], axis=1, keepdims=True))
        stats = [jnp.where(tile_lane == first_tile + j, c, s) for c, s in zip(cols, stats)]
    for s in range(N_STATS):
        stats_ref[0, s] = stats[s]

    vt = _dot_nt(wvt_ref[...], xb).astype(_BF16)
    for h in range(N_HEADS):
        vt_ref[0, h, :HEAD_DIM, :] = vt[h * HEAD_DIM:(h + 1) * HEAD_DIM, :]
        vt_ref[0, h, HEAD_DIM:, :] = jnp.ones((PAIR - HEAD_DIM, tm), _BF16)

    gates_ref[0] = jax.nn.sigmoid(_dot(xb, wg_ref[...])).astype(_BF16)

    in_a = lane < BIAS_COLS
    in_b = jnp.logical_and(lane >= BIAS_COLS, lane < 2 * BIAS_COLS)
    ka_ref[0] = jnp.where(in_a, -spread, jnp.where(in_b, 1.0, 0.0)).astype(_BF16)
    row = lax.broadcasted_iota(jnp.int32, (LANES, tm), 0)
    in_a = row < BIAS_COLS
    in_b = jnp.logical_and(row >= BIAS_COLS, row < 2 * BIAS_COLS)
    qat_ref[0] = jnp.where(in_a, 1.0, jnp.where(in_b, spread_t, 0.0)).astype(_BF16)


def _head_rows():
    r = np.zeros((2 * N_HEADS, ATTN_WIDTH), np.float32)
    for h in range(N_HEADS):
        r[h, h * HEAD_DIM:(h + 1) * HEAD_DIM] = 1.0
    return r


def _piece_rows():
    r = np.zeros((2 * N_HEADS, LANES), np.float32)
    for h in range(N_HEADS):
        for i in range(N_SPLIT):
            r[h, N_HEADS * i + h] = 1.0
    return r


def _expand_matrix():
    e = np.zeros((LANES, LANES), np.float32)
    for h in range(N_HEADS):
        for i in range(N_SPLIT):
            e[N_HEADS * i + h, N_SPLIT * h + i] = 1.0
            e[N_HEADS * i + h, BIAS_COLS + N_SPLIT * h + i] = 1.0
    return e


def _in_projection(x, g_pre, w_in, b_forget, g_sgu, b_sgu):
    B, S, D = x.shape
    tm = ROW_TILE
    o = np.cumsum((0, 2 * SGU_WIDTH, ATTN_WIDTH, ATTN_WIDTH, ATTN_WIDTH, N_HEADS, 2 * D_MODEL))
    wz = w_in[:, o[0]:o[1]].astype(_BF16)
    wqt = w_in[:, o[1]:o[2]].T.astype(_BF16)
    wk = w_in[:, o[2]:o[3]].astype(_BF16)
    wvt = w_in[:, o[3]:o[4]].T.astype(_BF16)
    lane_pad = ((0, 0), (0, LANES - N_SPLIT * N_HEADS))
    wf = jnp.pad(jnp.tile(w_in[:, o[4]:o[5]], (1, N_SPLIT)), lane_pad).astype(_BF16)
    wg = w_in[:, o[5]:o[6]].astype(_BF16)
    bf = jnp.pad(jnp.tile(b_forget.reshape(1, N_HEADS), (1, N_SPLIT)), lane_pad)
    tri = jnp.asarray(np.tril(np.ones((tm, tm), np.float32)), _BF16)
    expand = jnp.asarray(_expand_matrix(), _BF16)
    expand_t = jnp.asarray(_expand_matrix().T, _BF16)
    head_rows = jnp.asarray(_head_rows(), _BF16)
    piece_rows = jnp.asarray(_piece_rows(), _BF16)

    def const(shape):
        return pl.BlockSpec(shape, lambda b, i: (0,) * len(shape))

    def rows(width):
        return pl.BlockSpec((1, tm, width), lambda b, i: (b, i, 0))

    out_shape = (
        jax.ShapeDtypeStruct((B, S, SGU_WIDTH), _BF16),
        jax.ShapeDtypeStruct((B, S, SGU_WIDTH), _BF16),
        jax.ShapeDtypeStruct((B, ATTN_WIDTH, S), _BF16),
        jax.ShapeDtypeStruct((B, S, ATTN_WIDTH), _BF16),
        jax.ShapeDtypeStruct((B, N_HEADS, PAIR, S), _BF16),
        jax.ShapeDtypeStruct((B, S, 2 * D_MODEL), _BF16),
        jax.ShapeDtypeStruct((B, LANES, S), _BF16),
        jax.ShapeDtypeStruct((B, S, LANES), _BF16),
        jax.ShapeDtypeStruct((B, N_STATS, 2 * N_HEADS, LANES), _F32),
    )

    def cols(height):
        return pl.BlockSpec((1, height, tm), lambda b, i: (b, 0, i))

    out_specs = (
        rows(SGU_WIDTH), rows(SGU_WIDTH), cols(ATTN_WIDTH), rows(ATTN_WIDTH),
        pl.BlockSpec((1, N_HEADS, PAIR, tm), lambda b, i: (b, 0, 0, i)),
        rows(2 * D_MODEL), cols(LANES), rows(LANES),
        pl.BlockSpec((1, N_STATS, 2 * N_HEADS, LANES), lambda b, i: (b, 0, 0, 0)),
    )
    return pl.pallas_call(
        _inproj_kernel,
        grid=(B, S // tm),
        in_specs=[
            rows(D), const((1, D)),
            const(wz.shape), const(wqt.shape), const(wk.shape), const(wvt.shape),
            const(wf.shape), const(wg.shape),
            const((1, LANES)), const((1, SGU_WIDTH)), const((1, SGU_WIDTH)),
            const(tri.shape), const(expand.shape), const(expand_t.shape),
            const(head_rows.shape), const(piece_rows.shape),
        ],
        out_specs=out_specs,
        out_shape=out_shape,
        scratch_shapes=[pltpu.VMEM((8, LANES), _F32)],
        compiler_params=pltpu.CompilerParams(
            dimension_semantics=("arbitrary", "arbitrary"),
            vmem_limit_bytes=VMEM_LIMIT),
        name="in_projection",
    )(x, g_pre.reshape(1, D), wz, wqt, wk, wvt, wf, wg, bf,
      g_sgu.reshape(1, SGU_WIDTH), b_sgu.reshape(1, SGU_WIDTH), tri, expand, expand_t,
      head_rows, piece_rows)


def _attn_kernel(plan_ref, k_ref, ka_ref, vt_ref, qt_ref, qat_ref, o_ref,
                 qf_ref, st_a, st_b, m_ref, acc_ref):
    batch = pl.program_id(0)
    pair = pl.program_id(1)
    S = k_ref.shape[1]
    row = lax.broadcasted_iota(jnp.int32, (PAIR, 1), 0)
    all_tiles = [(e, t) for t in range(NQ) for e in range(2)]

    def key_tile(k0):
        return jnp.concatenate([k_ref[0, pl.ds(k0, TK), :], ka_ref[0, pl.ds(k0, TK), :]], axis=1)

    def offsets(g, j):
        n, skew = plan_ref[batch, pair, g, 0], plan_ref[batch, pair, g, 1]
        first = (g * NQ - n - skew + j) * TK
        return [pl.multiple_of(first + t * skew * TK, TK) for t in range(NQ)]

    def update(e, t, st, k0, masked):
        cs = slice(t * TQ, (t + 1) * TQ)
        if masked:
            key_i = lax.broadcasted_iota(jnp.int32, (TK, TQ), 0)
            qry_i = lax.broadcasted_iota(jnp.int32, (TK, TQ), 1)
            st = jnp.where(key_i <= qry_i, st, -jnp.inf)
        m = m_ref[e, :, cs]
        m_new = jnp.maximum(m, jnp.max(st, axis=0, keepdims=True))
        p = jnp.exp2(st - m_new)
        alpha = jnp.exp2(m - m_new)
        m_ref[e, :, cs] = m_new
        acc_ref[e, :, cs] = alpha * acc_ref[e, :, cs] + _dot(
            vt_ref[0, e, :, pl.ds(k0, TK)], p.astype(_BF16))

    def step(cur, cur_tiles, cur_offs, nxt, nxt_tiles, nxt_offs):
        for n in range(max(len(cur_tiles), len(nxt_tiles))):
            if n < len(nxt_tiles):
                e, t = nxt_tiles[n]
                nxt[e, t] = _dot(key_tile(nxt_offs[t]), qf_ref[e, :, t * TQ:(t + 1) * TQ])
            if n < len(cur_tiles):
                e, t, masked = cur_tiles[n]
                update(e, t, cur[e, t], cur_offs[t], masked)

    def load_queries(g):
        q0 = pl.multiple_of(g * QG, QG)
        qt = qt_ref[0, :, pl.ds(q0, QG)].astype(_F32)
        qat = qat_ref[0, :, pl.ds(q0, QG)].astype(_F32)
        for e in range(2):
            a0 = N_SPLIT * (2 * pair + e)
            q_mask = jnp.logical_and(row >= HEAD_DIM * e, row < HEAD_DIM * (e + 1))
            a_mask = jnp.logical_or(
                jnp.logical_and(row >= a0, row < a0 + N_SPLIT),
                jnp.logical_and(row >= BIAS_COLS + a0, row < BIAS_COLS + a0 + N_SPLIT))
            qf_ref[e, :PAIR, :] = jnp.where(q_mask, qt, 0.0).astype(_BF16)
            qf_ref[e, PAIR:, :] = jnp.where(a_mask, qat, 0.0).astype(_BF16)

    def reset_state():
        m_ref[...] = jnp.full(m_ref.shape, -jnp.inf, _F32)
        acc_ref[...] = jnp.zeros(acc_ref.shape, _F32)

    full = [(e, t, False) for e, t in all_tiles]
    bufs = (st_a, st_b)
    n_groups = S // QG

    load_queries(0)
    reset_state()
    step(None, [], None, st_a, all_tiles, offsets(0, 0))

    def q_group(g, _):
        q0 = pl.multiple_of(g * QG, QG)
        n = plan_ref[batch, pair, g, 0]
        skew = plan_ref[batch, pair, g, 1]
        g_next = jnp.minimum(g + 1, n_groups - 1)

        def full_steps(i, _):
            for u in range(KU):
                j = i * KU + u
                step(bufs[u % 2], full, offsets(g, j), bufs[(u + 1) % 2], all_tiles,
                     offsets(g, j + 1))
            return 0

        lax.fori_loop(0, n // KU, full_steps, 0)

        @pl.when(skew == 1)
        def _():
            step(st_a, full, offsets(g, n), st_b, all_tiles, offsets(g, n + 1))
            load_queries(g_next)
            step(st_b, [(e, t, True) for e, t in all_tiles], offsets(g, n + 1),
                 st_a, all_tiles, offsets(g_next, 0))

        @pl.when(skew == 0)
        def _():
            for i in range(NQ):
                cur_tiles = [(e, t, t == i) for t in range(i, NQ) for e in range(2)]
                cur_offs = [q0 + i * TK] * NQ
                if i + 1 < NQ:
                    nxt_tiles = [(e, t) for t in range(i + 1, NQ) for e in range(2)]
                    nxt_offs = [q0 + (i + 1) * TK] * NQ
                else:
                    load_queries(g_next)
                    nxt_tiles, nxt_offs = all_tiles, offsets(g_next, 0)
                step(bufs[i % 2], cur_tiles, cur_offs, bufs[(i + 1) % 2], nxt_tiles, nxt_offs)

        out = [acc_ref[e, :HEAD_DIM, :] * (1.0 / acc_ref[e, HEAD_DIM:HEAD_DIM + 1, :])
               for e in range(2)]
        o_ref[0, pl.ds(q0, QG), :] = jnp.concatenate(out, axis=0).T.astype(_BF16)
        reset_state()
        return 0

    lax.fori_loop(0, n_groups, q_group, 0)


def _sweep_plan(stats, n_tiles):
    cum_first, cum_last, qn2, kn2 = (stats[:, s, :N_HEADS, :n_tiles] for s in range(N_STATS))
    B = stats.shape[0]
    n_groups = n_tiles // NQ
    k_norm = jnp.sqrt(jnp.max(kn2, axis=-1, keepdims=True))
    reach = NORM_SLACK * jnp.sqrt(qn2) * k_norm + cum_first
    bound = reach[..., None] - cum_last[:, :, None, :]
    tile = jnp.arange(n_tiles)
    needed = jnp.logical_and(tile[None, :] < tile[:, None], bound >= -ZERO_EXP2)
    w = jnp.sum(needed, axis=-1).astype(jnp.int32)
    w = jnp.max(w.reshape(B, N_PAIRS, 2, n_groups, NQ), axis=2)
    start = jnp.arange(n_groups, dtype=jnp.int32) * NQ
    n_flat = jnp.max(jnp.maximum(w - jnp.arange(NQ, dtype=jnp.int32), 0), axis=-1)
    n_flat = jnp.minimum(((n_flat + KU - 1) // KU) * KU, start)
    n_skew = jnp.maximum(jnp.max(w, axis=-1), 1)
    n_skew = n_skew + (n_skew + 1) % 2
    units_flat = NQ * n_flat + NQ * (NQ + 1) // 2
    units_skew = NQ * n_skew + NQ
    skew = jnp.logical_and(n_skew < start, units_skew < units_flat)
    n = jnp.where(skew, n_skew - 1, n_flat)
    return jnp.stack([n, skew.astype(jnp.int32)], axis=-1)


def _attention(k, ka, vt, qt, qat, stats):
    B, S, _ = k.shape
    assert TQ == TK and KU == 2 and NQ % KU == 0 and S % QG == 0
    plan = _sweep_plan(stats, S // TK)
    pair_rows = pl.BlockSpec((1, S, PAIR), lambda b, j, nb: (b, 0, j))
    return pl.pallas_call(
        _attn_kernel,
        grid_spec=pltpu.PrefetchScalarGridSpec(
            num_scalar_prefetch=1,
            grid=(B, N_PAIRS),
            in_specs=[pair_rows, pl.BlockSpec((1, S, LANES), lambda b, j, nb: (b, 0, 0)),
                      pl.BlockSpec((1, 2, PAIR, S), lambda b, j, nb: (b, j, 0, 0)),
                      pl.BlockSpec((1, PAIR, S), lambda b, j, nb: (b, j, 0)),
                      pl.BlockSpec((1, LANES, S), lambda b, j, nb: (b, 0, 0))],
            out_specs=pair_rows,
            scratch_shapes=[pltpu.VMEM((2, PAIR + LANES, QG), _BF16),
                            pltpu.VMEM((2, NQ, TK, TQ), _F32), pltpu.VMEM((2, NQ, TK, TQ), _F32),
                            pltpu.VMEM((2, 1, QG), _F32), pltpu.VMEM((2, PAIR, QG), _F32)]),
        out_shape=jax.ShapeDtypeStruct((B, S, ATTN_WIDTH), _BF16),
        compiler_params=pltpu.CompilerParams(
            dimension_semantics=("parallel", "parallel"),
            vmem_limit_bytes=VMEM_LIMIT),
        name="forgetting_attention",
    )(plan, k, ka, vt, qt, qat)


def _rms_scale(v, g):
    ms = jnp.mean(v * v, axis=-1, keepdims=True)
    return (v * lax.rsqrt(ms + EPS)) * g


def _merge_mlp_kernel(x_ref, u_ref, vn_ref, ya_ref, gates_ref, wsp_ref, bsp_ref,
                      wbs_ref, wba_ref, wout_ref, gpost_ref,
                      gpre2_ref, wup_ref, wdown_ref, gpost2_ref, o_ref, ysgu_ref):
    tm = x_ref.shape[0]
    row = lax.broadcasted_iota(jnp.int32, (CHUNK, CHUNK), 0)
    col = lax.broadcasted_iota(jnp.int32, (CHUNK, CHUNK), 1)
    lane = lax.broadcasted_iota(jnp.int32, (CHUNK, PAIR), 1)
    ws = [jnp.where(row >= col, wsp_ref[g], 0.0).astype(_BF16) for g in range(N_GROUPS)]
    for c in range(tm // CHUNK):
        r = slice(c * CHUNK, (c + 1) * CHUNK)
        for j in range(N_GROUPS // 2):
            cs = slice(j * PAIR, (j + 1) * PAIR)
            vp = vn_ref[r, cs]
            s = jnp.where(lane < HEAD_DIM, _dot(ws[2 * j], vp), _dot(ws[2 * j + 1], vp))
            s = s + bsp_ref[:, cs]
            ysgu_ref[r, cs] = (u_ref[r, cs].astype(_F32) * s).astype(_BF16)

    blocks = [slice(n * tm // N_STREAMS, (n + 1) * tm // N_STREAMS) for n in range(N_STREAMS)]

    def mix(r):
        a = _dot(ysgu_ref[r, :], wbs_ref[...])
        b = _dot(ya_ref[r, :], wba_ref[...])
        merged = (gates_ref[r, :D_MODEL].astype(_F32) * a
                  + gates_ref[r, D_MODEL:].astype(_F32) * b).astype(_BF16)
        return _dot(merged, wout_ref[...])

    def mlp(xb):
        acc = jnp.zeros(xb.shape, _F32)
        for c in range(D_FF // FF_TILE):
            cs = slice(c * FF_TILE, (c + 1) * FF_TILE)
            hid = jnp.square(jnp.maximum(_dot(xb, wup_ref[:, cs]), 0.0)).astype(_BF16)
            acc = acc + _dot(hid, wdown_ref[cs, :])
        return acc

    mixed = [mix(r) for r in blocks]
    h1 = [x_ref[r, :] + _rms_scale(o, gpost_ref[...]) for r, o in zip(blocks, mixed)]
    xb = [_rms_scale(h, gpre2_ref[...]).astype(_BF16) for h in h1]
    ff = [mlp(v) for v in xb]
    for r, h, f in zip(blocks, h1, ff):
        o_ref[r, :] = h + _rms_scale(f, gpost2_ref[...])


def _merge_mlp(x2, u2, vn2, ya2, gates2, w_spatial, b_spatial, w_bs, w_ba, w_out, g_post,
               g_pre2, w_up, w_down, g_post2):
    R, D = x2.shape
    tm = ROW_TILE
    bsp = jnp.repeat(b_spatial.T, SGU_WIDTH // N_GROUPS, axis=1)

    def const(shape):
        return pl.BlockSpec(shape, lambda i: (0,) * len(shape), pipeline_mode=pl.Buffered(1))

    def rows(width):
        return pl.BlockSpec((tm, width), lambda i: (i, 0))

    return pl.pallas_call(
        _merge_mlp_kernel,
        grid=(R // tm,),
        in_specs=[rows(D), rows(SGU_WIDTH), rows(SGU_WIDTH), rows(ATTN_WIDTH), rows(2 * D),
                  const(w_spatial.shape), const(bsp.shape),
                  const(w_bs.shape), const(w_ba.shape), const(w_out.shape), const((1, D)),
                  const((1, D)), const(w_up.shape), const(w_down.shape), const((1, D))],
        out_specs=rows(D),
        out_shape=jax.ShapeDtypeStruct((R, D), _F32),
        scratch_shapes=[pltpu.VMEM((tm, SGU_WIDTH), _BF16)],
        compiler_params=pltpu.CompilerParams(
            dimension_semantics=("parallel",), vmem_limit_bytes=VMEM_LIMIT),
        name="merge_mlp",
    )(x2, u2, vn2, ya2, gates2, w_spatial, bsp,
      w_bs.astype(_BF16), w_ba.astype(_BF16), w_out.astype(_BF16), g_post.reshape(1, D),
      g_pre2.reshape(1, D), w_up.astype(_BF16), w_down.astype(_BF16), g_post2.reshape(1, D))


def kernel(x, g_mix_pre, w_in, b_forget, g_sgu, b_sgu, w_spatial, b_spatial, w_branch_sgu,
           w_branch_attn, w_out, g_mix_post, g_ffn_pre, w_up, w_down, g_ffn_post):
    B, S, D = x.shape
    h = x
    for l in range(g_mix_pre.shape[0]):
        u, vn, qt, k, vt, gates, qat, ka, stats = _in_projection(
            h, g_mix_pre[l], w_in[l], b_forget[l], g_sgu[l], b_sgu[l])
        y_attn = _attention(k, ka, vt, qt, qat, stats)
        h = _merge_mlp(h.reshape(B * S, D), u.reshape(B * S, -1), vn.reshape(B * S, -1),
                       y_attn.reshape(B * S, -1), gates.reshape(B * S, -1),
                       w_spatial[l], b_spatial[l], w_branch_sgu[l], w_branch_attn[l], w_out[l],
                       g_mix_post[l], g_ffn_pre[l], w_up[l], w_down[l],
                       g_ffn_post[l]).reshape(B, S, D)
    return h
```

```python
import functools

import jax
import jax.numpy as jnp
import numpy as np
from jax import lax
from jax.experimental import pallas as pl
from jax.experimental.pallas import tpu as pltpu

D_MODEL = 1024
N_HEADS = 8
HEAD_DIM = 64
ATTN_WIDTH = N_HEADS * HEAD_DIM
N_GROUPS = 8
SGU_WIDTH = D_MODEL // 2
CHUNK = 128
D_FF = 4 * D_MODEL
EPS = 1e-6
LOG2E = 1.4426950408889634

LANES = 128
N_SPLIT = 3
BIAS_COLS = N_HEADS * N_SPLIT
PAIR = 2 * HEAD_DIM
N_PAIRS = N_HEADS // 2

ROW_TILE = 512
PREP_ROWS = 128
FF_TILE = 1024
N_STREAMS = 2
TQ = 256
TK = 256
NQ = 4
QG = NQ * TQ
KU = 2
N_STATS = 4
ZERO_EXP2 = 160.0
NORM_SLACK = 2.05
VMEM_LIMIT = 56 * 1024 * 1024

_BF16 = jnp.bfloat16
_F32 = jnp.float32


def _split_bf16(x):
    parts = []
    r = x
    for _ in range(N_SPLIT):
        p = r.astype(_BF16)
        parts.append(p)
        r = r - p.astype(_F32)
    return parts


def _dot(a, b):
    return jnp.dot(a, b, preferred_element_type=_F32)


def _dot_nt(a, b):
    return lax.dot_general(a, b, (((1,), (1,)), ((), ())), preferred_element_type=_F32)


def _inproj_kernel(x_ref, gpre_ref, wz_ref, wqt_ref, wk_ref, wvt_ref, wf_ref, wg_ref,
                   bf_ref, gsgu_ref, bsgu_ref, tri_ref, expand_ref, expand_t_ref,
                   head_rows_ref, piece_rows_ref,
                   u_ref, vn_ref, qt_ref, k_ref, vt_ref, gates_ref, qat_ref, ka_ref, stats_ref,
                   carry_ref):
    @pl.when(pl.program_id(1) == 0)
    def _():
        carry_ref[...] = jnp.zeros_like(carry_ref)
        stats_ref[...] = jnp.zeros_like(stats_ref)

    x = x_ref[0]
    tm = x.shape[0]
    ms = jnp.mean(x * x, axis=-1, keepdims=True)
    xb = ((x * lax.rsqrt(ms + EPS)) * gpre_ref[...]).astype(_BF16)

    lane = lax.broadcasted_iota(jnp.int32, (tm, LANES), 1)

    def pack_pieces(v):
        hi, mid, lo = _split_bf16(v)
        zero = jnp.zeros((), _BF16)
        return jnp.where(lane < N_HEADS, hi,
                         jnp.where(lane < 2 * N_HEADS, mid,
                                   jnp.where(lane < N_SPLIT * N_HEADS, lo, zero)))

    f = _dot(xb, wf_ref[...]) + bf_ref[...]

    z = jax.nn.gelu(_dot(xb, wz_ref[...]), approximate=True)
    u_ref[0] = z[:, :SGU_WIDTH].astype(_BF16)
    v = z[:, SGU_WIDTH:]
    mu = jnp.mean(v, axis=-1, keepdims=True)
    vc = v - mu
    var = jnp.mean(vc * vc, axis=-1, keepdims=True)
    vn_ref[0] = ((vc * lax.rsqrt(var + EPS)) * gsgu_ref[...] + bsgu_ref[...]).astype(_BF16)

    log_f = jnp.minimum(f, 0.0) - jnp.log(1.0 + jnp.exp(-jnp.abs(f)))
    sums = _dot(tri_ref[...], pack_pieces(log_f))

    qt = _dot_nt(wqt_ref[...], xb) * (HEAD_DIM ** -0.5 * LOG2E)
    qt_ref[0] = qt.astype(_BF16)
    k = _dot(xb, wk_ref[...])
    k_ref[0] = k.astype(_BF16)
    qn2 = _dot(head_rows_ref[...], (qt * qt).astype(_BF16))
    kn2 = _dot_nt(head_rows_ref[...], (k * k).astype(_BF16))

    total = sums
    for shift in (N_HEADS, 2 * N_HEADS, LANES - N_HEADS, LANES - 2 * N_HEADS):
        total = total + pltpu.roll(sums, shift, 1)
    cum = carry_ref[0:1, :] + total
    carry_ref[0:1, :] = cum[tm - 1:tm, :]
    pieces = pack_pieces(cum * LOG2E)
    spread = _dot(pieces, expand_ref[...])
    spread_t = _dot_nt(expand_t_ref[...], pieces)

    cum_t = _dot_nt(piece_rows_ref[...], pieces)
    tile_lane = lax.broadcasted_iota(jnp.int32, (2 * N_HEADS, LANES), 1)
    first_tile = pl.program_id(1) * (tm // TK)
    stats = [stats_ref[0, s] for s in range(N_STATS)]
    for j in range(tm // TK):
        lo, hi = j * TK, (j + 1) * TK
        cols = (cum_t[:, lo:lo + 1], cum_t[:, hi - 1:hi],
                jnp.max(qn2[:, lo:hi], axis=1, keepdims=True),
                jnp.max(kn2[:, lo:hi], axis=1, keepdims=True))
        stats = [jnp.where(tile_lane == first_tile + j, c, s) for c, s in zip(cols, stats)]
    for s in range(N_STATS):
        stats_ref[0, s] = stats[s]

    vt = _dot_nt(wvt_ref[...], xb).astype(_BF16)
    for h in range(N_HEADS):
        vt_ref[0, h, :HEAD_DIM, :] = vt[h * HEAD_DIM:(h + 1) * HEAD_DIM, :]
        vt_ref[0, h, HEAD_DIM:, :] = jnp.ones((PAIR - HEAD_DIM, tm), _BF16)

    gates_ref[0] = jax.nn.sigmoid(_dot(xb, wg_ref[...])).astype(_BF16)

    in_a = lane < BIAS_COLS
    in_b = jnp.logical_and(lane >= BIAS_COLS, lane < 2 * BIAS_COLS)
    ka_ref[0] = jnp.where(in_a, -spread, jnp.where(in_b, 1.0, 0.0)).astype(_BF16)
    row = lax.broadcasted_iota(jnp.int32, (LANES, tm), 0)
    in_a = row < BIAS_COLS
    in_b = jnp.logical_and(row >= BIAS_COLS, row < 2 * BIAS_COLS)
    qat_ref[0] = jnp.where(in_a, 1.0, jnp.where(in_b, spread_t, 0.0)).astype(_BF16)


def _head_rows():
    r = np.zeros((2 * N_HEADS, ATTN_WIDTH), np.float32)
    for h in range(N_HEADS):
        r[h, h * HEAD_DIM:(h + 1) * HEAD_DIM] = 1.0
    return r


def _piece_rows():
    r = np.zeros((2 * N_HEADS, LANES), np.float32)
    for h in range(N_HEADS):
        for i in range(N_SPLIT):
            r[h, N_HEADS * i + h] = 1.0
    return r


def _expand_matrix():
    e = np.zeros((LANES, LANES), np.float32)
    for h in range(N_HEADS):
        for i in range(N_SPLIT):
            e[N_HEADS * i + h, N_SPLIT * h + i] = 1.0
            e[N_HEADS * i + h, BIAS_COLS + N_SPLIT * h + i] = 1.0
    return e


_IN_OFFSETS = tuple(int(v) for v in np.cumsum(
    (0, 2 * SGU_WIDTH, ATTN_WIDTH, ATTN_WIDTH, ATTN_WIDTH, N_HEADS, 2 * D_MODEL)))


def _split_w_in_kernel(w_ref, wz_ref, wqt_ref, wk_ref, wvt_ref, wf_ref, wg_ref):
    o = _IN_OFFSETS
    w = w_ref[...]
    wz_ref[...] = w[:, o[0]:o[1]].astype(_BF16)
    wqt_ref[...] = w[:, o[1]:o[2]].T.astype(_BF16)
    wk_ref[...] = w[:, o[2]:o[3]].astype(_BF16)
    wvt_ref[...] = w[:, o[3]:o[4]].T.astype(_BF16)
    block = w[:, o[4]:o[4] + LANES]
    lane = lax.broadcasted_iota(jnp.int32, block.shape, 1)
    f = jnp.where(lane < N_HEADS, block, 0.0)
    for i in range(1, N_SPLIT):
        f = f + pltpu.roll(jnp.where(lane < N_HEADS, block, 0.0), i * N_HEADS, 1)
    wf_ref[...] = f.astype(_BF16)
    wg_ref[...] = w[:, o[5]:o[6]].astype(_BF16)


def _split_w_in(w_in):
    K, N = w_in.shape
    rows = PREP_ROWS
    shapes = ((K, 2 * SGU_WIDTH), (ATTN_WIDTH, K), (K, ATTN_WIDTH), (ATTN_WIDTH, K),
              (K, LANES), (K, 2 * D_MODEL))
    transposed = (False, True, False, True, False, False)
    return pl.pallas_call(
        _split_w_in_kernel,
        grid=(K // rows,),
        in_specs=[pl.BlockSpec((rows, N), lambda i: (i, 0))],
        out_specs=[pl.BlockSpec((s[0], rows), lambda i: (0, i)) if t
                   else pl.BlockSpec((rows, s[1]), lambda i: (i, 0))
                   for s, t in zip(shapes, transposed)],
        out_shape=[jax.ShapeDtypeStruct(s, _BF16) for s in shapes],
        compiler_params=pltpu.CompilerParams(
            dimension_semantics=("parallel",), vmem_limit_bytes=VMEM_LIMIT),
        name="split_w_in",
    )(w_in)


def _in_projection(x, g_pre, w_in, b_forget, g_sgu, b_sgu):
    B, S, D = x.shape
    tm = ROW_TILE
    wz, wqt, wk, wvt, wf, wg = _split_w_in(w_in)
    lane_pad = ((0, 0), (0, LANES - N_SPLIT * N_HEADS))
    bf = jnp.pad(jnp.tile(b_forget.reshape(1, N_HEADS), (1, N_SPLIT)), lane_pad)
    tri = jnp.asarray(np.tril(np.ones((tm, tm), np.float32)), _BF16)
    expand = jnp.asarray(_expand_matrix(), _BF16)
    expand_t = jnp.asarray(_expand_matrix().T, _BF16)
    head_rows = jnp.asarray(_head_rows(), _BF16)
    piece_rows = jnp.asarray(_piece_rows(), _BF16)

    def const(shape):
        return pl.BlockSpec(shape, lambda b, i: (0,) * len(shape))

    def rows(width):
        return pl.BlockSpec((1, tm, width), lambda b, i: (b, i, 0))

    out_shape = (
        jax.ShapeDtypeStruct((B, S, SGU_WIDTH), _BF16),
        jax.ShapeDtypeStruct((B, S, SGU_WIDTH), _BF16),
        jax.ShapeDtypeStruct((B, ATTN_WIDTH, S), _BF16),
        jax.ShapeDtypeStruct((B, S, ATTN_WIDTH), _BF16),
        jax.ShapeDtypeStruct((B, N_HEADS, PAIR, S), _BF16),
        jax.ShapeDtypeStruct((B, S, 2 * D_MODEL), _BF16),
        jax.ShapeDtypeStruct((B, LANES, S), _BF16),
        jax.ShapeDtypeStruct((B, S, LANES), _BF16),
        jax.ShapeDtypeStruct((B, N_STATS, 2 * N_HEADS, LANES), _F32),
    )

    def cols(height):
        return pl.BlockSpec((1, height, tm), lambda b, i: (b, 0, i))

    out_specs = (
        rows(SGU_WIDTH), rows(SGU_WIDTH), cols(ATTN_WIDTH), rows(ATTN_WIDTH),
        pl.BlockSpec((1, N_HEADS, PAIR, tm), lambda b, i: (b, 0, 0, i)),
        rows(2 * D_MODEL), cols(LANES), rows(LANES),
        pl.BlockSpec((1, N_STATS, 2 * N_HEADS, LANES), lambda b, i: (b, 0, 0, 0)),
    )
    return pl.pallas_call(
        _inproj_kernel,
        grid=(B, S // tm),
        in_specs=[
            rows(D), const((1, D)),
            const(wz.shape), const(wqt.shape), const(wk.shape), const(wvt.shape),
            const(wf.shape), const(wg.shape),
            const((1, LANES)), const((1, SGU_WIDTH)), const((1, SGU_WIDTH)),
            const(tri.shape), const(expand.shape), const(expand_t.shape),
            const(head_rows.shape), const(piece_rows.shape),
        ],
        out_specs=out_specs,
        out_shape=out_shape,
        scratch_shapes=[pltpu.VMEM((8, LANES), _F32)],
        compiler_params=pltpu.CompilerParams(
            dimension_semantics=("arbitrary", "arbitrary"),
            vmem_limit_bytes=VMEM_LIMIT),
        name="in_projection",
    )(x, g_pre.reshape(1, D), wz, wqt, wk, wvt, wf, wg, bf,
      g_sgu.reshape(1, SGU_WIDTH), b_sgu.reshape(1, SGU_WIDTH), tri, expand, expand_t,
      head_rows, piece_rows)


def _attn_kernel(plan_ref, k_ref, ka_ref, vt_ref, qt_ref, qat_ref, o_ref,
                 qf_ref, st_a, st_b, m_ref, acc_ref):
    batch = pl.program_id(0)
    pair = pl.program_id(1)
    S = k_ref.shape[1]
    row = lax.broadcasted_iota(jnp.int32, (PAIR, 1), 0)
    all_tiles = [(e, t) for t in range(NQ) for e in range(2)]

    def key_tile(k0):
        return jnp.concatenate([k_ref[0, pl.ds(k0, TK), :], ka_ref[0, pl.ds(k0, TK), :]], axis=1)

    def offsets(g, j):
        n, skew = plan_ref[batch, pair, g, 0], plan_ref[batch, pair, g, 1]
        first = (g * NQ - n - skew + j) * TK
        return [pl.multiple_of(first + t * skew * TK, TK) for t in range(NQ)]

    def update(e, t, st, k0, masked):
        cs = slice(t * TQ, (t + 1) * TQ)
        if masked:
            key_i = lax.broadcasted_iota(jnp.int32, (TK, TQ), 0)
            qry_i = lax.broadcasted_iota(jnp.int32, (TK, TQ), 1)
            st = jnp.where(key_i <= qry_i, st, -jnp.inf)
        m = m_ref[e, :, cs]
        m_new = jnp.maximum(m, jnp.max(st, axis=0, keepdims=True))
        p = jnp.exp2(st - m_new)
        alpha = jnp.exp2(m - m_new)
        m_ref[e, :, cs] = m_new
        acc_ref[e, :, cs] = alpha * acc_ref[e, :, cs] + _dot(
            vt_ref[0, e, :, pl.ds(k0, TK)], p.astype(_BF16))

    def step(cur, cur_tiles, cur_offs, nxt, nxt_tiles, nxt_offs):
        for n in range(max(len(cur_tiles), len(nxt_tiles))):
            if n < len(nxt_tiles):
                e, t = nxt_tiles[n]
                nxt[e, t] = _dot(key_tile(nxt_offs[t]), qf_ref[e, :, t * TQ:(t + 1) * TQ])
            if n < len(cur_tiles):
                e, t, masked = cur_tiles[n]
                update(e, t, cur[e, t], cur_offs[t], masked)

    def load_queries(g):
        q0 = pl.multiple_of(g * QG, QG)
        qt = qt_ref[0, :, pl.ds(q0, QG)].astype(_F32)
        qat = qat_ref[0, :, pl.ds(q0, QG)].astype(_F32)
        for e in range(2):
            a0 = N_SPLIT * (2 * pair + e)
            q_mask = jnp.logical_and(row >= HEAD_DIM * e, row < HEAD_DIM * (e + 1))
            a_mask = jnp.logical_or(
                jnp.logical_and(row >= a0, row < a0 + N_SPLIT),
                jnp.logical_and(row >= BIAS_COLS + a0, row < BIAS_COLS + a0 + N_SPLIT))
            qf_ref[e, :PAIR, :] = jnp.where(q_mask, qt, 0.0).astype(_BF16)
            qf_ref[e, PAIR:, :] = jnp.where(a_mask, qat, 0.0).astype(_BF16)

    def reset_state():
        m_ref[...] = jnp.full(m_ref.shape, -jnp.inf, _F32)
        acc_ref[...] = jnp.zeros(acc_ref.shape, _F32)

    full = [(e, t, False) for e, t in all_tiles]
    bufs = (st_a, st_b)
    n_groups = S // QG

    load_queries(0)
    reset_state()
    step(None, [], None, st_a, all_tiles, offsets(0, 0))

    def q_group(g, _):
        q0 = pl.multiple_of(g * QG, QG)
        n = plan_ref[batch, pair, g, 0]
        skew = plan_ref[batch, pair, g, 1]
        g_next = jnp.minimum(g + 1, n_groups - 1)

        def full_steps(i, _):
            for u in range(KU):
                j = i * KU + u
                step(bufs[u % 2], full, offsets(g, j), bufs[(u + 1) % 2], all_tiles,
                     offsets(g, j + 1))
            return 0

        lax.fori_loop(0, n // KU, full_steps, 0)

        @pl.when(skew == 1)
        def _():
            step(st_a, full, offsets(g, n), st_b, all_tiles, offsets(g, n + 1))
            load_queries(g_next)
            step(st_b, [(e, t, True) for e, t in all_tiles], offsets(g, n + 1),
                 st_a, all_tiles, offsets(g_next, 0))

        @pl.when(skew == 0)
        def _():
            for i in range(NQ):
                cur_tiles = [(e, t, t == i) for t in range(i, NQ) for e in range(2)]
                cur_offs = [q0 + i * TK] * NQ
                if i + 1 < NQ:
                    nxt_tiles = [(e, t) for t in range(i + 1, NQ) for e in range(2)]
                    nxt_offs = [q0 + (i + 1) * TK] * NQ
                else:
                    load_queries(g_next)
                    nxt_tiles, nxt_offs = all_tiles, offsets(g_next, 0)
                step(bufs[i % 2], cur_tiles, cur_offs, bufs[(i + 1) % 2], nxt_tiles, nxt_offs)

        out = [acc_ref[e, :HEAD_DIM, :] * (1.0 / acc_ref[e, HEAD_DIM:HEAD_DIM + 1, :])
               for e in range(2)]
        o_ref[0, pl.ds(q0, QG), :] = jnp.concatenate(out, axis=0).T.astype(_BF16)
        reset_state()
        return 0

    lax.fori_loop(0, n_groups, q_group, 0)


def _sweep_plan(stats, n_tiles):
    cum_first, cum_last, qn2, kn2 = (stats[:, s, :N_HEADS, :n_tiles] for s in range(N_STATS))
    B = stats.shape[0]
    n_groups = n_tiles // NQ
    k_norm = jnp.sqrt(jnp.max(kn2, axis=-1, keepdims=True))
    reach = NORM_SLACK * jnp.sqrt(qn2) * k_norm + cum_first
    bound = reach[..., None] - cum_last[:, :, None, :]
    tile = jnp.arange(n_tiles)
    needed = jnp.logical_and(tile[None, :] < tile[:, None], bound >= -ZERO_EXP2)
    w = jnp.sum(needed, axis=-1).astype(jnp.int32)
    w = jnp.max(w.reshape(B, N_PAIRS, 2, n_groups, NQ), axis=2)
    start = jnp.arange(n_groups, dtype=jnp.int32) * NQ
    n_flat = jnp.max(jnp.maximum(w - jnp.arange(NQ, dtype=jnp.int32), 0), axis=-1)
    n_flat = jnp.minimum(((n_flat + KU - 1) // KU) * KU, start)
    n_skew = jnp.maximum(jnp.max(w, axis=-1), 1)
    n_skew = n_skew + (n_skew + 1) % 2
    units_flat = NQ * n_flat + NQ * (NQ + 1) // 2
    units_skew = NQ * n_skew + NQ
    skew = jnp.logical_and(n_skew < start, units_skew < units_flat)
    n = jnp.where(skew, n_skew - 1, n_flat)
    return jnp.stack([n, skew.astype(jnp.int32)], axis=-1)


def _attention(k, ka, vt, qt, qat, stats):
    B, S, _ = k.shape
    assert TQ == TK and KU == 2 and NQ % KU == 0 and S % QG == 0
    plan = _sweep_plan(stats, S // TK)
    pair_rows = pl.BlockSpec((1, S, PAIR), lambda b, j, nb: (b, 0, j))
    return pl.pallas_call(
        _attn_kernel,
        grid_spec=pltpu.PrefetchScalarGridSpec(
            num_scalar_prefetch=1,
            grid=(B, N_PAIRS),
            in_specs=[pair_rows, pl.BlockSpec((1, S, LANES), lambda b, j, nb: (b, 0, 0)),
                      pl.BlockSpec((1, 2, PAIR, S), lambda b, j, nb: (b, j, 0, 0)),
                      pl.BlockSpec((1, PAIR, S), lambda b, j, nb: (b, j, 0)),
                      pl.BlockSpec((1, LANES, S), lambda b, j, nb: (b, 0, 0))],
            out_specs=pair_rows,
            scratch_shapes=[pltpu.VMEM((2, PAIR + LANES, QG), _BF16),
                            pltpu.VMEM((2, NQ, TK, TQ), _F32), pltpu.VMEM((2, NQ, TK, TQ), _F32),
                            pltpu.VMEM((2, 1, QG), _F32), pltpu.VMEM((2, PAIR, QG), _F32)]),
        out_shape=jax.ShapeDtypeStruct((B, S, ATTN_WIDTH), _BF16),
        compiler_params=pltpu.CompilerParams(
            dimension_semantics=("parallel", "parallel"),
            vmem_limit_bytes=VMEM_LIMIT),
        name="forgetting_attention",
    )(plan, k, ka, vt, qt, qat)


def _rms_scale(v, g):
    ms = jnp.mean(v * v, axis=-1, keepdims=True)
    return (v * lax.rsqrt(ms + EPS)) * g


def _merge_mlp_kernel(x_ref, u_ref, vn_ref, ya_ref, gates_ref, wsp_ref, bsp_ref,
                      wbs_ref, wba_ref, wout_ref, gpost_ref,
                      gpre2_ref, wup_ref, wdown_ref, gpost2_ref, o_ref, ysgu_ref):
    tm = x_ref.shape[0]
    row = lax.broadcasted_iota(jnp.int32, (CHUNK, CHUNK), 0)
    col = lax.broadcasted_iota(jnp.int32, (CHUNK, CHUNK), 1)
    lane = lax.broadcasted_iota(jnp.int32, (CHUNK, PAIR), 1)
    ws = [jnp.where(row >= col, wsp_ref[g], 0.0).astype(_BF16) for g in range(N_GROUPS)]
    for c in range(tm // CHUNK):
        r = slice(c * CHUNK, (c + 1) * CHUNK)
        for j in range(N_GROUPS // 2):
            cs = slice(j * PAIR, (j + 1) * PAIR)
            vp = vn_ref[r, cs]
            s = jnp.where(lane < HEAD_DIM, _dot(ws[2 * j], vp), _dot(ws[2 * j + 1], vp))
            s = s + bsp_ref[:, cs]
            ysgu_ref[r, cs] = (u_ref[r, cs].astype(_F32) * s).astype(_BF16)

    blocks = [slice(n * tm // N_STREAMS, (n + 1) * tm // N_STREAMS) for n in range(N_STREAMS)]

    def mix(r):
        a = _dot(ysgu_ref[r, :], wbs_ref[...])
        b = _dot(ya_ref[r, :], wba_ref[...])
        merged = (gates_ref[r, :D_MODEL].astype(_F32) * a
                  + gates_ref[r, D_MODEL:].astype(_F32) * b).astype(_BF16)
        return _dot(merged, wout_ref[...])

    def mlp(xb):
        acc = jnp.zeros(xb.shape, _F32)
        for c in range(D_FF // FF_TILE):
            cs = slice(c * FF_TILE, (c + 1) * FF_TILE)
            hid = jnp.square(jnp.maximum(_dot(xb, wup_ref[:, cs]), 0.0)).astype(_BF16)
            acc = acc + _dot(hid, wdown_ref[cs, :])
        return acc

    mixed = [mix(r) for r in blocks]
    h1 = [x_ref[r, :] + _rms_scale(o, gpost_ref[...]) for r, o in zip(blocks, mixed)]
    xb = [_rms_scale(h, gpre2_ref[...]).astype(_BF16) for h in h1]
    ff = [mlp(v) for v in xb]
    for r, h, f in zip(blocks, h1, ff):
        o_ref[r, :] = h + _rms_scale(f, gpost2_ref[...])


def _merge_mlp(x2, u2, vn2, ya2, gates2, w_spatial, b_spatial, w_bs, w_ba, w_out, g_post,
               g_pre2, w_up, w_down, g_post2):
    R, D = x2.shape
    tm = ROW_TILE
    bsp = jnp.repeat(b_spatial.T, SGU_WIDTH // N_GROUPS, axis=1)

    def const(shape):
        return pl.BlockSpec(shape, lambda i: (0,) * len(shape), pipeline_mode=pl.Buffered(1))

    def rows(width):
        return pl.BlockSpec((tm, width), lambda i: (i, 0))

    return pl.pallas_call(
        _merge_mlp_kernel,
        grid=(R // tm,),
        in_specs=[rows(D), rows(SGU_WIDTH), rows(SGU_WIDTH), rows(ATTN_WIDTH), rows(2 * D),
                  const(w_spatial.shape), const(bsp.shape),
                  const(w_bs.shape), const(w_ba.shape), const(w_out.shape), const((1, D)),
                  const((1, D)), const(w_up.shape), const(w_down.shape), const((1, D))],
        out_specs=rows(D),
        out_shape=jax.ShapeDtypeStruct((R, D), _F32),
        scratch_shapes=[pltpu.VMEM((tm, SGU_WIDTH), _BF16)],
        compiler_params=pltpu.CompilerParams(
            dimension_semantics=("parallel",), vmem_limit_bytes=VMEM_LIMIT),
        name="merge_mlp",
    )(x2, u2, vn2, ya2, gates2, w_spatial, bsp,
      w_bs.astype(_BF16), w_ba.astype(_BF16), w_out.astype(_BF16), g_post.reshape(1, D),
      g_pre2.reshape(1, D), w_up.astype(_BF16), w_down.astype(_BF16), g_post2.reshape(1, D))


def kernel(x, g_mix_pre, w_in, b_forget, g_sgu, b_sgu, w_spatial, b_spatial, w_branch_sgu,
           w_branch_attn, w_out, g_mix_post, g_ffn_pre, w_up, w_down, g_ffn_post):
    B, S, D = x.shape
    h = x
    for l in range(g_mix_pre.shape[0]):
        u, vn, qt, k, vt, gates, qat, ka, stats = _in_projection(
            h, g_mix_pre[l], w_in[l], b_forget[l], g_sgu[l], b_sgu[l])
        y_attn = _attention(k, ka, vt, qt, qat, stats)
        h = _merge_mlp(h.reshape(B * S, D), u.reshape(B * S, -1), vn.reshape(B * S, -1),
                       y_attn.reshape(B * S, -1), gates.reshape(B * S, -1),
                       w_spatial[l], b_spatial[l], w_branch_sgu[l], w_branch_attn[l], w_out[l],
                       g_mix_post[l], g_ffn_pre[l], w_up[l], w_down[l],
                       g_ffn_post[l]).reshape(B, S, D)
    return h
```

```python
import functools

import jax
import jax.numpy as jnp
import numpy as np
from jax import lax
from jax.experimental import pallas as pl
from jax.experimental.pallas import tpu as pltpu

D_MODEL = 1024
N_HEADS = 8
HEAD_DIM = 64
ATTN_WIDTH = N_HEADS * HEAD_DIM
N_GROUPS = 8
SGU_WIDTH = D_MODEL // 2
CHUNK = 128
D_FF = 4 * D_MODEL
EPS = 1e-6
LOG2E = 1.4426950408889634

LANES = 128
N_SPLIT = 3
BIAS_COLS = N_HEADS * N_SPLIT
PAIR = 2 * HEAD_DIM
N_PAIRS = N_HEADS // 2

ROW_TILE = 512
FF_TILE = 1024
N_STREAMS = 2
TQ = 256
TK = 256
NQ = 4
QG = NQ * TQ
KU = 2
N_STATS = 4
ZERO_EXP2 = 160.0
NORM_SLACK = 2.05
VMEM_LIMIT = 56 * 1024 * 1024

_BF16 = jnp.bfloat16
_F32 = jnp.float32


def _split_bf16(x):
    parts = []
    r = x
    for _ in range(N_SPLIT):
        p = r.astype(_BF16)
        parts.append(p)
        r = r - p.astype(_F32)
    return parts


def _dot(a, b):
    return jnp.dot(a, b, preferred_element_type=_F32)


def _dot_nt(a, b):
    return lax.dot_general(a, b, (((1,), (1,)), ((), ())), preferred_element_type=_F32)


def _inproj_kernel(x_ref, gpre_ref, wzt_ref, wqt_ref, wkt_ref, wvt_ref, wft_ref, wgt_ref,
                   bf_ref, gsgu_ref, bsgu_ref, tri_ref, expand_ref, expand_t_ref,
                   head_rows_ref, piece_rows_ref,
                   u_ref, vn_ref, qt_ref, k_ref, vt_ref, gates_ref, qat_ref, ka_ref, stats_ref,
                   carry_ref):
    @pl.when(pl.program_id(1) == 0)
    def _():
        carry_ref[...] = jnp.zeros_like(carry_ref)
        stats_ref[...] = jnp.zeros_like(stats_ref)

    x = x_ref[0]
    tm = x.shape[0]
    ms = jnp.mean(x * x, axis=-1, keepdims=True)
    xb = ((x * lax.rsqrt(ms + EPS)) * gpre_ref[...]).astype(_BF16)

    lane = lax.broadcasted_iota(jnp.int32, (tm, LANES), 1)

    def pack_pieces(v):
        hi, mid, lo = _split_bf16(v)
        zero = jnp.zeros((), _BF16)
        return jnp.where(lane < N_HEADS, hi,
                         jnp.where(lane < 2 * N_HEADS, mid,
                                   jnp.where(lane < N_SPLIT * N_HEADS, lo, zero)))

    f = _dot_nt(xb, wft_ref[...]) + bf_ref[...]

    z = jax.nn.gelu(_dot_nt(xb, wzt_ref[...]), approximate=True)
    u_ref[0] = z[:, :SGU_WIDTH].astype(_BF16)
    v = z[:, SGU_WIDTH:]
    mu = jnp.mean(v, axis=-1, keepdims=True)
    vc = v - mu
    var = jnp.mean(vc * vc, axis=-1, keepdims=True)
    vn_ref[0] = ((vc * lax.rsqrt(var + EPS)) * gsgu_ref[...] + bsgu_ref[...]).astype(_BF16)

    log_f = jnp.minimum(f, 0.0) - jnp.log(1.0 + jnp.exp(-jnp.abs(f)))
    sums = _dot(tri_ref[...], pack_pieces(log_f))

    qt = _dot_nt(wqt_ref[...], xb) * (HEAD_DIM ** -0.5 * LOG2E)
    qt_ref[0] = qt.astype(_BF16)
    k = _dot_nt(xb, wkt_ref[...])
    k_ref[0] = k.astype(_BF16)
    qn2 = _dot(head_rows_ref[...], (qt * qt).astype(_BF16))
    kn2 = _dot_nt(head_rows_ref[...], (k * k).astype(_BF16))

    total = sums
    for shift in (N_HEADS, 2 * N_HEADS, LANES - N_HEADS, LANES - 2 * N_HEADS):
        total = total + pltpu.roll(sums, shift, 1)
    cum = carry_ref[0:1, :] + total
    carry_ref[0:1, :] = cum[tm - 1:tm, :]
    pieces = pack_pieces(cum * LOG2E)
    spread = _dot(pieces, expand_ref[...])
    spread_t = _dot_nt(expand_t_ref[...], pieces)

    cum_t = _dot_nt(piece_rows_ref[...], pieces)
    tile_lane = lax.broadcasted_iota(jnp.int32, (2 * N_HEADS, LANES), 1)
    first_tile = pl.program_id(1) * (tm // TK)
    stats = [stats_ref[0, s] for s in range(N_STATS)]
    for j in range(tm // TK):
        lo, hi = j * TK, (j + 1) * TK
        cols = (cum_t[:, lo:lo + 1], cum_t[:, hi - 1:hi],
                jnp.max(qn2[:, lo:hi], axis=1, keepdims=True),
                jnp.max(kn2[:, lo:hi], axis=1, keepdims=True))
        stats = [jnp.where(tile_lane == first_tile + j, c, s) for c, s in zip(cols, stats)]
    for s in range(N_STATS):
        stats_ref[0, s] = stats[s]

    vt = _dot_nt(wvt_ref[...], xb).astype(_BF16)
    for h in range(N_HEADS):
        vt_ref[0, h, :HEAD_DIM, :] = vt[h * HEAD_DIM:(h + 1) * HEAD_DIM, :]
        vt_ref[0, h, HEAD_DIM:, :] = jnp.ones((PAIR - HEAD_DIM, tm), _BF16)

    gates_ref[0] = jax.nn.sigmoid(_dot_nt(xb, wgt_ref[...])).astype(_BF16)

    in_a = lane < BIAS_COLS
    in_b = jnp.logical_and(lane >= BIAS_COLS, lane < 2 * BIAS_COLS)
    ka_ref[0] = jnp.where(in_a, -spread, jnp.where(in_b, 1.0, 0.0)).astype(_BF16)
    row = lax.broadcasted_iota(jnp.int32, (LANES, tm), 0)
    in_a = row < BIAS_COLS
    in_b = jnp.logical_and(row >= BIAS_COLS, row < 2 * BIAS_COLS)
    qat_ref[0] = jnp.where(in_a, 1.0, jnp.where(in_b, spread_t, 0.0)).astype(_BF16)


def _head_rows():
    r = np.zeros((2 * N_HEADS, ATTN_WIDTH), np.float32)
    for h in range(N_HEADS):
        r[h, h * HEAD_DIM:(h + 1) * HEAD_DIM] = 1.0
    return r


def _piece_rows():
    r = np.zeros((2 * N_HEADS, LANES), np.float32)
    for h in range(N_HEADS):
        for i in range(N_SPLIT):
            r[h, N_HEADS * i + h] = 1.0
    return r


def _expand_matrix():
    e = np.zeros((LANES, LANES), np.float32)
    for h in range(N_HEADS):
        for i in range(N_SPLIT):
            e[N_HEADS * i + h, N_SPLIT * h + i] = 1.0
            e[N_HEADS * i + h, BIAS_COLS + N_SPLIT * h + i] = 1.0
    return e


_IN_OFFSETS = tuple(int(v) for v in np.cumsum(
    (0, 2 * SGU_WIDTH, ATTN_WIDTH, ATTN_WIDTH, ATTN_WIDTH, N_HEADS, 2 * D_MODEL)))


def _in_projection(x, g_pre, w_in, b_forget, g_sgu, b_sgu):
    B, S, D = x.shape
    tm = ROW_TILE
    o = _IN_OFFSETS
    wt = jnp.swapaxes(w_in, 0, 1)
    wzt, wqt, wkt, wvt, wgt = (wt[o[i]:o[i + 1]].astype(_BF16) for i in (0, 1, 2, 3, 5))
    row_pad = ((0, LANES - N_SPLIT * N_HEADS), (0, 0))
    wft = jnp.pad(jnp.tile(wt[o[4]:o[5]], (N_SPLIT, 1)), row_pad).astype(_BF16)
    lane_pad = ((0, 0), (0, LANES - N_SPLIT * N_HEADS))
    bf = jnp.pad(jnp.tile(b_forget.reshape(1, N_HEADS), (1, N_SPLIT)), lane_pad)
    tri = jnp.asarray(np.tril(np.ones((tm, tm), np.float32)), _BF16)
    expand = jnp.asarray(_expand_matrix(), _BF16)
    expand_t = jnp.asarray(_expand_matrix().T, _BF16)
    head_rows = jnp.asarray(_head_rows(), _BF16)
    piece_rows = jnp.asarray(_piece_rows(), _BF16)

    def const(shape):
        return pl.BlockSpec(shape, lambda b, i: (0,) * len(shape))

    def rows(width):
        return pl.BlockSpec((1, tm, width), lambda b, i: (b, i, 0))

    out_shape = (
        jax.ShapeDtypeStruct((B, S, SGU_WIDTH), _BF16),
        jax.ShapeDtypeStruct((B, S, SGU_WIDTH), _BF16),
        jax.ShapeDtypeStruct((B, ATTN_WIDTH, S), _BF16),
        jax.ShapeDtypeStruct((B, S, ATTN_WIDTH), _BF16),
        jax.ShapeDtypeStruct((B, N_HEADS, PAIR, S), _BF16),
        jax.ShapeDtypeStruct((B, S, 2 * D_MODEL), _BF16),
        jax.ShapeDtypeStruct((B, LANES, S), _BF16),
        jax.ShapeDtypeStruct((B, S, LANES), _BF16),
        jax.ShapeDtypeStruct((B, N_STATS, 2 * N_HEADS, LANES), _F32),
    )

    def cols(height):
        return pl.BlockSpec((1, height, tm), lambda b, i: (b, 0, i))

    out_specs = (
        rows(SGU_WIDTH), rows(SGU_WIDTH), cols(ATTN_WIDTH), rows(ATTN_WIDTH),
        pl.BlockSpec((1, N_HEADS, PAIR, tm), lambda b, i: (b, 0, 0, i)),
        rows(2 * D_MODEL), cols(LANES), rows(LANES),
        pl.BlockSpec((1, N_STATS, 2 * N_HEADS, LANES), lambda b, i: (b, 0, 0, 0)),
    )
    return pl.pallas_call(
        _inproj_kernel,
        grid=(B, S // tm),
        in_specs=[
            rows(D), const((1, D)),
            const(wzt.shape), const(wqt.shape), const(wkt.shape), const(wvt.shape),
            const(wft.shape), const(wgt.shape),
            const((1, LANES)), const((1, SGU_WIDTH)), const((1, SGU_WIDTH)),
            const(tri.shape), const(expand.shape), const(expand_t.shape),
            const(head_rows.shape), const(piece_rows.shape),
        ],
        out_specs=out_specs,
        out_shape=out_shape,
        scratch_shapes=[pltpu.VMEM((8, LANES), _F32)],
        compiler_params=pltpu.CompilerParams(
            dimension_semantics=("arbitrary", "arbitrary"),
            vmem_limit_bytes=VMEM_LIMIT),
        name="in_projection",
    )(x, g_pre.reshape(1, D), wzt, wqt, wkt, wvt, wft, wgt, bf,
      g_sgu.reshape(1, SGU_WIDTH), b_sgu.reshape(1, SGU_WIDTH), tri, expand, expand_t,
      head_rows, piece_rows)


def _attn_kernel(plan_ref, k_ref, ka_ref, vt_ref, qt_ref, qat_ref, o_ref,
                 qf_ref, st_a, st_b, m_ref, acc_ref):
    batch = pl.program_id(0)
    pair = pl.program_id(1)
    S = k_ref.shape[1]
    row = lax.broadcasted_iota(jnp.int32, (PAIR, 1), 0)
    all_tiles = [(e, t) for t in range(NQ) for e in range(2)]

    def key_tile(k0):
        return jnp.concatenate([k_ref[0, pl.ds(k0, TK), :], ka_ref[0, pl.ds(k0, TK), :]], axis=1)

    def offsets(g, j):
        n, skew = plan_ref[batch, pair, g, 0], plan_ref[batch, pair, g, 1]
        first = (g * NQ - n - skew + j) * TK
        return [pl.multiple_of(first + t * skew * TK, TK) for t in range(NQ)]

    def update(e, t, st, k0, masked):
        cs = slice(t * TQ, (t + 1) * TQ)
        if masked:
            key_i = lax.broadcasted_iota(jnp.int32, (TK, TQ), 0)
            qry_i = lax.broadcasted_iota(jnp.int32, (TK, TQ), 1)
            st = jnp.where(key_i <= qry_i, st, -jnp.inf)
        m = m_ref[e, :, cs]
        m_new = jnp.maximum(m, jnp.max(st, axis=0, keepdims=True))
        p = jnp.exp2(st - m_new)
        alpha = jnp.exp2(m - m_new)
        m_ref[e, :, cs] = m_new
        acc_ref[e, :, cs] = alpha * acc_ref[e, :, cs] + _dot(
            vt_ref[0, e, :, pl.ds(k0, TK)], p.astype(_BF16))

    def step(cur, cur_tiles, cur_offs, nxt, nxt_tiles, nxt_offs):
        for n in range(max(len(cur_tiles), len(nxt_tiles))):
            if n < len(nxt_tiles):
                e, t = nxt_tiles[n]
                nxt[e, t] = _dot(key_tile(nxt_offs[t]), qf_ref[e, :, t * TQ:(t + 1) * TQ])
            if n < len(cur_tiles):
                e, t, masked = cur_tiles[n]
                update(e, t, cur[e, t], cur_offs[t], masked)

    def load_queries(g):
        q0 = pl.multiple_of(g * QG, QG)
        qt = qt_ref[0, :, pl.ds(q0, QG)].astype(_F32)
        qat = qat_ref[0, :, pl.ds(q0, QG)].astype(_F32)
        for e in range(2):
            a0 = N_SPLIT * (2 * pair + e)
            q_mask = jnp.logical_and(row >= HEAD_DIM * e, row < HEAD_DIM * (e + 1))
            a_mask = jnp.logical_or(
                jnp.logical_and(row >= a0, row < a0 + N_SPLIT),
                jnp.logical_and(row >= BIAS_COLS + a0, row < BIAS_COLS + a0 + N_SPLIT))
            qf_ref[e, :PAIR, :] = jnp.where(q_mask, qt, 0.0).astype(_BF16)
            qf_ref[e, PAIR:, :] = jnp.where(a_mask, qat, 0.0).astype(_BF16)

    def reset_state():
        m_ref[...] = jnp.full(m_ref.shape, -jnp.inf, _F32)
        acc_ref[...] = jnp.zeros(acc_ref.shape, _F32)

    full = [(e, t, False) for e, t in all_tiles]
    bufs = (st_a, st_b)
    n_groups = S // QG

    load_queries(0)
    reset_state()
    step(None, [], None, st_a, all_tiles, offsets(0, 0))

    def q_group(g, _):
        q0 = pl.multiple_of(g * QG, QG)
        n = plan_ref[batch, pair, g, 0]
        skew = plan_ref[batch, pair, g, 1]
        g_next = jnp.minimum(g + 1, n_groups - 1)

        def full_steps(i, _):
            for u in range(KU):
                j = i * KU + u
                step(bufs[u % 2], full, offsets(g, j), bufs[(u + 1) % 2], all_tiles,
                     offsets(g, j + 1))
            return 0

        lax.fori_loop(0, n // KU, full_steps, 0)

        @pl.when(skew == 1)
        def _():
            step(st_a, full, offsets(g, n), st_b, all_tiles, offsets(g, n + 1))
            load_queries(g_next)
            step(st_b, [(e, t, True) for e, t in all_tiles], offsets(g, n + 1),
                 st_a, all_tiles, offsets(g_next, 0))

        @pl.when(skew == 0)
        def _():
            for i in range(NQ):
                cur_tiles = [(e, t, t == i) for t in range(i, NQ) for e in range(2)]
                cur_offs = [q0 + i * TK] * NQ
                if i + 1 < NQ:
                    nxt_tiles = [(e, t) for t in range(i + 1, NQ) for e in range(2)]
                    nxt_offs = [q0 + (i + 1) * TK] * NQ
                else:
                    load_queries(g_next)
                    nxt_tiles, nxt_offs = all_tiles, offsets(g_next, 0)
                step(bufs[i % 2], cur_tiles, cur_offs, bufs[(i + 1) % 2], nxt_tiles, nxt_offs)

        out = [acc_ref[e, :HEAD_DIM, :] * (1.0 / acc_ref[e, HEAD_DIM:HEAD_DIM + 1, :])
               for e in range(2)]
        o_ref[0, pl.ds(q0, QG), :] = jnp.concatenate(out, axis=0).T.astype(_BF16)
        reset_state()
        return 0

    lax.fori_loop(0, n_groups, q_group, 0)


def _sweep_plan(stats, n_tiles):
    cum_first, cum_last, qn2, kn2 = (stats[:, s, :N_HEADS, :n_tiles] for s in range(N_STATS))
    B = stats.shape[0]
    n_groups = n_tiles // NQ
    k_norm = jnp.sqrt(jnp.max(kn2, axis=-1, keepdims=True))
    reach = NORM_SLACK * jnp.sqrt(qn2) * k_norm + cum_first
    bound = reach[..., None] - cum_last[:, :, None, :]
    tile = jnp.arange(n_tiles)
    needed = jnp.logical_and(tile[None, :] < tile[:, None], bound >= -ZERO_EXP2)
    w = jnp.sum(needed, axis=-1).astype(jnp.int32)
    w = jnp.max(w.reshape(B, N_PAIRS, 2, n_groups, NQ), axis=2)
    start = jnp.arange(n_groups, dtype=jnp.int32) * NQ
    n_flat = jnp.max(jnp.maximum(w - jnp.arange(NQ, dtype=jnp.int32), 0), axis=-1)
    n_flat = jnp.minimum(((n_flat + KU - 1) // KU) * KU, start)
    n_skew = jnp.maximum(jnp.max(w, axis=-1), 1)
    n_skew = n_skew + (n_skew + 1) % 2
    units_flat = NQ * n_flat + NQ * (NQ + 1) // 2
    units_skew = NQ * n_skew + NQ
    skew = jnp.logical_and(n_skew < start, units_skew < units_flat)
    n = jnp.where(skew, n_skew - 1, n_flat)
    return jnp.stack([n, skew.astype(jnp.int32)], axis=-1)


def _attention(k, ka, vt, qt, qat, stats):
    B, S, _ = k.shape
    assert TQ == TK and KU == 2 and NQ % KU == 0 and S % QG == 0
    plan = _sweep_plan(stats, S // TK)
    pair_rows = pl.BlockSpec((1, S, PAIR), lambda b, j, nb: (b, 0, j))
    return pl.pallas_call(
        _attn_kernel,
        grid_spec=pltpu.PrefetchScalarGridSpec(
            num_scalar_prefetch=1,
            grid=(B, N_PAIRS),
            in_specs=[pair_rows, pl.BlockSpec((1, S, LANES), lambda b, j, nb: (b, 0, 0)),
                      pl.BlockSpec((1, 2, PAIR, S), lambda b, j, nb: (b, j, 0, 0)),
                      pl.BlockSpec((1, PAIR, S), lambda b, j, nb: (b, j, 0)),
                      pl.BlockSpec((1, LANES, S), lambda b, j, nb: (b, 0, 0))],
            out_specs=pair_rows,
            scratch_shapes=[pltpu.VMEM((2, PAIR + LANES, QG), _BF16),
                            pltpu.VMEM((2, NQ, TK, TQ), _F32), pltpu.VMEM((2, NQ, TK, TQ), _F32),
                            pltpu.VMEM((2, 1, QG), _F32), pltpu.VMEM((2, PAIR, QG), _F32)]),
        out_shape=jax.ShapeDtypeStruct((B, S, ATTN_WIDTH), _BF16),
        compiler_params=pltpu.CompilerParams(
            dimension_semantics=("parallel", "parallel"),
            vmem_limit_bytes=VMEM_LIMIT),
        name="forgetting_attention",
    )(plan, k, ka, vt, qt, qat)


def _rms_scale(v, g):
    ms = jnp.mean(v * v, axis=-1, keepdims=True)
    return (v * lax.rsqrt(ms + EPS)) * g


def _merge_mlp_kernel(x_ref, u_ref, vn_ref, ya_ref, gates_ref, wsp_ref, bsp_ref,
                      wbs_ref, wba_ref, wout_ref, gpost_ref,
                      gpre2_ref, wup_ref, wdown_ref, gpost2_ref, o_ref, ysgu_ref):
    tm = x_ref.shape[0]
    row = lax.broadcasted_iota(jnp.int32, (CHUNK, CHUNK), 0)
    col = lax.broadcasted_iota(jnp.int32, (CHUNK, CHUNK), 1)
    lane = lax.broadcasted_iota(jnp.int32, (CHUNK, PAIR), 1)
    ws = [jnp.where(row >= col, wsp_ref[g], 0.0).astype(_BF16) for g in range(N_GROUPS)]
    for c in range(tm // CHUNK):
        r = slice(c * CHUNK, (c + 1) * CHUNK)
        for j in range(N_GROUPS // 2):
            cs = slice(j * PAIR, (j + 1) * PAIR)
            vp = vn_ref[r, cs]
            s = jnp.where(lane < HEAD_DIM, _dot(ws[2 * j], vp), _dot(ws[2 * j + 1], vp))
            s = s + bsp_ref[:, cs]
            ysgu_ref[r, cs] = (u_ref[r, cs].astype(_F32) * s).astype(_BF16)

    blocks = [slice(n * tm // N_STREAMS, (n + 1) * tm // N_STREAMS) for n in range(N_STREAMS)]

    def mix(r):
        a = _dot(ysgu_ref[r, :], wbs_ref[...])
        b = _dot(ya_ref[r, :], wba_ref[...])
        merged = (gates_ref[r, :D_MODEL].astype(_F32) * a
                  + gates_ref[r, D_MODEL:].astype(_F32) * b).astype(_BF16)
        return _dot(merged, wout_ref[...])

    def mlp(xb):
        acc = jnp.zeros(xb.shape, _F32)
        for c in range(D_FF // FF_TILE):
            cs = slice(c * FF_TILE, (c + 1) * FF_TILE)
            hid = jnp.square(jnp.maximum(_dot(xb, wup_ref[:, cs]), 0.0)).astype(_BF16)
            acc = acc + _dot(hid, wdown_ref[cs, :])
        return acc

    mixed = [mix(r) for r in blocks]
    h1 = [x_ref[r, :] + _rms_scale(o, gpost_ref[...]) for r, o in zip(blocks, mixed)]
    xb = [_rms_scale(h, gpre2_ref[...]).astype(_BF16) for h in h1]
    ff = [mlp(v) for v in xb]
    for r, h, f in zip(blocks, h1, ff):
        o_ref[r, :] = h + _rms_scale(f, gpost2_ref[...])


def _merge_mlp(x2, u2, vn2, ya2, gates2, w_spatial, b_spatial, w_bs, w_ba, w_out, g_post,
               g_pre2, w_up, w_down, g_post2):
    R, D = x2.shape
    tm = ROW_TILE
    bsp = jnp.repeat(b_spatial.T, SGU_WIDTH // N_GROUPS, axis=1)

    def const(shape):
        return pl.BlockSpec(shape, lambda i: (0,) * len(shape), pipeline_mode=pl.Buffered(1))

    def rows(width):
        return pl.BlockSpec((tm, width), lambda i: (i, 0))

    return pl.pallas_call(
        _merge_mlp_kernel,
        grid=(R // tm,),
        in_specs=[rows(D), rows(SGU_WIDTH), rows(SGU_WIDTH), rows(ATTN_WIDTH), rows(2 * D),
                  const(w_spatial.shape), const(bsp.shape),
                  const(w_bs.shape), const(w_ba.shape), const(w_out.shape), const((1, D)),
                  const((1, D)), const(w_up.shape), const(w_down.shape), const((1, D))],
        out_specs=rows(D),
        out_shape=jax.ShapeDtypeStruct((R, D), _F32),
        scratch_shapes=[pltpu.VMEM((tm, SGU_WIDTH), _BF16)],
        compiler_params=pltpu.CompilerParams(
            dimension_semantics=("parallel",), vmem_limit_bytes=VMEM_LIMIT),
        name="merge_mlp",
    )(x2, u2, vn2, ya2, gates2, w_spatial, bsp,
      w_bs.astype(_BF16), w_ba.astype(_BF16), w_out.astype(_BF16), g_post.reshape(1, D),
      g_pre2.reshape(1, D), w_up.astype(_BF16), w_down.astype(_BF16), g_post2.reshape(1, D))


def kernel(x, g_mix_pre, w_in, b_forget, g_sgu, b_sgu, w_spatial, b_spatial, w_branch_sgu,
           w_branch_attn, w_out, g_mix_post, g_ffn_pre, w_up, w_down, g_ffn_post):
    B, S, D = x.shape
    h = x
    for l in range(g_mix_pre.shape[0]):
        u, vn, qt, k, vt, gates, qat, ka, stats = _in_projection(
            h, g_mix_pre[l], w_in[l], b_forget[l], g_sgu[l], b_sgu[l])
        y_attn = _attention(k, ka, vt, qt, qat, stats)
        h = _merge_mlp(h.reshape(B * S, D), u.reshape(B * S, -1), vn.reshape(B * S, -1),
                       y_attn.reshape(B * S, -1), gates.reshape(B * S, -1),
                       w_spatial[l], b_spatial[l], w_branch_sgu[l], w_branch_attn[l], w_out[l],
                       g_mix_post[l], g_ffn_pre[l], w_up[l], w_down[l],
                       g_ffn_post[l]).reshape(B, S, D)
    return h
```

```python
import functools

import jax
import jax.numpy as jnp
import numpy as np
from jax import lax
from jax.experimental import pallas as pl
from jax.experimental.pallas import tpu as pltpu

D_MODEL = 1024
N_HEADS = 8
HEAD_DIM = 64
ATTN_WIDTH = N_HEADS * HEAD_DIM
N_GROUPS = 8
SGU_WIDTH = D_MODEL // 2
CHUNK = 128
D_FF = 4 * D_MODEL
EPS = 1e-6
LOG2E = 1.4426950408889634

LANES = 128
N_SPLIT = 3
BIAS_COLS = N_HEADS * N_SPLIT
PAIR = 2 * HEAD_DIM
N_PAIRS = N_HEADS // 2

ROW_TILE = 512
FF_TILE = 1024
N_STREAMS = 2
TQ = 256
TK = 256
NQ = 4
QG = NQ * TQ
KU = 2
N_STATS = 4
ZERO_EXP2 = 160.0
NORM_SLACK = 2.05
VMEM_LIMIT = 56 * 1024 * 1024

_BF16 = jnp.bfloat16
_F32 = jnp.float32


def _split_bf16(x):
    parts = []
    r = x
    for _ in range(N_SPLIT):
        p = r.astype(_BF16)
        parts.append(p)
        r = r - p.astype(_F32)
    return parts


def _dot(a, b):
    return jnp.dot(a, b, preferred_element_type=_F32)


def _dot_nt(a, b):
    return lax.dot_general(a, b, (((1,), (1,)), ((), ())), preferred_element_type=_F32)


def _inproj_kernel(x_ref, gpre_ref, wzt_ref, wqt_ref, wkt_ref, wvt_ref, wft_ref, wgt_ref,
                   bf_ref, gsgu_ref, bsgu_ref, tri_ref, expand_ref, expand_t_ref,
                   head_rows_ref, piece_rows_ref, *rest, n_cast):
    cast_in = rest[:n_cast]
    (u_ref, vn_ref, qt_ref, k_ref, vt_ref, gates_ref, qat_ref, ka_ref,
     stats_ref) = rest[n_cast:n_cast + 9]
    cast_out = rest[n_cast + 9:2 * n_cast + 9]
    carry_ref, = rest[2 * n_cast + 9:]
    for src, dst in zip(cast_in, cast_out):
        dst[...] = src[...].astype(_BF16)

    @pl.when(pl.program_id(1) == 0)
    def _():
        carry_ref[...] = jnp.zeros_like(carry_ref)
        stats_ref[...] = jnp.zeros_like(stats_ref)

    x = x_ref[0]
    tm = x.shape[0]
    ms = jnp.mean(x * x, axis=-1, keepdims=True)
    xb = ((x * lax.rsqrt(ms + EPS)) * gpre_ref[...]).astype(_BF16)

    lane = lax.broadcasted_iota(jnp.int32, (tm, LANES), 1)

    def pack_pieces(v):
        hi, mid, lo = _split_bf16(v)
        zero = jnp.zeros((), _BF16)
        return jnp.where(lane < N_HEADS, hi,
                         jnp.where(lane < 2 * N_HEADS, mid,
                                   jnp.where(lane < N_SPLIT * N_HEADS, lo, zero)))

    f = _dot_nt(xb, wft_ref[...]) + bf_ref[...]

    z = jax.nn.gelu(_dot_nt(xb, wzt_ref[...]), approximate=True)
    u_ref[0] = z[:, :SGU_WIDTH].astype(_BF16)
    v = z[:, SGU_WIDTH:]
    mu = jnp.mean(v, axis=-1, keepdims=True)
    vc = v - mu
    var = jnp.mean(vc * vc, axis=-1, keepdims=True)
    vn_ref[0] = ((vc * lax.rsqrt(var + EPS)) * gsgu_ref[...] + bsgu_ref[...]).astype(_BF16)

    log_f = jnp.minimum(f, 0.0) - jnp.log(1.0 + jnp.exp(-jnp.abs(f)))
    sums = _dot(tri_ref[...], pack_pieces(log_f))

    qt = _dot_nt(wqt_ref[...], xb) * (HEAD_DIM ** -0.5 * LOG2E)
    qt_ref[0] = qt.astype(_BF16)
    k = _dot_nt(xb, wkt_ref[...])
    k_ref[0] = k.astype(_BF16)
    qn2 = _dot(head_rows_ref[...], (qt * qt).astype(_BF16))
    kn2 = _dot_nt(head_rows_ref[...], (k * k).astype(_BF16))

    total = sums
    for shift in (N_HEADS, 2 * N_HEADS, LANES - N_HEADS, LANES - 2 * N_HEADS):
        total = total + pltpu.roll(sums, shift, 1)
    cum = carry_ref[0:1, :] + total
    carry_ref[0:1, :] = cum[tm - 1:tm, :]
    pieces = pack_pieces(cum * LOG2E)
    spread = _dot(pieces, expand_ref[...])
    spread_t = _dot_nt(expand_t_ref[...], pieces)

    cum_t = _dot_nt(piece_rows_ref[...], pieces)
    tile_lane = lax.broadcasted_iota(jnp.int32, (2 * N_HEADS, LANES), 1)
    first_tile = pl.program_id(1) * (tm // TK)
    stats = [stats_ref[0, s] for s in range(N_STATS)]
    for j in range(tm // TK):
        lo, hi = j * TK, (j + 1) * TK
        cols = (cum_t[:, lo:lo + 1], cum_t[:, hi - 1:hi],
                jnp.max(qn2[:, lo:hi], axis=1, keepdims=True),
                jnp.max(kn2[:, lo:hi], axis=1, keepdims=True))
        stats = [jnp.where(tile_lane == first_tile + j, c, s) for c, s in zip(cols, stats)]
    for s in range(N_STATS):
        stats_ref[0, s] = stats[s]

    vt = _dot_nt(wvt_ref[...], xb).astype(_BF16)
    for h in range(N_HEADS):
        vt_ref[0, h, :HEAD_DIM, :] = vt[h * HEAD_DIM:(h + 1) * HEAD_DIM, :]
        vt_ref[0, h, HEAD_DIM:, :] = jnp.ones((PAIR - HEAD_DIM, tm), _BF16)

    gates_ref[0] = jax.nn.sigmoid(_dot_nt(xb, wgt_ref[...])).astype(_BF16)

    in_a = lane < BIAS_COLS
    in_b = jnp.logical_and(lane >= BIAS_COLS, lane < 2 * BIAS_COLS)
    ka_ref[0] = jnp.where(in_a, -spread, jnp.where(in_b, 1.0, 0.0)).astype(_BF16)
    row = lax.broadcasted_iota(jnp.int32, (LANES, tm), 0)
    in_a = row < BIAS_COLS
    in_b = jnp.logical_and(row >= BIAS_COLS, row < 2 * BIAS_COLS)
    qat_ref[0] = jnp.where(in_a, 1.0, jnp.where(in_b, spread_t, 0.0)).astype(_BF16)


def _head_rows():
    r = np.zeros((2 * N_HEADS, ATTN_WIDTH), np.float32)
    for h in range(N_HEADS):
        r[h, h * HEAD_DIM:(h + 1) * HEAD_DIM] = 1.0
    return r


def _piece_rows():
    r = np.zeros((2 * N_HEADS, LANES), np.float32)
    for h in range(N_HEADS):
        for i in range(N_SPLIT):
            r[h, N_HEADS * i + h] = 1.0
    return r


def _expand_matrix():
    e = np.zeros((LANES, LANES), np.float32)
    for h in range(N_HEADS):
        for i in range(N_SPLIT):
            e[N_HEADS * i + h, N_SPLIT * h + i] = 1.0
            e[N_HEADS * i + h, BIAS_COLS + N_SPLIT * h + i] = 1.0
    return e


_IN_OFFSETS = tuple(int(v) for v in np.cumsum(
    (0, 2 * SGU_WIDTH, ATTN_WIDTH, ATTN_WIDTH, ATTN_WIDTH, N_HEADS, 2 * D_MODEL)))


def _in_projection(x, g_pre, w_in, b_forget, g_sgu, b_sgu, later_weights):
    B, S, D = x.shape
    tm = ROW_TILE
    n_steps = B * (S // tm)
    o = _IN_OFFSETS
    wt = jnp.swapaxes(w_in, 0, 1)
    wzt, wqt, wkt, wvt, wgt = (wt[o[i]:o[i + 1]].astype(_BF16) for i in (0, 1, 2, 3, 5))
    row_pad = ((0, LANES - N_SPLIT * N_HEADS), (0, 0))
    wft = jnp.pad(jnp.tile(wt[o[4]:o[5]], (N_SPLIT, 1)), row_pad).astype(_BF16)
    lane_pad = ((0, 0), (0, LANES - N_SPLIT * N_HEADS))
    bf = jnp.pad(jnp.tile(b_forget.reshape(1, N_HEADS), (1, N_SPLIT)), lane_pad)
    tri = jnp.asarray(np.tril(np.ones((tm, tm), np.float32)), _BF16)
    expand = jnp.asarray(_expand_matrix(), _BF16)
    expand_t = jnp.asarray(_expand_matrix().T, _BF16)
    head_rows = jnp.asarray(_head_rows(), _BF16)
    piece_rows = jnp.asarray(_piece_rows(), _BF16)

    def const(shape):
        return pl.BlockSpec(shape, lambda b, i: (0,) * len(shape))

    def rows(width):
        return pl.BlockSpec((1, tm, width), lambda b, i: (b, i, 0))

    out_shape = (
        jax.ShapeDtypeStruct((B, S, SGU_WIDTH), _BF16),
        jax.ShapeDtypeStruct((B, S, SGU_WIDTH), _BF16),
        jax.ShapeDtypeStruct((B, ATTN_WIDTH, S), _BF16),
        jax.ShapeDtypeStruct((B, S, ATTN_WIDTH), _BF16),
        jax.ShapeDtypeStruct((B, N_HEADS, PAIR, S), _BF16),
        jax.ShapeDtypeStruct((B, S, 2 * D_MODEL), _BF16),
        jax.ShapeDtypeStruct((B, LANES, S), _BF16),
        jax.ShapeDtypeStruct((B, S, LANES), _BF16),
        jax.ShapeDtypeStruct((B, N_STATS, 2 * N_HEADS, LANES), _F32),
    )

    def cols(height):
        return pl.BlockSpec((1, height, tm), lambda b, i: (b, 0, i))

    def row_block(w):
        assert w.shape[0] % (n_steps * 16) == 0
        return pl.BlockSpec((w.shape[0] // n_steps, w.shape[1]),
                            lambda b, i: (b * (S // tm) + i, 0))

    out_specs = (
        rows(SGU_WIDTH), rows(SGU_WIDTH), cols(ATTN_WIDTH), rows(ATTN_WIDTH),
        pl.BlockSpec((1, N_HEADS, PAIR, tm), lambda b, i: (b, 0, 0, i)),
        rows(2 * D_MODEL), cols(LANES), rows(LANES),
        pl.BlockSpec((1, N_STATS, 2 * N_HEADS, LANES), lambda b, i: (b, 0, 0, 0)),
    ) + tuple(row_block(w) for w in later_weights)
    out_shape += tuple(jax.ShapeDtypeStruct(w.shape, _BF16) for w in later_weights)
    return pl.pallas_call(
        functools.partial(_inproj_kernel, n_cast=len(later_weights)),
        grid=(B, S // tm),
        in_specs=[
            rows(D), const((1, D)),
            const(wzt.shape), const(wqt.shape), const(wkt.shape), const(wvt.shape),
            const(wft.shape), const(wgt.shape),
            const((1, LANES)), const((1, SGU_WIDTH)), const((1, SGU_WIDTH)),
            const(tri.shape), const(expand.shape), const(expand_t.shape),
            const(head_rows.shape), const(piece_rows.shape),
        ] + [row_block(w) for w in later_weights],
        out_specs=out_specs,
        out_shape=out_shape,
        scratch_shapes=[pltpu.VMEM((8, LANES), _F32)],
        compiler_params=pltpu.CompilerParams(
            dimension_semantics=("arbitrary", "arbitrary"),
            vmem_limit_bytes=VMEM_LIMIT),
        name="in_projection",
    )(x, g_pre.reshape(1, D), wzt, wqt, wkt, wvt, wft, wgt, bf,
      g_sgu.reshape(1, SGU_WIDTH), b_sgu.reshape(1, SGU_WIDTH), tri, expand, expand_t,
      head_rows, piece_rows, *later_weights)


def _attn_kernel(plan_ref, k_ref, ka_ref, vt_ref, qt_ref, qat_ref, o_ref,
                 qf_ref, st_a, st_b, m_ref, acc_ref):
    batch = pl.program_id(0)
    pair = pl.program_id(1)
    S = k_ref.shape[1]
    row = lax.broadcasted_iota(jnp.int32, (PAIR, 1), 0)
    all_tiles = [(e, t) for t in range(NQ) for e in range(2)]

    def key_tile(k0):
        return jnp.concatenate([k_ref[0, pl.ds(k0, TK), :], ka_ref[0, pl.ds(k0, TK), :]], axis=1)

    def offsets(g, j):
        n, skew = plan_ref[batch, pair, g, 0], plan_ref[batch, pair, g, 1]
        first = (g * NQ - n - skew + j) * TK
        return [pl.multiple_of(first + t * skew * TK, TK) for t in range(NQ)]

    def update(e, t, st, k0, masked):
        cs = slice(t * TQ, (t + 1) * TQ)
        if masked:
            key_i = lax.broadcasted_iota(jnp.int32, (TK, TQ), 0)
            qry_i = lax.broadcasted_iota(jnp.int32, (TK, TQ), 1)
            st = jnp.where(key_i <= qry_i, st, -jnp.inf)
        m = m_ref[e, :, cs]
        m_new = jnp.maximum(m, jnp.max(st, axis=0, keepdims=True))
        p = jnp.exp2(st - m_new)
        alpha = jnp.exp2(m - m_new)
        m_ref[e, :, cs] = m_new
        acc_ref[e, :, cs] = alpha * acc_ref[e, :, cs] + _dot(
            vt_ref[0, e, :, pl.ds(k0, TK)], p.astype(_BF16))

    def step(cur, cur_tiles, cur_offs, nxt, nxt_tiles, nxt_offs):
        for n in range(max(len(cur_tiles), len(nxt_tiles))):
            if n < len(nxt_tiles):
                e, t = nxt_tiles[n]
                nxt[e, t] = _dot(key_tile(nxt_offs[t]), qf_ref[e, :, t * TQ:(t + 1) * TQ])
            if n < len(cur_tiles):
                e, t, masked = cur_tiles[n]
                update(e, t, cur[e, t], cur_offs[t], masked)

    def load_queries(g):
        q0 = pl.multiple_of(g * QG, QG)
        qt = qt_ref[0, :, pl.ds(q0, QG)].astype(_F32)
        qat = qat_ref[0, :, pl.ds(q0, QG)].astype(_F32)
        for e in range(2):
            a0 = N_SPLIT * (2 * pair + e)
            q_mask = jnp.logical_and(row >= HEAD_DIM * e, row < HEAD_DIM * (e + 1))
            a_mask = jnp.logical_or(
                jnp.logical_and(row >= a0, row < a0 + N_SPLIT),
                jnp.logical_and(row >= BIAS_COLS + a0, row < BIAS_COLS + a0 + N_SPLIT))
            qf_ref[e, :PAIR, :] = jnp.where(q_mask, qt, 0.0).astype(_BF16)
            qf_ref[e, PAIR:, :] = jnp.where(a_mask, qat, 0.0).astype(_BF16)

    def reset_state():
        m_ref[...] = jnp.full(m_ref.shape, -jnp.inf, _F32)
        acc_ref[...] = jnp.zeros(acc_ref.shape, _F32)

    full = [(e, t, False) for e, t in all_tiles]
    bufs = (st_a, st_b)
    n_groups = S // QG

    load_queries(0)
    reset_state()
    step(None, [], None, st_a, all_tiles, offsets(0, 0))

    def q_group(g, _):
        q0 = pl.multiple_of(g * QG, QG)
        n = plan_ref[batch, pair, g, 0]
        skew = plan_ref[batch, pair, g, 1]
        g_next = jnp.minimum(g + 1, n_groups - 1)

        def full_steps(i, _):
            for u in range(KU):
                j = i * KU + u
                step(bufs[u % 2], full, offsets(g, j), bufs[(u + 1) % 2], all_tiles,
                     offsets(g, j + 1))
            return 0

        lax.fori_loop(0, n // KU, full_steps, 0)

        @pl.when(skew == 1)
        def _():
            step(st_a, full, offsets(g, n), st_b, all_tiles, offsets(g, n + 1))
            load_queries(g_next)
            step(st_b, [(e, t, True) for e, t in all_tiles], offsets(g, n + 1),
                 st_a, all_tiles, offsets(g_next, 0))

        @pl.when(skew == 0)
        def _():
            for i in range(NQ):
                cur_tiles = [(e, t, t == i) for t in range(i, NQ) for e in range(2)]
                cur_offs = [q0 + i * TK] * NQ
                if i + 1 < NQ:
                    nxt_tiles = [(e, t) for t in range(i + 1, NQ) for e in range(2)]
                    nxt_offs = [q0 + (i + 1) * TK] * NQ
                else:
                    load_queries(g_next)
                    nxt_tiles, nxt_offs = all_tiles, offsets(g_next, 0)
                step(bufs[i % 2], cur_tiles, cur_offs, bufs[(i + 1) % 2], nxt_tiles, nxt_offs)

        out = [acc_ref[e, :HEAD_DIM, :] * (1.0 / acc_ref[e, HEAD_DIM:HEAD_DIM + 1, :])
               for e in range(2)]
        o_ref[0, pl.ds(q0, QG), :] = jnp.concatenate(out, axis=0).T.astype(_BF16)
        reset_state()
        return 0

    lax.fori_loop(0, n_groups, q_group, 0)


def _sweep_plan(stats, n_tiles):
    cum_first, cum_last, qn2, kn2 = (stats[:, s, :N_HEADS, :n_tiles] for s in range(N_STATS))
    B = stats.shape[0]
    n_groups = n_tiles // NQ
    k_norm = jnp.sqrt(jnp.max(kn2, axis=-1, keepdims=True))
    reach = NORM_SLACK * jnp.sqrt(qn2) * k_norm + cum_first
    bound = reach[..., None] - cum_last[:, :, None, :]
    tile = jnp.arange(n_tiles)
    needed = jnp.logical_and(tile[None, :] < tile[:, None], bound >= -ZERO_EXP2)
    w = jnp.sum(needed, axis=-1).astype(jnp.int32)
    w = jnp.max(w.reshape(B, N_PAIRS, 2, n_groups, NQ), axis=2)
    start = jnp.arange(n_groups, dtype=jnp.int32) * NQ
    n_flat = jnp.max(jnp.maximum(w - jnp.arange(NQ, dtype=jnp.int32), 0), axis=-1)
    n_flat = jnp.minimum(((n_flat + KU - 1) // KU) * KU, start)
    n_skew = jnp.maximum(jnp.max(w, axis=-1), 1)
    n_skew = n_skew + (n_skew + 1) % 2
    units_flat = NQ * n_flat + NQ * (NQ + 1) // 2
    units_skew = NQ * n_skew + NQ
    skew = jnp.logical_and(n_skew < start, units_skew < units_flat)
    n = jnp.where(skew, n_skew - 1, n_flat)
    return jnp.stack([n, skew.astype(jnp.int32)], axis=-1)


def _attention(k, ka, vt, qt, qat, stats):
    B, S, _ = k.shape
    assert TQ == TK and KU == 2 and NQ % KU == 0 and S % QG == 0
    plan = _sweep_plan(stats, S // TK)
    pair_rows = pl.BlockSpec((1, S, PAIR), lambda b, j, nb: (b, 0, j))
    return pl.pallas_call(
        _attn_kernel,
        grid_spec=pltpu.PrefetchScalarGridSpec(
            num_scalar_prefetch=1,
            grid=(B, N_PAIRS),
            in_specs=[pair_rows, pl.BlockSpec((1, S, LANES), lambda b, j, nb: (b, 0, 0)),
                      pl.BlockSpec((1, 2, PAIR, S), lambda b, j, nb: (b, j, 0, 0)),
                      pl.BlockSpec((1, PAIR, S), lambda b, j, nb: (b, j, 0)),
                      pl.BlockSpec((1, LANES, S), lambda b, j, nb: (b, 0, 0))],
            out_specs=pair_rows,
            scratch_shapes=[pltpu.VMEM((2, PAIR + LANES, QG), _BF16),
                            pltpu.VMEM((2, NQ, TK, TQ), _F32), pltpu.VMEM((2, NQ, TK, TQ), _F32),
                            pltpu.VMEM((2, 1, QG), _F32), pltpu.VMEM((2, PAIR, QG), _F32)]),
        out_shape=jax.ShapeDtypeStruct((B, S, ATTN_WIDTH), _BF16),
        compiler_params=pltpu.CompilerParams(
            dimension_semantics=("parallel", "parallel"),
            vmem_limit_bytes=VMEM_LIMIT),
        name="forgetting_attention",
    )(plan, k, ka, vt, qt, qat)


def _rms_scale(v, g):
    ms = jnp.mean(v * v, axis=-1, keepdims=True)
    return (v * lax.rsqrt(ms + EPS)) * g


def _merge_mlp_kernel(x_ref, u_ref, vn_ref, ya_ref, gates_ref, wsp_ref, bsp_ref,
                      wbs_ref, wba_ref, wout_ref, gpost_ref,
                      gpre2_ref, wup_ref, wdown_ref, gpost2_ref, o_ref, ysgu_ref):
    tm = x_ref.shape[0]
    row = lax.broadcasted_iota(jnp.int32, (CHUNK, CHUNK), 0)
    col = lax.broadcasted_iota(jnp.int32, (CHUNK, CHUNK), 1)
    lane = lax.broadcasted_iota(jnp.int32, (CHUNK, PAIR), 1)
    ws = [jnp.where(row >= col, wsp_ref[g], 0.0).astype(_BF16) for g in range(N_GROUPS)]
    for c in range(tm // CHUNK):
        r = slice(c * CHUNK, (c + 1) * CHUNK)
        for j in range(N_GROUPS // 2):
            cs = slice(j * PAIR, (j + 1) * PAIR)
            vp = vn_ref[r, cs]
            s = jnp.where(lane < HEAD_DIM, _dot(ws[2 * j], vp), _dot(ws[2 * j + 1], vp))
            s = s + bsp_ref[:, cs]
            ysgu_ref[r, cs] = (u_ref[r, cs].astype(_F32) * s).astype(_BF16)

    blocks = [slice(n * tm // N_STREAMS, (n + 1) * tm // N_STREAMS) for n in range(N_STREAMS)]

    def mix(r):
        a = _dot(ysgu_ref[r, :], wbs_ref[...])
        b = _dot(ya_ref[r, :], wba_ref[...])
        merged = (gates_ref[r, :D_MODEL].astype(_F32) * a
                  + gates_ref[r, D_MODEL:].astype(_F32) * b).astype(_BF16)
        return _dot(merged, wout_ref[...])

    def mlp(xb):
        acc = jnp.zeros(xb.shape, _F32)
        for c in range(D_FF // FF_TILE):
            cs = slice(c * FF_TILE, (c + 1) * FF_TILE)
            hid = jnp.square(jnp.maximum(_dot(xb, wup_ref[:, cs]), 0.0)).astype(_BF16)
            acc = acc + _dot(hid, wdown_ref[cs, :])
        return acc

    mixed = [mix(r) for r in blocks]
    h1 = [x_ref[r, :] + _rms_scale(o, gpost_ref[...]) for r, o in zip(blocks, mixed)]
    xb = [_rms_scale(h, gpre2_ref[...]).astype(_BF16) for h in h1]
    ff = [mlp(v) for v in xb]
    for r, h, f in zip(blocks, h1, ff):
        o_ref[r, :] = h + _rms_scale(f, gpost2_ref[...])


def _merge_mlp(x2, u2, vn2, ya2, gates2, w_spatial, b_spatial, w_bs, w_ba, w_out, g_post,
               g_pre2, w_up, w_down, g_post2):
    R, D = x2.shape
    tm = ROW_TILE
    bsp = jnp.repeat(b_spatial.T, SGU_WIDTH // N_GROUPS, axis=1)

    def const(shape):
        return pl.BlockSpec(shape, lambda i: (0,) * len(shape), pipeline_mode=pl.Buffered(1))

    def rows(width):
        return pl.BlockSpec((tm, width), lambda i: (i, 0))

    return pl.pallas_call(
        _merge_mlp_kernel,
        grid=(R // tm,),
        in_specs=[rows(D), rows(SGU_WIDTH), rows(SGU_WIDTH), rows(ATTN_WIDTH), rows(2 * D),
                  const(w_spatial.shape), const(bsp.shape),
                  const(w_bs.shape), const(w_ba.shape), const(w_out.shape), const((1, D)),
                  const((1, D)), const(w_up.shape), const(w_down.shape), const((1, D))],
        out_specs=rows(D),
        out_shape=jax.ShapeDtypeStruct((R, D), _F32),
        scratch_shapes=[pltpu.VMEM((tm, SGU_WIDTH), _BF16)],
        compiler_params=pltpu.CompilerParams(
            dimension_semantics=("parallel",), vmem_limit_bytes=VMEM_LIMIT),
        name="merge_mlp",
    )(x2, u2, vn2, ya2, gates2, w_spatial, bsp,
      w_bs, w_ba, w_out, g_post.reshape(1, D),
      g_pre2.reshape(1, D), w_up, w_down, g_post2.reshape(1, D))


def kernel(x, g_mix_pre, w_in, b_forget, g_sgu, b_sgu, w_spatial, b_spatial, w_branch_sgu,
           w_branch_attn, w_out, g_mix_post, g_ffn_pre, w_up, w_down, g_ffn_post):
    B, S, D = x.shape
    h = x
    for l in range(g_mix_pre.shape[0]):
        later = (w_branch_sgu[l], w_branch_attn[l], w_out[l], w_up[l], w_down[l])
        u, vn, qt, k, vt, gates, qat, ka, stats, w_bs, w_ba, w_o, w_u, w_d = _in_projection(
            h, g_mix_pre[l], w_in[l], b_forget[l], g_sgu[l], b_sgu[l], later)
        y_attn = _attention(k, ka, vt, qt, qat, stats)
        h = _merge_mlp(h.reshape(B * S, D), u.reshape(B * S, -1), vn.reshape(B * S, -1),
                       y_attn.reshape(B * S, -1), gates.reshape(B * S, -1),
                       w_spatial[l], b_spatial[l], w_bs, w_ba, w_o,
                       g_mix_post[l], g_ffn_pre[l], w_u, w_d,
                       g_ffn_post[l]).reshape(B, S, D)
    return h
```

```python
import functools

import jax
import jax.numpy as jnp
import numpy as np
from jax import lax
from jax.experimental import pallas as pl
from jax.experimental.pallas import tpu as pltpu

D_MODEL = 1024
N_HEADS = 8
HEAD_DIM = 64
ATTN_WIDTH = N_HEADS * HEAD_DIM
N_GROUPS = 8
SGU_WIDTH = D_MODEL // 2
CHUNK = 128
D_FF = 4 * D_MODEL
EPS = 1e-6
LOG2E = 1.4426950408889634

LANES = 128
N_SPLIT = 3
BIAS_COLS = N_HEADS * N_SPLIT
PAIR = 2 * HEAD_DIM
N_PAIRS = N_HEADS // 2

ROW_TILE = 512
FF_TILE = 1024
N_STREAMS = 2
TQ = 256
TK = 256
NQ = 4
QG = NQ * TQ
KU = 2
N_STATS = 4
F_ROWS = 32
ZERO_EXP2 = 160.0
NORM_SLACK = 2.05
VMEM_LIMIT = 56 * 1024 * 1024

_BF16 = jnp.bfloat16
_F32 = jnp.float32


def _split_bf16(x):
    parts = []
    r = x
    for _ in range(N_SPLIT):
        p = r.astype(_BF16)
        parts.append(p)
        r = r - p.astype(_F32)
    return parts


def _dot(a, b):
    return jnp.dot(a, b, preferred_element_type=_F32)


def _dot_nt(a, b):
    return lax.dot_general(a, b, (((1,), (1,)), ((), ())), preferred_element_type=_F32)


def _inproj_kernel(x_ref, gpre_ref, wzt_ref, wqft_ref, wkt_ref, wvt_ref, wgt_ref,
                   bf_ref, gsgu_ref, bsgu_ref, triu_ref, head_rows_ref, *rest, n_cast):
    cast_in = rest[:n_cast]
    (u_ref, vn_ref, qt_ref, k_ref, vt_ref, gates_ref, qat_ref, ka_ref,
     stats_ref) = rest[n_cast:n_cast + 9]
    cast_out = rest[n_cast + 9:2 * n_cast + 9]
    carry_ref, = rest[2 * n_cast + 9:]
    for src, dst in zip(cast_in, cast_out):
        dst[...] = src[...].astype(_BF16)

    @pl.when(pl.program_id(1) == 0)
    def _():
        carry_ref[...] = jnp.zeros_like(carry_ref)
        stats_ref[...] = jnp.zeros_like(stats_ref)

    x = x_ref[0]
    tm = x.shape[0]
    ms = jnp.mean(x * x, axis=-1, keepdims=True)
    xb = ((x * lax.rsqrt(ms + EPS)) * gpre_ref[...]).astype(_BF16)

    qf = _dot_nt(wqft_ref[...], xb)
    qt = qf[:ATTN_WIDTH] * (HEAD_DIM ** -0.5 * LOG2E)
    qt_ref[0] = qt.astype(_BF16)
    f = qf[ATTN_WIDTH:] + bf_ref[...]

    z = jax.nn.gelu(_dot_nt(xb, wzt_ref[...]), approximate=True)
    u_ref[0] = z[:, :SGU_WIDTH].astype(_BF16)
    v = z[:, SGU_WIDTH:]
    mu = jnp.mean(v, axis=-1, keepdims=True)
    vc = v - mu
    var = jnp.mean(vc * vc, axis=-1, keepdims=True)
    vn_ref[0] = ((vc * lax.rsqrt(var + EPS)) * gsgu_ref[...] + bsgu_ref[...]).astype(_BF16)

    k = _dot_nt(xb, wkt_ref[...])
    k_ref[0] = k.astype(_BF16)

    log_f = jnp.minimum(f, 0.0) - jnp.log(1.0 + jnp.exp(-jnp.abs(f)))
    f_row = lax.broadcasted_iota(jnp.int32, (F_ROWS, tm), 0)
    hi, mid, lo = (p.astype(_F32) for p in _split_bf16(log_f))
    pieces = jnp.where(f_row < N_HEADS, hi,
                       jnp.where(f_row < 2 * N_HEADS, mid,
                                 jnp.where(f_row < N_SPLIT * N_HEADS, lo, 0.0)))
    sums = _dot(pieces.astype(_BF16), triu_ref[...])

    half = D_MODEL
    gates_ref[0, :, :half] = jax.nn.sigmoid(_dot_nt(xb, wgt_ref[:half, :])).astype(_BF16)
    qn2 = _dot(head_rows_ref[...], (qt * qt).astype(_BF16))
    kn2 = _dot_nt(head_rows_ref[...], (k * k).astype(_BF16))
    gates_ref[0, :, half:] = jax.nn.sigmoid(_dot_nt(xb, wgt_ref[half:, :])).astype(_BF16)

    cum = carry_ref[:, 0:1] + (sums[:N_HEADS] + sums[N_HEADS:2 * N_HEADS]
                               + sums[2 * N_HEADS:N_SPLIT * N_HEADS])
    carry_ref[...] = jnp.broadcast_to(cum[:, tm - 1:tm], carry_ref.shape)
    cum = cum * LOG2E
    cum_pieces = jnp.concatenate([p.astype(_F32) for p in _split_bf16(cum)], axis=0)
    ones = jnp.ones((BIAS_COLS, tm), _F32)
    rest = jnp.zeros((LANES - 2 * BIAS_COLS, tm), _F32)
    qat_ref[0] = jnp.concatenate([ones, cum_pieces, rest], axis=0).astype(_BF16)
    ka_ref[0] = jnp.concatenate([-cum_pieces, ones, rest], axis=0).T.astype(_BF16)

    tile_lane = lax.broadcasted_iota(jnp.int32, (N_HEADS, LANES), 1)
    first_tile = pl.program_id(1) * (tm // TK)
    stats = [stats_ref[0, s] for s in range(N_STATS)]
    for j in range(tm // TK):
        lo_t, hi_t = j * TK, (j + 1) * TK
        cols = (cum[:, lo_t:lo_t + 1], cum[:, hi_t - 1:hi_t],
                jnp.max(qn2[:N_HEADS, lo_t:hi_t], axis=1, keepdims=True),
                jnp.max(kn2[:N_HEADS, lo_t:hi_t], axis=1, keepdims=True))
        stats = [jnp.where(tile_lane == first_tile + j, c, s) for c, s in zip(cols, stats)]
    for s in range(N_STATS):
        stats_ref[0, s] = stats[s]

    vt = _dot_nt(wvt_ref[...], xb).astype(_BF16)
    for h in range(N_HEADS):
        vt_ref[0, h, :HEAD_DIM, :] = vt[h * HEAD_DIM:(h + 1) * HEAD_DIM, :]
        vt_ref[0, h, HEAD_DIM:, :] = jnp.ones((PAIR - HEAD_DIM, tm), _BF16)


def _head_rows():
    r = np.zeros((2 * N_HEADS, ATTN_WIDTH), np.float32)
    for h in range(N_HEADS):
        r[h, h * HEAD_DIM:(h + 1) * HEAD_DIM] = 1.0
    return r


_IN_OFFSETS = tuple(int(v) for v in np.cumsum(
    (0, 2 * SGU_WIDTH, ATTN_WIDTH, ATTN_WIDTH, ATTN_WIDTH, N_HEADS, 2 * D_MODEL)))


def _in_projection(x, g_pre, w_in, b_forget, g_sgu, b_sgu, later_weights):
    B, S, D = x.shape
    tm = ROW_TILE
    n_steps = B * (S // tm)
    o = _IN_OFFSETS
    wt = jnp.swapaxes(w_in, 0, 1)
    wzt, wqt, wkt, wvt, wgt = (wt[o[i]:o[i + 1]].astype(_BF16) for i in (0, 1, 2, 3, 5))
    row_pad = ((0, F_ROWS - N_SPLIT * N_HEADS), (0, 0))
    wft = jnp.pad(jnp.tile(wt[o[4]:o[5]], (N_SPLIT, 1)), row_pad).astype(_BF16)
    wqft = jnp.concatenate([wqt, wft], axis=0)
    bf = jnp.pad(jnp.tile(b_forget.reshape(N_HEADS, 1), (N_SPLIT, 1)), row_pad)
    triu = jnp.asarray(np.triu(np.ones((tm, tm), np.float32)), _BF16)
    head_rows = jnp.asarray(_head_rows(), _BF16)

    def const(shape):
        return pl.BlockSpec(shape, lambda b, i: (0,) * len(shape))

    def rows(width):
        return pl.BlockSpec((1, tm, width), lambda b, i: (b, i, 0))

    out_shape = (
        jax.ShapeDtypeStruct((B, S, SGU_WIDTH), _BF16),
        jax.ShapeDtypeStruct((B, S, SGU_WIDTH), _BF16),
        jax.ShapeDtypeStruct((B, ATTN_WIDTH, S), _BF16),
        jax.ShapeDtypeStruct((B, S, ATTN_WIDTH), _BF16),
        jax.ShapeDtypeStruct((B, N_HEADS, PAIR, S), _BF16),
        jax.ShapeDtypeStruct((B, S, 2 * D_MODEL), _BF16),
        jax.ShapeDtypeStruct((B, LANES, S), _BF16),
        jax.ShapeDtypeStruct((B, S, LANES), _BF16),
        jax.ShapeDtypeStruct((B, N_STATS, N_HEADS, LANES), _F32),
    )

    def cols(height):
        return pl.BlockSpec((1, height, tm), lambda b, i: (b, 0, i))

    def row_block(w):
        assert w.shape[0] % (n_steps * 16) == 0
        return pl.BlockSpec((w.shape[0] // n_steps, w.shape[1]),
                            lambda b, i: (b * (S // tm) + i, 0))

    out_specs = (
        rows(SGU_WIDTH), rows(SGU_WIDTH), cols(ATTN_WIDTH), rows(ATTN_WIDTH),
        pl.BlockSpec((1, N_HEADS, PAIR, tm), lambda b, i: (b, 0, 0, i)),
        rows(2 * D_MODEL), cols(LANES), rows(LANES),
        pl.BlockSpec((1, N_STATS, N_HEADS, LANES), lambda b, i: (b, 0, 0, 0)),
    ) + tuple(row_block(w) for w in later_weights)
    out_shape += tuple(jax.ShapeDtypeStruct(w.shape, _BF16) for w in later_weights)
    return pl.pallas_call(
        functools.partial(_inproj_kernel, n_cast=len(later_weights)),
        grid=(B, S // tm),
        in_specs=[
            rows(D), const((1, D)),
            const(wzt.shape), const(wqft.shape), const(wkt.shape), const(wvt.shape),
            const(wgt.shape),
            const(bf.shape), const((1, SGU_WIDTH)), const((1, SGU_WIDTH)),
            const(triu.shape), const(head_rows.shape),
        ] + [row_block(w) for w in later_weights],
        out_specs=out_specs,
        out_shape=out_shape,
        scratch_shapes=[pltpu.VMEM((N_HEADS, LANES), _F32)],
        compiler_params=pltpu.CompilerParams(
            dimension_semantics=("arbitrary", "arbitrary"),
            vmem_limit_bytes=VMEM_LIMIT),
        name="in_projection",
    )(x, g_pre.reshape(1, D), wzt, wqft, wkt, wvt, wgt, bf,
      g_sgu.reshape(1, SGU_WIDTH), b_sgu.reshape(1, SGU_WIDTH), triu, head_rows,
      *later_weights)


def _attn_kernel(plan_ref, k_ref, ka_ref, vt_ref, qt_ref, qat_ref, o_ref,
                 qf_ref, st_a, st_b, m_ref, acc_ref):
    batch = pl.program_id(0)
    pair = pl.program_id(1)
    S = k_ref.shape[1]
    row = lax.broadcasted_iota(jnp.int32, (PAIR, 1), 0)
    all_tiles = [(e, t) for t in range(NQ) for e in range(2)]

    def key_tile(k0):
        return jnp.concatenate([k_ref[0, pl.ds(k0, TK), :], ka_ref[0, pl.ds(k0, TK), :]], axis=1)

    def offsets(g, j):
        n, skew = plan_ref[batch, pair, g, 0], plan_ref[batch, pair, g, 1]
        first = (g * NQ - n - skew + j) * TK
        return [pl.multiple_of(first + t * skew * TK, TK) for t in range(NQ)]

    def update(e, t, st, k0, masked):
        cs = slice(t * TQ, (t + 1) * TQ)
        if masked:
            key_i = lax.broadcasted_iota(jnp.int32, (TK, TQ), 0)
            qry_i = lax.broadcasted_iota(jnp.int32, (TK, TQ), 1)
            st = jnp.where(key_i <= qry_i, st, -jnp.inf)
        m = m_ref[e, :, cs]
        m_new = jnp.maximum(m, jnp.max(st, axis=0, keepdims=True))
        p = jnp.exp2(st - m_new)
        alpha = jnp.exp2(m - m_new)
        m_ref[e, :, cs] = m_new
        acc_ref[e, :, cs] = alpha * acc_ref[e, :, cs] + _dot(
            vt_ref[0, e, :, pl.ds(k0, TK)], p.astype(_BF16))

    def step(cur, cur_tiles, cur_offs, nxt, nxt_tiles, nxt_offs):
        for n in range(max(len(cur_tiles), len(nxt_tiles))):
            if n < len(nxt_tiles):
                e, t = nxt_tiles[n]
                nxt[e, t] = _dot(key_tile(nxt_offs[t]), qf_ref[e, :, t * TQ:(t + 1) * TQ])
            if n < len(cur_tiles):
                e, t, masked = cur_tiles[n]
                update(e, t, cur[e, t], cur_offs[t], masked)

    def load_queries(g):
        q0 = pl.multiple_of(g * QG, QG)
        qt = qt_ref[0, :, pl.ds(q0, QG)].astype(_F32)
        qat = qat_ref[0, :, pl.ds(q0, QG)].astype(_F32)
        for e in range(2):
            q_mask = jnp.logical_and(row >= HEAD_DIM * e, row < HEAD_DIM * (e + 1))
            a_mask = jnp.logical_and(row < 2 * BIAS_COLS,
                                     jnp.bitwise_and(row, N_HEADS - 1) == 2 * pair + e)
            qf_ref[e, :PAIR, :] = jnp.where(q_mask, qt, 0.0).astype(_BF16)
            qf_ref[e, PAIR:, :] = jnp.where(a_mask, qat, 0.0).astype(_BF16)

    def reset_state():
        m_ref[...] = jnp.full(m_ref.shape, -jnp.inf, _F32)
        acc_ref[...] = jnp.zeros(acc_ref.shape, _F32)

    full = [(e, t, False) for e, t in all_tiles]
    bufs = (st_a, st_b)
    n_groups = S // QG

    load_queries(0)
    reset_state()
    step(None, [], None, st_a, all_tiles, offsets(0, 0))

    def q_group(g, _):
        q0 = pl.multiple_of(g * QG, QG)
        n = plan_ref[batch, pair, g, 0]
        skew = plan_ref[batch, pair, g, 1]
        g_next = jnp.minimum(g + 1, n_groups - 1)

        def full_steps(i, _):
            for u in range(KU):
                j = i * KU + u
                step(bufs[u % 2], full, offsets(g, j), bufs[(u + 1) % 2], all_tiles,
                     offsets(g, j + 1))
            return 0

        lax.fori_loop(0, n // KU, full_steps, 0)

        @pl.when(skew == 1)
        def _():
            step(st_a, full, offsets(g, n), st_b, all_tiles, offsets(g, n + 1))
            load_queries(g_next)
            step(st_b, [(e, t, True) for e, t in all_tiles], offsets(g, n + 1),
                 st_a, all_tiles, offsets(g_next, 0))

        @pl.when(skew == 0)
        def _():
            for i in range(NQ):
                cur_tiles = [(e, t, t == i) for t in range(i, NQ) for e in range(2)]
                cur_offs = [q0 + i * TK] * NQ
                if i + 1 < NQ:
                    nxt_tiles = [(e, t) for t in range(i + 1, NQ) for e in range(2)]
                    nxt_offs = [q0 + (i + 1) * TK] * NQ
                else:
                    load_queries(g_next)
                    nxt_tiles, nxt_offs = all_tiles, offsets(g_next, 0)
                step(bufs[i % 2], cur_tiles, cur_offs, bufs[(i + 1) % 2], nxt_tiles, nxt_offs)

        out = [acc_ref[e, :HEAD_DIM, :] * (1.0 / acc_ref[e, HEAD_DIM:HEAD_DIM + 1, :])
               for e in range(2)]
        o_ref[0, pl.ds(q0, QG), :] = jnp.concatenate(out, axis=0).T.astype(_BF16)
        reset_state()
        return 0

    lax.fori_loop(0, n_groups, q_group, 0)


def _sweep_plan(stats, n_tiles):
    cum_first, cum_last, qn2, kn2 = (stats[:, s, :, :n_tiles] for s in range(N_STATS))
    B = stats.shape[0]
    n_groups = n_tiles // NQ
    k_norm = jnp.sqrt(jnp.max(kn2, axis=-1, keepdims=True))
    reach = NORM_SLACK * jnp.sqrt(qn2) * k_norm + cum_first
    bound = reach[..., None] - cum_last[:, :, None, :]
    tile = jnp.arange(n_tiles)
    needed = jnp.logical_and(tile[None, :] < tile[:, None], bound >= -ZERO_EXP2)
    w = jnp.sum(needed, axis=-1).astype(jnp.int32)
    w = jnp.max(w.reshape(B, N_PAIRS, 2, n_groups, NQ), axis=2)
    start = jnp.arange(n_groups, dtype=jnp.int32) * NQ
    n_flat = jnp.max(jnp.maximum(w - jnp.arange(NQ, dtype=jnp.int32), 0), axis=-1)
    n_flat = jnp.minimum(((n_flat + KU - 1) // KU) * KU, start)
    n_skew = jnp.maximum(jnp.max(w, axis=-1), 1)
    n_skew = n_skew + (n_skew + 1) % 2
    units_flat = NQ * n_flat + NQ * (NQ + 1) // 2
    units_skew = NQ * n_skew + NQ
    skew = jnp.logical_and(n_skew < start, units_skew < units_flat)
    n = jnp.where(skew, n_skew - 1, n_flat)
    return jnp.stack([n, skew.astype(jnp.int32)], axis=-1)


def _attention(k, ka, vt, qt, qat, stats):
    B, S, _ = k.shape
    assert TQ == TK and KU == 2 and NQ % KU == 0 and S % QG == 0
    plan = _sweep_plan(stats, S // TK)
    pair_rows = pl.BlockSpec((1, S, PAIR), lambda b, j, nb: (b, 0, j))
    return pl.pallas_call(
        _attn_kernel,
        grid_spec=pltpu.PrefetchScalarGridSpec(
            num_scalar_prefetch=1,
            grid=(B, N_PAIRS),
            in_specs=[pair_rows, pl.BlockSpec((1, S, LANES), lambda b, j, nb: (b, 0, 0)),
                      pl.BlockSpec((1, 2, PAIR, S), lambda b, j, nb: (b, j, 0, 0)),
                      pl.BlockSpec((1, PAIR, S), lambda b, j, nb: (b, j, 0)),
                      pl.BlockSpec((1, LANES, S), lambda b, j, nb: (b, 0, 0))],
            out_specs=pair_rows,
            scratch_shapes=[pltpu.VMEM((2, PAIR + LANES, QG), _BF16),
                            pltpu.VMEM((2, NQ, TK, TQ), _F32), pltpu.VMEM((2, NQ, TK, TQ), _F32),
                            pltpu.VMEM((2, 1, QG), _F32), pltpu.VMEM((2, PAIR, QG), _F32)]),
        out_shape=jax.ShapeDtypeStruct((B, S, ATTN_WIDTH), _BF16),
        compiler_params=pltpu.CompilerParams(
            dimension_semantics=("parallel", "parallel"),
            vmem_limit_bytes=VMEM_LIMIT),
        name="forgetting_attention",
    )(plan, k, ka, vt, qt, qat)


def _rms_scale(v, g):
    ms = jnp.mean(v * v, axis=-1, keepdims=True)
    return (v * lax.rsqrt(ms + EPS)) * g


def _merge_mlp_kernel(x_ref, u_ref, vn_ref, ya_ref, gates_ref, wsp_ref, bsp_ref,
                      wbs_ref, wba_ref, wout_ref, gpost_ref,
                      gpre2_ref, wup_ref, wdown_ref, gpost2_ref, o_ref, ysgu_ref):
    tm = x_ref.shape[0]
    row = lax.broadcasted_iota(jnp.int32, (CHUNK, CHUNK), 0)
    col = lax.broadcasted_iota(jnp.int32, (CHUNK, CHUNK), 1)
    lane = lax.broadcasted_iota(jnp.int32, (CHUNK, PAIR), 1)
    ws = [jnp.where(row >= col, wsp_ref[g], 0.0).astype(_BF16) for g in range(N_GROUPS)]
    for c in range(tm // CHUNK):
        r = slice(c * CHUNK, (c + 1) * CHUNK)
        for j in range(N_GROUPS // 2):
            cs = slice(j * PAIR, (j + 1) * PAIR)
            vp = vn_ref[r, cs]
            s = jnp.where(lane < HEAD_DIM, _dot(ws[2 * j], vp), _dot(ws[2 * j + 1], vp))
            s = s + bsp_ref[:, cs]
            ysgu_ref[r, cs] = (u_ref[r, cs].astype(_F32) * s).astype(_BF16)

    blocks = [slice(n * tm // N_STREAMS, (n + 1) * tm // N_STREAMS) for n in range(N_STREAMS)]

    def mix(r):
        a = _dot(ysgu_ref[r, :], wbs_ref[...])
        b = _dot(ya_ref[r, :], wba_ref[...])
        merged = (gates_ref[r, :D_MODEL].astype(_F32) * a
                  + gates_ref[r, D_MODEL:].astype(_F32) * b).astype(_BF16)
        return _dot(merged, wout_ref[...])

    def mlp(xb):
        acc = jnp.zeros(xb.shape, _F32)
        for c in range(D_FF // FF_TILE):
            cs = slice(c * FF_TILE, (c + 1) * FF_TILE)
            hid = jnp.square(jnp.maximum(_dot(xb, wup_ref[:, cs]), 0.0)).astype(_BF16)
            acc = acc + _dot(hid, wdown_ref[cs, :])
        return acc

    mixed = [mix(r) for r in blocks]
    h1 = [x_ref[r, :] + _rms_scale(o, gpost_ref[...]) for r, o in zip(blocks, mixed)]
    xb = [_rms_scale(h, gpre2_ref[...]).astype(_BF16) for h in h1]
    ff = [mlp(v) for v in xb]
    for r, h, f in zip(blocks, h1, ff):
        o_ref[r, :] = h + _rms_scale(f, gpost2_ref[...])


def _merge_mlp(x2, u2, vn2, ya2, gates2, w_spatial, b_spatial, w_bs, w_ba, w_out, g_post,
               g_pre2, w_up, w_down, g_post2):
    R, D = x2.shape
    tm = ROW_TILE
    bsp = jnp.repeat(b_spatial.T, SGU_WIDTH // N_GROUPS, axis=1)

    def const(shape):
        return pl.BlockSpec(shape, lambda i: (0,) * len(shape), pipeline_mode=pl.Buffered(1))

    def rows(width):
        return pl.BlockSpec((tm, width), lambda i: (i, 0))

    return pl.pallas_call(
        _merge_mlp_kernel,
        grid=(R // tm,),
        in_specs=[rows(D), rows(SGU_WIDTH), rows(SGU_WIDTH), rows(ATTN_WIDTH), rows(2 * D),
                  const(w_spatial.shape), const(bsp.shape),
                  const(w_bs.shape), const(w_ba.shape), const(w_out.shape), const((1, D)),
                  const((1, D)), const(w_up.shape), const(w_down.shape), const((1, D))],
        out_specs=rows(D),
        out_shape=jax.ShapeDtypeStruct((R, D), _F32),
        scratch_shapes=[pltpu.VMEM((tm, SGU_WIDTH), _BF16)],
        compiler_params=pltpu.CompilerParams(
            dimension_semantics=("parallel",), vmem_limit_bytes=VMEM_LIMIT),
        name="merge_mlp",
    )(x2, u2, vn2, ya2, gates2, w_spatial, bsp,
      w_bs, w_ba, w_out, g_post.reshape(1, D),
      g_pre2.reshape(1, D), w_up, w_down, g_post2.reshape(1, D))


def kernel(x, g_mix_pre, w_in, b_forget, g_sgu, b_sgu, w_spatial, b_spatial, w_branch_sgu,
           w_branch_attn, w_out, g_mix_post, g_ffn_pre, w_up, w_down, g_ffn_post):
    B, S, D = x.shape
    h = x
    for l in range(g_mix_pre.shape[0]):
        later = (w_branch_sgu[l], w_branch_attn[l], w_out[l], w_up[l], w_down[l])
        u, vn, qt, k, vt, gates, qat, ka, stats, w_bs, w_ba, w_o, w_u, w_d = _in_projection(
            h, g_mix_pre[l], w_in[l], b_forget[l], g_sgu[l], b_sgu[l], later)
        y_attn = _attention(k, ka, vt, qt, qat, stats)
        h = _merge_mlp(h.reshape(B * S, D), u.reshape(B * S, -1), vn.reshape(B * S, -1),
                       y_attn.reshape(B * S, -1), gates.reshape(B * S, -1),
                       w_spatial[l], b_spatial[l], w_bs, w_ba, w_o,
                       g_mix_post[l], g_ffn_pre[l], w_u, w_d,
                       g_ffn_post[l]).reshape(B, S, D)
    return h
```

```python
import functools

import jax
import jax.numpy as jnp
import numpy as np
from jax import lax
from jax.experimental import pallas as pl
from jax.experimental.pallas import tpu as pltpu

D_MODEL = 1024
N_HEADS = 8
HEAD_DIM = 64
ATTN_WIDTH = N_HEADS * HEAD_DIM
N_GROUPS = 8
SGU_WIDTH = D_MODEL // 2
CHUNK = 128
D_FF = 4 * D_MODEL
EPS = 1e-6
LOG2E = 1.4426950408889634

LANES = 128
N_SPLIT = 3
BIAS_COLS = N_HEADS * N_SPLIT
PAIR = 2 * HEAD_DIM
N_PAIRS = N_HEADS // 2

ROW_TILE = 512
FF_TILE = 1024
N_STREAMS = 2
TQ = 256
TK = 256
NQ = 4
QG = NQ * TQ
KU = 2
UNROLLED_SKEW_STEPS = (4, 6)
N_STATS = 4
ZERO_EXP2 = 160.0
NORM_SLACK = 2.05
VMEM_LIMIT = 56 * 1024 * 1024

_BF16 = jnp.bfloat16
_F32 = jnp.float32


def _split_bf16(x):
    parts = []
    r = x
    for _ in range(N_SPLIT):
        p = r.astype(_BF16)
        parts.append(p)
        r = r - p.astype(_F32)
    return parts


def _dot(a, b):
    return jnp.dot(a, b, preferred_element_type=_F32)


def _dot_nt(a, b):
    return lax.dot_general(a, b, (((1,), (1,)), ((), ())), preferred_element_type=_F32)


def _inproj_kernel(x_ref, gpre_ref, wzt_ref, wqt_ref, wkt_ref, wvt_ref, wft_ref, wgt_ref,
                   bf_ref, gsgu_ref, bsgu_ref, tri_ref, expand_ref, expand_t_ref,
                   head_rows_ref, piece_rows_ref, *rest, n_cast):
    cast_in = rest[:n_cast]
    (u_ref, vn_ref, qt_ref, k_ref, vt_ref, gates_ref, qat_ref, ka_ref,
     stats_ref) = rest[n_cast:n_cast + 9]
    cast_out = rest[n_cast + 9:2 * n_cast + 9]
    carry_ref, = rest[2 * n_cast + 9:]
    for src, dst in zip(cast_in, cast_out):
        dst[...] = src[...].astype(_BF16)

    @pl.when(pl.program_id(1) == 0)
    def _():
        carry_ref[...] = jnp.zeros_like(carry_ref)
        stats_ref[...] = jnp.zeros_like(stats_ref)

    x = x_ref[0]
    tm = x.shape[0]
    ms = jnp.mean(x * x, axis=-1, keepdims=True)
    xb = ((x * lax.rsqrt(ms + EPS)) * gpre_ref[...]).astype(_BF16)

    lane = lax.broadcasted_iota(jnp.int32, (tm, LANES), 1)

    def pack_pieces(v):
        hi, mid, lo = _split_bf16(v)
        zero = jnp.zeros((), _BF16)
        return jnp.where(lane < N_HEADS, hi,
                         jnp.where(lane < 2 * N_HEADS, mid,
                                   jnp.where(lane < N_SPLIT * N_HEADS, lo, zero)))

    f = _dot_nt(xb, wft_ref[...]) + bf_ref[...]

    z = jax.nn.gelu(_dot_nt(xb, wzt_ref[...]), approximate=True)
    u_ref[0] = z[:, :SGU_WIDTH].astype(_BF16)
    v = z[:, SGU_WIDTH:]
    mu = jnp.mean(v, axis=-1, keepdims=True)
    vc = v - mu
    var = jnp.mean(vc * vc, axis=-1, keepdims=True)
    vn_ref[0] = ((vc * lax.rsqrt(var + EPS)) * gsgu_ref[...] + bsgu_ref[...]).astype(_BF16)

    log_f = jnp.minimum(f, 0.0) - jnp.log(1.0 + jnp.exp(-jnp.abs(f)))
    sums = _dot(tri_ref[...], pack_pieces(log_f))

    qt = _dot_nt(wqt_ref[...], xb) * (HEAD_DIM ** -0.5 * LOG2E)
    qt_ref[0] = qt.astype(_BF16)
    k = _dot_nt(xb, wkt_ref[...])
    k_ref[0] = k.astype(_BF16)
    qn2 = _dot(head_rows_ref[...], (qt * qt).astype(_BF16))
    kn2 = _dot_nt(head_rows_ref[...], (k * k).astype(_BF16))

    total = sums
    for shift in (N_HEADS, 2 * N_HEADS, LANES - N_HEADS, LANES - 2 * N_HEADS):
        total = total + pltpu.roll(sums, shift, 1)
    cum = carry_ref[0:1, :] + total
    carry_ref[0:1, :] = cum[tm - 1:tm, :]
    pieces = pack_pieces(cum * LOG2E)
    spread = _dot(pieces, expand_ref[...])
    spread_t = _dot_nt(expand_t_ref[...], pieces)

    cum_t = _dot_nt(piece_rows_ref[...], pieces)
    tile_lane = lax.broadcasted_iota(jnp.int32, (2 * N_HEADS, LANES), 1)
    first_tile = pl.program_id(1) * (tm // TK)
    stats = [stats_ref[0, s] for s in range(N_STATS)]
    for j in range(tm // TK):
        lo, hi = j * TK, (j + 1) * TK
        cols = (cum_t[:, lo:lo + 1], cum_t[:, hi - 1:hi],
                jnp.max(qn2[:, lo:hi], axis=1, keepdims=True),
                jnp.max(kn2[:, lo:hi], axis=1, keepdims=True))
        stats = [jnp.where(tile_lane == first_tile + j, c, s) for c, s in zip(cols, stats)]
    for s in range(N_STATS):
        stats_ref[0, s] = stats[s]

    vt = _dot_nt(wvt_ref[...], xb).astype(_BF16)
    for h in range(N_HEADS):
        vt_ref[0, h, :HEAD_DIM, :] = vt[h * HEAD_DIM:(h + 1) * HEAD_DIM, :]
        vt_ref[0, h, HEAD_DIM:, :] = jnp.ones((PAIR - HEAD_DIM, tm), _BF16)

    gates_ref[0] = jax.nn.sigmoid(_dot_nt(xb, wgt_ref[...])).astype(_BF16)

    in_a = lane < BIAS_COLS
    in_b = jnp.logical_and(lane >= BIAS_COLS, lane < 2 * BIAS_COLS)
    ka_ref[0] = jnp.where(in_a, -spread, jnp.where(in_b, 1.0, 0.0)).astype(_BF16)
    row = lax.broadcasted_iota(jnp.int32, (LANES, tm), 0)
    in_a = row < BIAS_COLS
    in_b = jnp.logical_and(row >= BIAS_COLS, row < 2 * BIAS_COLS)
    qat_ref[0] = jnp.where(in_a, 1.0, jnp.where(in_b, spread_t, 0.0)).astype(_BF16)


def _head_rows():
    r = np.zeros((2 * N_HEADS, ATTN_WIDTH), np.float32)
    for h in range(N_HEADS):
        r[h, h * HEAD_DIM:(h + 1) * HEAD_DIM] = 1.0
    return r


def _piece_rows():
    r = np.zeros((2 * N_HEADS, LANES), np.float32)
    for h in range(N_HEADS):
        for i in range(N_SPLIT):
            r[h, N_HEADS * i + h] = 1.0
    return r


def _expand_matrix():
    e = np.zeros((LANES, LANES), np.float32)
    for h in range(N_HEADS):
        for i in range(N_SPLIT):
            e[N_HEADS * i + h, N_SPLIT * h + i] = 1.0
            e[N_HEADS * i + h, BIAS_COLS + N_SPLIT * h + i] = 1.0
    return e


_IN_OFFSETS = tuple(int(v) for v in np.cumsum(
    (0, 2 * SGU_WIDTH, ATTN_WIDTH, ATTN_WIDTH, ATTN_WIDTH, N_HEADS, 2 * D_MODEL)))


def _in_projection(x, g_pre, w_in, b_forget, g_sgu, b_sgu, later_weights):
    B, S, D = x.shape
    tm = ROW_TILE
    n_steps = B * (S // tm)
    o = _IN_OFFSETS
    wt = jnp.swapaxes(w_in, 0, 1)
    wzt, wqt, wkt, wvt, wgt = (wt[o[i]:o[i + 1]].astype(_BF16) for i in (0, 1, 2, 3, 5))
    row_pad = ((0, LANES - N_SPLIT * N_HEADS), (0, 0))
    wft = jnp.pad(jnp.tile(wt[o[4]:o[5]], (N_SPLIT, 1)), row_pad).astype(_BF16)
    lane_pad = ((0, 0), (0, LANES - N_SPLIT * N_HEADS))
    bf = jnp.pad(jnp.tile(b_forget.reshape(1, N_HEADS), (1, N_SPLIT)), lane_pad)
    tri = jnp.asarray(np.tril(np.ones((tm, tm), np.float32)), _BF16)
    expand = jnp.asarray(_expand_matrix(), _BF16)
    expand_t = jnp.asarray(_expand_matrix().T, _BF16)
    head_rows = jnp.asarray(_head_rows(), _BF16)
    piece_rows = jnp.asarray(_piece_rows(), _BF16)

    def const(shape):
        return pl.BlockSpec(shape, lambda b, i: (0,) * len(shape))

    def rows(width):
        return pl.BlockSpec((1, tm, width), lambda b, i: (b, i, 0))

    out_shape = (
        jax.ShapeDtypeStruct((B, S, SGU_WIDTH), _BF16),
        jax.ShapeDtypeStruct((B, S, SGU_WIDTH), _BF16),
        jax.ShapeDtypeStruct((B, ATTN_WIDTH, S), _BF16),
        jax.ShapeDtypeStruct((B, S, ATTN_WIDTH), _BF16),
        jax.ShapeDtypeStruct((B, N_HEADS, PAIR, S), _BF16),
        jax.ShapeDtypeStruct((B, S, 2 * D_MODEL), _BF16),
        jax.ShapeDtypeStruct((B, LANES, S), _BF16),
        jax.ShapeDtypeStruct((B, S, LANES), _BF16),
        jax.ShapeDtypeStruct((B, N_STATS, 2 * N_HEADS, LANES), _F32),
    )

    def cols(height):
        return pl.BlockSpec((1, height, tm), lambda b, i: (b, 0, i))

    def row_block(w):
        assert w.shape[0] % (n_steps * 16) == 0
        return pl.BlockSpec((w.shape[0] // n_steps, w.shape[1]),
                            lambda b, i: (b * (S // tm) + i, 0))

    out_specs = (
        rows(SGU_WIDTH), rows(SGU_WIDTH), cols(ATTN_WIDTH), rows(ATTN_WIDTH),
        pl.BlockSpec((1, N_HEADS, PAIR, tm), lambda b, i: (b, 0, 0, i)),
        rows(2 * D_MODEL), cols(LANES), rows(LANES),
        pl.BlockSpec((1, N_STATS, 2 * N_HEADS, LANES), lambda b, i: (b, 0, 0, 0)),
    ) + tuple(row_block(w) for w in later_weights)
    out_shape += tuple(jax.ShapeDtypeStruct(w.shape, _BF16) for w in later_weights)
    return pl.pallas_call(
        functools.partial(_inproj_kernel, n_cast=len(later_weights)),
        grid=(B, S // tm),
        in_specs=[
            rows(D), const((1, D)),
            const(wzt.shape), const(wqt.shape), const(wkt.shape), const(wvt.shape),
            const(wft.shape), const(wgt.shape),
            const((1, LANES)), const((1, SGU_WIDTH)), const((1, SGU_WIDTH)),
            const(tri.shape), const(expand.shape), const(expand_t.shape),
            const(head_rows.shape), const(piece_rows.shape),
        ] + [row_block(w) for w in later_weights],
        out_specs=out_specs,
        out_shape=out_shape,
        scratch_shapes=[pltpu.VMEM((8, LANES), _F32)],
        compiler_params=pltpu.CompilerParams(
            dimension_semantics=("arbitrary", "arbitrary"),
            vmem_limit_bytes=VMEM_LIMIT),
        name="in_projection",
    )(x, g_pre.reshape(1, D), wzt, wqt, wkt, wvt, wft, wgt, bf,
      g_sgu.reshape(1, SGU_WIDTH), b_sgu.reshape(1, SGU_WIDTH), tri, expand, expand_t,
      head_rows, piece_rows, *later_weights)


def _attn_kernel(plan_ref, k_ref, ka_ref, vt_ref, qt_ref, qat_ref, o_ref,
                 qf_ref, st_a, st_b, m_ref, acc_ref):
    batch = pl.program_id(0)
    pair = pl.program_id(1)
    S = k_ref.shape[1]
    row = lax.broadcasted_iota(jnp.int32, (PAIR, 1), 0)
    all_tiles = [(e, t) for t in range(NQ) for e in range(2)]

    def key_tile(k0):
        return jnp.concatenate([k_ref[0, pl.ds(k0, TK), :], ka_ref[0, pl.ds(k0, TK), :]], axis=1)

    def offsets(g, j):
        n, skew = plan_ref[batch, pair, g, 0], plan_ref[batch, pair, g, 1]
        first = (g * NQ - n - skew + j) * TK
        return [pl.multiple_of(first + t * skew * TK, TK) for t in range(NQ)]

    def update(e, t, st, k0, masked):
        cs = slice(t * TQ, (t + 1) * TQ)
        if masked:
            key_i = lax.broadcasted_iota(jnp.int32, (TK, TQ), 0)
            qry_i = lax.broadcasted_iota(jnp.int32, (TK, TQ), 1)
            st = jnp.where(key_i <= qry_i, st, -jnp.inf)
        m = m_ref[e, :, cs]
        m_new = jnp.maximum(m, jnp.max(st, axis=0, keepdims=True))
        p = jnp.exp2(st - m_new)
        alpha = jnp.exp2(m - m_new)
        m_ref[e, :, cs] = m_new
        acc_ref[e, :, cs] = alpha * acc_ref[e, :, cs] + _dot(
            vt_ref[0, e, :, pl.ds(k0, TK)], p.astype(_BF16))

    def step(cur, cur_tiles, cur_offs, nxt, nxt_tiles, nxt_offs):
        for n in range(max(len(cur_tiles), len(nxt_tiles))):
            if n < len(nxt_tiles):
                e, t = nxt_tiles[n]
                nxt[e, t] = _dot(key_tile(nxt_offs[t]), qf_ref[e, :, t * TQ:(t + 1) * TQ])
            if n < len(cur_tiles):
                e, t, masked = cur_tiles[n]
                update(e, t, cur[e, t], cur_offs[t], masked)

    def load_queries(g):
        q0 = pl.multiple_of(g * QG, QG)
        qt = qt_ref[0, :, pl.ds(q0, QG)].astype(_F32)
        qat = qat_ref[0, :, pl.ds(q0, QG)].astype(_F32)
        for e in range(2):
            a0 = N_SPLIT * (2 * pair + e)
            q_mask = jnp.logical_and(row >= HEAD_DIM * e, row < HEAD_DIM * (e + 1))
            a_mask = jnp.logical_or(
                jnp.logical_and(row >= a0, row < a0 + N_SPLIT),
                jnp.logical_and(row >= BIAS_COLS + a0, row < BIAS_COLS + a0 + N_SPLIT))
            qf_ref[e, :PAIR, :] = jnp.where(q_mask, qt, 0.0).astype(_BF16)
            qf_ref[e, PAIR:, :] = jnp.where(a_mask, qat, 0.0).astype(_BF16)

    def reset_state():
        m_ref[...] = jnp.full(m_ref.shape, -jnp.inf, _F32)
        acc_ref[...] = jnp.zeros(acc_ref.shape, _F32)

    full = [(e, t, False) for e, t in all_tiles]
    bufs = (st_a, st_b)
    n_groups = S // QG

    load_queries(0)
    reset_state()
    step(None, [], None, st_a, all_tiles, offsets(0, 0))

    def q_group(g, _):
        q0 = pl.multiple_of(g * QG, QG)
        n = plan_ref[batch, pair, g, 0]
        skew = plan_ref[batch, pair, g, 1]
        g_next = jnp.minimum(g + 1, n_groups - 1)

        def full_steps(i, _):
            for u in range(KU):
                j = i * KU + u
                step(bufs[u % 2], full, offsets(g, j), bufs[(u + 1) % 2], all_tiles,
                     offsets(g, j + 1))
            return 0

        def skew_tail(j):
            step(st_a, full, offsets(g, j), st_b, all_tiles, offsets(g, j + 1))
            load_queries(g_next)
            step(st_b, [(e, t, True) for e, t in all_tiles], offsets(g, j + 1),
                 st_a, all_tiles, offsets(g_next, 0))

        unrolled = jnp.logical_and(skew == 1, functools.reduce(
            jnp.logical_or, [n == n_static for n_static in UNROLLED_SKEW_STEPS]))
        for n_static in UNROLLED_SKEW_STEPS:
            @pl.when(jnp.logical_and(skew == 1, n == n_static))
            def _(n_static=n_static):
                for j in range(n_static):
                    step(bufs[j % 2], full, offsets(g, j), bufs[(j + 1) % 2], all_tiles,
                         offsets(g, j + 1))
                skew_tail(n_static)

        @pl.when(jnp.logical_not(unrolled))
        def _():
            lax.fori_loop(0, n // KU, full_steps, 0)

        @pl.when(jnp.logical_and(skew == 1, jnp.logical_not(unrolled)))
        def _():
            skew_tail(n)

        @pl.when(skew == 0)
        def _():
            for i in range(NQ):
                cur_tiles = [(e, t, t == i) for t in range(i, NQ) for e in range(2)]
                cur_offs = [q0 + i * TK] * NQ
                if i + 1 < NQ:
                    nxt_tiles = [(e, t) for t in range(i + 1, NQ) for e in range(2)]
                    nxt_offs = [q0 + (i + 1) * TK] * NQ
                else:
                    load_queries(g_next)
                    nxt_tiles, nxt_offs = all_tiles, offsets(g_next, 0)
                step(bufs[i % 2], cur_tiles, cur_offs, bufs[(i + 1) % 2], nxt_tiles, nxt_offs)

        out = [acc_ref[e, :HEAD_DIM, :] * (1.0 / acc_ref[e, HEAD_DIM:HEAD_DIM + 1, :])
               for e in range(2)]
        o_ref[0, pl.ds(q0, QG), :] = jnp.concatenate(out, axis=0).T.astype(_BF16)
        reset_state()
        return 0

    lax.fori_loop(0, n_groups, q_group, 0)


def _sweep_plan(stats, n_tiles):
    cum_first, cum_last, qn2, kn2 = (stats[:, s, :N_HEADS, :n_tiles] for s in range(N_STATS))
    B = stats.shape[0]
    n_groups = n_tiles // NQ
    k_norm = jnp.sqrt(jnp.max(kn2, axis=-1, keepdims=True))
    reach = NORM_SLACK * jnp.sqrt(qn2) * k_norm + cum_first
    bound = reach[..., None] - cum_last[:, :, None, :]
    tile = jnp.arange(n_tiles)
    needed = jnp.logical_and(tile[None, :] < tile[:, None], bound >= -ZERO_EXP2)
    w = jnp.sum(needed, axis=-1).astype(jnp.int32)
    w = jnp.max(w.reshape(B, N_PAIRS, 2, n_groups, NQ), axis=2)
    start = jnp.arange(n_groups, dtype=jnp.int32) * NQ
    n_flat = jnp.max(jnp.maximum(w - jnp.arange(NQ, dtype=jnp.int32), 0), axis=-1)
    n_flat = jnp.minimum(((n_flat + KU - 1) // KU) * KU, start)
    n_skew = jnp.maximum(jnp.max(w, axis=-1), 1)
    n_skew = n_skew + (n_skew + 1) % 2
    units_flat = NQ * n_flat + NQ * (NQ + 1) // 2
    units_skew = NQ * n_skew + NQ
    skew = jnp.logical_and(n_skew < start, units_skew < units_flat)
    n = jnp.where(skew, n_skew - 1, n_flat)
    return jnp.stack([n, skew.astype(jnp.int32)], axis=-1)


def _attention(k, ka, vt, qt, qat, stats):
    B, S, _ = k.shape
    assert TQ == TK and KU == 2 and NQ % KU == 0 and S % QG == 0
    plan = _sweep_plan(stats, S // TK)
    pair_rows = pl.BlockSpec((1, S, PAIR), lambda b, j, nb: (b, 0, j))
    return pl.pallas_call(
        _attn_kernel,
        grid_spec=pltpu.PrefetchScalarGridSpec(
            num_scalar_prefetch=1,
            grid=(B, N_PAIRS),
            in_specs=[pair_rows, pl.BlockSpec((1, S, LANES), lambda b, j, nb: (b, 0, 0)),
                      pl.BlockSpec((1, 2, PAIR, S), lambda b, j, nb: (b, j, 0, 0)),
                      pl.BlockSpec((1, PAIR, S), lambda b, j, nb: (b, j, 0)),
                      pl.BlockSpec((1, LANES, S), lambda b, j, nb: (b, 0, 0))],
            out_specs=pair_rows,
            scratch_shapes=[pltpu.VMEM((2, PAIR + LANES, QG), _BF16),
                            pltpu.VMEM((2, NQ, TK, TQ), _F32), pltpu.VMEM((2, NQ, TK, TQ), _F32),
                            pltpu.VMEM((2, 1, QG), _F32), pltpu.VMEM((2, PAIR, QG), _F32)]),
        out_shape=jax.ShapeDtypeStruct((B, S, ATTN_WIDTH), _BF16),
        compiler_params=pltpu.CompilerParams(
            dimension_semantics=("parallel", "parallel"),
            vmem_limit_bytes=VMEM_LIMIT),
        name="forgetting_attention",
    )(plan, k, ka, vt, qt, qat)


def _rms_scale(v, g):
    ms = jnp.mean(v * v, axis=-1, keepdims=True)
    return (v * lax.rsqrt(ms + EPS)) * g


def _merge_mlp_kernel(x_ref, u_ref, vn_ref, ya_ref, gates_ref, wsp_ref, bsp_ref,
                      wbs_ref, wba_ref, wout_ref, gpost_ref,
                      gpre2_ref, wup_ref, wdown_ref, gpost2_ref, o_ref, ysgu_ref):
    tm = x_ref.shape[0]
    row = lax.broadcasted_iota(jnp.int32, (CHUNK, CHUNK), 0)
    col = lax.broadcasted_iota(jnp.int32, (CHUNK, CHUNK), 1)
    lane = lax.broadcasted_iota(jnp.int32, (CHUNK, PAIR), 1)
    ws = [jnp.where(row >= col, wsp_ref[g], 0.0).astype(_BF16) for g in range(N_GROUPS)]
    for c in range(tm // CHUNK):
        r = slice(c * CHUNK, (c + 1) * CHUNK)
        for j in range(N_GROUPS // 2):
            cs = slice(j * PAIR, (j + 1) * PAIR)
            vp = vn_ref[r, cs]
            s = jnp.where(lane < HEAD_DIM, _dot(ws[2 * j], vp), _dot(ws[2 * j + 1], vp))
            s = s + bsp_ref[:, cs]
            ysgu_ref[r, cs] = (u_ref[r, cs].astype(_F32) * s).astype(_BF16)

    blocks = [slice(n * tm // N_STREAMS, (n + 1) * tm // N_STREAMS) for n in range(N_STREAMS)]

    def mix(r):
        a = _dot(ysgu_ref[r, :], wbs_ref[...])
        b = _dot(ya_ref[r, :], wba_ref[...])
        merged = (gates_ref[r, :D_MODEL].astype(_F32) * a
                  + gates_ref[r, D_MODEL:].astype(_F32) * b).astype(_BF16)
        return _dot(merged, wout_ref[...])

    def mlp(xb):
        acc = jnp.zeros(xb.shape, _F32)
        for c in range(D_FF // FF_TILE):
            cs = slice(c * FF_TILE, (c + 1) * FF_TILE)
            hid = jnp.square(jnp.maximum(_dot(xb, wup_ref[:, cs]), 0.0)).astype(_BF16)
            acc = acc + _dot(hid, wdown_ref[cs, :])
        return acc

    mixed = [mix(r) for r in blocks]
    h1 = [x_ref[r, :] + _rms_scale(o, gpost_ref[...]) for r, o in zip(blocks, mixed)]
    xb = [_rms_scale(h, gpre2_ref[...]).astype(_BF16) for h in h1]
    ff = [mlp(v) for v in xb]
    for r, h, f in zip(blocks, h1, ff):
        o_ref[r, :] = h + _rms_scale(f, gpost2_ref[...])


def _merge_mlp(x2, u2, vn2, ya2, gates2, w_spatial, b_spatial, w_bs, w_ba, w_out, g_post,
               g_pre2, w_up, w_down, g_post2):
    R, D = x2.shape
    tm = ROW_TILE
    bsp = jnp.repeat(b_spatial.T, SGU_WIDTH // N_GROUPS, axis=1)

    def const(shape):
        return pl.BlockSpec(shape, lambda i: (0,) * len(shape), pipeline_mode=pl.Buffered(1))

    def rows(width):
        return pl.BlockSpec((tm, width), lambda i: (i, 0))

    return pl.pallas_call(
        _merge_mlp_kernel,
        grid=(R // tm,),
        in_specs=[rows(D), rows(SGU_WIDTH), rows(SGU_WIDTH), rows(ATTN_WIDTH), rows(2 * D),
                  const(w_spatial.shape), const(bsp.shape),
                  const(w_bs.shape), const(w_ba.shape), const(w_out.shape), const((1, D)),
                  const((1, D)), const(w_up.shape), const(w_down.shape), const((1, D))],
        out_specs=rows(D),
        out_shape=jax.ShapeDtypeStruct((R, D), _F32),
        scratch_shapes=[pltpu.VMEM((tm, SGU_WIDTH), _BF16)],
        compiler_params=pltpu.CompilerParams(
            dimension_semantics=("parallel",), vmem_limit_bytes=VMEM_LIMIT),
        name="merge_mlp",
    )(x2, u2, vn2, ya2, gates2, w_spatial, bsp,
      w_bs, w_ba, w_out, g_post.reshape(1, D),
      g_pre2.reshape(1, D), w_up, w_down, g_post2.reshape(1, D))


def kernel(x, g_mix_pre, w_in, b_forget, g_sgu, b_sgu, w_spatial, b_spatial, w_branch_sgu,
           w_branch_attn, w_out, g_mix_post, g_ffn_pre, w_up, w_down, g_ffn_post):
    B, S, D = x.shape
    h = x
    for l in range(g_mix_pre.shape[0]):
        later = (w_branch_sgu[l], w_branch_attn[l], w_out[l], w_up[l], w_down[l])
        u, vn, qt, k, vt, gates, qat, ka, stats, w_bs, w_ba, w_o, w_u, w_d = _in_projection(
            h, g_mix_pre[l], w_in[l], b_forget[l], g_sgu[l], b_sgu[l], later)
        y_attn = _attention(k, ka, vt, qt, qat, stats)
        h = _merge_mlp(h.reshape(B * S, D), u.reshape(B * S, -1), vn.reshape(B * S, -1),
                       y_attn.reshape(B * S, -1), gates.reshape(B * S, -1),
                       w_spatial[l], b_spatial[l], w_bs, w_ba, w_o,
                       g_mix_post[l], g_ffn_pre[l], w_u, w_d,
                       g_ffn_post[l]).reshape(B, S, D)
    return h
```

```python
import functools

import jax
import jax.numpy as jnp
import numpy as np
from jax import lax
from jax.experimental import pallas as pl
from jax.experimental.pallas import tpu as pltpu

D_MODEL = 1024
N_HEADS = 8
HEAD_DIM = 64
ATTN_WIDTH = N_HEADS * HEAD_DIM
N_GROUPS = 8
SGU_WIDTH = D_MODEL // 2
CHUNK = 128
D_FF = 4 * D_MODEL
EPS = 1e-6
LOG2E = 1.4426950408889634

LANES = 128
N_SPLIT = 3
BIAS_COLS = N_HEADS * N_SPLIT
PAIR = 2 * HEAD_DIM
N_PAIRS = N_HEADS // 2

ROW_TILE = 512
FF_TILE = 1024
N_STREAMS = 2
TQ = 256
TK = 256
NQ = 4
QG = NQ * TQ
KU = 2
UNROLLED_SKEW_STEPS = (4, 6)
N_STATS = 4
ZERO_EXP2 = 160.0
NORM_SLACK = 2.05
VMEM_LIMIT = 56 * 1024 * 1024

_BF16 = jnp.bfloat16
_F32 = jnp.float32


def _split_bf16(x):
    parts = []
    r = x
    for _ in range(N_SPLIT):
        p = r.astype(_BF16)
        parts.append(p)
        r = r - p.astype(_F32)
    return parts


def _dot(a, b):
    return jnp.dot(a, b, preferred_element_type=_F32)


def _dot_nt(a, b):
    return lax.dot_general(a, b, (((1,), (1,)), ((), ())), preferred_element_type=_F32)


def _inproj_kernel(x_ref, gpre_ref, wzt_ref, wqt_ref, wkt_ref, wvt_ref, wft_ref, wgt_ref,
                   bf_ref, gsgu_ref, bsgu_ref, tri_ref, expand_ref, expand_t_ref,
                   head_rows_ref, piece_rows_ref, *rest, n_cast):
    cast_in = rest[:n_cast]
    (u_ref, vn_ref, qt_ref, k_ref, vt_ref, gates_ref, qat_ref, ka_ref,
     stats_ref) = rest[n_cast:n_cast + 9]
    cast_out = rest[n_cast + 9:2 * n_cast + 9]
    carry_ref, = rest[2 * n_cast + 9:]
    for src, dst in zip(cast_in, cast_out):
        dst[...] = src[...].astype(_BF16)

    @pl.when(pl.program_id(1) == 0)
    def _():
        carry_ref[...] = jnp.zeros_like(carry_ref)
        stats_ref[...] = jnp.zeros_like(stats_ref)

    x = x_ref[0]
    tm = x.shape[0]
    ms = jnp.mean(x * x, axis=-1, keepdims=True)
    xb = ((x * lax.rsqrt(ms + EPS)) * gpre_ref[...]).astype(_BF16)

    lane = lax.broadcasted_iota(jnp.int32, (tm, LANES), 1)

    def pack_pieces(v):
        hi, mid, lo = _split_bf16(v)
        zero = jnp.zeros((), _BF16)
        return jnp.where(lane < N_HEADS, hi,
                         jnp.where(lane < 2 * N_HEADS, mid,
                                   jnp.where(lane < N_SPLIT * N_HEADS, lo, zero)))

    f = _dot_nt(xb, wft_ref[...]) + bf_ref[...]

    z = jax.nn.gelu(_dot_nt(xb, wzt_ref[...]), approximate=True)
    u_ref[0] = z[:, :SGU_WIDTH].astype(_BF16)
    v = z[:, SGU_WIDTH:]
    mu = jnp.mean(v, axis=-1, keepdims=True)
    vc = v - mu
    var = jnp.mean(vc * vc, axis=-1, keepdims=True)
    vn_ref[0] = ((vc * lax.rsqrt(var + EPS)) * gsgu_ref[...] + bsgu_ref[...]).astype(_BF16)

    log_f = jnp.minimum(f, 0.0) - jnp.log(1.0 + jnp.exp(-jnp.abs(f)))
    sums = _dot(tri_ref[...], pack_pieces(log_f))

    qt = _dot_nt(wqt_ref[...], xb) * (HEAD_DIM ** -0.5 * LOG2E)
    qt_ref[0] = qt.astype(_BF16)
    k = _dot_nt(xb, wkt_ref[...])
    k_ref[0] = k.astype(_BF16)
    qn2 = _dot(head_rows_ref[...], (qt * qt).astype(_BF16))
    kn2 = _dot_nt(head_rows_ref[...], (k * k).astype(_BF16))

    total = sums
    for shift in (N_HEADS, 2 * N_HEADS, LANES - N_HEADS, LANES - 2 * N_HEADS):
        total = total + pltpu.roll(sums, shift, 1)
    cum = carry_ref[0:1, :] + total
    carry_ref[0:1, :] = cum[tm - 1:tm, :]
    pieces = pack_pieces(cum * LOG2E)
    spread = _dot(pieces, expand_ref[...])
    spread_t = _dot_nt(expand_t_ref[...], pieces)

    cum_t = _dot_nt(piece_rows_ref[...], pieces)
    tile_lane = lax.broadcasted_iota(jnp.int32, (2 * N_HEADS, LANES), 1)
    first_tile = pl.program_id(1) * (tm // TK)
    stats = [stats_ref[0, s] for s in range(N_STATS)]
    for j in range(tm // TK):
        lo, hi = j * TK, (j + 1) * TK
        cols = (cum_t[:, lo:lo + 1], cum_t[:, hi - 1:hi],
                jnp.max(qn2[:, lo:hi], axis=1, keepdims=True),
                jnp.max(kn2[:, lo:hi], axis=1, keepdims=True))
        stats = [jnp.where(tile_lane == first_tile + j, c, s) for c, s in zip(cols, stats)]
    for s in range(N_STATS):
        stats_ref[0, s] = stats[s]

    vt = _dot_nt(wvt_ref[...], xb).astype(_BF16)
    for h in range(N_HEADS):
        vt_ref[0, h, :HEAD_DIM, :] = vt[h * HEAD_DIM:(h + 1) * HEAD_DIM, :]
        vt_ref[0, h, HEAD_DIM:, :] = jnp.ones((PAIR - HEAD_DIM, tm), _BF16)

    gates_ref[0] = jax.nn.sigmoid(_dot_nt(xb, wgt_ref[...])).astype(_BF16)

    in_a = lane < BIAS_COLS
    in_b = jnp.logical_and(lane >= BIAS_COLS, lane < 2 * BIAS_COLS)
    ka_ref[0] = jnp.where(in_a, -spread, jnp.where(in_b, 1.0, 0.0)).astype(_BF16)
    row = lax.broadcasted_iota(jnp.int32, (LANES, tm), 0)
    in_a = row < BIAS_COLS
    in_b = jnp.logical_and(row >= BIAS_COLS, row < 2 * BIAS_COLS)
    qat_ref[0] = jnp.where(in_a, 1.0, jnp.where(in_b, spread_t, 0.0)).astype(_BF16)


def _head_rows():
    r = np.zeros((2 * N_HEADS, ATTN_WIDTH), np.float32)
    for h in range(N_HEADS):
        r[h, h * HEAD_DIM:(h + 1) * HEAD_DIM] = 1.0
    return r


def _piece_rows():
    r = np.zeros((2 * N_HEADS, LANES), np.float32)
    for h in range(N_HEADS):
        for i in range(N_SPLIT):
            r[h, N_HEADS * i + h] = 1.0
    return r


def _expand_matrix():
    e = np.zeros((LANES, LANES), np.float32)
    for h in range(N_HEADS):
        for i in range(N_SPLIT):
            e[N_HEADS * i + h, N_SPLIT * h + i] = 1.0
            e[N_HEADS * i + h, BIAS_COLS + N_SPLIT * h + i] = 1.0
    return e


_IN_OFFSETS = tuple(int(v) for v in np.cumsum(
    (0, 2 * SGU_WIDTH, ATTN_WIDTH, ATTN_WIDTH, ATTN_WIDTH, N_HEADS, 2 * D_MODEL)))


def _in_projection(x, g_pre, w_in, b_forget, g_sgu, b_sgu, later_weights):
    B, S, D = x.shape
    tm = ROW_TILE
    n_steps = B * (S // tm)
    o = _IN_OFFSETS
    wt = jnp.swapaxes(w_in, 0, 1)
    wzt, wqt, wkt, wvt, wgt = (wt[o[i]:o[i + 1]].astype(_BF16) for i in (0, 1, 2, 3, 5))
    row_pad = ((0, LANES - N_SPLIT * N_HEADS), (0, 0))
    wft = jnp.pad(jnp.tile(wt[o[4]:o[5]], (N_SPLIT, 1)), row_pad).astype(_BF16)
    lane_pad = ((0, 0), (0, LANES - N_SPLIT * N_HEADS))
    bf = jnp.pad(jnp.tile(b_forget.reshape(1, N_HEADS), (1, N_SPLIT)), lane_pad)
    tri = jnp.asarray(np.tril(np.ones((tm, tm), np.float32)), _BF16)
    expand = jnp.asarray(_expand_matrix(), _BF16)
    expand_t = jnp.asarray(_expand_matrix().T, _BF16)
    head_rows = jnp.asarray(_head_rows(), _BF16)
    piece_rows = jnp.asarray(_piece_rows(), _BF16)

    def const(shape):
        return pl.BlockSpec(shape, lambda b, i: (0,) * len(shape))

    def rows(width):
        return pl.BlockSpec((1, tm, width), lambda b, i: (b, i, 0))

    out_shape = (
        jax.ShapeDtypeStruct((B, S, SGU_WIDTH), _BF16),
        jax.ShapeDtypeStruct((B, S, SGU_WIDTH), _BF16),
        jax.ShapeDtypeStruct((B, ATTN_WIDTH, S), _BF16),
        jax.ShapeDtypeStruct((B, S, ATTN_WIDTH), _BF16),
        jax.ShapeDtypeStruct((B, N_HEADS, PAIR, S), _BF16),
        jax.ShapeDtypeStruct((B, S, 2 * D_MODEL), _BF16),
        jax.ShapeDtypeStruct((B, LANES, S), _BF16),
        jax.ShapeDtypeStruct((B, S, LANES), _BF16),
        jax.ShapeDtypeStruct((B, N_STATS, 2 * N_HEADS, LANES), _F32),
    )

    def cols(height):
        return pl.BlockSpec((1, height, tm), lambda b, i: (b, 0, i))

    def row_block(w):
        assert w.shape[0] % (n_steps * 16) == 0
        return pl.BlockSpec((w.shape[0] // n_steps, w.shape[1]),
                            lambda b, i: (b * (S // tm) + i, 0))

    out_specs = (
        rows(SGU_WIDTH), rows(SGU_WIDTH), cols(ATTN_WIDTH), rows(ATTN_WIDTH),
        pl.BlockSpec((1, N_HEADS, PAIR, tm), lambda b, i: (b, 0, 0, i)),
        rows(2 * D_MODEL), cols(LANES), rows(LANES),
        pl.BlockSpec((1, N_STATS, 2 * N_HEADS, LANES), lambda b, i: (b, 0, 0, 0)),
    ) + tuple(row_block(w) for w in later_weights)
    out_shape += tuple(jax.ShapeDtypeStruct(w.shape, _BF16) for w in later_weights)
    return pl.pallas_call(
        functools.partial(_inproj_kernel, n_cast=len(later_weights)),
        grid=(B, S // tm),
        in_specs=[
            rows(D), const((1, D)),
            const(wzt.shape), const(wqt.shape), const(wkt.shape), const(wvt.shape),
            const(wft.shape), const(wgt.shape),
            const((1, LANES)), const((1, SGU_WIDTH)), const((1, SGU_WIDTH)),
            const(tri.shape), const(expand.shape), const(expand_t.shape),
            const(head_rows.shape), const(piece_rows.shape),
        ] + [row_block(w) for w in later_weights],
        out_specs=out_specs,
        out_shape=out_shape,
        scratch_shapes=[pltpu.VMEM((8, LANES), _F32)],
        compiler_params=pltpu.CompilerParams(
            dimension_semantics=("arbitrary", "arbitrary"),
            vmem_limit_bytes=VMEM_LIMIT),
        name="in_projection",
    )(x, g_pre.reshape(1, D), wzt, wqt, wkt, wvt, wft, wgt, bf,
      g_sgu.reshape(1, SGU_WIDTH), b_sgu.reshape(1, SGU_WIDTH), tri, expand, expand_t,
      head_rows, piece_rows, *later_weights)


def _attn_kernel(plan_ref, k_ref, ka_ref, vt_ref, qt_ref, qat_ref, o_ref,
                 qf_ref, st_a, st_b, m_ref, acc_ref):
    batch = pl.program_id(0)
    pair = pl.program_id(1)
    S = k_ref.shape[1]
    row = lax.broadcasted_iota(jnp.int32, (PAIR, 1), 0)
    all_tiles = [(e, t) for t in range(NQ) for e in range(2)]

    def key_tile(k0):
        return jnp.concatenate([k_ref[0, pl.ds(k0, TK), :], ka_ref[0, pl.ds(k0, TK), :]], axis=1)

    def offsets(g, j):
        n, skew = plan_ref[batch, pair, g, 0], plan_ref[batch, pair, g, 1]
        first = (g * NQ - n - skew + j) * TK
        return [pl.multiple_of(first + t * skew * TK, TK) for t in range(NQ)]

    def update(e, t, st, k0, masked):
        cs = slice(t * TQ, (t + 1) * TQ)
        if masked:
            key_i = lax.broadcasted_iota(jnp.int32, (TK, TQ), 0)
            qry_i = lax.broadcasted_iota(jnp.int32, (TK, TQ), 1)
            st = jnp.where(key_i <= qry_i, st, -jnp.inf)
        m = m_ref[e, :, cs]
        m_new = jnp.maximum(m, jnp.max(st, axis=0, keepdims=True))
        p = jnp.exp2((st - m_new).astype(_BF16))
        alpha = jnp.exp2(m - m_new)
        m_ref[e, :, cs] = m_new
        acc_ref[e, :, cs] = alpha * acc_ref[e, :, cs] + _dot(
            vt_ref[0, e, :, pl.ds(k0, TK)], p)

    def step(cur, cur_tiles, cur_offs, nxt, nxt_tiles, nxt_offs):
        for n in range(max(len(cur_tiles), len(nxt_tiles))):
            if n < len(nxt_tiles):
                e, t = nxt_tiles[n]
                nxt[e, t] = _dot(key_tile(nxt_offs[t]), qf_ref[e, :, t * TQ:(t + 1) * TQ])
            if n < len(cur_tiles):
                e, t, masked = cur_tiles[n]
                update(e, t, cur[e, t], cur_offs[t], masked)

    def load_queries(g):
        q0 = pl.multiple_of(g * QG, QG)
        qt = qt_ref[0, :, pl.ds(q0, QG)].astype(_F32)
        qat = qat_ref[0, :, pl.ds(q0, QG)].astype(_F32)
        for e in range(2):
            a0 = N_SPLIT * (2 * pair + e)
            q_mask = jnp.logical_and(row >= HEAD_DIM * e, row < HEAD_DIM * (e + 1))
            a_mask = jnp.logical_or(
                jnp.logical_and(row >= a0, row < a0 + N_SPLIT),
                jnp.logical_and(row >= BIAS_COLS + a0, row < BIAS_COLS + a0 + N_SPLIT))
            qf_ref[e, :PAIR, :] = jnp.where(q_mask, qt, 0.0).astype(_BF16)
            qf_ref[e, PAIR:, :] = jnp.where(a_mask, qat, 0.0).astype(_BF16)

    def reset_state():
        m_ref[...] = jnp.full(m_ref.shape, -jnp.inf, _F32)
        acc_ref[...] = jnp.zeros(acc_ref.shape, _F32)

    full = [(e, t, False) for e, t in all_tiles]
    bufs = (st_a, st_b)
    n_groups = S // QG

    load_queries(0)
    reset_state()
    step(None, [], None, st_a, all_tiles, offsets(0, 0))

    def q_group(g, _):
        q0 = pl.multiple_of(g * QG, QG)
        n = plan_ref[batch, pair, g, 0]
        skew = plan_ref[batch, pair, g, 1]
        g_next = jnp.minimum(g + 1, n_groups - 1)

        def full_steps(i, _):
            for u in range(KU):
                j = i * KU + u
                step(bufs[u % 2], full, offsets(g, j), bufs[(u + 1) % 2], all_tiles,
                     offsets(g, j + 1))
            return 0

        def skew_tail(j):
            step(st_a, full, offsets(g, j), st_b, all_tiles, offsets(g, j + 1))
            load_queries(g_next)
            step(st_b, [(e, t, True) for e, t in all_tiles], offsets(g, j + 1),
                 st_a, all_tiles, offsets(g_next, 0))

        unrolled = jnp.logical_and(skew == 1, functools.reduce(
            jnp.logical_or, [n == n_static for n_static in UNROLLED_SKEW_STEPS]))
        for n_static in UNROLLED_SKEW_STEPS:
            @pl.when(jnp.logical_and(skew == 1, n == n_static))
            def _(n_static=n_static):
                for j in range(n_static):
                    step(bufs[j % 2], full, offsets(g, j), bufs[(j + 1) % 2], all_tiles,
                         offsets(g, j + 1))
                skew_tail(n_static)

        @pl.when(jnp.logical_not(unrolled))
        def _():
            lax.fori_loop(0, n // KU, full_steps, 0)

        @pl.when(jnp.logical_and(skew == 1, jnp.logical_not(unrolled)))
        def _():
            skew_tail(n)

        @pl.when(skew == 0)
        def _():
            for i in range(NQ):
                cur_tiles = [(e, t, t == i) for t in range(i, NQ) for e in range(2)]
                cur_offs = [q0 + i * TK] * NQ
                if i + 1 < NQ:
                    nxt_tiles = [(e, t) for t in range(i + 1, NQ) for e in range(2)]
                    nxt_offs = [q0 + (i + 1) * TK] * NQ
                else:
                    load_queries(g_next)
                    nxt_tiles, nxt_offs = all_tiles, offsets(g_next, 0)
                step(bufs[i % 2], cur_tiles, cur_offs, bufs[(i + 1) % 2], nxt_tiles, nxt_offs)

        out = [acc_ref[e, :HEAD_DIM, :] * (1.0 / acc_ref[e, HEAD_DIM:HEAD_DIM + 1, :])
               for e in range(2)]
        o_ref[0, pl.ds(q0, QG), :] = jnp.concatenate(out, axis=0).T.astype(_BF16)
        reset_state()
        return 0

    lax.fori_loop(0, n_groups, q_group, 0)


def _sweep_plan(stats, n_tiles):
    cum_first, cum_last, qn2, kn2 = (stats[:, s, :N_HEADS, :n_tiles] for s in range(N_STATS))
    B = stats.shape[0]
    n_groups = n_tiles // NQ
    k_norm = jnp.sqrt(jnp.max(kn2, axis=-1, keepdims=True))
    reach = NORM_SLACK * jnp.sqrt(qn2) * k_norm + cum_first
    bound = reach[..., None] - cum_last[:, :, None, :]
    tile = jnp.arange(n_tiles)
    needed = jnp.logical_and(tile[None, :] < tile[:, None], bound >= -ZERO_EXP2)
    w = jnp.sum(needed, axis=-1).astype(jnp.int32)
    w = jnp.max(w.reshape(B, N_PAIRS, 2, n_groups, NQ), axis=2)
    start = jnp.arange(n_groups, dtype=jnp.int32) * NQ
    n_flat = jnp.max(jnp.maximum(w - jnp.arange(NQ, dtype=jnp.int32), 0), axis=-1)
    n_flat = jnp.minimum(((n_flat + KU - 1) // KU) * KU, start)
    n_skew = jnp.maximum(jnp.max(w, axis=-1), 1)
    n_skew = n_skew + (n_skew + 1) % 2
    units_flat = NQ * n_flat + NQ * (NQ + 1) // 2
    units_skew = NQ * n_skew + NQ
    skew = jnp.logical_and(n_skew < start, units_skew < units_flat)
    n = jnp.where(skew, n_skew - 1, n_flat)
    return jnp.stack([n, skew.astype(jnp.int32)], axis=-1)


def _attention(k, ka, vt, qt, qat, stats):
    B, S, _ = k.shape
    assert TQ == TK and KU == 2 and NQ % KU == 0 and S % QG == 0
    plan = _sweep_plan(stats, S // TK)
    pair_rows = pl.BlockSpec((1, S, PAIR), lambda b, j, nb: (b, 0, j))
    return pl.pallas_call(
        _attn_kernel,
        grid_spec=pltpu.PrefetchScalarGridSpec(
            num_scalar_prefetch=1,
            grid=(B, N_PAIRS),
            in_specs=[pair_rows, pl.BlockSpec((1, S, LANES), lambda b, j, nb: (b, 0, 0)),
                      pl.BlockSpec((1, 2, PAIR, S), lambda b, j, nb: (b, j, 0, 0)),
                      pl.BlockSpec((1, PAIR, S), lambda b, j, nb: (b, j, 0)),
                      pl.BlockSpec((1, LANES, S), lambda b, j, nb: (b, 0, 0))],
            out_specs=pair_rows,
            scratch_shapes=[pltpu.VMEM((2, PAIR + LANES, QG), _BF16),
                            pltpu.VMEM((2, NQ, TK, TQ), _F32), pltpu.VMEM((2, NQ, TK, TQ), _F32),
                            pltpu.VMEM((2, 1, QG), _F32), pltpu.VMEM((2, PAIR, QG), _F32)]),
        out_shape=jax.ShapeDtypeStruct((B, S, ATTN_WIDTH), _BF16),
        compiler_params=pltpu.CompilerParams(
            dimension_semantics=("parallel", "parallel"),
            vmem_limit_bytes=VMEM_LIMIT),
        name="forgetting_attention",
    )(plan, k, ka, vt, qt, qat)


def _rms_scale(v, g):
    ms = jnp.mean(v * v, axis=-1, keepdims=True)
    return (v * lax.rsqrt(ms + EPS)) * g


def _merge_mlp_kernel(x_ref, u_ref, vn_ref, ya_ref, gates_ref, wsp_ref, bsp_ref,
                      wbs_ref, wba_ref, wout_ref, gpost_ref,
                      gpre2_ref, wup_ref, wdown_ref, gpost2_ref, o_ref, ysgu_ref):
    tm = x_ref.shape[0]
    row = lax.broadcasted_iota(jnp.int32, (CHUNK, CHUNK), 0)
    col = lax.broadcasted_iota(jnp.int32, (CHUNK, CHUNK), 1)
    lane = lax.broadcasted_iota(jnp.int32, (CHUNK, PAIR), 1)
    ws = [jnp.where(row >= col, wsp_ref[g], 0.0).astype(_BF16) for g in range(N_GROUPS)]
    for c in range(tm // CHUNK):
        r = slice(c * CHUNK, (c + 1) * CHUNK)
        for j in range(N_GROUPS // 2):
            cs = slice(j * PAIR, (j + 1) * PAIR)
            vp = vn_ref[r, cs]
            s = jnp.where(lane < HEAD_DIM, _dot(ws[2 * j], vp), _dot(ws[2 * j + 1], vp))
            s = s + bsp_ref[:, cs]
            ysgu_ref[r, cs] = (u_ref[r, cs].astype(_F32) * s).astype(_BF16)

    blocks = [slice(n * tm // N_STREAMS, (n + 1) * tm // N_STREAMS) for n in range(N_STREAMS)]

    def mix(r):
        a = _dot(ysgu_ref[r, :], wbs_ref[...])
        b = _dot(ya_ref[r, :], wba_ref[...])
        merged = (gates_ref[r, :D_MODEL].astype(_F32) * a
                  + gates_ref[r, D_MODEL:].astype(_F32) * b).astype(_BF16)
        return _dot(merged, wout_ref[...])

    def mlp(xb):
        acc = jnp.zeros(xb.shape, _F32)
        for c in range(D_FF // FF_TILE):
            cs = slice(c * FF_TILE, (c + 1) * FF_TILE)
            hid = jnp.square(jnp.maximum(_dot(xb, wup_ref[:, cs]), 0.0)).astype(_BF16)
            acc = acc + _dot(hid, wdown_ref[cs, :])
        return acc

    mixed = [mix(r) for r in blocks]
    h1 = [x_ref[r, :] + _rms_scale(o, gpost_ref[...]) for r, o in zip(blocks, mixed)]
    xb = [_rms_scale(h, gpre2_ref[...]).astype(_BF16) for h in h1]
    ff = [mlp(v) for v in xb]
    for r, h, f in zip(blocks, h1, ff):
        o_ref[r, :] = h + _rms_scale(f, gpost2_ref[...])


def _merge_mlp(x2, u2, vn2, ya2, gates2, w_spatial, b_spatial, w_bs, w_ba, w_out, g_post,
               g_pre2, w_up, w_down, g_post2):
    R, D = x2.shape
    tm = ROW_TILE
    bsp = jnp.repeat(b_spatial.T, SGU_WIDTH // N_GROUPS, axis=1)

    def const(shape):
        return pl.BlockSpec(shape, lambda i: (0,) * len(shape), pipeline_mode=pl.Buffered(1))

    def rows(width):
        return pl.BlockSpec((tm, width), lambda i: (i, 0))

    return pl.pallas_call(
        _merge_mlp_kernel,
        grid=(R // tm,),
        in_specs=[rows(D), rows(SGU_WIDTH), rows(SGU_WIDTH), rows(ATTN_WIDTH), rows(2 * D),
                  const(w_spatial.shape), const(bsp.shape),
                  const(w_bs.shape), const(w_ba.shape), const(w_out.shape), const((1, D)),
                  const((1, D)), const(w_up.shape), const(w_down.shape), const((1, D))],
        out_specs=rows(D),
        out_shape=jax.ShapeDtypeStruct((R, D), _F32),
        scratch_shapes=[pltpu.VMEM((tm, SGU_WIDTH), _BF16)],
        compiler_params=pltpu.CompilerParams(
            dimension_semantics=("parallel",), vmem_limit_bytes=VMEM_LIMIT),
        name="merge_mlp",
    )(x2, u2, vn2, ya2, gates2, w_spatial, bsp,
      w_bs, w_ba, w_out, g_post.reshape(1, D),
      g_pre2.reshape(1, D), w_up, w_down, g_post2.reshape(1, D))


def kernel(x, g_mix_pre, w_in, b_forget, g_sgu, b_sgu, w_spatial, b_spatial, w_branch_sgu,
           w_branch_attn, w_out, g_mix_post, g_ffn_pre, w_up, w_down, g_ffn_post):
    B, S, D = x.shape
    h = x
    for l in range(g_mix_pre.shape[0]):
        later = (w_branch_sgu[l], w_branch_attn[l], w_out[l], w_up[l], w_down[l])
        u, vn, qt, k, vt, gates, qat, ka, stats, w_bs, w_ba, w_o, w_u, w_d = _in_projection(
            h, g_mix_pre[l], w_in[l], b_forget[l], g_sgu[l], b_sgu[l], later)
        y_attn = _attention(k, ka, vt, qt, qat, stats)
        h = _merge_mlp(h.reshape(B * S, D), u.reshape(B * S, -1), vn.reshape(B * S, -1),
                       y_attn.reshape(B * S, -1), gates.reshape(B * S, -1),
                       w_spatial[l], b_spatial[l], w_bs, w_ba, w_o,
                       g_mix_post[l], g_ffn_pre[l], w_u, w_d,
                       g_ffn_post[l]).reshape(B, S, D)
    return h
```

```python
import functools

import jax
import jax.numpy as jnp
import numpy as np
from jax import lax
from jax.experimental import pallas as pl
from jax.experimental.pallas import tpu as pltpu

D_MODEL = 1024
N_HEADS = 8
HEAD_DIM = 64
ATTN_WIDTH = N_HEADS * HEAD_DIM
N_GROUPS = 8
SGU_WIDTH = D_MODEL // 2
CHUNK = 128
D_FF = 4 * D_MODEL
EPS = 1e-6
LOG2E = 1.4426950408889634

LANES = 128
N_SPLIT = 3
BIAS_COLS = N_HEADS * N_SPLIT
PAIR = 2 * HEAD_DIM
N_PAIRS = N_HEADS // 2

ROW_TILE = 512
FF_TILE = 1024
N_STREAMS = 2
TQ = 256
TK = 256
NQ = 4
QG = NQ * TQ
KU = 2
UNROLLED_SKEW_STEPS = (4, 6)
N_STATS = 4
ZERO_EXP2 = 160.0
NORM_SLACK = 2.05
VMEM_LIMIT = 56 * 1024 * 1024

_BF16 = jnp.bfloat16
_F32 = jnp.float32


def _split_bf16(x):
    parts = []
    r = x
    for _ in range(N_SPLIT):
        p = r.astype(_BF16)
        parts.append(p)
        r = r - p.astype(_F32)
    return parts


def _dot(a, b):
    return jnp.dot(a, b, preferred_element_type=_F32)


def _dot_nt(a, b):
    return lax.dot_general(a, b, (((1,), (1,)), ((), ())), preferred_element_type=_F32)


def _inproj_kernel(x_ref, gpre_ref, wzt_ref, wqt_ref, wkt_ref, wvt_ref, wft_ref, wgt_ref,
                   bf_ref, gsgu_ref, bsgu_ref, tri_ref, expand_ref, expand_t_ref,
                   head_rows_ref, piece_rows_ref, *rest, n_cast):
    cast_in = rest[:n_cast]
    (u_ref, vn_ref, qt_ref, k_ref, vt_ref, gates_ref, qat_ref, ka_ref,
     stats_ref) = rest[n_cast:n_cast + 9]
    cast_out = rest[n_cast + 9:2 * n_cast + 9]
    carry_ref, = rest[2 * n_cast + 9:]
    for src, dst in zip(cast_in, cast_out):
        dst[...] = src[...].astype(_BF16)

    @pl.when(pl.program_id(1) == 0)
    def _():
        carry_ref[...] = jnp.zeros_like(carry_ref)
        stats_ref[...] = jnp.zeros_like(stats_ref)

    x = x_ref[0]
    tm = x.shape[0]
    ms = jnp.mean(x * x, axis=-1, keepdims=True)
    xb = ((x * lax.rsqrt(ms + EPS)) * gpre_ref[...]).astype(_BF16)

    lane = lax.broadcasted_iota(jnp.int32, (tm, LANES), 1)

    def pack_pieces(v):
        hi, mid, lo = _split_bf16(v)
        zero = jnp.zeros((), _BF16)
        return jnp.where(lane < N_HEADS, hi,
                         jnp.where(lane < 2 * N_HEADS, mid,
                                   jnp.where(lane < N_SPLIT * N_HEADS, lo, zero)))

    f = _dot_nt(xb, wft_ref[...]) + bf_ref[...]

    z = jax.nn.gelu(_dot_nt(xb, wzt_ref[...]), approximate=True)
    u_ref[0] = z[:, :SGU_WIDTH].astype(_BF16)
    v = z[:, SGU_WIDTH:]
    mu = jnp.mean(v, axis=-1, keepdims=True)
    vc = v - mu
    var = jnp.mean(vc * vc, axis=-1, keepdims=True)
    vn_ref[0] = ((vc * lax.rsqrt(var + EPS)) * gsgu_ref[...] + bsgu_ref[...]).astype(_BF16)

    log_f = jnp.minimum(f, 0.0) - jnp.log(1.0 + jnp.exp(-jnp.abs(f)))
    sums = _dot(tri_ref[...], pack_pieces(log_f))

    qt = _dot_nt(wqt_ref[...], xb) * (HEAD_DIM ** -0.5 * LOG2E)
    qt_ref[0] = qt.astype(_BF16)
    k = _dot_nt(xb, wkt_ref[...])
    k_ref[0] = k.astype(_BF16)
    qn2 = _dot(head_rows_ref[...], (qt * qt).astype(_BF16))
    kn2 = _dot_nt(head_rows_ref[...], (k * k).astype(_BF16))

    total = sums
    for shift in (N_HEADS, 2 * N_HEADS, LANES - N_HEADS, LANES - 2 * N_HEADS):
        total = total + pltpu.roll(sums, shift, 1)
    cum = carry_ref[0:1, :] + total
    carry_ref[0:1, :] = cum[tm - 1:tm, :]
    pieces = pack_pieces(cum * LOG2E)
    spread = _dot(pieces, expand_ref[...])
    spread_t = _dot_nt(expand_t_ref[...], pieces)

    cum_t = _dot_nt(piece_rows_ref[...], pieces)
    tile_lane = lax.broadcasted_iota(jnp.int32, (2 * N_HEADS, LANES), 1)
    first_tile = pl.program_id(1) * (tm // TK)
    stats = [stats_ref[0, s] for s in range(N_STATS)]
    for j in range(tm // TK):
        lo, hi = j * TK, (j + 1) * TK
        cols = (cum_t[:, lo:lo + 1], cum_t[:, hi - 1:hi],
                jnp.max(qn2[:, lo:hi], axis=1, keepdims=True),
                jnp.max(kn2[:, lo:hi], axis=1, keepdims=True))
        stats = [jnp.where(tile_lane == first_tile + j, c, s) for c, s in zip(cols, stats)]
    for s in range(N_STATS):
        stats_ref[0, s] = stats[s]

    vt = _dot_nt(wvt_ref[...], xb).astype(_BF16)
    for h in range(N_HEADS):
        vt_ref[0, h, :HEAD_DIM, :] = vt[h * HEAD_DIM:(h + 1) * HEAD_DIM, :]
        vt_ref[0, h, HEAD_DIM:, :] = jnp.ones((PAIR - HEAD_DIM, tm), _BF16)

    gates_ref[0] = jax.nn.sigmoid(_dot_nt(xb, wgt_ref[...])).astype(_BF16)

    in_a = lane < BIAS_COLS
    in_b = jnp.logical_and(lane >= BIAS_COLS, lane < 2 * BIAS_COLS)
    ka_ref[0] = jnp.where(in_a, -spread, jnp.where(in_b, 1.0, 0.0)).astype(_BF16)
    row = lax.broadcasted_iota(jnp.int32, (LANES, tm), 0)
    in_a = row < BIAS_COLS
    in_b = jnp.logical_and(row >= BIAS_COLS, row < 2 * BIAS_COLS)
    qat_ref[0] = jnp.where(in_a, 1.0, jnp.where(in_b, spread_t, 0.0)).astype(_BF16)


def _head_rows():
    r = np.zeros((2 * N_HEADS, ATTN_WIDTH), np.float32)
    for h in range(N_HEADS):
        r[h, h * HEAD_DIM:(h + 1) * HEAD_DIM] = 1.0
    return r


def _piece_rows():
    r = np.zeros((2 * N_HEADS, LANES), np.float32)
    for h in range(N_HEADS):
        for i in range(N_SPLIT):
            r[h, N_HEADS * i + h] = 1.0
    return r


def _expand_matrix():
    e = np.zeros((LANES, LANES), np.float32)
    for h in range(N_HEADS):
        for i in range(N_SPLIT):
            e[N_HEADS * i + h, N_SPLIT * h + i] = 1.0
            e[N_HEADS * i + h, BIAS_COLS + N_SPLIT * h + i] = 1.0
    return e


_IN_OFFSETS = tuple(int(v) for v in np.cumsum(
    (0, 2 * SGU_WIDTH, ATTN_WIDTH, ATTN_WIDTH, ATTN_WIDTH, N_HEADS, 2 * D_MODEL)))


def _in_projection(x, g_pre, w_in, b_forget, g_sgu, b_sgu, later_weights):
    B, S, D = x.shape
    tm = ROW_TILE
    n_steps = B * (S // tm)
    o = _IN_OFFSETS
    wt = jnp.swapaxes(w_in, 0, 1)
    wzt, wqt, wkt, wvt, wgt = (wt[o[i]:o[i + 1]].astype(_BF16) for i in (0, 1, 2, 3, 5))
    row_pad = ((0, LANES - N_SPLIT * N_HEADS), (0, 0))
    wft = jnp.pad(jnp.tile(wt[o[4]:o[5]], (N_SPLIT, 1)), row_pad).astype(_BF16)
    lane_pad = ((0, 0), (0, LANES - N_SPLIT * N_HEADS))
    bf = jnp.pad(jnp.tile(b_forget.reshape(1, N_HEADS), (1, N_SPLIT)), lane_pad)
    tri = jnp.asarray(np.tril(np.ones((tm, tm), np.float32)), _BF16)
    expand = jnp.asarray(_expand_matrix(), _BF16)
    expand_t = jnp.asarray(_expand_matrix().T, _BF16)
    head_rows = jnp.asarray(_head_rows(), _BF16)
    piece_rows = jnp.asarray(_piece_rows(), _BF16)

    def const(shape):
        return pl.BlockSpec(shape, lambda b, i: (0,) * len(shape))

    def rows(width):
        return pl.BlockSpec((1, tm, width), lambda b, i: (b, i, 0))

    out_shape = (
        jax.ShapeDtypeStruct((B, S, SGU_WIDTH), _BF16),
        jax.ShapeDtypeStruct((B, S, SGU_WIDTH), _BF16),
        jax.ShapeDtypeStruct((B, ATTN_WIDTH, S), _BF16),
        jax.ShapeDtypeStruct((B, S, ATTN_WIDTH), _BF16),
        jax.ShapeDtypeStruct((B, N_HEADS, PAIR, S), _BF16),
        jax.ShapeDtypeStruct((B, S, 2 * D_MODEL), _BF16),
        jax.ShapeDtypeStruct((B, LANES, S), _BF16),
        jax.ShapeDtypeStruct((B, S, LANES), _BF16),
        jax.ShapeDtypeStruct((B, N_STATS, 2 * N_HEADS, LANES), _F32),
    )

    def cols(height):
        return pl.BlockSpec((1, height, tm), lambda b, i: (b, 0, i))

    def row_block(w):
        assert w.shape[0] % (n_steps * 16) == 0
        return pl.BlockSpec((w.shape[0] // n_steps, w.shape[1]),
                            lambda b, i: (b * (S // tm) + i, 0))

    out_specs = (
        rows(SGU_WIDTH), rows(SGU_WIDTH), cols(ATTN_WIDTH), rows(ATTN_WIDTH),
        pl.BlockSpec((1, N_HEADS, PAIR, tm), lambda b, i: (b, 0, 0, i)),
        rows(2 * D_MODEL), cols(LANES), rows(LANES),
        pl.BlockSpec((1, N_STATS, 2 * N_HEADS, LANES), lambda b, i: (b, 0, 0, 0)),
    ) + tuple(row_block(w) for w in later_weights)
    out_shape += tuple(jax.ShapeDtypeStruct(w.shape, _BF16) for w in later_weights)
    return pl.pallas_call(
        functools.partial(_inproj_kernel, n_cast=len(later_weights)),
        grid=(B, S // tm),
        in_specs=[
            rows(D), const((1, D)),
            const(wzt.shape), const(wqt.shape), const(wkt.shape), const(wvt.shape),
            const(wft.shape), const(wgt.shape),
            const((1, LANES)), const((1, SGU_WIDTH)), const((1, SGU_WIDTH)),
            const(tri.shape), const(expand.shape), const(expand_t.shape),
            const(head_rows.shape), const(piece_rows.shape),
        ] + [row_block(w) for w in later_weights],
        out_specs=out_specs,
        out_shape=out_shape,
        scratch_shapes=[pltpu.VMEM((8, LANES), _F32)],
        compiler_params=pltpu.CompilerParams(
            dimension_semantics=("arbitrary", "arbitrary"),
            vmem_limit_bytes=VMEM_LIMIT),
        name="in_projection",
    )(x, g_pre.reshape(1, D), wzt, wqt, wkt, wvt, wft, wgt, bf,
      g_sgu.reshape(1, SGU_WIDTH), b_sgu.reshape(1, SGU_WIDTH), tri, expand, expand_t,
      head_rows, piece_rows, *later_weights)


def _attn_kernel(plan_ref, k_ref, ka_ref, vt_ref, qt_ref, qat_ref, o_ref,
                 qf_ref, st_a, st_b, m_ref, acc_ref):
    batch = pl.program_id(0)
    pair = pl.program_id(1)
    S = k_ref.shape[1]
    row = lax.broadcasted_iota(jnp.int32, (PAIR, 1), 0)
    all_tiles = [(e, t) for t in range(NQ) for e in range(2)]

    def key_tile(k0):
        return jnp.concatenate([k_ref[0, pl.ds(k0, TK), :], ka_ref[0, pl.ds(k0, TK), :]], axis=1)

    def offsets(g, j):
        n, skew = plan_ref[batch, pair, g, 0], plan_ref[batch, pair, g, 1]
        first = (g * NQ - n - skew + j) * TK
        return [pl.multiple_of(first + t * skew * TK, TK) for t in range(NQ)]

    def update(e, t, st, k0, masked):
        cs = slice(t * TQ, (t + 1) * TQ)
        if masked:
            key_i = lax.broadcasted_iota(jnp.int32, (TK, TQ), 0)
            qry_i = lax.broadcasted_iota(jnp.int32, (TK, TQ), 1)
            st = jnp.where(key_i <= qry_i, st, -jnp.inf)
        m = m_ref[e, :, cs]
        m_new = jnp.maximum(m, jnp.max(st, axis=0, keepdims=True))
        p = jnp.exp2(st - m_new)
        alpha = jnp.exp2(m - m_new)
        m_ref[e, :, cs] = m_new
        acc_ref[e, :, cs] = alpha * acc_ref[e, :, cs] + _dot(
            vt_ref[0, e, :, pl.ds(k0, TK)], p.astype(_BF16))

    def step(cur, cur_tiles, cur_offs, nxt, nxt_tiles, nxt_offs):
        for n in range(max(len(cur_tiles), len(nxt_tiles))):
            if n < len(nxt_tiles):
                e, t = nxt_tiles[n]
                nxt[e, t] = _dot(key_tile(nxt_offs[t]), qf_ref[e, :, t * TQ:(t + 1) * TQ])
            if n < len(cur_tiles):
                e, t, masked = cur_tiles[n]
                update(e, t, cur[e, t], cur_offs[t], masked)

    def load_queries(g):
        q0 = pl.multiple_of(g * QG, QG)
        qt = qt_ref[0, :, pl.ds(q0, QG)].astype(_F32)
        qat = qat_ref[0, :, pl.ds(q0, QG)].astype(_F32)
        for e in range(2):
            a0 = N_SPLIT * (2 * pair + e)
            q_mask = jnp.logical_and(row >= HEAD_DIM * e, row < HEAD_DIM * (e + 1))
            a_mask = jnp.logical_or(
                jnp.logical_and(row >= a0, row < a0 + N_SPLIT),
                jnp.logical_and(row >= BIAS_COLS + a0, row < BIAS_COLS + a0 + N_SPLIT))
            qf_ref[e, :PAIR, :] = jnp.where(q_mask, qt, 0.0).astype(_BF16)
            qf_ref[e, PAIR:, :] = jnp.where(a_mask, qat, 0.0).astype(_BF16)

    def reset_state():
        m_ref[...] = jnp.full(m_ref.shape, -jnp.inf, _F32)
        acc_ref[...] = jnp.zeros(acc_ref.shape, _F32)

    full = [(e, t, False) for e, t in all_tiles]
    bufs = (st_a, st_b)
    n_groups = S // QG

    def finish(g):
        out = [acc_ref[e, :HEAD_DIM, :] * (1.0 / acc_ref[e, HEAD_DIM:HEAD_DIM + 1, :])
               for e in range(2)]
        o_ref[0, pl.ds(pl.multiple_of(g * QG, QG), QG), :] = (
            jnp.concatenate(out, axis=0).T.astype(_BF16))
        reset_state()

    def next_group(g):
        return jnp.minimum(g + 1, n_groups - 1)

    def triangle(g):
        q0 = pl.multiple_of(g * QG, QG)
        for i in range(NQ):
            cur_tiles = [(e, t, t == i) for t in range(i, NQ) for e in range(2)]
            cur_offs = [q0 + i * TK] * NQ
            if i + 1 < NQ:
                nxt_tiles = [(e, t) for t in range(i + 1, NQ) for e in range(2)]
                nxt_offs = [q0 + (i + 1) * TK] * NQ
            else:
                load_queries(next_group(g))
                nxt_tiles, nxt_offs = all_tiles, offsets(next_group(g), 0)
            step(bufs[i % 2], cur_tiles, cur_offs, bufs[(i + 1) % 2], nxt_tiles, nxt_offs)

    load_queries(0)
    reset_state()
    step(None, [], None, st_a, all_tiles, [0] * NQ)
    triangle(0)

    def q_group(g, _):
        n = plan_ref[batch, pair, g, 0]
        skew = plan_ref[batch, pair, g, 1]

        def full_steps(i, _):
            for u in range(KU):
                j = i * KU + u
                step(bufs[u % 2], full, offsets(g, j), bufs[(u + 1) % 2], all_tiles,
                     offsets(g, j + 1))
            return 0

        def skew_tail(j):
            step(st_a, full, offsets(g, j), st_b, all_tiles, offsets(g, j + 1))
            load_queries(next_group(g))
            step(st_b, [(e, t, True) for e, t in all_tiles], offsets(g, j + 1),
                 st_a, all_tiles, offsets(next_group(g), 0))

        unrolled = jnp.logical_and(skew == 1, functools.reduce(
            jnp.logical_or, [n == n_static for n_static in UNROLLED_SKEW_STEPS]))
        for n_static in UNROLLED_SKEW_STEPS:
            @pl.when(jnp.logical_and(skew == 1, n == n_static))
            def _(n_static=n_static):
                finish(g - 1)
                for j in range(n_static):
                    step(bufs[j % 2], full, offsets(g, j), bufs[(j + 1) % 2], all_tiles,
                         offsets(g, j + 1))
                skew_tail(n_static)

        @pl.when(jnp.logical_not(unrolled))
        def _():
            finish(g - 1)
            lax.fori_loop(0, n // KU, full_steps, 0)

        @pl.when(jnp.logical_and(skew == 1, jnp.logical_not(unrolled)))
        def _():
            skew_tail(n)

        @pl.when(skew == 0)
        def _():
            triangle(g)

        return 0

    lax.fori_loop(1, n_groups, q_group, 0)
    finish(n_groups - 1)


def _sweep_plan(stats, n_tiles):
    cum_first, cum_last, qn2, kn2 = (stats[:, s, :N_HEADS, :n_tiles] for s in range(N_STATS))
    B = stats.shape[0]
    n_groups = n_tiles // NQ
    k_norm = jnp.sqrt(jnp.max(kn2, axis=-1, keepdims=True))
    reach = NORM_SLACK * jnp.sqrt(qn2) * k_norm + cum_first
    bound = reach[..., None] - cum_last[:, :, None, :]
    tile = jnp.arange(n_tiles)
    needed = jnp.logical_and(tile[None, :] < tile[:, None], bound >= -ZERO_EXP2)
    w = jnp.sum(needed, axis=-1).astype(jnp.int32)
    w = jnp.max(w.reshape(B, N_PAIRS, 2, n_groups, NQ), axis=2)
    start = jnp.arange(n_groups, dtype=jnp.int32) * NQ
    n_flat = jnp.max(jnp.maximum(w - jnp.arange(NQ, dtype=jnp.int32), 0), axis=-1)
    n_flat = jnp.minimum(((n_flat + KU - 1) // KU) * KU, start)
    n_skew = jnp.maximum(jnp.max(w, axis=-1), 1)
    n_skew = n_skew + (n_skew + 1) % 2
    units_flat = NQ * n_flat + NQ * (NQ + 1) // 2
    units_skew = NQ * n_skew + NQ
    skew = jnp.logical_and(n_skew < start, units_skew < units_flat)
    n = jnp.where(skew, n_skew - 1, n_flat)
    return jnp.stack([n, skew.astype(jnp.int32)], axis=-1)


def _attention(k, ka, vt, qt, qat, stats):
    B, S, _ = k.shape
    assert TQ == TK and KU == 2 and NQ % KU == 0 and S % QG == 0
    plan = _sweep_plan(stats, S // TK)
    pair_rows = pl.BlockSpec((1, S, PAIR), lambda b, j, nb: (b, 0, j))
    return pl.pallas_call(
        _attn_kernel,
        grid_spec=pltpu.PrefetchScalarGridSpec(
            num_scalar_prefetch=1,
            grid=(B, N_PAIRS),
            in_specs=[pair_rows, pl.BlockSpec((1, S, LANES), lambda b, j, nb: (b, 0, 0)),
                      pl.BlockSpec((1, 2, PAIR, S), lambda b, j, nb: (b, j, 0, 0)),
                      pl.BlockSpec((1, PAIR, S), lambda b, j, nb: (b, j, 0)),
                      pl.BlockSpec((1, LANES, S), lambda b, j, nb: (b, 0, 0))],
            out_specs=pair_rows,
            scratch_shapes=[pltpu.VMEM((2, PAIR + LANES, QG), _BF16),
                            pltpu.VMEM((2, NQ, TK, TQ), _F32), pltpu.VMEM((2, NQ, TK, TQ), _F32),
                            pltpu.VMEM((2, 1, QG), _F32), pltpu.VMEM((2, PAIR, QG), _F32)]),
        out_shape=jax.ShapeDtypeStruct((B, S, ATTN_WIDTH), _BF16),
        compiler_params=pltpu.CompilerParams(
            dimension_semantics=("parallel", "parallel"),
            vmem_limit_bytes=VMEM_LIMIT),
        name="forgetting_attention",
    )(plan, k, ka, vt, qt, qat)


def _rms_scale(v, g):
    ms = jnp.mean(v * v, axis=-1, keepdims=True)
    return (v * lax.rsqrt(ms + EPS)) * g


def _merge_mlp_kernel(x_ref, u_ref, vn_ref, ya_ref, gates_ref, wsp_ref, bsp_ref,
                      wbs_ref, wba_ref, wout_ref, gpost_ref,
                      gpre2_ref, wup_ref, wdown_ref, gpost2_ref, o_ref, ysgu_ref):
    tm = x_ref.shape[0]
    row = lax.broadcasted_iota(jnp.int32, (CHUNK, CHUNK), 0)
    col = lax.broadcasted_iota(jnp.int32, (CHUNK, CHUNK), 1)
    lane = lax.broadcasted_iota(jnp.int32, (CHUNK, PAIR), 1)
    ws = [jnp.where(row >= col, wsp_ref[g], 0.0).astype(_BF16) for g in range(N_GROUPS)]
    for c in range(tm // CHUNK):
        r = slice(c * CHUNK, (c + 1) * CHUNK)
        for j in range(N_GROUPS // 2):
            cs = slice(j * PAIR, (j + 1) * PAIR)
            vp = vn_ref[r, cs]
            s = jnp.where(lane < HEAD_DIM, _dot(ws[2 * j], vp), _dot(ws[2 * j + 1], vp))
            s = s + bsp_ref[:, cs]
            ysgu_ref[r, cs] = (u_ref[r, cs].astype(_F32) * s).astype(_BF16)

    blocks = [slice(n * tm // N_STREAMS, (n + 1) * tm // N_STREAMS) for n in range(N_STREAMS)]

    def mix(r):
        a = _dot(ysgu_ref[r, :], wbs_ref[...])
        b = _dot(ya_ref[r, :], wba_ref[...])
        merged = (gates_ref[r, :D_MODEL].astype(_F32) * a
                  + gates_ref[r, D_MODEL:].astype(_F32) * b).astype(_BF16)
        return _dot(merged, wout_ref[...])

    def mlp(xb):
        acc = jnp.zeros(xb.shape, _F32)
        for c in range(D_FF // FF_TILE):
            cs = slice(c * FF_TILE, (c + 1) * FF_TILE)
            hid = jnp.square(jnp.maximum(_dot(xb, wup_ref[:, cs]), 0.0)).astype(_BF16)
            acc = acc + _dot(hid, wdown_ref[cs, :])
        return acc

    mixed = [mix(r) for r in blocks]
    h1 = [x_ref[r, :] + _rms_scale(o, gpost_ref[...]) for r, o in zip(blocks, mixed)]
    xb = [_rms_scale(h, gpre2_ref[...]).astype(_BF16) for h in h1]
    ff = [mlp(v) for v in xb]
    for r, h, f in zip(blocks, h1, ff):
        o_ref[r, :] = h + _rms_scale(f, gpost2_ref[...])


def _merge_mlp(x2, u2, vn2, ya2, gates2, w_spatial, b_spatial, w_bs, w_ba, w_out, g_post,
               g_pre2, w_up, w_down, g_post2):
    R, D = x2.shape
    tm = ROW_TILE
    bsp = jnp.repeat(b_spatial.T, SGU_WIDTH // N_GROUPS, axis=1)

    def const(shape):
        return pl.BlockSpec(shape, lambda i: (0,) * len(shape), pipeline_mode=pl.Buffered(1))

    def rows(width):
        return pl.BlockSpec((tm, width), lambda i: (i, 0))

    return pl.pallas_call(
        _merge_mlp_kernel,
        grid=(R // tm,),
        in_specs=[rows(D), rows(SGU_WIDTH), rows(SGU_WIDTH), rows(ATTN_WIDTH), rows(2 * D),
                  const(w_spatial.shape), const(bsp.shape),
                  const(w_bs.shape), const(w_ba.shape), const(w_out.shape), const((1, D)),
                  const((1, D)), const(w_up.shape), const(w_down.shape), const((1, D))],
        out_specs=rows(D),
        out_shape=jax.ShapeDtypeStruct((R, D), _F32),
        scratch_shapes=[pltpu.VMEM((tm, SGU_WIDTH), _BF16)],
        compiler_params=pltpu.CompilerParams(
            dimension_semantics=("parallel",), vmem_limit_bytes=VMEM_LIMIT),
        name="merge_mlp",
    )(x2, u2, vn2, ya2, gates2, w_spatial, bsp,
      w_bs, w_ba, w_out, g_post.reshape(1, D),
      g_pre2.reshape(1, D), w_up, w_down, g_post2.reshape(1, D))


def kernel(x, g_mix_pre, w_in, b_forget, g_sgu, b_sgu, w_spatial, b_spatial, w_branch_sgu,
           w_branch_attn, w_out, g_mix_post, g_ffn_pre, w_up, w_down, g_ffn_post):
    B, S, D = x.shape
    h = x
    for l in range(g_mix_pre.shape[0]):
        later = (w_branch_sgu[l], w_branch_attn[l], w_out[l], w_up[l], w_down[l])
        u, vn, qt, k, vt, gates, qat, ka, stats, w_bs, w_ba, w_o, w_u, w_d = _in_projection(
            h, g_mix_pre[l], w_in[l], b_forget[l], g_sgu[l], b_sgu[l], later)
        y_attn = _attention(k, ka, vt, qt, qat, stats)
        h = _merge_mlp(h.reshape(B * S, D), u.reshape(B * S, -1), vn.reshape(B * S, -1),
                       y_attn.reshape(B * S, -1), gates.reshape(B * S, -1),
                       w_spatial[l], b_spatial[l], w_bs, w_ba, w_o,
                       g_mix_post[l], g_ffn_pre[l], w_u, w_d,
                       g_ffn_post[l]).reshape(B, S, D)
    return h
```

```python
import functools

import jax
import jax.numpy as jnp
import numpy as np
from jax import lax
from jax.experimental import pallas as pl
from jax.experimental.pallas import tpu as pltpu

D_MODEL = 1024
N_HEADS = 8
HEAD_DIM = 64
ATTN_WIDTH = N_HEADS * HEAD_DIM
N_GROUPS = 8
SGU_WIDTH = D_MODEL // 2
CHUNK = 128
D_FF = 4 * D_MODEL
EPS = 1e-6
LOG2E = 1.4426950408889634

LANES = 128
N_SPLIT = 3
BIAS_COLS = N_HEADS * N_SPLIT
PAIR = 2 * HEAD_DIM
N_PAIRS = N_HEADS // 2

ROW_TILE = 512
FF_TILE = 1024
N_STREAMS = 2
TQ = 256
TK = 256
NQ = 4
QG = NQ * TQ
KU = 2
UNROLLED_SKEW_STEPS = (4, 6)
N_STATS = 4
ZERO_EXP2 = 160.0
NORM_SLACK = 2.05
V7X_VMEM_BYTES = 64 * 1024 * 1024
VMEM_LIMIT = V7X_VMEM_BYTES * 7 // 8
BF16_SUBLANES = 16

_BF16 = jnp.bfloat16
_F32 = jnp.float32


def _split_bf16(x):
    parts = []
    r = x
    for _ in range(N_SPLIT):
        p = r.astype(_BF16)
        parts.append(p)
        r = r - p.astype(_F32)
    return parts


def _dot(a, b):
    return jnp.dot(a, b, preferred_element_type=_F32)


def _dot_nt(a, b):
    return lax.dot_general(a, b, (((1,), (1,)), ((), ())), preferred_element_type=_F32)


def _inproj_kernel(x_ref, gpre_ref, wzt_ref, wqt_ref, wkt_ref, wvt_ref, wft_ref, wgt_ref,
                   bf_ref, gsgu_ref, bsgu_ref, tri_ref, expand_ref, expand_t_ref,
                   head_rows_ref, piece_rows_ref, *rest, n_cast):
    cast_in = rest[:n_cast]
    (u_ref, vn_ref, qt_ref, k_ref, vt_ref, gates_ref, qat_ref, ka_ref,
     stats_ref) = rest[n_cast:n_cast + 9]
    cast_out = rest[n_cast + 9:2 * n_cast + 9]
    carry_ref, = rest[2 * n_cast + 9:]
    for src, dst in zip(cast_in, cast_out):
        dst[...] = src[...].astype(_BF16)

    @pl.when(pl.program_id(1) == 0)
    def _():
        carry_ref[...] = jnp.zeros_like(carry_ref)
        stats_ref[...] = jnp.zeros_like(stats_ref)

    x = x_ref[0]
    tm = x.shape[0]
    ms = jnp.mean(x * x, axis=-1, keepdims=True)
    xb = ((x * lax.rsqrt(ms + EPS)) * gpre_ref[...]).astype(_BF16)

    lane = lax.broadcasted_iota(jnp.int32, (tm, LANES), 1)

    def pack_pieces(v):
        hi, mid, lo = _split_bf16(v)
        zero = jnp.zeros((), _BF16)
        return jnp.where(lane < N_HEADS, hi,
                         jnp.where(lane < 2 * N_HEADS, mid,
                                   jnp.where(lane < N_SPLIT * N_HEADS, lo, zero)))

    f = _dot_nt(xb, wft_ref[...]) + bf_ref[...]

    z = jax.nn.gelu(_dot_nt(xb, wzt_ref[...]), approximate=True)
    u_ref[0] = z[:, :SGU_WIDTH].astype(_BF16)
    v = z[:, SGU_WIDTH:]
    mu = jnp.mean(v, axis=-1, keepdims=True)
    vc = v - mu
    var = jnp.mean(vc * vc, axis=-1, keepdims=True)
    vn_ref[0] = ((vc * lax.rsqrt(var + EPS)) * gsgu_ref[...] + bsgu_ref[...]).astype(_BF16)

    log_f = jnp.minimum(f, 0.0) - jnp.log(1.0 + jnp.exp(-jnp.abs(f)))
    sums = _dot(tri_ref[...], pack_pieces(log_f))

    qt = _dot_nt(wqt_ref[...], xb) * (HEAD_DIM ** -0.5 * LOG2E)
    qt_ref[0] = qt.astype(_BF16)
    k = _dot_nt(xb, wkt_ref[...])
    k_ref[0] = k.astype(_BF16)
    qn2 = _dot(head_rows_ref[...], (qt * qt).astype(_BF16))
    kn2 = _dot_nt(head_rows_ref[...], (k * k).astype(_BF16))

    total = sums
    for shift in (N_HEADS, 2 * N_HEADS, LANES - N_HEADS, LANES - 2 * N_HEADS):
        total = total + pltpu.roll(sums, shift, 1)
    cum = carry_ref[0:1, :] + total
    carry_ref[0:1, :] = cum[tm - 1:tm, :]
    pieces = pack_pieces(cum * LOG2E)
    spread = _dot(pieces, expand_ref[...])
    spread_t = _dot_nt(expand_t_ref[...], pieces)

    cum_t = _dot_nt(piece_rows_ref[...], pieces)
    tile_lane = lax.broadcasted_iota(jnp.int32, (2 * N_HEADS, LANES), 1)
    first_tile = pl.program_id(1) * (tm // TK)
    stats = [stats_ref[0, s] for s in range(N_STATS)]
    for j in range(tm // TK):
        lo, hi = j * TK, (j + 1) * TK
        cols = (cum_t[:, lo:lo + 1], cum_t[:, hi - 1:hi],
                jnp.max(qn2[:, lo:hi], axis=1, keepdims=True),
                jnp.max(kn2[:, lo:hi], axis=1, keepdims=True))
        stats = [jnp.where(tile_lane == first_tile + j, c, s) for c, s in zip(cols, stats)]
    for s in range(N_STATS):
        stats_ref[0, s] = stats[s]

    vt = _dot_nt(wvt_ref[...], xb).astype(_BF16)
    for h in range(N_HEADS):
        vt_ref[0, h, :HEAD_DIM, :] = vt[h * HEAD_DIM:(h + 1) * HEAD_DIM, :]
        vt_ref[0, h, HEAD_DIM:, :] = jnp.ones((PAIR - HEAD_DIM, tm), _BF16)

    gates_ref[0] = jax.nn.sigmoid(_dot_nt(xb, wgt_ref[...])).astype(_BF16)

    in_a = lane < BIAS_COLS
    in_b = jnp.logical_and(lane >= BIAS_COLS, lane < 2 * BIAS_COLS)
    ka_ref[0] = jnp.where(in_a, -spread, jnp.where(in_b, 1.0, 0.0)).astype(_BF16)
    row = lax.broadcasted_iota(jnp.int32, (LANES, tm), 0)
    in_a = row < BIAS_COLS
    in_b = jnp.logical_and(row >= BIAS_COLS, row < 2 * BIAS_COLS)
    qat_ref[0] = jnp.where(in_a, 1.0, jnp.where(in_b, spread_t, 0.0)).astype(_BF16)


def _head_rows():
    r = np.zeros((2 * N_HEADS, ATTN_WIDTH), np.float32)
    for h in range(N_HEADS):
        r[h, h * HEAD_DIM:(h + 1) * HEAD_DIM] = 1.0
    return r


def _piece_rows():
    r = np.zeros((2 * N_HEADS, LANES), np.float32)
    for h in range(N_HEADS):
        for i in range(N_SPLIT):
            r[h, N_HEADS * i + h] = 1.0
    return r


def _expand_matrix():
    e = np.zeros((LANES, LANES), np.float32)
    for h in range(N_HEADS):
        for i in range(N_SPLIT):
            e[N_HEADS * i + h, N_SPLIT * h + i] = 1.0
            e[N_HEADS * i + h, BIAS_COLS + N_SPLIT * h + i] = 1.0
    return e


_IN_OFFSETS = tuple(int(v) for v in np.cumsum(
    (0, 2 * SGU_WIDTH, ATTN_WIDTH, ATTN_WIDTH, ATTN_WIDTH, N_HEADS, 2 * D_MODEL)))


def _in_projection(x, g_pre, w_in, b_forget, g_sgu, b_sgu, later_weights):
    B, S, D = x.shape
    tm = ROW_TILE
    n_steps = B * (S // tm)
    o = _IN_OFFSETS
    wt = jnp.swapaxes(w_in, 0, 1)
    wzt, wqt, wkt, wvt, wgt = (wt[o[i]:o[i + 1]].astype(_BF16) for i in (0, 1, 2, 3, 5))
    row_pad = ((0, LANES - N_SPLIT * N_HEADS), (0, 0))
    wft = jnp.pad(jnp.tile(wt[o[4]:o[5]], (N_SPLIT, 1)), row_pad).astype(_BF16)
    lane_pad = ((0, 0), (0, LANES - N_SPLIT * N_HEADS))
    bf = jnp.pad(jnp.tile(b_forget.reshape(1, N_HEADS), (1, N_SPLIT)), lane_pad)
    tri = jnp.asarray(np.tril(np.ones((tm, tm), np.float32)), _BF16)
    expand = jnp.asarray(_expand_matrix(), _BF16)
    expand_t = jnp.asarray(_expand_matrix().T, _BF16)
    head_rows = jnp.asarray(_head_rows(), _BF16)
    piece_rows = jnp.asarray(_piece_rows(), _BF16)

    def const(shape):
        return pl.BlockSpec(shape, lambda b, i: (0,) * len(shape))

    def rows(width):
        return pl.BlockSpec((1, tm, width), lambda b, i: (b, i, 0))

    out_shape = (
        jax.ShapeDtypeStruct((B, S, SGU_WIDTH), _BF16),
        jax.ShapeDtypeStruct((B, S, SGU_WIDTH), _BF16),
        jax.ShapeDtypeStruct((B, ATTN_WIDTH, S), _BF16),
        jax.ShapeDtypeStruct((B, S, ATTN_WIDTH), _BF16),
        jax.ShapeDtypeStruct((B, N_HEADS, PAIR, S), _BF16),
        jax.ShapeDtypeStruct((B, S, 2 * D_MODEL), _BF16),
        jax.ShapeDtypeStruct((B, LANES, S), _BF16),
        jax.ShapeDtypeStruct((B, S, LANES), _BF16),
        jax.ShapeDtypeStruct((B, N_STATS, 2 * N_HEADS, LANES), _F32),
    )

    def cols(height):
        return pl.BlockSpec((1, height, tm), lambda b, i: (b, 0, i))

    def row_block(w):
        assert w.shape[0] % (n_steps * BF16_SUBLANES) == 0
        return pl.BlockSpec((w.shape[0] // n_steps, w.shape[1]),
                            lambda b, i: (b * (S // tm) + i, 0))

    out_specs = (
        rows(SGU_WIDTH), rows(SGU_WIDTH), cols(ATTN_WIDTH), rows(ATTN_WIDTH),
        pl.BlockSpec((1, N_HEADS, PAIR, tm), lambda b, i: (b, 0, 0, i)),
        rows(2 * D_MODEL), cols(LANES), rows(LANES),
        pl.BlockSpec((1, N_STATS, 2 * N_HEADS, LANES), lambda b, i: (b, 0, 0, 0)),
    ) + tuple(row_block(w) for w in later_weights)
    out_shape += tuple(jax.ShapeDtypeStruct(w.shape, _BF16) for w in later_weights)
    return pl.pallas_call(
        functools.partial(_inproj_kernel, n_cast=len(later_weights)),
        grid=(B, S // tm),
        in_specs=[
            rows(D), const((1, D)),
            const(wzt.shape), const(wqt.shape), const(wkt.shape), const(wvt.shape),
            const(wft.shape), const(wgt.shape),
            const((1, LANES)), const((1, SGU_WIDTH)), const((1, SGU_WIDTH)),
            const(tri.shape), const(expand.shape), const(expand_t.shape),
            const(head_rows.shape), const(piece_rows.shape),
        ] + [row_block(w) for w in later_weights],
        out_specs=out_specs,
        out_shape=out_shape,
        scratch_shapes=[pltpu.VMEM((8, LANES), _F32)],
        compiler_params=pltpu.CompilerParams(
            dimension_semantics=("arbitrary", "arbitrary"),
            vmem_limit_bytes=VMEM_LIMIT),
        name="in_projection",
    )(x, g_pre.reshape(1, D), wzt, wqt, wkt, wvt, wft, wgt, bf,
      g_sgu.reshape(1, SGU_WIDTH), b_sgu.reshape(1, SGU_WIDTH), tri, expand, expand_t,
      head_rows, piece_rows, *later_weights)


def _attn_kernel(plan_ref, k_ref, ka_ref, vt_ref, qt_ref, qat_ref, o_ref,
                 qf_ref, st_a, st_b, m_ref, acc_ref):
    batch = pl.program_id(0)
    pair = pl.program_id(1)
    S = k_ref.shape[1]
    row = lax.broadcasted_iota(jnp.int32, (PAIR, 1), 0)
    all_tiles = [(e, t) for t in range(NQ) for e in range(2)]

    def key_tile(k0):
        return jnp.concatenate([k_ref[0, pl.ds(k0, TK), :], ka_ref[0, pl.ds(k0, TK), :]], axis=1)

    def offsets(g, j):
        n, skew = plan_ref[batch, pair, g, 0], plan_ref[batch, pair, g, 1]
        first = (g * NQ - n - skew + j) * TK
        return [pl.multiple_of(first + t * skew * TK, TK) for t in range(NQ)]

    def update(e, t, st, k0, masked):
        cs = slice(t * TQ, (t + 1) * TQ)
        if masked:
            key_i = lax.broadcasted_iota(jnp.int32, (TK, TQ), 0)
            qry_i = lax.broadcasted_iota(jnp.int32, (TK, TQ), 1)
            st = jnp.where(key_i <= qry_i, st, -jnp.inf)
        m = m_ref[e, :, cs]
        m_new = jnp.maximum(m, jnp.max(st, axis=0, keepdims=True))
        p = jnp.exp2(st - m_new)
        alpha = jnp.exp2(m - m_new)
        m_ref[e, :, cs] = m_new
        acc_ref[e, :, cs] = alpha * acc_ref[e, :, cs] + _dot(
            vt_ref[0, e, :, pl.ds(k0, TK)], p.astype(_BF16))

    def step(cur, cur_tiles, cur_offs, nxt, nxt_tiles, nxt_offs):
        for n in range(max(len(cur_tiles), len(nxt_tiles))):
            if n < len(nxt_tiles):
                e, t = nxt_tiles[n]
                nxt[e, t] = _dot(key_tile(nxt_offs[t]), qf_ref[e, :, t * TQ:(t + 1) * TQ])
            if n < len(cur_tiles):
                e, t, masked = cur_tiles[n]
                update(e, t, cur[e, t], cur_offs[t], masked)

    def load_queries(g):
        q0 = pl.multiple_of(g * QG, QG)
        qt = qt_ref[0, :, pl.ds(q0, QG)].astype(_F32)
        qat = qat_ref[0, :, pl.ds(q0, QG)].astype(_F32)
        for e in range(2):
            a0 = N_SPLIT * (2 * pair + e)
            q_mask = jnp.logical_and(row >= HEAD_DIM * e, row < HEAD_DIM * (e + 1))
            a_mask = jnp.logical_or(
                jnp.logical_and(row >= a0, row < a0 + N_SPLIT),
                jnp.logical_and(row >= BIAS_COLS + a0, row < BIAS_COLS + a0 + N_SPLIT))
            qf_ref[e, :PAIR, :] = jnp.where(q_mask, qt, 0.0).astype(_BF16)
            qf_ref[e, PAIR:, :] = jnp.where(a_mask, qat, 0.0).astype(_BF16)

    def reset_state():
        m_ref[...] = jnp.full(m_ref.shape, -jnp.inf, _F32)
        acc_ref[...] = jnp.zeros(acc_ref.shape, _F32)

    full = [(e, t, False) for e, t in all_tiles]
    bufs = (st_a, st_b)
    n_groups = S // QG

    def finish(g):
        out = [acc_ref[e, :HEAD_DIM, :] * (1.0 / acc_ref[e, HEAD_DIM:HEAD_DIM + 1, :])
               for e in range(2)]
        o_ref[0, pl.ds(pl.multiple_of(g * QG, QG), QG), :] = (
            jnp.concatenate(out, axis=0).T.astype(_BF16))
        reset_state()

    def next_group(g):
        return jnp.minimum(g + 1, n_groups - 1)

    def triangle(g):
        q0 = pl.multiple_of(g * QG, QG)
        for i in range(NQ):
            cur_tiles = [(e, t, t == i) for t in range(i, NQ) for e in range(2)]
            cur_offs = [q0 + i * TK] * NQ
            if i + 1 < NQ:
                nxt_tiles = [(e, t) for t in range(i + 1, NQ) for e in range(2)]
                nxt_offs = [q0 + (i + 1) * TK] * NQ
            else:
                load_queries(next_group(g))
                nxt_tiles, nxt_offs = all_tiles, offsets(next_group(g), 0)
            step(bufs[i % 2], cur_tiles, cur_offs, bufs[(i + 1) % 2], nxt_tiles, nxt_offs)

    load_queries(0)
    reset_state()
    step(None, [], None, st_a, all_tiles, [0] * NQ)
    triangle(0)

    def q_group(g, _):
        n = plan_ref[batch, pair, g, 0]
        skew = plan_ref[batch, pair, g, 1]

        def full_steps(i, _):
            for u in range(KU):
                j = i * KU + u
                step(bufs[u % 2], full, offsets(g, j), bufs[(u + 1) % 2], all_tiles,
                     offsets(g, j + 1))
            return 0

        def skew_tail(j):
            step(st_a, full, offsets(g, j), st_b, all_tiles, offsets(g, j + 1))
            load_queries(next_group(g))
            step(st_b, [(e, t, True) for e, t in all_tiles], offsets(g, j + 1),
                 st_a, all_tiles, offsets(next_group(g), 0))

        unrolled = jnp.logical_and(skew == 1, functools.reduce(
            jnp.logical_or, [n == n_static for n_static in UNROLLED_SKEW_STEPS]))
        for n_static in UNROLLED_SKEW_STEPS:
            @pl.when(jnp.logical_and(skew == 1, n == n_static))
            def _(n_static=n_static):
                finish(g - 1)
                for j in range(n_static):
                    step(bufs[j % 2], full, offsets(g, j), bufs[(j + 1) % 2], all_tiles,
                         offsets(g, j + 1))
                skew_tail(n_static)

        @pl.when(jnp.logical_not(unrolled))
        def _():
            finish(g - 1)
            lax.fori_loop(0, n // KU, full_steps, 0)

        @pl.when(jnp.logical_and(skew == 1, jnp.logical_not(unrolled)))
        def _():
            skew_tail(n)

        @pl.when(skew == 0)
        def _():
            triangle(g)

        return 0

    lax.fori_loop(1, n_groups, q_group, 0)
    finish(n_groups - 1)


def _sweep_plan(stats, n_tiles):
    cum_first, cum_last, qn2, kn2 = (stats[:, s, :N_HEADS, :n_tiles] for s in range(N_STATS))
    B = stats.shape[0]
    n_groups = n_tiles // NQ
    k_norm = jnp.sqrt(jnp.max(kn2, axis=-1, keepdims=True))
    reach = NORM_SLACK * jnp.sqrt(qn2) * k_norm + cum_first
    bound = reach[..., None] - cum_last[:, :, None, :]
    tile = jnp.arange(n_tiles)
    needed = jnp.logical_and(tile[None, :] < tile[:, None], bound >= -ZERO_EXP2)
    w = jnp.sum(needed, axis=-1).astype(jnp.int32)
    w = jnp.max(w.reshape(B, N_PAIRS, 2, n_groups, NQ), axis=2)
    start = jnp.arange(n_groups, dtype=jnp.int32) * NQ
    n_flat = jnp.max(jnp.maximum(w - jnp.arange(NQ, dtype=jnp.int32), 0), axis=-1)
    n_flat = jnp.minimum(((n_flat + KU - 1) // KU) * KU, start)
    n_skew = jnp.maximum(jnp.max(w, axis=-1), 1)
    n_skew = n_skew + (n_skew + 1) % 2
    units_flat = NQ * n_flat + NQ * (NQ + 1) // 2
    units_skew = NQ * n_skew + NQ
    skew = jnp.logical_and(n_skew < start, units_skew < units_flat)
    n = jnp.where(skew, n_skew - 1, n_flat)
    return jnp.stack([n, skew.astype(jnp.int32)], axis=-1)


def _attention(k, ka, vt, qt, qat, stats):
    B, S, _ = k.shape
    assert TQ == TK and KU == 2 and NQ % KU == 0 and S % QG == 0
    assert S // TK <= LANES
    plan = _sweep_plan(stats, S // TK)
    pair_rows = pl.BlockSpec((1, S, PAIR), lambda b, j, nb: (b, 0, j))
    return pl.pallas_call(
        _attn_kernel,
        grid_spec=pltpu.PrefetchScalarGridSpec(
            num_scalar_prefetch=1,
            grid=(B, N_PAIRS),
            in_specs=[pair_rows, pl.BlockSpec((1, S, LANES), lambda b, j, nb: (b, 0, 0)),
                      pl.BlockSpec((1, 2, PAIR, S), lambda b, j, nb: (b, j, 0, 0)),
                      pl.BlockSpec((1, PAIR, S), lambda b, j, nb: (b, j, 0)),
                      pl.BlockSpec((1, LANES, S), lambda b, j, nb: (b, 0, 0))],
            out_specs=pair_rows,
            scratch_shapes=[pltpu.VMEM((2, PAIR + LANES, QG), _BF16),
                            pltpu.VMEM((2, NQ, TK, TQ), _F32), pltpu.VMEM((2, NQ, TK, TQ), _F32),
                            pltpu.VMEM((2, 1, QG), _F32), pltpu.VMEM((2, PAIR, QG), _F32)]),
        out_shape=jax.ShapeDtypeStruct((B, S, ATTN_WIDTH), _BF16),
        compiler_params=pltpu.CompilerParams(
            dimension_semantics=("parallel", "parallel"),
            vmem_limit_bytes=VMEM_LIMIT),
        name="forgetting_attention",
    )(plan, k, ka, vt, qt, qat)


def _rms_scale(v, g):
    ms = jnp.mean(v * v, axis=-1, keepdims=True)
    return (v * lax.rsqrt(ms + EPS)) * g


def _merge_mlp_kernel(x_ref, u_ref, vn_ref, ya_ref, gates_ref, wsp_ref, bsp_ref,
                      wbs_ref, wba_ref, wout_ref, gpost_ref,
                      gpre2_ref, wup_ref, wdown_ref, gpost2_ref, o_ref, ysgu_ref):
    tm = x_ref.shape[0]
    row = lax.broadcasted_iota(jnp.int32, (CHUNK, CHUNK), 0)
    col = lax.broadcasted_iota(jnp.int32, (CHUNK, CHUNK), 1)
    lane = lax.broadcasted_iota(jnp.int32, (CHUNK, PAIR), 1)
    ws = [jnp.where(row >= col, wsp_ref[g], 0.0).astype(_BF16) for g in range(N_GROUPS)]
    for c in range(tm // CHUNK):
        r = slice(c * CHUNK, (c + 1) * CHUNK)
        for j in range(N_GROUPS // 2):
            cs = slice(j * PAIR, (j + 1) * PAIR)
            vp = vn_ref[r, cs]
            s = jnp.where(lane < HEAD_DIM, _dot(ws[2 * j], vp), _dot(ws[2 * j + 1], vp))
            s = s + bsp_ref[:, cs]
            ysgu_ref[r, cs] = (u_ref[r, cs].astype(_F32) * s).astype(_BF16)

    blocks = [slice(n * tm // N_STREAMS, (n + 1) * tm // N_STREAMS) for n in range(N_STREAMS)]

    def mix(r):
        a = _dot(ysgu_ref[r, :], wbs_ref[...])
        b = _dot(ya_ref[r, :], wba_ref[...])
        merged = (gates_ref[r, :D_MODEL].astype(_F32) * a
                  + gates_ref[r, D_MODEL:].astype(_F32) * b).astype(_BF16)
        return _dot(merged, wout_ref[...])

    def mlp(xb):
        acc = jnp.zeros(xb.shape, _F32)
        for c in range(D_FF // FF_TILE):
            cs = slice(c * FF_TILE, (c + 1) * FF_TILE)
            hid = jnp.square(jnp.maximum(_dot(xb, wup_ref[:, cs]), 0.0)).astype(_BF16)
            acc = acc + _dot(hid, wdown_ref[cs, :])
        return acc

    mixed = [mix(r) for r in blocks]
    h1 = [x_ref[r, :] + _rms_scale(o, gpost_ref[...]) for r, o in zip(blocks, mixed)]
    xb = [_rms_scale(h, gpre2_ref[...]).astype(_BF16) for h in h1]
    ff = [mlp(v) for v in xb]
    for r, h, f in zip(blocks, h1, ff):
        o_ref[r, :] = h + _rms_scale(f, gpost2_ref[...])


def _merge_mlp(x2, u2, vn2, ya2, gates2, w_spatial, b_spatial, w_bs, w_ba, w_out, g_post,
               g_pre2, w_up, w_down, g_post2):
    R, D = x2.shape
    tm = ROW_TILE
    bsp = jnp.repeat(b_spatial.T, SGU_WIDTH // N_GROUPS, axis=1)

    def const(shape):
        return pl.BlockSpec(shape, lambda i: (0,) * len(shape), pipeline_mode=pl.Buffered(1))

    def rows(width):
        return pl.BlockSpec((tm, width), lambda i: (i, 0))

    return pl.pallas_call(
        _merge_mlp_kernel,
        grid=(R // tm,),
        in_specs=[rows(D), rows(SGU_WIDTH), rows(SGU_WIDTH), rows(ATTN_WIDTH), rows(2 * D),
                  const(w_spatial.shape), const(bsp.shape),
                  const(w_bs.shape), const(w_ba.shape), const(w_out.shape), const((1, D)),
                  const((1, D)), const(w_up.shape), const(w_down.shape), const((1, D))],
        out_specs=rows(D),
        out_shape=jax.ShapeDtypeStruct((R, D), _F32),
        scratch_shapes=[pltpu.VMEM((tm, SGU_WIDTH), _BF16)],
        compiler_params=pltpu.CompilerParams(
            dimension_semantics=("parallel",), vmem_limit_bytes=VMEM_LIMIT),
        name="merge_mlp",
    )(x2, u2, vn2, ya2, gates2, w_spatial, bsp,
      w_bs, w_ba, w_out, g_post.reshape(1, D),
      g_pre2.reshape(1, D), w_up, w_down, g_post2.reshape(1, D))


def kernel(x, g_mix_pre, w_in, b_forget, g_sgu, b_sgu, w_spatial, b_spatial, w_branch_sgu,
           w_branch_attn, w_out, g_mix_post, g_ffn_pre, w_up, w_down, g_ffn_post):
    B, S, D = x.shape
    h = x
    for l in range(g_mix_pre.shape[0]):
        later = (w_branch_sgu[l], w_branch_attn[l], w_out[l], w_up[l], w_down[l])
        u, vn, qt, k, vt, gates, qat, ka, stats, w_bs, w_ba, w_o, w_u, w_d = _in_projection(
            h, g_mix_pre[l], w_in[l], b_forget[l], g_sgu[l], b_sgu[l], later)
        y_attn = _attention(k, ka, vt, qt, qat, stats)
        h = _merge_mlp(h.reshape(B * S, D), u.reshape(B * S, -1), vn.reshape(B * S, -1),
                       y_attn.reshape(B * S, -1), gates.reshape(B * S, -1),
                       w_spatial[l], b_spatial[l], w_bs, w_ba, w_o,
                       g_mix_post[l], g_ffn_pre[l], w_u, w_d,
                       g_ffn_post[l]).reshape(B, S, D)
    return h
```

```python
import functools

import jax
import jax.numpy as jnp
import numpy as np
from jax import lax
from jax.experimental import pallas as pl
from jax.experimental.pallas import tpu as pltpu

D_MODEL = 1024
N_HEADS = 8
HEAD_DIM = 64
ATTN_WIDTH = N_HEADS * HEAD_DIM
N_GROUPS = 8
SGU_WIDTH = D_MODEL // 2
CHUNK = 128
D_FF = 4 * D_MODEL
EPS = 1e-6
LOG2E = 1.4426950408889634

LANES = 128
N_SPLIT = 3
BIAS_COLS = N_HEADS * N_SPLIT
PAIR = 2 * HEAD_DIM
N_PAIRS = N_HEADS // 2

ROW_TILE = 512
FF_TILE = 1024
N_STREAMS = 2
TQ = 256
TK = 256
NQ = 4
QG = NQ * TQ
KU = 2
UNROLLED_SKEW_STEPS = (3, 4, 5, 6)
N_STATS = 4
ZERO_EXP2 = 136.0
NORM_SLACK = 2.05
V7X_VMEM_BYTES = 64 * 1024 * 1024
VMEM_LIMIT = V7X_VMEM_BYTES * 7 // 8
BF16_SUBLANES = 16

_BF16 = jnp.bfloat16
_F32 = jnp.float32


def _split_bf16(x):
    parts = []
    r = x
    for _ in range(N_SPLIT):
        p = r.astype(_BF16)
        parts.append(p)
        r = r - p.astype(_F32)
    return parts


def _dot(a, b):
    return jnp.dot(a, b, preferred_element_type=_F32)


def _dot_nt(a, b):
    return lax.dot_general(a, b, (((1,), (1,)), ((), ())), preferred_element_type=_F32)


def _inproj_kernel(x_ref, gpre_ref, wzt_ref, wqt_ref, wkt_ref, wvt_ref, wft_ref, wgt_ref,
                   bf_ref, gsgu_ref, bsgu_ref, tri_ref, expand_ref, expand_t_ref,
                   head_rows_ref, piece_rows_ref, *rest, n_cast):
    cast_in = rest[:n_cast]
    (u_ref, vn_ref, qt_ref, k_ref, vt_ref, gates_ref, qat_ref, ka_ref,
     stats_ref) = rest[n_cast:n_cast + 9]
    cast_out = rest[n_cast + 9:2 * n_cast + 9]
    carry_ref, = rest[2 * n_cast + 9:]
    for src, dst in zip(cast_in, cast_out):
        dst[...] = src[...].astype(_BF16)

    @pl.when(pl.program_id(1) == 0)
    def _():
        carry_ref[...] = jnp.zeros_like(carry_ref)
        stats_ref[...] = jnp.zeros_like(stats_ref)

    x = x_ref[0]
    tm = x.shape[0]
    ms = jnp.mean(x * x, axis=-1, keepdims=True)
    xb = ((x * lax.rsqrt(ms + EPS)) * gpre_ref[...]).astype(_BF16)

    lane = lax.broadcasted_iota(jnp.int32, (tm, LANES), 1)

    def pack_pieces(v):
        hi, mid, lo = _split_bf16(v)
        zero = jnp.zeros((), _BF16)
        return jnp.where(lane < N_HEADS, hi,
                         jnp.where(lane < 2 * N_HEADS, mid,
                                   jnp.where(lane < N_SPLIT * N_HEADS, lo, zero)))

    f = _dot_nt(xb, wft_ref[...]) + bf_ref[...]

    z = jax.nn.gelu(_dot_nt(xb, wzt_ref[...]), approximate=True)
    u_ref[0] = z[:, :SGU_WIDTH].astype(_BF16)
    v = z[:, SGU_WIDTH:]
    mu = jnp.mean(v, axis=-1, keepdims=True)
    vc = v - mu
    var = jnp.mean(vc * vc, axis=-1, keepdims=True)
    vn_ref[0] = ((vc * lax.rsqrt(var + EPS)) * gsgu_ref[...] + bsgu_ref[...]).astype(_BF16)

    log_f = jnp.minimum(f, 0.0) - jnp.log(1.0 + jnp.exp(-jnp.abs(f)))
    sums = _dot(tri_ref[...], pack_pieces(log_f))

    qt = _dot_nt(wqt_ref[...], xb) * (HEAD_DIM ** -0.5 * LOG2E)
    qt_ref[0] = qt.astype(_BF16)
    k = _dot_nt(xb, wkt_ref[...])
    k_ref[0] = k.astype(_BF16)
    qn2 = _dot(head_rows_ref[...], (qt * qt).astype(_BF16))
    kn2 = _dot_nt(head_rows_ref[...], (k * k).astype(_BF16))

    total = sums
    for shift in (N_HEADS, 2 * N_HEADS, LANES - N_HEADS, LANES - 2 * N_HEADS):
        total = total + pltpu.roll(sums, shift, 1)
    cum = carry_ref[0:1, :] + total
    carry_ref[0:1, :] = cum[tm - 1:tm, :]
    pieces = pack_pieces(cum * LOG2E)
    spread = _dot(pieces, expand_ref[...])
    spread_t = _dot_nt(expand_t_ref[...], pieces)

    cum_t = _dot_nt(piece_rows_ref[...], pieces)
    tile_lane = lax.broadcasted_iota(jnp.int32, (2 * N_HEADS, LANES), 1)
    first_tile = pl.program_id(1) * (tm // TK)
    stats = [stats_ref[0, s] for s in range(N_STATS)]
    for j in range(tm // TK):
        lo, hi = j * TK, (j + 1) * TK
        cols = (cum_t[:, lo:lo + 1], cum_t[:, hi - 1:hi],
                jnp.max(qn2[:, lo:hi], axis=1, keepdims=True),
                jnp.max(kn2[:, lo:hi], axis=1, keepdims=True))
        stats = [jnp.where(tile_lane == first_tile + j, c, s) for c, s in zip(cols, stats)]
    for s in range(N_STATS):
        stats_ref[0, s] = stats[s]

    vt = _dot_nt(wvt_ref[...], xb).astype(_BF16)
    for h in range(N_HEADS):
        vt_ref[0, h, :HEAD_DIM, :] = vt[h * HEAD_DIM:(h + 1) * HEAD_DIM, :]
        vt_ref[0, h, HEAD_DIM:, :] = jnp.ones((PAIR - HEAD_DIM, tm), _BF16)

    gates_ref[0] = jax.nn.sigmoid(_dot_nt(xb, wgt_ref[...])).astype(_BF16)

    in_a = lane < BIAS_COLS
    in_b = jnp.logical_and(lane >= BIAS_COLS, lane < 2 * BIAS_COLS)
    ka_ref[0] = jnp.where(in_a, -spread, jnp.where(in_b, 1.0, 0.0)).astype(_BF16)
    row = lax.broadcasted_iota(jnp.int32, (LANES, tm), 0)
    in_a = row < BIAS_COLS
    in_b = jnp.logical_and(row >= BIAS_COLS, row < 2 * BIAS_COLS)
    qat_ref[0] = jnp.where(in_a, 1.0, jnp.where(in_b, spread_t, 0.0)).astype(_BF16)


def _head_rows():
    r = np.zeros((2 * N_HEADS, ATTN_WIDTH), np.float32)
    for h in range(N_HEADS):
        r[h, h * HEAD_DIM:(h + 1) * HEAD_DIM] = 1.0
    return r


def _piece_rows():
    r = np.zeros((2 * N_HEADS, LANES), np.float32)
    for h in range(N_HEADS):
        for i in range(N_SPLIT):
            r[h, N_HEADS * i + h] = 1.0
    return r


def _expand_matrix():
    e = np.zeros((LANES, LANES), np.float32)
    for h in range(N_HEADS):
        for i in range(N_SPLIT):
            e[N_HEADS * i + h, N_SPLIT * h + i] = 1.0
            e[N_HEADS * i + h, BIAS_COLS + N_SPLIT * h + i] = 1.0
    return e


_IN_OFFSETS = tuple(int(v) for v in np.cumsum(
    (0, 2 * SGU_WIDTH, ATTN_WIDTH, ATTN_WIDTH, ATTN_WIDTH, N_HEADS, 2 * D_MODEL)))


def _in_projection(x, g_pre, w_in, b_forget, g_sgu, b_sgu, later_weights):
    B, S, D = x.shape
    tm = ROW_TILE
    n_steps = B * (S // tm)
    o = _IN_OFFSETS
    wt = jnp.swapaxes(w_in, 0, 1)
    wzt, wqt, wkt, wvt, wgt = (wt[o[i]:o[i + 1]].astype(_BF16) for i in (0, 1, 2, 3, 5))
    row_pad = ((0, LANES - N_SPLIT * N_HEADS), (0, 0))
    wft = jnp.pad(jnp.tile(wt[o[4]:o[5]], (N_SPLIT, 1)), row_pad).astype(_BF16)
    lane_pad = ((0, 0), (0, LANES - N_SPLIT * N_HEADS))
    bf = jnp.pad(jnp.tile(b_forget.reshape(1, N_HEADS), (1, N_SPLIT)), lane_pad)
    tri = jnp.asarray(np.tril(np.ones((tm, tm), np.float32)), _BF16)
    expand = jnp.asarray(_expand_matrix(), _BF16)
    expand_t = jnp.asarray(_expand_matrix().T, _BF16)
    head_rows = jnp.asarray(_head_rows(), _BF16)
    piece_rows = jnp.asarray(_piece_rows(), _BF16)

    def const(shape):
        return pl.BlockSpec(shape, lambda b, i: (0,) * len(shape))

    def rows(width):
        return pl.BlockSpec((1, tm, width), lambda b, i: (b, i, 0))

    out_shape = (
        jax.ShapeDtypeStruct((B, S, SGU_WIDTH), _BF16),
        jax.ShapeDtypeStruct((B, S, SGU_WIDTH), _BF16),
        jax.ShapeDtypeStruct((B, ATTN_WIDTH, S), _BF16),
        jax.ShapeDtypeStruct((B, S, ATTN_WIDTH), _BF16),
        jax.ShapeDtypeStruct((B, N_HEADS, PAIR, S), _BF16),
        jax.ShapeDtypeStruct((B, S, 2 * D_MODEL), _BF16),
        jax.ShapeDtypeStruct((B, LANES, S), _BF16),
        jax.ShapeDtypeStruct((B, S, LANES), _BF16),
        jax.ShapeDtypeStruct((B, N_STATS, 2 * N_HEADS, LANES), _F32),
    )

    def cols(height):
        return pl.BlockSpec((1, height, tm), lambda b, i: (b, 0, i))

    def row_block(w):
        assert w.shape[0] % (n_steps * BF16_SUBLANES) == 0
        return pl.BlockSpec((w.shape[0] // n_steps, w.shape[1]),
                            lambda b, i: (b * (S // tm) + i, 0))

    out_specs = (
        rows(SGU_WIDTH), rows(SGU_WIDTH), cols(ATTN_WIDTH), rows(ATTN_WIDTH),
        pl.BlockSpec((1, N_HEADS, PAIR, tm), lambda b, i: (b, 0, 0, i)),
        rows(2 * D_MODEL), cols(LANES), rows(LANES),
        pl.BlockSpec((1, N_STATS, 2 * N_HEADS, LANES), lambda b, i: (b, 0, 0, 0)),
    ) + tuple(row_block(w) for w in later_weights)
    out_shape += tuple(jax.ShapeDtypeStruct(w.shape, _BF16) for w in later_weights)
    return pl.pallas_call(
        functools.partial(_inproj_kernel, n_cast=len(later_weights)),
        grid=(B, S // tm),
        in_specs=[
            rows(D), const((1, D)),
            const(wzt.shape), const(wqt.shape), const(wkt.shape), const(wvt.shape),
            const(wft.shape), const(wgt.shape),
            const((1, LANES)), const((1, SGU_WIDTH)), const((1, SGU_WIDTH)),
            const(tri.shape), const(expand.shape), const(expand_t.shape),
            const(head_rows.shape), const(piece_rows.shape),
        ] + [row_block(w) for w in later_weights],
        out_specs=out_specs,
        out_shape=out_shape,
        scratch_shapes=[pltpu.VMEM((8, LANES), _F32)],
        compiler_params=pltpu.CompilerParams(
            dimension_semantics=("arbitrary", "arbitrary"),
            vmem_limit_bytes=VMEM_LIMIT),
        name="in_projection",
    )(x, g_pre.reshape(1, D), wzt, wqt, wkt, wvt, wft, wgt, bf,
      g_sgu.reshape(1, SGU_WIDTH), b_sgu.reshape(1, SGU_WIDTH), tri, expand, expand_t,
      head_rows, piece_rows, *later_weights)


def _attn_kernel(plan_ref, k_ref, ka_ref, vt_ref, qt_ref, qat_ref, o_ref,
                 qf_ref, st_a, st_b, st_c, m_ref, acc_ref):
    batch = pl.program_id(0)
    pair = pl.program_id(1)
    S = k_ref.shape[1]
    row = lax.broadcasted_iota(jnp.int32, (PAIR, 1), 0)
    all_tiles = [(e, t) for t in range(NQ) for e in range(2)]

    def key_tile(k0):
        return jnp.concatenate([k_ref[0, pl.ds(k0, TK), :], ka_ref[0, pl.ds(k0, TK), :]], axis=1)

    def offsets(g, j):
        n, skew = plan_ref[batch, pair, g, 0], plan_ref[batch, pair, g, 1]
        first = (g * NQ - n - skew + j) * TK
        return [pl.multiple_of(first + t * skew * TK, TK) for t in range(NQ)]

    def update(e, t, st, k0, masked):
        cs = slice(t * TQ, (t + 1) * TQ)
        if masked:
            key_i = lax.broadcasted_iota(jnp.int32, (TK, TQ), 0)
            qry_i = lax.broadcasted_iota(jnp.int32, (TK, TQ), 1)
            st = jnp.where(key_i <= qry_i, st, -jnp.inf)
        m = m_ref[e, :, cs]
        m_new = jnp.maximum(m, jnp.max(st, axis=0, keepdims=True))
        p = jnp.exp2(st - m_new)
        alpha = jnp.exp2(m - m_new)
        m_ref[e, :, cs] = m_new
        acc_ref[e, :, cs] = alpha * acc_ref[e, :, cs] + _dot(
            vt_ref[0, e, :, pl.ds(k0, TK)], p.astype(_BF16))

    def step(cur, cur_tiles, cur_offs, nxt, nxt_tiles, nxt_offs):
        for n in range(max(len(cur_tiles), len(nxt_tiles))):
            if n < len(nxt_tiles):
                e, t = nxt_tiles[n]
                nxt[e, t] = _dot(key_tile(nxt_offs[t]), qf_ref[e, :, t * TQ:(t + 1) * TQ])
            if n < len(cur_tiles):
                e, t, masked = cur_tiles[n]
                update(e, t, cur[e, t], cur_offs[t], masked)

    def load_queries(g):
        q0 = pl.multiple_of(g * QG, QG)
        qt = qt_ref[0, :, pl.ds(q0, QG)].astype(_F32)
        qat = qat_ref[0, :, pl.ds(q0, QG)].astype(_F32)
        for e in range(2):
            a0 = N_SPLIT * (2 * pair + e)
            q_mask = jnp.logical_and(row >= HEAD_DIM * e, row < HEAD_DIM * (e + 1))
            a_mask = jnp.logical_or(
                jnp.logical_and(row >= a0, row < a0 + N_SPLIT),
                jnp.logical_and(row >= BIAS_COLS + a0, row < BIAS_COLS + a0 + N_SPLIT))
            qf_ref[e, :PAIR, :] = jnp.where(q_mask, qt, 0.0).astype(_BF16)
            qf_ref[e, PAIR:, :] = jnp.where(a_mask, qat, 0.0).astype(_BF16)

    def reset_state():
        m_ref[...] = jnp.full(m_ref.shape, -jnp.inf, _F32)
        acc_ref[...] = jnp.zeros(acc_ref.shape, _F32)

    full = [(e, t, False) for e, t in all_tiles]
    bufs = (st_a, st_b)
    n_groups = S // QG

    def finish(g):
        out = [acc_ref[e, :HEAD_DIM, :] * (1.0 / acc_ref[e, HEAD_DIM:HEAD_DIM + 1, :])
               for e in range(2)]
        o_ref[0, pl.ds(pl.multiple_of(g * QG, QG), QG), :] = (
            jnp.concatenate(out, axis=0).T.astype(_BF16))
        reset_state()

    def next_group(g):
        return jnp.minimum(g + 1, n_groups - 1)

    def triangle(g):
        q0 = pl.multiple_of(g * QG, QG)
        for i in range(NQ):
            cur_tiles = [(e, t, t == i) for t in range(i, NQ) for e in range(2)]
            cur_offs = [q0 + i * TK] * NQ
            if i + 1 < NQ:
                nxt_tiles = [(e, t) for t in range(i + 1, NQ) for e in range(2)]
                nxt_offs = [q0 + (i + 1) * TK] * NQ
            else:
                load_queries(next_group(g))
                nxt_tiles, nxt_offs = all_tiles, offsets(next_group(g), 0)
            step(bufs[i % 2], cur_tiles, cur_offs, bufs[(i + 1) % 2], nxt_tiles, nxt_offs)

    load_queries(0)
    reset_state()
    step(None, [], None, st_a, all_tiles, [0] * NQ)
    triangle(0)

    def q_group(g, _):
        n = plan_ref[batch, pair, g, 0]
        skew = plan_ref[batch, pair, g, 1]

        def full_steps(i, _):
            for u in range(KU):
                j = i * KU + u
                step(bufs[u % 2], full, offsets(g, j), bufs[(u + 1) % 2], all_tiles,
                     offsets(g, j + 1))
            return 0

        def skew_tail(j):
            step(st_a, full, offsets(g, j), st_b, all_tiles, offsets(g, j + 1))
            load_queries(next_group(g))
            step(st_b, [(e, t, True) for e, t in all_tiles], offsets(g, j + 1),
                 st_a, all_tiles, offsets(next_group(g), 0))

        unrolled = jnp.logical_and(skew == 1, functools.reduce(
            jnp.logical_or, [n == n_static for n_static in UNROLLED_SKEW_STEPS]))
        for n_static in UNROLLED_SKEW_STEPS:
            @pl.when(jnp.logical_and(skew == 1, n == n_static))
            def _(n_static=n_static):
                last = n_static + 1
                seq = [bufs[j % 2] for j in range(last)] + [st_c if last % 2 == 0 else st_b,
                                                            st_a]
                finish(g - 1)
                for j in range(last):
                    step(seq[j], full, offsets(g, j), seq[j + 1], all_tiles, offsets(g, j + 1))
                load_queries(next_group(g))
                step(seq[last], [(e, t, True) for e, t in all_tiles], offsets(g, last),
                     st_a, all_tiles, offsets(next_group(g), 0))

        @pl.when(jnp.logical_not(unrolled))
        def _():
            finish(g - 1)
            lax.fori_loop(0, n // KU, full_steps, 0)

        @pl.when(jnp.logical_and(skew == 1, jnp.logical_not(unrolled)))
        def _():
            skew_tail(n)

        @pl.when(skew == 0)
        def _():
            triangle(g)

        return 0

    lax.fori_loop(1, n_groups, q_group, 0)
    finish(n_groups - 1)


def _sweep_plan(stats, n_tiles):
    cum_first, cum_last, qn2, kn2 = (stats[:, s, :N_HEADS, :n_tiles] for s in range(N_STATS))
    B = stats.shape[0]
    n_groups = n_tiles // NQ
    k_norm = jnp.sqrt(jnp.max(kn2, axis=-1, keepdims=True))
    reach = NORM_SLACK * jnp.sqrt(qn2) * k_norm + cum_first
    bound = reach[..., None] - cum_last[:, :, None, :]
    tile = jnp.arange(n_tiles)
    needed = jnp.logical_and(tile[None, :] < tile[:, None], bound >= -ZERO_EXP2)
    w = jnp.sum(needed, axis=-1).astype(jnp.int32)
    w = jnp.max(w.reshape(B, N_PAIRS, 2, n_groups, NQ), axis=2)
    start = jnp.arange(n_groups, dtype=jnp.int32) * NQ
    n_flat = jnp.max(jnp.maximum(w - jnp.arange(NQ, dtype=jnp.int32), 0), axis=-1)
    n_flat = jnp.minimum(((n_flat + KU - 1) // KU) * KU, start)
    n_skew = jnp.maximum(jnp.max(w, axis=-1), 1)
    exact = functools.reduce(jnp.logical_or, [n_skew == s + 1 for s in UNROLLED_SKEW_STEPS])
    n_skew = jnp.where(exact, n_skew, n_skew + (n_skew + 1) % 2)
    units_flat = NQ * n_flat + NQ * (NQ + 1) // 2
    units_skew = NQ * n_skew + NQ
    skew = jnp.logical_and(n_skew < start, units_skew < units_flat)
    n = jnp.where(skew, n_skew - 1, n_flat)
    return jnp.stack([n, skew.astype(jnp.int32)], axis=-1)


def _attention(k, ka, vt, qt, qat, stats):
    B, S, _ = k.shape
    assert TQ == TK and KU == 2 and NQ % KU == 0 and S % QG == 0
    assert S // TK <= LANES
    plan = _sweep_plan(stats, S // TK)
    pair_rows = pl.BlockSpec((1, S, PAIR), lambda b, j, nb: (b, 0, j))
    return pl.pallas_call(
        _attn_kernel,
        grid_spec=pltpu.PrefetchScalarGridSpec(
            num_scalar_prefetch=1,
            grid=(B, N_PAIRS),
            in_specs=[pair_rows, pl.BlockSpec((1, S, LANES), lambda b, j, nb: (b, 0, 0)),
                      pl.BlockSpec((1, 2, PAIR, S), lambda b, j, nb: (b, j, 0, 0)),
                      pl.BlockSpec((1, PAIR, S), lambda b, j, nb: (b, j, 0)),
                      pl.BlockSpec((1, LANES, S), lambda b, j, nb: (b, 0, 0))],
            out_specs=pair_rows,
            scratch_shapes=[pltpu.VMEM((2, PAIR + LANES, QG), _BF16),
                            pltpu.VMEM((2, NQ, TK, TQ), _F32), pltpu.VMEM((2, NQ, TK, TQ), _F32),
                            pltpu.VMEM((2, NQ, TK, TQ), _F32),
                            pltpu.VMEM((2, 1, QG), _F32), pltpu.VMEM((2, PAIR, QG), _F32)]),
        out_shape=jax.ShapeDtypeStruct((B, S, ATTN_WIDTH), _BF16),
        compiler_params=pltpu.CompilerParams(
            dimension_semantics=("parallel", "parallel"),
            vmem_limit_bytes=VMEM_LIMIT),
        name="forgetting_attention",
    )(plan, k, ka, vt, qt, qat)


def _rms_scale(v, g):
    ms = jnp.mean(v * v, axis=-1, keepdims=True)
    return (v * lax.rsqrt(ms + EPS)) * g


def _merge_mlp_kernel(x_ref, u_ref, vn_ref, ya_ref, gates_ref, wsp_ref, bsp_ref,
                      wbs_ref, wba_ref, wout_ref, gpost_ref,
                      gpre2_ref, wup_ref, wdown_ref, gpost2_ref, o_ref, ysgu_ref):
    tm = x_ref.shape[0]
    row = lax.broadcasted_iota(jnp.int32, (CHUNK, CHUNK), 0)
    col = lax.broadcasted_iota(jnp.int32, (CHUNK, CHUNK), 1)
    lane = lax.broadcasted_iota(jnp.int32, (CHUNK, PAIR), 1)
    ws = [jnp.where(row >= col, wsp_ref[g], 0.0).astype(_BF16) for g in range(N_GROUPS)]
    for c in range(tm // CHUNK):
        r = slice(c * CHUNK, (c + 1) * CHUNK)
        for j in range(N_GROUPS // 2):
            cs = slice(j * PAIR, (j + 1) * PAIR)
            vp = vn_ref[r, cs]
            s = jnp.where(lane < HEAD_DIM, _dot(ws[2 * j], vp), _dot(ws[2 * j + 1], vp))
            s = s + bsp_ref[:, cs]
            ysgu_ref[r, cs] = (u_ref[r, cs].astype(_F32) * s).astype(_BF16)

    blocks = [slice(n * tm // N_STREAMS, (n + 1) * tm // N_STREAMS) for n in range(N_STREAMS)]

    def mix(r):
        a = _dot(ysgu_ref[r, :], wbs_ref[...])
        b = _dot(ya_ref[r, :], wba_ref[...])
        merged = (gates_ref[r, :D_MODEL].astype(_F32) * a
                  + gates_ref[r, D_MODEL:].astype(_F32) * b).astype(_BF16)
        return _dot(merged, wout_ref[...])

    def mlp(xb):
        acc = jnp.zeros(xb.shape, _F32)
        for c in range(D_FF // FF_TILE):
            cs = slice(c * FF_TILE, (c + 1) * FF_TILE)
            hid = jnp.square(jnp.maximum(_dot(xb, wup_ref[:, cs]), 0.0)).astype(_BF16)
            acc = acc + _dot(hid, wdown_ref[cs, :])
        return acc

    mixed = [mix(r) for r in blocks]
    h1 = [x_ref[r, :] + _rms_scale(o, gpost_ref[...]) for r, o in zip(blocks, mixed)]
    xb = [_rms_scale(h, gpre2_ref[...]).astype(_BF16) for h in h1]
    ff = [mlp(v) for v in xb]
    for r, h, f in zip(blocks, h1, ff):
        o_ref[r, :] = h + _rms_scale(f, gpost2_ref[...])


def _merge_mlp(x2, u2, vn2, ya2, gates2, w_spatial, b_spatial, w_bs, w_ba, w_out, g_post,
               g_pre2, w_up, w_down, g_post2):
    R, D = x2.shape
    tm = ROW_TILE
    bsp = jnp.repeat(b_spatial.T, SGU_WIDTH // N_GROUPS, axis=1)

    def const(shape):
        return pl.BlockSpec(shape, lambda i: (0,) * len(shape), pipeline_mode=pl.Buffered(1))

    def rows(width):
        return pl.BlockSpec((tm, width), lambda i: (i, 0))

    return pl.pallas_call(
        _merge_mlp_kernel,
        grid=(R // tm,),
        in_specs=[rows(D), rows(SGU_WIDTH), rows(SGU_WIDTH), rows(ATTN_WIDTH), rows(2 * D),
                  const(w_spatial.shape), const(bsp.shape),
                  const(w_bs.shape), const(w_ba.shape), const(w_out.shape), const((1, D)),
                  const((1, D)), const(w_up.shape), const(w_down.shape), const((1, D))],
        out_specs=rows(D),
        out_shape=jax.ShapeDtypeStruct((R, D), _F32),
        scratch_shapes=[pltpu.VMEM((tm, SGU_WIDTH), _BF16)],
        compiler_params=pltpu.CompilerParams(
            dimension_semantics=("parallel",), vmem_limit_bytes=VMEM_LIMIT),
        name="merge_mlp",
    )(x2, u2, vn2, ya2, gates2, w_spatial, bsp,
      w_bs, w_ba, w_out, g_post.reshape(1, D),
      g_pre2.reshape(1, D), w_up, w_down, g_post2.reshape(1, D))


def kernel(x, g_mix_pre, w_in, b_forget, g_sgu, b_sgu, w_spatial, b_spatial, w_branch_sgu,
           w_branch_attn, w_out, g_mix_post, g_ffn_pre, w_up, w_down, g_ffn_post):
    B, S, D = x.shape
    h = x
    for l in range(g_mix_pre.shape[0]):
        later = (w_branch_sgu[l], w_branch_attn[l], w_out[l], w_up[l], w_down[l])
        u, vn, qt, k, vt, gates, qat, ka, stats, w_bs, w_ba, w_o, w_u, w_d = _in_projection(
            h, g_mix_pre[l], w_in[l], b_forget[l], g_sgu[l], b_sgu[l], later)
        y_attn = _attention(k, ka, vt, qt, qat, stats)
        h = _merge_mlp(h.reshape(B * S, D), u.reshape(B * S, -1), vn.reshape(B * S, -1),
                       y_attn.reshape(B * S, -1), gates.reshape(B * S, -1),
                       w_spatial[l], b_spatial[l], w_bs, w_ba, w_o,
                       g_mix_post[l], g_ffn_pre[l], w_u, w_d,
                       g_ffn_post[l]).reshape(B, S, D)
    return h
```

```python
import functools

import jax
import jax.numpy as jnp
import numpy as np
from jax import lax
from jax.experimental import pallas as pl
from jax.experimental.pallas import tpu as pltpu

D_MODEL = 1024
N_HEADS = 8
HEAD_DIM = 64
ATTN_WIDTH = N_HEADS * HEAD_DIM
N_GROUPS = 8
SGU_WIDTH = D_MODEL // 2
CHUNK = 128
D_FF = 4 * D_MODEL
EPS = 1e-6
LOG2E = 1.4426950408889634

LANES = 128
N_SPLIT = 3
BIAS_COLS = N_HEADS * N_SPLIT
PAIR = 2 * HEAD_DIM
N_PAIRS = N_HEADS // 2

ROW_TILE = 512
FF_TILE = 1024
N_STREAMS = 2
TQ = 256
TK = 256
NQ = 4
QG = NQ * TQ
KU = 2
UNROLLED_SKEW_STEPS = (3, 4, 5, 6)
N_STATS = 4
ZERO_EXP2 = 136.0
NORM_SLACK = 2.05
V7X_VMEM_BYTES = 64 * 1024 * 1024
VMEM_LIMIT = V7X_VMEM_BYTES * 7 // 8
BF16_SUBLANES = 16

_BF16 = jnp.bfloat16
_F32 = jnp.float32


def _split_bf16(x):
    parts = []
    r = x
    for _ in range(N_SPLIT):
        p = r.astype(_BF16)
        parts.append(p)
        r = r - p.astype(_F32)
    return parts


def _dot(a, b):
    return jnp.dot(a, b, preferred_element_type=_F32)


def _dot_nt(a, b):
    return lax.dot_general(a, b, (((1,), (1,)), ((), ())), preferred_element_type=_F32)


def _inproj_kernel(x_ref, gpre_ref, wzt_ref, wqt_ref, wkt_ref, wvt_ref, wft_ref, wgt_ref,
                   bf_ref, gsgu_ref, bsgu_ref, tri_ref, expand_ref, expand_t_ref,
                   head_rows_ref, piece_rows_ref, *rest, n_cast):
    cast_in = rest[:n_cast]
    (u_ref, vn_ref, qt_ref, k_ref, vt_ref, gates_ref, qat_ref, ka_ref,
     stats_ref) = rest[n_cast:n_cast + 9]
    cast_out = rest[n_cast + 9:2 * n_cast + 9]
    carry_ref, = rest[2 * n_cast + 9:]
    for src, dst in zip(cast_in, cast_out):
        dst[...] = src[...].astype(_BF16)

    @pl.when(pl.program_id(1) == 0)
    def _():
        carry_ref[...] = jnp.zeros_like(carry_ref)
        stats_ref[...] = jnp.zeros_like(stats_ref)

    x = x_ref[0]
    tm = x.shape[0]
    ms = jnp.mean(x * x, axis=-1, keepdims=True)
    xb = ((x * lax.rsqrt(ms + EPS)) * gpre_ref[...]).astype(_BF16)

    lane = lax.broadcasted_iota(jnp.int32, (tm, LANES), 1)

    def pack_pieces(v):
        hi, mid, lo = _split_bf16(v)
        zero = jnp.zeros((), _BF16)
        return jnp.where(lane < N_HEADS, hi,
                         jnp.where(lane < 2 * N_HEADS, mid,
                                   jnp.where(lane < N_SPLIT * N_HEADS, lo, zero)))

    f = _dot_nt(xb, wft_ref[...]) + bf_ref[...]

    z = jax.nn.gelu(_dot_nt(xb, wzt_ref[...]), approximate=True)
    u_ref[0] = z[:, :SGU_WIDTH].astype(_BF16)
    v = z[:, SGU_WIDTH:]
    mu = jnp.mean(v, axis=-1, keepdims=True)
    vc = v - mu
    var = jnp.mean(vc * vc, axis=-1, keepdims=True)
    vn_ref[0] = ((vc * lax.rsqrt(var + EPS)) * gsgu_ref[...] + bsgu_ref[...]).astype(_BF16)

    log_f = jnp.minimum(f, 0.0) - jnp.log(1.0 + jnp.exp(-jnp.abs(f)))
    sums = _dot(tri_ref[...], pack_pieces(log_f))

    qt = _dot_nt(wqt_ref[...], xb) * (HEAD_DIM ** -0.5 * LOG2E)
    qt_ref[0] = qt.astype(_BF16)
    k = _dot_nt(xb, wkt_ref[...])
    k_ref[0] = k.astype(_BF16)
    qn2 = _dot(head_rows_ref[...], (qt * qt).astype(_BF16))
    kn2 = _dot_nt(head_rows_ref[...], (k * k).astype(_BF16))

    total = sums
    for shift in (N_HEADS, 2 * N_HEADS, LANES - N_HEADS, LANES - 2 * N_HEADS):
        total = total + pltpu.roll(sums, shift, 1)
    cum = carry_ref[0:1, :] + total
    carry_ref[0:1, :] = cum[tm - 1:tm, :]
    pieces = pack_pieces(cum * LOG2E)
    spread = _dot(pieces, expand_ref[...])
    spread_t = _dot_nt(expand_t_ref[...], pieces)

    cum_t = _dot_nt(piece_rows_ref[...], pieces)
    tile_lane = lax.broadcasted_iota(jnp.int32, (2 * N_HEADS, LANES), 1)
    first_tile = pl.program_id(1) * (tm // TK)
    stats = [stats_ref[0, s] for s in range(N_STATS)]
    for j in range(tm // TK):
        lo, hi = j * TK, (j + 1) * TK
        cols = (cum_t[:, lo:lo + 1], cum_t[:, hi - 1:hi],
                jnp.max(qn2[:, lo:hi], axis=1, keepdims=True),
                jnp.max(kn2[:, lo:hi], axis=1, keepdims=True))
        stats = [jnp.where(tile_lane == first_tile + j, c, s) for c, s in zip(cols, stats)]
    for s in range(N_STATS):
        stats_ref[0, s] = stats[s]

    vt = _dot_nt(wvt_ref[...], xb).astype(_BF16)
    for h in range(N_HEADS):
        vt_ref[0, h, :HEAD_DIM, :] = vt[h * HEAD_DIM:(h + 1) * HEAD_DIM, :]
        vt_ref[0, h, HEAD_DIM:, :] = jnp.ones((PAIR - HEAD_DIM, tm), _BF16)

    gates_ref[0] = jax.nn.sigmoid(_dot_nt(xb, wgt_ref[...])).astype(_BF16)

    in_a = lane < BIAS_COLS
    in_b = jnp.logical_and(lane >= BIAS_COLS, lane < 2 * BIAS_COLS)
    ka_ref[0] = jnp.where(in_a, -spread, jnp.where(in_b, 1.0, 0.0)).astype(_BF16)
    row = lax.broadcasted_iota(jnp.int32, (LANES, tm), 0)
    in_a = row < BIAS_COLS
    in_b = jnp.logical_and(row >= BIAS_COLS, row < 2 * BIAS_COLS)
    qat_ref[0] = jnp.where(in_a, 1.0, jnp.where(in_b, spread_t, 0.0)).astype(_BF16)


def _head_rows():
    r = np.zeros((2 * N_HEADS, ATTN_WIDTH), np.float32)
    for h in range(N_HEADS):
        r[h, h * HEAD_DIM:(h + 1) * HEAD_DIM] = 1.0
    return r


def _piece_rows():
    r = np.zeros((2 * N_HEADS, LANES), np.float32)
    for h in range(N_HEADS):
        for i in range(N_SPLIT):
            r[h, N_HEADS * i + h] = 1.0
    return r


def _expand_matrix():
    e = np.zeros((LANES, LANES), np.float32)
    for h in range(N_HEADS):
        for i in range(N_SPLIT):
            e[N_HEADS * i + h, N_SPLIT * h + i] = 1.0
            e[N_HEADS * i + h, BIAS_COLS + N_SPLIT * h + i] = 1.0
    return e


_IN_OFFSETS = tuple(int(v) for v in np.cumsum(
    (0, 2 * SGU_WIDTH, ATTN_WIDTH, ATTN_WIDTH, ATTN_WIDTH, N_HEADS, 2 * D_MODEL)))


def _heads_in_order(w, head_order):
    return w.reshape(N_HEADS, HEAD_DIM, -1)[head_order].reshape(w.shape)


def _in_projection(x, g_pre, w_in, b_forget, g_sgu, b_sgu, later_weights, head_order):
    B, S, D = x.shape
    tm = ROW_TILE
    n_steps = B * (S // tm)
    o = _IN_OFFSETS
    wt = jnp.swapaxes(w_in, 0, 1)
    wzt, wqt, wkt, wvt, wgt = (wt[o[i]:o[i + 1]].astype(_BF16) for i in (0, 1, 2, 3, 5))
    wqt, wkt, wvt = (_heads_in_order(w, head_order) for w in (wqt, wkt, wvt))
    row_pad = ((0, LANES - N_SPLIT * N_HEADS), (0, 0))
    wft = jnp.pad(jnp.tile(wt[o[4]:o[5]][head_order], (N_SPLIT, 1)), row_pad).astype(_BF16)
    lane_pad = ((0, 0), (0, LANES - N_SPLIT * N_HEADS))
    bf = jnp.pad(jnp.tile(b_forget[head_order].reshape(1, N_HEADS), (1, N_SPLIT)), lane_pad)
    tri = jnp.asarray(np.tril(np.ones((tm, tm), np.float32)), _BF16)
    expand = jnp.asarray(_expand_matrix(), _BF16)
    expand_t = jnp.asarray(_expand_matrix().T, _BF16)
    head_rows = jnp.asarray(_head_rows(), _BF16)
    piece_rows = jnp.asarray(_piece_rows(), _BF16)

    def const(shape):
        return pl.BlockSpec(shape, lambda b, i: (0,) * len(shape))

    def rows(width):
        return pl.BlockSpec((1, tm, width), lambda b, i: (b, i, 0))

    out_shape = (
        jax.ShapeDtypeStruct((B, S, SGU_WIDTH), _BF16),
        jax.ShapeDtypeStruct((B, S, SGU_WIDTH), _BF16),
        jax.ShapeDtypeStruct((B, ATTN_WIDTH, S), _BF16),
        jax.ShapeDtypeStruct((B, S, ATTN_WIDTH), _BF16),
        jax.ShapeDtypeStruct((B, N_HEADS, PAIR, S), _BF16),
        jax.ShapeDtypeStruct((B, S, 2 * D_MODEL), _BF16),
        jax.ShapeDtypeStruct((B, LANES, S), _BF16),
        jax.ShapeDtypeStruct((B, S, LANES), _BF16),
        jax.ShapeDtypeStruct((B, N_STATS, 2 * N_HEADS, LANES), _F32),
    )

    def cols(height):
        return pl.BlockSpec((1, height, tm), lambda b, i: (b, 0, i))

    def row_block(w):
        assert w.shape[0] % (n_steps * BF16_SUBLANES) == 0
        return pl.BlockSpec((w.shape[0] // n_steps, w.shape[1]),
                            lambda b, i: (b * (S // tm) + i, 0))

    out_specs = (
        rows(SGU_WIDTH), rows(SGU_WIDTH), cols(ATTN_WIDTH), rows(ATTN_WIDTH),
        pl.BlockSpec((1, N_HEADS, PAIR, tm), lambda b, i: (b, 0, 0, i)),
        rows(2 * D_MODEL), cols(LANES), rows(LANES),
        pl.BlockSpec((1, N_STATS, 2 * N_HEADS, LANES), lambda b, i: (b, 0, 0, 0)),
    ) + tuple(row_block(w) for w in later_weights)
    out_shape += tuple(jax.ShapeDtypeStruct(w.shape, _BF16) for w in later_weights)
    return pl.pallas_call(
        functools.partial(_inproj_kernel, n_cast=len(later_weights)),
        grid=(B, S // tm),
        in_specs=[
            rows(D), const((1, D)),
            const(wzt.shape), const(wqt.shape), const(wkt.shape), const(wvt.shape),
            const(wft.shape), const(wgt.shape),
            const((1, LANES)), const((1, SGU_WIDTH)), const((1, SGU_WIDTH)),
            const(tri.shape), const(expand.shape), const(expand_t.shape),
            const(head_rows.shape), const(piece_rows.shape),
        ] + [row_block(w) for w in later_weights],
        out_specs=out_specs,
        out_shape=out_shape,
        scratch_shapes=[pltpu.VMEM((8, LANES), _F32)],
        compiler_params=pltpu.CompilerParams(
            dimension_semantics=("arbitrary", "arbitrary"),
            vmem_limit_bytes=VMEM_LIMIT),
        name="in_projection",
    )(x, g_pre.reshape(1, D), wzt, wqt, wkt, wvt, wft, wgt, bf,
      g_sgu.reshape(1, SGU_WIDTH), b_sgu.reshape(1, SGU_WIDTH), tri, expand, expand_t,
      head_rows, piece_rows, *later_weights)


def _attn_kernel(plan_ref, k_ref, ka_ref, vt_ref, qt_ref, qat_ref, o_ref,
                 qf_ref, st_a, st_b, st_c, m_ref, acc_ref):
    batch = pl.program_id(0)
    pair = pl.program_id(1)
    S = k_ref.shape[1]
    row = lax.broadcasted_iota(jnp.int32, (PAIR, 1), 0)
    all_tiles = [(e, t) for t in range(NQ) for e in range(2)]

    def key_tile(k0):
        return jnp.concatenate([k_ref[0, pl.ds(k0, TK), :], ka_ref[0, pl.ds(k0, TK), :]], axis=1)

    def offsets(g, j):
        n, skew = plan_ref[batch, pair, g, 0], plan_ref[batch, pair, g, 1]
        first = (g * NQ - n - skew + j) * TK
        return [pl.multiple_of(first + t * skew * TK, TK) for t in range(NQ)]

    def update(e, t, st, k0, masked):
        cs = slice(t * TQ, (t + 1) * TQ)
        if masked:
            key_i = lax.broadcasted_iota(jnp.int32, (TK, TQ), 0)
            qry_i = lax.broadcasted_iota(jnp.int32, (TK, TQ), 1)
            st = jnp.where(key_i <= qry_i, st, -jnp.inf)
        m = m_ref[e, :, cs]
        m_new = jnp.maximum(m, jnp.max(st, axis=0, keepdims=True))
        p = jnp.exp2(st - m_new)
        alpha = jnp.exp2(m - m_new)
        m_ref[e, :, cs] = m_new
        acc_ref[e, :, cs] = alpha * acc_ref[e, :, cs] + _dot(
            vt_ref[0, e, :, pl.ds(k0, TK)], p.astype(_BF16))

    def step(cur, cur_tiles, cur_offs, nxt, nxt_tiles, nxt_offs):
        for n in range(max(len(cur_tiles), len(nxt_tiles))):
            if n < len(nxt_tiles):
                e, t = nxt_tiles[n]
                nxt[e, t] = _dot(key_tile(nxt_offs[t]), qf_ref[e, :, t * TQ:(t + 1) * TQ])
            if n < len(cur_tiles):
                e, t, masked = cur_tiles[n]
                update(e, t, cur[e, t], cur_offs[t], masked)

    def load_queries(g):
        q0 = pl.multiple_of(g * QG, QG)
        qt = qt_ref[0, :, pl.ds(q0, QG)].astype(_F32)
        qat = qat_ref[0, :, pl.ds(q0, QG)].astype(_F32)
        for e in range(2):
            a0 = N_SPLIT * (2 * pair + e)
            q_mask = jnp.logical_and(row >= HEAD_DIM * e, row < HEAD_DIM * (e + 1))
            a_mask = jnp.logical_or(
                jnp.logical_and(row >= a0, row < a0 + N_SPLIT),
                jnp.logical_and(row >= BIAS_COLS + a0, row < BIAS_COLS + a0 + N_SPLIT))
            qf_ref[e, :PAIR, :] = jnp.where(q_mask, qt, 0.0).astype(_BF16)
            qf_ref[e, PAIR:, :] = jnp.where(a_mask, qat, 0.0).astype(_BF16)

    def reset_state():
        m_ref[...] = jnp.full(m_ref.shape, -jnp.inf, _F32)
        acc_ref[...] = jnp.zeros(acc_ref.shape, _F32)

    full = [(e, t, False) for e, t in all_tiles]
    bufs = (st_a, st_b)
    n_groups = S // QG

    def finish(g):
        out = [acc_ref[e, :HEAD_DIM, :] * (1.0 / acc_ref[e, HEAD_DIM:HEAD_DIM + 1, :])
               for e in range(2)]
        o_ref[0, pl.ds(pl.multiple_of(g * QG, QG), QG), :] = (
            jnp.concatenate(out, axis=0).T.astype(_BF16))
        reset_state()

    def next_group(g):
        return jnp.minimum(g + 1, n_groups - 1)

    def triangle(g):
        q0 = pl.multiple_of(g * QG, QG)
        for i in range(NQ):
            cur_tiles = [(e, t, t == i) for t in range(i, NQ) for e in range(2)]
            cur_offs = [q0 + i * TK] * NQ
            if i + 1 < NQ:
                nxt_tiles = [(e, t) for t in range(i + 1, NQ) for e in range(2)]
                nxt_offs = [q0 + (i + 1) * TK] * NQ
            else:
                load_queries(next_group(g))
                nxt_tiles, nxt_offs = all_tiles, offsets(next_group(g), 0)
            step(bufs[i % 2], cur_tiles, cur_offs, bufs[(i + 1) % 2], nxt_tiles, nxt_offs)

    load_queries(0)
    reset_state()
    step(None, [], None, st_a, all_tiles, [0] * NQ)
    triangle(0)

    def q_group(g, _):
        n = plan_ref[batch, pair, g, 0]
        skew = plan_ref[batch, pair, g, 1]

        def full_steps(i, _):
            for u in range(KU):
                j = i * KU + u
                step(bufs[u % 2], full, offsets(g, j), bufs[(u + 1) % 2], all_tiles,
                     offsets(g, j + 1))
            return 0

        def skew_tail(j):
            step(st_a, full, offsets(g, j), st_b, all_tiles, offsets(g, j + 1))
            load_queries(next_group(g))
            step(st_b, [(e, t, True) for e, t in all_tiles], offsets(g, j + 1),
                 st_a, all_tiles, offsets(next_group(g), 0))

        unrolled = jnp.logical_and(skew == 1, functools.reduce(
            jnp.logical_or, [n == n_static for n_static in UNROLLED_SKEW_STEPS]))
        for n_static in UNROLLED_SKEW_STEPS:
            @pl.when(jnp.logical_and(skew == 1, n == n_static))
            def _(n_static=n_static):
                last = n_static + 1
                seq = [bufs[j % 2] for j in range(last)] + [st_c if last % 2 == 0 else st_b,
                                                            st_a]
                finish(g - 1)
                for j in range(last):
                    step(seq[j], full, offsets(g, j), seq[j + 1], all_tiles, offsets(g, j + 1))
                load_queries(next_group(g))
                step(seq[last], [(e, t, True) for e, t in all_tiles], offsets(g, last),
                     st_a, all_tiles, offsets(next_group(g), 0))

        @pl.when(jnp.logical_not(unrolled))
        def _():
            finish(g - 1)
            lax.fori_loop(0, n // KU, full_steps, 0)

        @pl.when(jnp.logical_and(skew == 1, jnp.logical_not(unrolled)))
        def _():
            skew_tail(n)

        @pl.when(skew == 0)
        def _():
            triangle(g)

        return 0

    lax.fori_loop(1, n_groups, q_group, 0)
    finish(n_groups - 1)


def _sweep_plan(stats, n_tiles):
    cum_first, cum_last, qn2, kn2 = (stats[:, s, :N_HEADS, :n_tiles] for s in range(N_STATS))
    B = stats.shape[0]
    n_groups = n_tiles // NQ
    k_norm = jnp.sqrt(jnp.max(kn2, axis=-1, keepdims=True))
    reach = NORM_SLACK * jnp.sqrt(qn2) * k_norm + cum_first
    bound = reach[..., None] - cum_last[:, :, None, :]
    tile = jnp.arange(n_tiles)
    needed = jnp.logical_and(tile[None, :] < tile[:, None], bound >= -ZERO_EXP2)
    w = jnp.sum(needed, axis=-1).astype(jnp.int32)
    w = jnp.max(w.reshape(B, N_PAIRS, 2, n_groups, NQ), axis=2)
    start = jnp.arange(n_groups, dtype=jnp.int32) * NQ
    n_flat = jnp.max(jnp.maximum(w - jnp.arange(NQ, dtype=jnp.int32), 0), axis=-1)
    n_flat = jnp.minimum(((n_flat + KU - 1) // KU) * KU, start)
    n_skew = jnp.maximum(jnp.max(w, axis=-1), 1)
    exact = functools.reduce(jnp.logical_or, [n_skew == s + 1 for s in UNROLLED_SKEW_STEPS])
    n_skew = jnp.where(exact, n_skew, n_skew + (n_skew + 1) % 2)
    units_flat = NQ * n_flat + NQ * (NQ + 1) // 2
    units_skew = NQ * n_skew + NQ
    skew = jnp.logical_and(n_skew < start, units_skew < units_flat)
    n = jnp.where(skew, n_skew - 1, n_flat)
    return jnp.stack([n, skew.astype(jnp.int32)], axis=-1)


def _attention(k, ka, vt, qt, qat, stats):
    B, S, _ = k.shape
    assert TQ == TK and KU == 2 and NQ % KU == 0 and S % QG == 0
    assert S // TK <= LANES
    plan = _sweep_plan(stats, S // TK)
    pair_rows = pl.BlockSpec((1, S, PAIR), lambda b, j, nb: (b, 0, j))
    return pl.pallas_call(
        _attn_kernel,
        grid_spec=pltpu.PrefetchScalarGridSpec(
            num_scalar_prefetch=1,
            grid=(B, N_PAIRS),
            in_specs=[pair_rows, pl.BlockSpec((1, S, LANES), lambda b, j, nb: (b, 0, 0)),
                      pl.BlockSpec((1, 2, PAIR, S), lambda b, j, nb: (b, j, 0, 0)),
                      pl.BlockSpec((1, PAIR, S), lambda b, j, nb: (b, j, 0)),
                      pl.BlockSpec((1, LANES, S), lambda b, j, nb: (b, 0, 0))],
            out_specs=pair_rows,
            scratch_shapes=[pltpu.VMEM((2, PAIR + LANES, QG), _BF16),
                            pltpu.VMEM((2, NQ, TK, TQ), _F32), pltpu.VMEM((2, NQ, TK, TQ), _F32),
                            pltpu.VMEM((2, NQ, TK, TQ), _F32),
                            pltpu.VMEM((2, 1, QG), _F32), pltpu.VMEM((2, PAIR, QG), _F32)]),
        out_shape=jax.ShapeDtypeStruct((B, S, ATTN_WIDTH), _BF16),
        compiler_params=pltpu.CompilerParams(
            dimension_semantics=("parallel", "parallel"),
            vmem_limit_bytes=VMEM_LIMIT),
        name="forgetting_attention",
    )(plan, k, ka, vt, qt, qat)


def _rms_scale(v, g):
    ms = jnp.mean(v * v, axis=-1, keepdims=True)
    return (v * lax.rsqrt(ms + EPS)) * g


def _merge_mlp_kernel(x_ref, u_ref, vn_ref, ya_ref, gates_ref, wsp_ref, bsp_ref,
                      wbs_ref, wba_ref, wout_ref, gpost_ref,
                      gpre2_ref, wup_ref, wdown_ref, gpost2_ref, o_ref, ysgu_ref):
    tm = x_ref.shape[0]
    row = lax.broadcasted_iota(jnp.int32, (CHUNK, CHUNK), 0)
    col = lax.broadcasted_iota(jnp.int32, (CHUNK, CHUNK), 1)
    lane = lax.broadcasted_iota(jnp.int32, (CHUNK, PAIR), 1)
    ws = [jnp.where(row >= col, wsp_ref[g], 0.0).astype(_BF16) for g in range(N_GROUPS)]
    for c in range(tm // CHUNK):
        r = slice(c * CHUNK, (c + 1) * CHUNK)
        for j in range(N_GROUPS // 2):
            cs = slice(j * PAIR, (j + 1) * PAIR)
            vp = vn_ref[r, cs]
            s = jnp.where(lane < HEAD_DIM, _dot(ws[2 * j], vp), _dot(ws[2 * j + 1], vp))
            s = s + bsp_ref[:, cs]
            ysgu_ref[r, cs] = (u_ref[r, cs].astype(_F32) * s).astype(_BF16)

    blocks = [slice(n * tm // N_STREAMS, (n + 1) * tm // N_STREAMS) for n in range(N_STREAMS)]

    def mix(r):
        a = _dot(ysgu_ref[r, :], wbs_ref[...])
        b = _dot(ya_ref[r, :], wba_ref[...])
        merged = (gates_ref[r, :D_MODEL].astype(_F32) * a
                  + gates_ref[r, D_MODEL:].astype(_F32) * b).astype(_BF16)
        return _dot(merged, wout_ref[...])

    def mlp(xb):
        acc = jnp.zeros(xb.shape, _F32)
        for c in range(D_FF // FF_TILE):
            cs = slice(c * FF_TILE, (c + 1) * FF_TILE)
            hid = jnp.square(jnp.maximum(_dot(xb, wup_ref[:, cs]), 0.0)).astype(_BF16)
            acc = acc + _dot(hid, wdown_ref[cs, :])
        return acc

    mixed = [mix(r) for r in blocks]
    h1 = [x_ref[r, :] + _rms_scale(o, gpost_ref[...]) for r, o in zip(blocks, mixed)]
    xb = [_rms_scale(h, gpre2_ref[...]).astype(_BF16) for h in h1]
    ff = [mlp(v) for v in xb]
    for r, h, f in zip(blocks, h1, ff):
        o_ref[r, :] = h + _rms_scale(f, gpost2_ref[...])


def _merge_mlp(x2, u2, vn2, ya2, gates2, w_spatial, b_spatial, w_bs, w_ba, w_out, g_post,
               g_pre2, w_up, w_down, g_post2):
    R, D = x2.shape
    tm = ROW_TILE
    bsp = jnp.repeat(b_spatial.T, SGU_WIDTH // N_GROUPS, axis=1)

    def const(shape):
        return pl.BlockSpec(shape, lambda i: (0,) * len(shape), pipeline_mode=pl.Buffered(1))

    def rows(width):
        return pl.BlockSpec((tm, width), lambda i: (i, 0))

    return pl.pallas_call(
        _merge_mlp_kernel,
        grid=(R // tm,),
        in_specs=[rows(D), rows(SGU_WIDTH), rows(SGU_WIDTH), rows(ATTN_WIDTH), rows(2 * D),
                  const(w_spatial.shape), const(bsp.shape),
                  const(w_bs.shape), const(w_ba.shape), const(w_out.shape), const((1, D)),
                  const((1, D)), const(w_up.shape), const(w_down.shape), const((1, D))],
        out_specs=rows(D),
        out_shape=jax.ShapeDtypeStruct((R, D), _F32),
        scratch_shapes=[pltpu.VMEM((tm, SGU_WIDTH), _BF16)],
        compiler_params=pltpu.CompilerParams(
            dimension_semantics=("parallel",), vmem_limit_bytes=VMEM_LIMIT),
        name="merge_mlp",
    )(x2, u2, vn2, ya2, gates2, w_spatial, bsp,
      w_bs, w_ba, w_out, g_post.reshape(1, D),
      g_pre2.reshape(1, D), w_up, w_down, g_post2.reshape(1, D))


def kernel(x, g_mix_pre, w_in, b_forget, g_sgu, b_sgu, w_spatial, b_spatial, w_branch_sgu,
           w_branch_attn, w_out, g_mix_post, g_ffn_pre, w_up, w_down, g_ffn_post):
    B, S, D = x.shape
    h = x
    for l in range(g_mix_pre.shape[0]):
        head_order = jnp.argsort(b_forget[l])
        later = (w_branch_sgu[l], _heads_in_order(w_branch_attn[l], head_order), w_out[l],
                 w_up[l], w_down[l])
        u, vn, qt, k, vt, gates, qat, ka, stats, w_bs, w_ba, w_o, w_u, w_d = _in_projection(
            h, g_mix_pre[l], w_in[l], b_forget[l], g_sgu[l], b_sgu[l], later, head_order)
        y_attn = _attention(k, ka, vt, qt, qat, stats)
        h = _merge_mlp(h.reshape(B * S, D), u.reshape(B * S, -1), vn.reshape(B * S, -1),
                       y_attn.reshape(B * S, -1), gates.reshape(B * S, -1),
                       w_spatial[l], b_spatial[l], w_bs, w_ba, w_o,
                       g_mix_post[l], g_ffn_pre[l], w_u, w_d,
                       g_ffn_post[l]).reshape(B, S, D)
    return h
```

```python
import functools

import jax
import jax.numpy as jnp
import numpy as np
from jax import lax
from jax.experimental import pallas as pl
from jax.experimental.pallas import tpu as pltpu

D_MODEL = 1024
N_HEADS = 8
HEAD_DIM = 64
ATTN_WIDTH = N_HEADS * HEAD_DIM
N_GROUPS = 8
SGU_WIDTH = D_MODEL // 2
CHUNK = 128
D_FF = 4 * D_MODEL
EPS = 1e-6
LOG2E = 1.4426950408889634

LANES = 128
N_SPLIT = 3
BIAS_COLS = N_HEADS * N_SPLIT
PAIR = 2 * HEAD_DIM
N_PAIRS = N_HEADS // 2

ROW_TILE = 512
FF_TILE = 1024
N_STREAMS = 2
TQ = 256
TK = 256
NQ = 4
QG = NQ * TQ
KU = 2
UNROLLED_SKEW_STEPS = (3, 4, 5, 6)
N_STATS = 4
ZERO_EXP2 = 136.0
NORM_SLACK = 2.05
V7X_VMEM_BYTES = 64 * 1024 * 1024
VMEM_LIMIT = V7X_VMEM_BYTES * 7 // 8
BF16_SUBLANES = 16

_BF16 = jnp.bfloat16
_F32 = jnp.float32


def _split_bf16(x):
    parts = []
    r = x
    for _ in range(N_SPLIT):
        p = r.astype(_BF16)
        parts.append(p)
        r = r - p.astype(_F32)
    return parts


def _dot(a, b):
    return jnp.dot(a, b, preferred_element_type=_F32)


def _dot_nt(a, b):
    return lax.dot_general(a, b, (((1,), (1,)), ((), ())), preferred_element_type=_F32)


def _inproj_kernel(x_ref, gpre_ref, wzt_ref, wqt_ref, wkt_ref, wvt_ref, wft_ref, wgt_ref,
                   bf_ref, gsgu_ref, bsgu_ref, tri_ref, expand_ref, expand_t_ref,
                   head_rows_ref, piece_rows_ref, *rest, n_cast):
    cast_in = rest[:n_cast]
    (u_ref, vn_ref, qt_ref, k_ref, vt_ref, gates_ref, qat_ref, ka_ref,
     stats_ref) = rest[n_cast:n_cast + 9]
    cast_out = rest[n_cast + 9:2 * n_cast + 9]
    carry_ref, = rest[2 * n_cast + 9:]
    for src, dst in zip(cast_in, cast_out):
        dst[...] = src[...].astype(_BF16)

    @pl.when(pl.program_id(1) == 0)
    def _():
        carry_ref[...] = jnp.zeros_like(carry_ref)
        stats_ref[...] = jnp.zeros_like(stats_ref)

    x = x_ref[0]
    tm = x.shape[0]
    ms = jnp.mean(x * x, axis=-1, keepdims=True)
    xb = ((x * lax.rsqrt(ms + EPS)) * gpre_ref[...]).astype(_BF16)

    lane = lax.broadcasted_iota(jnp.int32, (tm, LANES), 1)

    def pack_pieces(v):
        hi, mid, lo = _split_bf16(v)
        zero = jnp.zeros((), _BF16)
        return jnp.where(lane < N_HEADS, hi,
                         jnp.where(lane < 2 * N_HEADS, mid,
                                   jnp.where(lane < N_SPLIT * N_HEADS, lo, zero)))

    f = _dot_nt(xb, wft_ref[...]) + bf_ref[...]

    z = jax.nn.gelu(_dot_nt(xb, wzt_ref[...]), approximate=True)
    u_ref[0] = z[:, :SGU_WIDTH].astype(_BF16)
    v = z[:, SGU_WIDTH:]
    mu = jnp.mean(v, axis=-1, keepdims=True)
    vc = v - mu
    var = jnp.mean(vc * vc, axis=-1, keepdims=True)
    vn_ref[0] = ((vc * lax.rsqrt(var + EPS)) * gsgu_ref[...] + bsgu_ref[...]).astype(_BF16)

    log_f = jnp.minimum(f, 0.0) - jnp.log(1.0 + jnp.exp(-jnp.abs(f)))
    sums = _dot(tri_ref[...], pack_pieces(log_f))

    qt = _dot_nt(wqt_ref[...], xb) * (HEAD_DIM ** -0.5 * LOG2E)
    qt_ref[0] = qt.astype(_BF16)
    k = _dot_nt(xb, wkt_ref[...])
    k_ref[0] = k.astype(_BF16)
    qn2 = _dot(head_rows_ref[...], (qt * qt).astype(_BF16))
    kn2 = _dot_nt(head_rows_ref[...], (k * k).astype(_BF16))

    total = sums
    for shift in (N_HEADS, 2 * N_HEADS, LANES - N_HEADS, LANES - 2 * N_HEADS):
        total = total + pltpu.roll(sums, shift, 1)
    cum = carry_ref[0:1, :] + total
    carry_ref[0:1, :] = cum[tm - 1:tm, :]
    pieces = pack_pieces(cum * LOG2E)
    spread = _dot(pieces, expand_ref[...])
    spread_t = _dot_nt(expand_t_ref[...], pieces)

    cum_t = _dot_nt(piece_rows_ref[...], pieces)
    tile_lane = lax.broadcasted_iota(jnp.int32, (2 * N_HEADS, LANES), 1)
    first_tile = pl.program_id(1) * (tm // TK)
    stats = [stats_ref[0, s] for s in range(N_STATS)]
    for j in range(tm // TK):
        lo, hi = j * TK, (j + 1) * TK
        cols = (cum_t[:, lo:lo + 1], cum_t[:, hi - 1:hi],
                jnp.max(qn2[:, lo:hi], axis=1, keepdims=True),
                jnp.max(kn2[:, lo:hi], axis=1, keepdims=True))
        stats = [jnp.where(tile_lane == first_tile + j, c, s) for c, s in zip(cols, stats)]
    for s in range(N_STATS):
        stats_ref[0, s] = stats[s]

    vt = _dot_nt(wvt_ref[...], xb).astype(_BF16)
    for h in range(N_HEADS):
        vt_ref[0, h, :HEAD_DIM, :] = vt[h * HEAD_DIM:(h + 1) * HEAD_DIM, :]
        vt_ref[0, h, HEAD_DIM:, :] = jnp.ones((PAIR - HEAD_DIM, tm), _BF16)

    gates_ref[0] = _dot_nt(xb, wgt_ref[...]).astype(_BF16)

    in_a = lane < BIAS_COLS
    in_b = jnp.logical_and(lane >= BIAS_COLS, lane < 2 * BIAS_COLS)
    ka_ref[0] = jnp.where(in_a, -spread, jnp.where(in_b, 1.0, 0.0)).astype(_BF16)
    row = lax.broadcasted_iota(jnp.int32, (LANES, tm), 0)
    in_a = row < BIAS_COLS
    in_b = jnp.logical_and(row >= BIAS_COLS, row < 2 * BIAS_COLS)
    qat_ref[0] = jnp.where(in_a, 1.0, jnp.where(in_b, spread_t, 0.0)).astype(_BF16)


def _head_rows():
    r = np.zeros((2 * N_HEADS, ATTN_WIDTH), np.float32)
    for h in range(N_HEADS):
        r[h, h * HEAD_DIM:(h + 1) * HEAD_DIM] = 1.0
    return r


def _piece_rows():
    r = np.zeros((2 * N_HEADS, LANES), np.float32)
    for h in range(N_HEADS):
        for i in range(N_SPLIT):
            r[h, N_HEADS * i + h] = 1.0
    return r


def _expand_matrix():
    e = np.zeros((LANES, LANES), np.float32)
    for h in range(N_HEADS):
        for i in range(N_SPLIT):
            e[N_HEADS * i + h, N_SPLIT * h + i] = 1.0
            e[N_HEADS * i + h, BIAS_COLS + N_SPLIT * h + i] = 1.0
    return e


_IN_OFFSETS = tuple(int(v) for v in np.cumsum(
    (0, 2 * SGU_WIDTH, ATTN_WIDTH, ATTN_WIDTH, ATTN_WIDTH, N_HEADS, 2 * D_MODEL)))


def _heads_in_order(w, head_order):
    return w.reshape(N_HEADS, HEAD_DIM, -1)[head_order].reshape(w.shape)


def _in_projection(x, g_pre, w_in, b_forget, g_sgu, b_sgu, later_weights, head_order):
    B, S, D = x.shape
    tm = ROW_TILE
    n_steps = B * (S // tm)
    o = _IN_OFFSETS
    wt = jnp.swapaxes(w_in, 0, 1)
    wzt, wqt, wkt, wvt, wgt = (wt[o[i]:o[i + 1]].astype(_BF16) for i in (0, 1, 2, 3, 5))
    wqt, wkt, wvt = (_heads_in_order(w, head_order) for w in (wqt, wkt, wvt))
    row_pad = ((0, LANES - N_SPLIT * N_HEADS), (0, 0))
    wft = jnp.pad(jnp.tile(wt[o[4]:o[5]][head_order], (N_SPLIT, 1)), row_pad).astype(_BF16)
    lane_pad = ((0, 0), (0, LANES - N_SPLIT * N_HEADS))
    bf = jnp.pad(jnp.tile(b_forget[head_order].reshape(1, N_HEADS), (1, N_SPLIT)), lane_pad)
    tri = jnp.asarray(np.tril(np.ones((tm, tm), np.float32)), _BF16)
    expand = jnp.asarray(_expand_matrix(), _BF16)
    expand_t = jnp.asarray(_expand_matrix().T, _BF16)
    head_rows = jnp.asarray(_head_rows(), _BF16)
    piece_rows = jnp.asarray(_piece_rows(), _BF16)

    def const(shape):
        return pl.BlockSpec(shape, lambda b, i: (0,) * len(shape))

    def rows(width):
        return pl.BlockSpec((1, tm, width), lambda b, i: (b, i, 0))

    out_shape = (
        jax.ShapeDtypeStruct((B, S, SGU_WIDTH), _BF16),
        jax.ShapeDtypeStruct((B, S, SGU_WIDTH), _BF16),
        jax.ShapeDtypeStruct((B, ATTN_WIDTH, S), _BF16),
        jax.ShapeDtypeStruct((B, S, ATTN_WIDTH), _BF16),
        jax.ShapeDtypeStruct((B, N_HEADS, PAIR, S), _BF16),
        jax.ShapeDtypeStruct((B, S, 2 * D_MODEL), _BF16),
        jax.ShapeDtypeStruct((B, LANES, S), _BF16),
        jax.ShapeDtypeStruct((B, S, LANES), _BF16),
        jax.ShapeDtypeStruct((B, N_STATS, 2 * N_HEADS, LANES), _F32),
    )

    def cols(height):
        return pl.BlockSpec((1, height, tm), lambda b, i: (b, 0, i))

    def row_block(w):
        assert w.shape[0] % (n_steps * BF16_SUBLANES) == 0
        return pl.BlockSpec((w.shape[0] // n_steps, w.shape[1]),
                            lambda b, i: (b * (S // tm) + i, 0))

    out_specs = (
        rows(SGU_WIDTH), rows(SGU_WIDTH), cols(ATTN_WIDTH), rows(ATTN_WIDTH),
        pl.BlockSpec((1, N_HEADS, PAIR, tm), lambda b, i: (b, 0, 0, i)),
        rows(2 * D_MODEL), cols(LANES), rows(LANES),
        pl.BlockSpec((1, N_STATS, 2 * N_HEADS, LANES), lambda b, i: (b, 0, 0, 0)),
    ) + tuple(row_block(w) for w in later_weights)
    out_shape += tuple(jax.ShapeDtypeStruct(w.shape, _BF16) for w in later_weights)
    return pl.pallas_call(
        functools.partial(_inproj_kernel, n_cast=len(later_weights)),
        grid=(B, S // tm),
        in_specs=[
            rows(D), const((1, D)),
            const(wzt.shape), const(wqt.shape), const(wkt.shape), const(wvt.shape),
            const(wft.shape), const(wgt.shape),
            const((1, LANES)), const((1, SGU_WIDTH)), const((1, SGU_WIDTH)),
            const(tri.shape), const(expand.shape), const(expand_t.shape),
            const(head_rows.shape), const(piece_rows.shape),
        ] + [row_block(w) for w in later_weights],
        out_specs=out_specs,
        out_shape=out_shape,
        scratch_shapes=[pltpu.VMEM((8, LANES), _F32)],
        compiler_params=pltpu.CompilerParams(
            dimension_semantics=("arbitrary", "arbitrary"),
            vmem_limit_bytes=VMEM_LIMIT),
        name="in_projection",
    )(x, g_pre.reshape(1, D), wzt, wqt, wkt, wvt, wft, wgt, bf,
      g_sgu.reshape(1, SGU_WIDTH), b_sgu.reshape(1, SGU_WIDTH), tri, expand, expand_t,
      head_rows, piece_rows, *later_weights)


def _attn_kernel(plan_ref, k_ref, ka_ref, vt_ref, qt_ref, qat_ref, o_ref,
                 qf_ref, st_a, st_b, st_c, m_ref, acc_ref):
    batch = pl.program_id(0)
    pair = pl.program_id(1)
    S = k_ref.shape[1]
    row = lax.broadcasted_iota(jnp.int32, (PAIR, 1), 0)
    all_tiles = [(e, t) for t in range(NQ) for e in range(2)]

    def key_tile(k0):
        return jnp.concatenate([k_ref[0, pl.ds(k0, TK), :], ka_ref[0, pl.ds(k0, TK), :]], axis=1)

    def offsets(g, j):
        n, skew = plan_ref[batch, pair, g, 0], plan_ref[batch, pair, g, 1]
        first = (g * NQ - n - skew + j) * TK
        return [pl.multiple_of(first + t * skew * TK, TK) for t in range(NQ)]

    def update(e, t, st, k0, masked):
        cs = slice(t * TQ, (t + 1) * TQ)
        if masked:
            key_i = lax.broadcasted_iota(jnp.int32, (TK, TQ), 0)
            qry_i = lax.broadcasted_iota(jnp.int32, (TK, TQ), 1)
            st = jnp.where(key_i <= qry_i, st, -jnp.inf)
        m = m_ref[e, :, cs]
        m_new = jnp.maximum(m, jnp.max(st, axis=0, keepdims=True))
        p = jnp.exp2(st - m_new)
        alpha = jnp.exp2(m - m_new)
        m_ref[e, :, cs] = m_new
        acc_ref[e, :, cs] = alpha * acc_ref[e, :, cs] + _dot(
            vt_ref[0, e, :, pl.ds(k0, TK)], p.astype(_BF16))

    def step(cur, cur_tiles, cur_offs, nxt, nxt_tiles, nxt_offs):
        for n in range(max(len(cur_tiles), len(nxt_tiles))):
            if n < len(nxt_tiles):
                e, t = nxt_tiles[n]
                nxt[e, t] = _dot(key_tile(nxt_offs[t]), qf_ref[e, :, t * TQ:(t + 1) * TQ])
            if n < len(cur_tiles):
                e, t, masked = cur_tiles[n]
                update(e, t, cur[e, t], cur_offs[t], masked)

    def load_queries(g):
        q0 = pl.multiple_of(g * QG, QG)
        qt = qt_ref[0, :, pl.ds(q0, QG)].astype(_F32)
        qat = qat_ref[0, :, pl.ds(q0, QG)].astype(_F32)
        for e in range(2):
            a0 = N_SPLIT * (2 * pair + e)
            q_mask = jnp.logical_and(row >= HEAD_DIM * e, row < HEAD_DIM * (e + 1))
            a_mask = jnp.logical_or(
                jnp.logical_and(row >= a0, row < a0 + N_SPLIT),
                jnp.logical_and(row >= BIAS_COLS + a0, row < BIAS_COLS + a0 + N_SPLIT))
            qf_ref[e, :PAIR, :] = jnp.where(q_mask, qt, 0.0).astype(_BF16)
            qf_ref[e, PAIR:, :] = jnp.where(a_mask, qat, 0.0).astype(_BF16)

    def reset_state():
        m_ref[...] = jnp.full(m_ref.shape, -jnp.inf, _F32)
        acc_ref[...] = jnp.zeros(acc_ref.shape, _F32)

    full = [(e, t, False) for e, t in all_tiles]
    bufs = (st_a, st_b)
    n_groups = S // QG

    def finish(g):
        out = [acc_ref[e, :HEAD_DIM, :] * (1.0 / acc_ref[e, HEAD_DIM:HEAD_DIM + 1, :])
               for e in range(2)]
        o_ref[0, pl.ds(pl.multiple_of(g * QG, QG), QG), :] = (
            jnp.concatenate(out, axis=0).T.astype(_BF16))
        reset_state()

    def next_group(g):
        return jnp.minimum(g + 1, n_groups - 1)

    def triangle(g):
        q0 = pl.multiple_of(g * QG, QG)
        for i in range(NQ):
            cur_tiles = [(e, t, t == i) for t in range(i, NQ) for e in range(2)]
            cur_offs = [q0 + i * TK] * NQ
            if i + 1 < NQ:
                nxt_tiles = [(e, t) for t in range(i + 1, NQ) for e in range(2)]
                nxt_offs = [q0 + (i + 1) * TK] * NQ
            else:
                load_queries(next_group(g))
                nxt_tiles, nxt_offs = all_tiles, offsets(next_group(g), 0)
            step(bufs[i % 2], cur_tiles, cur_offs, bufs[(i + 1) % 2], nxt_tiles, nxt_offs)

    load_queries(0)
    reset_state()
    step(None, [], None, st_a, all_tiles, [0] * NQ)
    triangle(0)

    def q_group(g, _):
        n = plan_ref[batch, pair, g, 0]
        skew = plan_ref[batch, pair, g, 1]

        def full_steps(i, _):
            for u in range(KU):
                j = i * KU + u
                step(bufs[u % 2], full, offsets(g, j), bufs[(u + 1) % 2], all_tiles,
                     offsets(g, j + 1))
            return 0

        def skew_tail(j):
            step(st_a, full, offsets(g, j), st_b, all_tiles, offsets(g, j + 1))
            load_queries(next_group(g))
            step(st_b, [(e, t, True) for e, t in all_tiles], offsets(g, j + 1),
                 st_a, all_tiles, offsets(next_group(g), 0))

        unrolled = jnp.logical_and(skew == 1, functools.reduce(
            jnp.logical_or, [n == n_static for n_static in UNROLLED_SKEW_STEPS]))
        for n_static in UNROLLED_SKEW_STEPS:
            @pl.when(jnp.logical_and(skew == 1, n == n_static))
            def _(n_static=n_static):
                last = n_static + 1
                seq = [bufs[j % 2] for j in range(last)] + [st_c if last % 2 == 0 else st_b,
                                                            st_a]
                finish(g - 1)
                for j in range(last):
                    step(seq[j], full, offsets(g, j), seq[j + 1], all_tiles, offsets(g, j + 1))
                load_queries(next_group(g))
                step(seq[last], [(e, t, True) for e, t in all_tiles], offsets(g, last),
                     st_a, all_tiles, offsets(next_group(g), 0))

        @pl.when(jnp.logical_not(unrolled))
        def _():
            finish(g - 1)
            lax.fori_loop(0, n // KU, full_steps, 0)

        @pl.when(jnp.logical_and(skew == 1, jnp.logical_not(unrolled)))
        def _():
            skew_tail(n)

        @pl.when(skew == 0)
        def _():
            triangle(g)

        return 0

    lax.fori_loop(1, n_groups, q_group, 0)
    finish(n_groups - 1)


def _sweep_plan(stats, n_tiles):
    cum_first, cum_last, qn2, kn2 = (stats[:, s, :N_HEADS, :n_tiles] for s in range(N_STATS))
    B = stats.shape[0]
    n_groups = n_tiles // NQ
    k_norm = jnp.sqrt(jnp.max(kn2, axis=-1, keepdims=True))
    reach = NORM_SLACK * jnp.sqrt(qn2) * k_norm + cum_first
    bound = reach[..., None] - cum_last[:, :, None, :]
    tile = jnp.arange(n_tiles)
    needed = jnp.logical_and(tile[None, :] < tile[:, None], bound >= -ZERO_EXP2)
    w = jnp.sum(needed, axis=-1).astype(jnp.int32)
    w = jnp.max(w.reshape(B, N_PAIRS, 2, n_groups, NQ), axis=2)
    start = jnp.arange(n_groups, dtype=jnp.int32) * NQ
    n_flat = jnp.max(jnp.maximum(w - jnp.arange(NQ, dtype=jnp.int32), 0), axis=-1)
    n_flat = jnp.minimum(((n_flat + KU - 1) // KU) * KU, start)
    n_skew = jnp.maximum(jnp.max(w, axis=-1), 1)
    exact = functools.reduce(jnp.logical_or, [n_skew == s + 1 for s in UNROLLED_SKEW_STEPS])
    n_skew = jnp.where(exact, n_skew, n_skew + (n_skew + 1) % 2)
    units_flat = NQ * n_flat + NQ * (NQ + 1) // 2
    units_skew = NQ * n_skew + NQ
    skew = jnp.logical_and(n_skew < start, units_skew < units_flat)
    n = jnp.where(skew, n_skew - 1, n_flat)
    return jnp.stack([n, skew.astype(jnp.int32)], axis=-1)


def _attention(k, ka, vt, qt, qat, stats):
    B, S, _ = k.shape
    assert TQ == TK and KU == 2 and NQ % KU == 0 and S % QG == 0
    assert S // TK <= LANES
    plan = _sweep_plan(stats, S // TK)
    pair_rows = pl.BlockSpec((1, S, PAIR), lambda b, j, nb: (b, 0, j))
    return pl.pallas_call(
        _attn_kernel,
        grid_spec=pltpu.PrefetchScalarGridSpec(
            num_scalar_prefetch=1,
            grid=(B, N_PAIRS),
            in_specs=[pair_rows, pl.BlockSpec((1, S, LANES), lambda b, j, nb: (b, 0, 0)),
                      pl.BlockSpec((1, 2, PAIR, S), lambda b, j, nb: (b, j, 0, 0)),
                      pl.BlockSpec((1, PAIR, S), lambda b, j, nb: (b, j, 0)),
                      pl.BlockSpec((1, LANES, S), lambda b, j, nb: (b, 0, 0))],
            out_specs=pair_rows,
            scratch_shapes=[pltpu.VMEM((2, PAIR + LANES, QG), _BF16),
                            pltpu.VMEM((2, NQ, TK, TQ), _F32), pltpu.VMEM((2, NQ, TK, TQ), _F32),
                            pltpu.VMEM((2, NQ, TK, TQ), _F32),
                            pltpu.VMEM((2, 1, QG), _F32), pltpu.VMEM((2, PAIR, QG), _F32)]),
        out_shape=jax.ShapeDtypeStruct((B, S, ATTN_WIDTH), _BF16),
        compiler_params=pltpu.CompilerParams(
            dimension_semantics=("parallel", "parallel"),
            vmem_limit_bytes=VMEM_LIMIT),
        name="forgetting_attention",
    )(plan, k, ka, vt, qt, qat)


def _rms_scale(v, g):
    ms = jnp.mean(v * v, axis=-1, keepdims=True)
    return (v * lax.rsqrt(ms + EPS)) * g


def _merge_mlp_kernel(x_ref, u_ref, vn_ref, ya_ref, gates_ref, wsp_ref, bsp_ref,
                      wbs_ref, wba_ref, wout_ref, gpost_ref,
                      gpre2_ref, wup_ref, wdown_ref, gpost2_ref, o_ref, ysgu_ref):
    tm = x_ref.shape[0]
    row = lax.broadcasted_iota(jnp.int32, (CHUNK, CHUNK), 0)
    col = lax.broadcasted_iota(jnp.int32, (CHUNK, CHUNK), 1)
    lane = lax.broadcasted_iota(jnp.int32, (CHUNK, PAIR), 1)
    ws = [jnp.where(row >= col, wsp_ref[g], 0.0).astype(_BF16) for g in range(N_GROUPS)]
    for c in range(tm // CHUNK):
        r = slice(c * CHUNK, (c + 1) * CHUNK)
        for j in range(N_GROUPS // 2):
            cs = slice(j * PAIR, (j + 1) * PAIR)
            vp = vn_ref[r, cs]
            s = jnp.where(lane < HEAD_DIM, _dot(ws[2 * j], vp), _dot(ws[2 * j + 1], vp))
            s = s + bsp_ref[:, cs]
            ysgu_ref[r, cs] = (u_ref[r, cs].astype(_F32) * s).astype(_BF16)

    blocks = [slice(n * tm // N_STREAMS, (n + 1) * tm // N_STREAMS) for n in range(N_STREAMS)]

    def mix(r):
        a = _dot(ysgu_ref[r, :], wbs_ref[...])
        b = _dot(ya_ref[r, :], wba_ref[...])
        gate_a = jax.nn.sigmoid(gates_ref[r, :D_MODEL].astype(_F32))
        gate_b = jax.nn.sigmoid(gates_ref[r, D_MODEL:].astype(_F32))
        merged = (gate_a * a + gate_b * b).astype(_BF16)
        return _dot(merged, wout_ref[...])

    def mlp(xb):
        acc = jnp.zeros(xb.shape, _F32)
        for c in range(D_FF // FF_TILE):
            cs = slice(c * FF_TILE, (c + 1) * FF_TILE)
            hid = jnp.square(jnp.maximum(_dot(xb, wup_ref[:, cs]), 0.0)).astype(_BF16)
            acc = acc + _dot(hid, wdown_ref[cs, :])
        return acc

    mixed = [mix(r) for r in blocks]
    h1 = [x_ref[r, :] + _rms_scale(o, gpost_ref[...]) for r, o in zip(blocks, mixed)]
    xb = [_rms_scale(h, gpre2_ref[...]).astype(_BF16) for h in h1]
    ff = [mlp(v) for v in xb]
    for r, h, f in zip(blocks, h1, ff):
        o_ref[r, :] = h + _rms_scale(f, gpost2_ref[...])


def _merge_mlp(x2, u2, vn2, ya2, gates2, w_spatial, b_spatial, w_bs, w_ba, w_out, g_post,
               g_pre2, w_up, w_down, g_post2):
    R, D = x2.shape
    tm = ROW_TILE
    bsp = jnp.repeat(b_spatial.T, SGU_WIDTH // N_GROUPS, axis=1)

    def const(shape):
        return pl.BlockSpec(shape, lambda i: (0,) * len(shape), pipeline_mode=pl.Buffered(1))

    def rows(width):
        return pl.BlockSpec((tm, width), lambda i: (i, 0))

    return pl.pallas_call(
        _merge_mlp_kernel,
        grid=(R // tm,),
        in_specs=[rows(D), rows(SGU_WIDTH), rows(SGU_WIDTH), rows(ATTN_WIDTH), rows(2 * D),
                  const(w_spatial.shape), const(bsp.shape),
                  const(w_bs.shape), const(w_ba.shape), const(w_out.shape), const((1, D)),
                  const((1, D)), const(w_up.shape), const(w_down.shape), const((1, D))],
        out_specs=rows(D),
        out_shape=jax.ShapeDtypeStruct((R, D), _F32),
        scratch_shapes=[pltpu.VMEM((tm, SGU_WIDTH), _BF16)],
        compiler_params=pltpu.CompilerParams(
            dimension_semantics=("parallel",), vmem_limit_bytes=VMEM_LIMIT),
        name="merge_mlp",
    )(x2, u2, vn2, ya2, gates2, w_spatial, bsp,
      w_bs, w_ba, w_out, g_post.reshape(1, D),
      g_pre2.reshape(1, D), w_up, w_down, g_post2.reshape(1, D))


def kernel(x, g_mix_pre, w_in, b_forget, g_sgu, b_sgu, w_spatial, b_spatial, w_branch_sgu,
           w_branch_attn, w_out, g_mix_post, g_ffn_pre, w_up, w_down, g_ffn_post):
    B, S, D = x.shape
    h = x
    for l in range(g_mix_pre.shape[0]):
        head_order = jnp.argsort(b_forget[l])
        later = (w_branch_sgu[l], _heads_in_order(w_branch_attn[l], head_order), w_out[l],
                 w_up[l], w_down[l])
        u, vn, qt, k, vt, gates, qat, ka, stats, w_bs, w_ba, w_o, w_u, w_d = _in_projection(
            h, g_mix_pre[l], w_in[l], b_forget[l], g_sgu[l], b_sgu[l], later, head_order)
        y_attn = _attention(k, ka, vt, qt, qat, stats)
        h = _merge_mlp(h.reshape(B * S, D), u.reshape(B * S, -1), vn.reshape(B * S, -1),
                       y_attn.reshape(B * S, -1), gates.reshape(B * S, -1),
                       w_spatial[l], b_spatial[l], w_bs, w_ba, w_o,
                       g_mix_post[l], g_ffn_pre[l], w_u, w_d,
                       g_ffn_post[l]).reshape(B, S, D)
    return h
```

```python
import functools

import jax
import jax.numpy as jnp
import numpy as np
from jax import lax
from jax.experimental import pallas as pl
from jax.experimental.pallas import tpu as pltpu

D_MODEL = 1024
N_HEADS = 8
HEAD_DIM = 64
ATTN_WIDTH = N_HEADS * HEAD_DIM
N_GROUPS = 8
SGU_WIDTH = D_MODEL // 2
CHUNK = 128
D_FF = 4 * D_MODEL
EPS = 1e-6
LOG2E = 1.4426950408889634

LANES = 128
N_SPLIT = 3
BIAS_COLS = N_HEADS * N_SPLIT
PAIR = 2 * HEAD_DIM
N_PAIRS = N_HEADS // 2

ROW_TILE = 512
FF_TILE = 1024
N_STREAMS = 2
TQ = 256
TK = 256
NQ = 4
QG = NQ * TQ
KU = 2
UNROLLED_SKEW_STEPS = (3, 4, 5, 6)
N_STATS = 4
ZERO_EXP2 = 136.0
NORM_SLACK = 2.05
V7X_VMEM_BYTES = 64 * 1024 * 1024
VMEM_LIMIT = V7X_VMEM_BYTES * 7 // 8
BF16_SUBLANES = 16

_BF16 = jnp.bfloat16
_F32 = jnp.float32


def _split_bf16(x):
    parts = []
    r = x
    for _ in range(N_SPLIT):
        p = r.astype(_BF16)
        parts.append(p)
        r = r - p.astype(_F32)
    return parts


def _gelu_tanh(x):
    c1 = (2.0 / np.pi) ** 0.5
    inner = x * (c1 + (c1 * 0.044715) * (x * x))
    hx = 0.5 * x
    return hx + hx * jnp.tanh(inner)


def _dot(a, b):
    return jnp.dot(a, b, preferred_element_type=_F32)


def _dot_nt(a, b):
    return lax.dot_general(a, b, (((1,), (1,)), ((), ())), preferred_element_type=_F32)


def _inproj_kernel(x_ref, gpre_ref, wzt_ref, wqt_ref, wkt_ref, wvt_ref, wft_ref, wgt_ref,
                   bf_ref, gsgu_ref, bsgu_ref, tri_ref, expand_ref, expand_t_ref,
                   head_rows_ref, piece_rows_ref, *rest, n_cast):
    cast_in = rest[:n_cast]
    (u_ref, vn_ref, qt_ref, k_ref, vt_ref, gates_ref, qat_ref, ka_ref,
     stats_ref) = rest[n_cast:n_cast + 9]
    cast_out = rest[n_cast + 9:2 * n_cast + 9]
    carry_ref, = rest[2 * n_cast + 9:]
    for src, dst in zip(cast_in, cast_out):
        dst[...] = src[...].astype(_BF16)

    @pl.when(pl.program_id(1) == 0)
    def _():
        carry_ref[...] = jnp.zeros_like(carry_ref)
        stats_ref[...] = jnp.zeros_like(stats_ref)

    x = x_ref[0]
    tm = x.shape[0]
    ms = jnp.mean(x * x, axis=-1, keepdims=True)
    xb = ((x * lax.rsqrt(ms + EPS)) * gpre_ref[...]).astype(_BF16)

    lane = lax.broadcasted_iota(jnp.int32, (tm, LANES), 1)

    def pack_pieces(v):
        hi, mid, lo = _split_bf16(v)
        zero = jnp.zeros((), _BF16)
        return jnp.where(lane < N_HEADS, hi,
                         jnp.where(lane < 2 * N_HEADS, mid,
                                   jnp.where(lane < N_SPLIT * N_HEADS, lo, zero)))

    f = _dot_nt(xb, wft_ref[...]) + bf_ref[...]

    z = _gelu_tanh(_dot_nt(xb, wzt_ref[...]))
    u_ref[0] = z[:, :SGU_WIDTH].astype(_BF16)
    v = z[:, SGU_WIDTH:]
    mu = jnp.mean(v, axis=-1, keepdims=True)
    vc = v - mu
    var = jnp.mean(vc * vc, axis=-1, keepdims=True)
    vn_ref[0] = ((vc * lax.rsqrt(var + EPS)) * gsgu_ref[...] + bsgu_ref[...]).astype(_BF16)

    log_f = jnp.minimum(f, 0.0) - jnp.log(1.0 + jnp.exp(-jnp.abs(f)))
    sums = _dot(tri_ref[...], pack_pieces(log_f))

    qt = _dot_nt(wqt_ref[...], xb).astype(_BF16)
    qt_ref[0] = qt
    k = _dot_nt(xb, wkt_ref[...]).astype(_BF16)
    k_ref[0] = k
    qn2 = _dot(head_rows_ref[...], qt * qt)
    kn2 = _dot_nt(head_rows_ref[...], k * k)

    total = sums
    for shift in (N_HEADS, 2 * N_HEADS, LANES - N_HEADS, LANES - 2 * N_HEADS):
        total = total + pltpu.roll(sums, shift, 1)
    cum = carry_ref[0:1, :] + total
    carry_ref[0:1, :] = cum[tm - 1:tm, :]
    pieces = pack_pieces(cum * LOG2E)
    spread = _dot(pieces, expand_ref[...])
    spread_t = _dot_nt(expand_t_ref[...], pieces)

    cum_t = _dot_nt(piece_rows_ref[...], pieces)
    tile_lane = lax.broadcasted_iota(jnp.int32, (2 * N_HEADS, LANES), 1)
    first_tile = pl.program_id(1) * (tm // TK)
    stats = [stats_ref[0, s] for s in range(N_STATS)]
    for j in range(tm // TK):
        lo, hi = j * TK, (j + 1) * TK
        cols = (cum_t[:, lo:lo + 1], cum_t[:, hi - 1:hi],
                jnp.max(qn2[:, lo:hi], axis=1, keepdims=True),
                jnp.max(kn2[:, lo:hi], axis=1, keepdims=True))
        stats = [jnp.where(tile_lane == first_tile + j, c, s) for c, s in zip(cols, stats)]
    for s in range(N_STATS):
        stats_ref[0, s] = stats[s]

    vt = _dot_nt(wvt_ref[...], xb).astype(_BF16)
    for h in range(N_HEADS):
        vt_ref[0, h, :HEAD_DIM, :] = vt[h * HEAD_DIM:(h + 1) * HEAD_DIM, :]
        vt_ref[0, h, HEAD_DIM:, :] = jnp.ones((PAIR - HEAD_DIM, tm), _BF16)

    gates_ref[0] = _dot_nt(xb, wgt_ref[...]).astype(_BF16)

    in_a = lane < BIAS_COLS
    in_b = jnp.logical_and(lane >= BIAS_COLS, lane < 2 * BIAS_COLS)
    ka_ref[0] = jnp.where(in_a, -spread, jnp.where(in_b, 1.0, 0.0)).astype(_BF16)
    row = lax.broadcasted_iota(jnp.int32, (LANES, tm), 0)
    in_a = row < BIAS_COLS
    in_b = jnp.logical_and(row >= BIAS_COLS, row < 2 * BIAS_COLS)
    qat_ref[0] = jnp.where(in_a, 1.0, jnp.where(in_b, spread_t, 0.0)).astype(_BF16)


def _head_rows():
    r = np.zeros((2 * N_HEADS, ATTN_WIDTH), np.float32)
    for h in range(N_HEADS):
        r[h, h * HEAD_DIM:(h + 1) * HEAD_DIM] = 1.0
    return r


def _piece_rows():
    r = np.zeros((2 * N_HEADS, LANES), np.float32)
    for h in range(N_HEADS):
        for i in range(N_SPLIT):
            r[h, N_HEADS * i + h] = 1.0
    return r


def _expand_matrix():
    e = np.zeros((LANES, LANES), np.float32)
    for h in range(N_HEADS):
        for i in range(N_SPLIT):
            e[N_HEADS * i + h, N_SPLIT * h + i] = 1.0
            e[N_HEADS * i + h, BIAS_COLS + N_SPLIT * h + i] = 1.0
    return e


_IN_OFFSETS = tuple(int(v) for v in np.cumsum(
    (0, 2 * SGU_WIDTH, ATTN_WIDTH, ATTN_WIDTH, ATTN_WIDTH, N_HEADS, 2 * D_MODEL)))


def _heads_in_order(w, head_order):
    return w.reshape(N_HEADS, HEAD_DIM, -1)[head_order].reshape(w.shape)


def _in_projection(x, g_pre, w_in, b_forget, g_sgu, b_sgu, later_weights, head_order):
    B, S, D = x.shape
    tm = ROW_TILE
    n_steps = B * (S // tm)
    o = _IN_OFFSETS
    wt = jnp.swapaxes(w_in, 0, 1)
    wzt, wkt, wvt, wgt = (wt[o[i]:o[i + 1]].astype(_BF16) for i in (0, 2, 3, 5))
    wqt = (wt[o[1]:o[2]] * (HEAD_DIM ** -0.5 * LOG2E)).astype(_BF16)
    wqt, wkt, wvt = (_heads_in_order(w, head_order) for w in (wqt, wkt, wvt))
    row_pad = ((0, LANES - N_SPLIT * N_HEADS), (0, 0))
    wft = jnp.pad(jnp.tile(wt[o[4]:o[5]][head_order], (N_SPLIT, 1)), row_pad).astype(_BF16)
    lane_pad = ((0, 0), (0, LANES - N_SPLIT * N_HEADS))
    bf = jnp.pad(jnp.tile(b_forget[head_order].reshape(1, N_HEADS), (1, N_SPLIT)), lane_pad)
    tri = jnp.asarray(np.tril(np.ones((tm, tm), np.float32)), _BF16)
    expand = jnp.asarray(_expand_matrix(), _BF16)
    expand_t = jnp.asarray(_expand_matrix().T, _BF16)
    head_rows = jnp.asarray(_head_rows(), _BF16)
    piece_rows = jnp.asarray(_piece_rows(), _BF16)

    def const(shape):
        return pl.BlockSpec(shape, lambda b, i: (0,) * len(shape))

    def rows(width):
        return pl.BlockSpec((1, tm, width), lambda b, i: (b, i, 0))

    out_shape = (
        jax.ShapeDtypeStruct((B, S, SGU_WIDTH), _BF16),
        jax.ShapeDtypeStruct((B, S, SGU_WIDTH), _BF16),
        jax.ShapeDtypeStruct((B, ATTN_WIDTH, S), _BF16),
        jax.ShapeDtypeStruct((B, S, ATTN_WIDTH), _BF16),
        jax.ShapeDtypeStruct((B, N_HEADS, PAIR, S), _BF16),
        jax.ShapeDtypeStruct((B, S, 2 * D_MODEL), _BF16),
        jax.ShapeDtypeStruct((B, LANES, S), _BF16),
        jax.ShapeDtypeStruct((B, S, LANES), _BF16),
        jax.ShapeDtypeStruct((B, N_STATS, 2 * N_HEADS, LANES), _F32),
    )

    def cols(height):
        return pl.BlockSpec((1, height, tm), lambda b, i: (b, 0, i))

    def row_block(w):
        assert w.shape[0] % (n_steps * BF16_SUBLANES) == 0
        return pl.BlockSpec((w.shape[0] // n_steps, w.shape[1]),
                            lambda b, i: (b * (S // tm) + i, 0))

    out_specs = (
        rows(SGU_WIDTH), rows(SGU_WIDTH), cols(ATTN_WIDTH), rows(ATTN_WIDTH),
        pl.BlockSpec((1, N_HEADS, PAIR, tm), lambda b, i: (b, 0, 0, i)),
        rows(2 * D_MODEL), cols(LANES), rows(LANES),
        pl.BlockSpec((1, N_STATS, 2 * N_HEADS, LANES), lambda b, i: (b, 0, 0, 0)),
    ) + tuple(row_block(w) for w in later_weights)
    out_shape += tuple(jax.ShapeDtypeStruct(w.shape, _BF16) for w in later_weights)
    return pl.pallas_call(
        functools.partial(_inproj_kernel, n_cast=len(later_weights)),
        grid=(B, S // tm),
        in_specs=[
            rows(D), const((1, D)),
            const(wzt.shape), const(wqt.shape), const(wkt.shape), const(wvt.shape),
            const(wft.shape), const(wgt.shape),
            const((1, LANES)), const((1, SGU_WIDTH)), const((1, SGU_WIDTH)),
            const(tri.shape), const(expand.shape), const(expand_t.shape),
            const(head_rows.shape), const(piece_rows.shape),
        ] + [row_block(w) for w in later_weights],
        out_specs=out_specs,
        out_shape=out_shape,
        scratch_shapes=[pltpu.VMEM((8, LANES), _F32)],
        compiler_params=pltpu.CompilerParams(
            dimension_semantics=("arbitrary", "arbitrary"),
            vmem_limit_bytes=VMEM_LIMIT),
        name="in_projection",
    )(x, g_pre.reshape(1, D), wzt, wqt, wkt, wvt, wft, wgt, bf,
      g_sgu.reshape(1, SGU_WIDTH), b_sgu.reshape(1, SGU_WIDTH), tri, expand, expand_t,
      head_rows, piece_rows, *later_weights)


def _attn_kernel(plan_ref, k_ref, ka_ref, vt_ref, qt_ref, qat_ref, o_ref,
                 qf_ref, st_a, st_b, st_c, m_ref, acc_ref):
    batch = pl.program_id(0)
    pair = pl.program_id(1)
    S = k_ref.shape[1]
    row = lax.broadcasted_iota(jnp.int32, (PAIR, 1), 0)
    all_tiles = [(e, t) for t in range(NQ) for e in range(2)]

    def key_tile(k0):
        return jnp.concatenate([k_ref[0, pl.ds(k0, TK), :], ka_ref[0, pl.ds(k0, TK), :]], axis=1)

    def offsets(g, j):
        n, skew = plan_ref[batch, pair, g, 0], plan_ref[batch, pair, g, 1]
        first = (g * NQ - n - skew + j) * TK
        return [pl.multiple_of(first + t * skew * TK, TK) for t in range(NQ)]

    def update(e, t, st, k0, masked):
        cs = slice(t * TQ, (t + 1) * TQ)
        if masked:
            key_i = lax.broadcasted_iota(jnp.int32, (TK, TQ), 0)
            qry_i = lax.broadcasted_iota(jnp.int32, (TK, TQ), 1)
            st = jnp.where(key_i <= qry_i, st, -jnp.inf)
        m = m_ref[e, :, cs]
        m_new = jnp.maximum(m, jnp.max(st, axis=0, keepdims=True))
        p = jnp.exp2(st - m_new)
        alpha = jnp.exp2(m - m_new)
        m_ref[e, :, cs] = m_new
        acc_ref[e, :, cs] = alpha * acc_ref[e, :, cs] + _dot(
            vt_ref[0, e, :, pl.ds(k0, TK)], p.astype(_BF16))

    def step(cur, cur_tiles, cur_offs, nxt, nxt_tiles, nxt_offs):
        for n in range(max(len(cur_tiles), len(nxt_tiles))):
            if n < len(nxt_tiles):
                e, t = nxt_tiles[n]
                nxt[e, t] = _dot(key_tile(nxt_offs[t]), qf_ref[e, :, t * TQ:(t + 1) * TQ])
            if n < len(cur_tiles):
                e, t, masked = cur_tiles[n]
                update(e, t, cur[e, t], cur_offs[t], masked)

    def load_queries(g):
        q0 = pl.multiple_of(g * QG, QG)
        qt = qt_ref[0, :, pl.ds(q0, QG)].astype(_F32)
        qat = qat_ref[0, :, pl.ds(q0, QG)].astype(_F32)
        for e in range(2):
            a0 = N_SPLIT * (2 * pair + e)
            q_mask = jnp.logical_and(row >= HEAD_DIM * e, row < HEAD_DIM * (e + 1))
            a_mask = jnp.logical_or(
                jnp.logical_and(row >= a0, row < a0 + N_SPLIT),
                jnp.logical_and(row >= BIAS_COLS + a0, row < BIAS_COLS + a0 + N_SPLIT))
            qf_ref[e, :PAIR, :] = jnp.where(q_mask, qt, 0.0).astype(_BF16)
            qf_ref[e, PAIR:, :] = jnp.where(a_mask, qat, 0.0).astype(_BF16)

    def reset_state():
        m_ref[...] = jnp.full(m_ref.shape, -jnp.inf, _F32)
        acc_ref[...] = jnp.zeros(acc_ref.shape, _F32)

    full = [(e, t, False) for e, t in all_tiles]
    bufs = (st_a, st_b)
    n_groups = S // QG

    def finish(g):
        out = [acc_ref[e, :HEAD_DIM, :] * (1.0 / acc_ref[e, HEAD_DIM:HEAD_DIM + 1, :])
               for e in range(2)]
        o_ref[0, pl.ds(pl.multiple_of(g * QG, QG), QG), :] = (
            jnp.concatenate(out, axis=0).T.astype(_BF16))
        reset_state()

    def next_group(g):
        return jnp.minimum(g + 1, n_groups - 1)

    def triangle(g):
        q0 = pl.multiple_of(g * QG, QG)
        for i in range(NQ):
            cur_tiles = [(e, t, t == i) for t in range(i, NQ) for e in range(2)]
            cur_offs = [q0 + i * TK] * NQ
            if i + 1 < NQ:
                nxt_tiles = [(e, t) for t in range(i + 1, NQ) for e in range(2)]
                nxt_offs = [q0 + (i + 1) * TK] * NQ
            else:
                load_queries(next_group(g))
                nxt_tiles, nxt_offs = all_tiles, offsets(next_group(g), 0)
            step(bufs[i % 2], cur_tiles, cur_offs, bufs[(i + 1) % 2], nxt_tiles, nxt_offs)

    load_queries(0)
    reset_state()
    step(None, [], None, st_a, all_tiles, [0] * NQ)
    triangle(0)

    def q_group(g, _):
        n = plan_ref[batch, pair, g, 0]
        skew = plan_ref[batch, pair, g, 1]

        def full_steps(i, _):
            for u in range(KU):
                j = i * KU + u
                step(bufs[u % 2], full, offsets(g, j), bufs[(u + 1) % 2], all_tiles,
                     offsets(g, j + 1))
            return 0

        def skew_tail(j):
            step(st_a, full, offsets(g, j), st_b, all_tiles, offsets(g, j + 1))
            load_queries(next_group(g))
            step(st_b, [(e, t, True) for e, t in all_tiles], offsets(g, j + 1),
                 st_a, all_tiles, offsets(next_group(g), 0))

        unrolled = jnp.logical_and(skew == 1, functools.reduce(
            jnp.logical_or, [n == n_static for n_static in UNROLLED_SKEW_STEPS]))
        for n_static in UNROLLED_SKEW_STEPS:
            @pl.when(jnp.logical_and(skew == 1, n == n_static))
            def _(n_static=n_static):
                last = n_static + 1
                seq = [bufs[j % 2] for j in range(last)] + [st_c if last % 2 == 0 else st_b,
                                                            st_a]
                finish(g - 1)
                for j in range(last):
                    step(seq[j], full, offsets(g, j), seq[j + 1], all_tiles, offsets(g, j + 1))
                load_queries(next_group(g))
                step(seq[last], [(e, t, True) for e, t in all_tiles], offsets(g, last),
                     st_a, all_tiles, offsets(next_group(g), 0))

        @pl.when(jnp.logical_not(unrolled))
        def _():
            finish(g - 1)
            lax.fori_loop(0, n // KU, full_steps, 0)

        @pl.when(jnp.logical_and(skew == 1, jnp.logical_not(unrolled)))
        def _():
            skew_tail(n)

        @pl.when(skew == 0)
        def _():
            triangle(g)

        return 0

    lax.fori_loop(1, n_groups, q_group, 0)
    finish(n_groups - 1)


def _sweep_plan(stats, n_tiles):
    cum_first, cum_last, qn2, kn2 = (stats[:, s, :N_HEADS, :n_tiles] for s in range(N_STATS))
    B = stats.shape[0]
    n_groups = n_tiles // NQ
    k_norm = jnp.sqrt(jnp.max(kn2, axis=-1, keepdims=True))
    reach = NORM_SLACK * jnp.sqrt(qn2) * k_norm + cum_first
    bound = reach[..., None] - cum_last[:, :, None, :]
    tile = jnp.arange(n_tiles)
    needed = jnp.logical_and(tile[None, :] < tile[:, None], bound >= -ZERO_EXP2)
    w = jnp.sum(needed, axis=-1).astype(jnp.int32)
    w = jnp.max(w.reshape(B, N_PAIRS, 2, n_groups, NQ), axis=2)
    start = jnp.arange(n_groups, dtype=jnp.int32) * NQ
    n_flat = jnp.max(jnp.maximum(w - jnp.arange(NQ, dtype=jnp.int32), 0), axis=-1)
    n_flat = jnp.minimum(((n_flat + KU - 1) // KU) * KU, start)
    n_skew = jnp.maximum(jnp.max(w, axis=-1), 1)
    exact = functools.reduce(jnp.logical_or, [n_skew == s + 1 for s in UNROLLED_SKEW_STEPS])
    n_skew = jnp.where(exact, n_skew, n_skew + (n_skew + 1) % 2)
    units_flat = NQ * n_flat + NQ * (NQ + 1) // 2
    units_skew = NQ * n_skew + NQ
    skew = jnp.logical_and(n_skew < start, units_skew < units_flat)
    n = jnp.where(skew, n_skew - 1, n_flat)
    return jnp.stack([n, skew.astype(jnp.int32)], axis=-1)


def _attention(k, ka, vt, qt, qat, stats):
    B, S, _ = k.shape
    assert TQ == TK and KU == 2 and NQ % KU == 0 and S % QG == 0
    assert S // TK <= LANES
    plan = _sweep_plan(stats, S // TK)
    pair_rows = pl.BlockSpec((1, S, PAIR), lambda b, j, nb: (b, 0, j))
    return pl.pallas_call(
        _attn_kernel,
        grid_spec=pltpu.PrefetchScalarGridSpec(
            num_scalar_prefetch=1,
            grid=(B, N_PAIRS),
            in_specs=[pair_rows, pl.BlockSpec((1, S, LANES), lambda b, j, nb: (b, 0, 0)),
                      pl.BlockSpec((1, 2, PAIR, S), lambda b, j, nb: (b, j, 0, 0)),
                      pl.BlockSpec((1, PAIR, S), lambda b, j, nb: (b, j, 0)),
                      pl.BlockSpec((1, LANES, S), lambda b, j, nb: (b, 0, 0))],
            out_specs=pair_rows,
            scratch_shapes=[pltpu.VMEM((2, PAIR + LANES, QG), _BF16),
                            pltpu.VMEM((2, NQ, TK, TQ), _F32), pltpu.VMEM((2, NQ, TK, TQ), _F32),
                            pltpu.VMEM((2, NQ, TK, TQ), _F32),
                            pltpu.VMEM((2, 1, QG), _F32), pltpu.VMEM((2, PAIR, QG), _F32)]),
        out_shape=jax.ShapeDtypeStruct((B, S, ATTN_WIDTH), _BF16),
        compiler_params=pltpu.CompilerParams(
            dimension_semantics=("parallel", "parallel"),
            vmem_limit_bytes=VMEM_LIMIT),
        name="forgetting_attention",
    )(plan, k, ka, vt, qt, qat)


def _rms_scale(v, g):
    ms = jnp.mean(v * v, axis=-1, keepdims=True)
    return (v * lax.rsqrt(ms + EPS)) * g


def _merge_mlp_kernel(x_ref, u_ref, vn_ref, ya_ref, gates_ref, wsp_ref, bsp_ref,
                      wbs_ref, wba_ref, wout_ref, gpost_ref,
                      gpre2_ref, wup_ref, wdown_ref, gpost2_ref, o_ref, ysgu_ref):
    tm = x_ref.shape[0]
    row = lax.broadcasted_iota(jnp.int32, (CHUNK, CHUNK), 0)
    col = lax.broadcasted_iota(jnp.int32, (CHUNK, CHUNK), 1)
    lane = lax.broadcasted_iota(jnp.int32, (CHUNK, PAIR), 1)
    ws = [jnp.where(row >= col, wsp_ref[g], 0.0).astype(_BF16) for g in range(N_GROUPS)]
    for c in range(tm // CHUNK):
        r = slice(c * CHUNK, (c + 1) * CHUNK)
        for j in range(N_GROUPS // 2):
            cs = slice(j * PAIR, (j + 1) * PAIR)
            vp = vn_ref[r, cs]
            s = jnp.where(lane < HEAD_DIM, _dot(ws[2 * j], vp), _dot(ws[2 * j + 1], vp))
            s = s + bsp_ref[:, cs]
            ysgu_ref[r, cs] = (u_ref[r, cs].astype(_F32) * s).astype(_BF16)

    blocks = [slice(n * tm // N_STREAMS, (n + 1) * tm // N_STREAMS) for n in range(N_STREAMS)]

    def mix(r):
        a = _dot(ysgu_ref[r, :], wbs_ref[...])
        b = _dot(ya_ref[r, :], wba_ref[...])
        gate_a = jax.nn.sigmoid(gates_ref[r, :D_MODEL].astype(_F32))
        gate_b = jax.nn.sigmoid(gates_ref[r, D_MODEL:].astype(_F32))
        merged = (gate_a * a + gate_b * b).astype(_BF16)
        return _dot(merged, wout_ref[...])

    def mlp(xb):
        acc = jnp.zeros(xb.shape, _F32)
        for c in range(D_FF // FF_TILE):
            cs = slice(c * FF_TILE, (c + 1) * FF_TILE)
            hid = jnp.square(jnp.maximum(_dot(xb, wup_ref[:, cs]), 0.0)).astype(_BF16)
            acc = acc + _dot(hid, wdown_ref[cs, :])
        return acc

    mixed = [mix(r) for r in blocks]
    h1 = [x_ref[r, :] + _rms_scale(o, gpost_ref[...]) for r, o in zip(blocks, mixed)]
    xb = [_rms_scale(h, gpre2_ref[...]).astype(_BF16) for h in h1]
    ff = [mlp(v) for v in xb]
    for r, h, f in zip(blocks, h1, ff):
        o_ref[r, :] = h + _rms_scale(f, gpost2_ref[...])


def _merge_mlp(x2, u2, vn2, ya2, gates2, w_spatial, b_spatial, w_bs, w_ba, w_out, g_post,
               g_pre2, w_up, w_down, g_post2):
    R, D = x2.shape
    tm = ROW_TILE
    bsp = jnp.repeat(b_spatial.T, SGU_WIDTH // N_GROUPS, axis=1)

    def const(shape):
        return pl.BlockSpec(shape, lambda i: (0,) * len(shape), pipeline_mode=pl.Buffered(1))

    def rows(width):
        return pl.BlockSpec((tm, width), lambda i: (i, 0))

    return pl.pallas_call(
        _merge_mlp_kernel,
        grid=(R // tm,),
        in_specs=[rows(D), rows(SGU_WIDTH), rows(SGU_WIDTH), rows(ATTN_WIDTH), rows(2 * D),
                  const(w_spatial.shape), const(bsp.shape),
                  const(w_bs.shape), const(w_ba.shape), const(w_out.shape), const((1, D)),
                  const((1, D)), const(w_up.shape), const(w_down.shape), const((1, D))],
        out_specs=rows(D),
        out_shape=jax.ShapeDtypeStruct((R, D), _F32),
        scratch_shapes=[pltpu.VMEM((tm, SGU_WIDTH), _BF16)],
        compiler_params=pltpu.CompilerParams(
            dimension_semantics=("parallel",), vmem_limit_bytes=VMEM_LIMIT),
        name="merge_mlp",
    )(x2, u2, vn2, ya2, gates2, w_spatial, bsp,
      w_bs, w_ba, w_out, g_post.reshape(1, D),
      g_pre2.reshape(1, D), w_up, w_down, g_post2.reshape(1, D))


def kernel(x, g_mix_pre, w_in, b_forget, g_sgu, b_sgu, w_spatial, b_spatial, w_branch_sgu,
           w_branch_attn, w_out, g_mix_post, g_ffn_pre, w_up, w_down, g_ffn_post):
    B, S, D = x.shape
    h = x
    for l in range(g_mix_pre.shape[0]):
        head_order = jnp.argsort(b_forget[l])
        later = (w_branch_sgu[l], _heads_in_order(w_branch_attn[l], head_order), w_out[l],
                 w_up[l], w_down[l])
        u, vn, qt, k, vt, gates, qat, ka, stats, w_bs, w_ba, w_o, w_u, w_d = _in_projection(
            h, g_mix_pre[l], w_in[l], b_forget[l], g_sgu[l], b_sgu[l], later, head_order)
        y_attn = _attention(k, ka, vt, qt, qat, stats)
        h = _merge_mlp(h.reshape(B * S, D), u.reshape(B * S, -1), vn.reshape(B * S, -1),
                       y_attn.reshape(B * S, -1), gates.reshape(B * S, -1),
                       w_spatial[l], b_spatial[l], w_bs, w_ba, w_o,
                       g_mix_post[l], g_ffn_pre[l], w_u, w_d,
                       g_ffn_post[l]).reshape(B, S, D)
    return h
```

```python
import functools

import jax
import jax.numpy as jnp
import numpy as np
from jax import lax
from jax.experimental import pallas as pl
from jax.experimental.pallas import tpu as pltpu

D_MODEL = 1024
N_HEADS = 8
HEAD_DIM = 64
ATTN_WIDTH = N_HEADS * HEAD_DIM
N_GROUPS = 8
SGU_WIDTH = D_MODEL // 2
CHUNK = 128
D_FF = 4 * D_MODEL
EPS = 1e-6
LOG2E = 1.4426950408889634

LANES = 128
N_SPLIT = 3
BIAS_COLS = N_HEADS * N_SPLIT
PAIR = 2 * HEAD_DIM
N_PAIRS = N_HEADS // 2

ROW_TILE = 512
FF_TILE = 1024
N_STREAMS = 2
TQ = 256
TK = 256
NQ = 4
QG = NQ * TQ
KU = 2
UNROLLED_SKEW_STEPS = (1, 2, 3, 4, 5, 6)
N_STATS = 4
ZERO_EXP2 = 136.0
NORM_SLACK = 2.05
V7X_VMEM_BYTES = 64 * 1024 * 1024
VMEM_LIMIT = V7X_VMEM_BYTES * 7 // 8
BF16_SUBLANES = 16

_BF16 = jnp.bfloat16
_F32 = jnp.float32


def _split_bf16(x):
    parts = []
    r = x
    for _ in range(N_SPLIT):
        p = r.astype(_BF16)
        parts.append(p)
        r = r - p.astype(_F32)
    return parts


def _dot(a, b):
    return jnp.dot(a, b, preferred_element_type=_F32)


def _dot_nt(a, b):
    return lax.dot_general(a, b, (((1,), (1,)), ((), ())), preferred_element_type=_F32)


def _inproj_kernel(x_ref, gpre_ref, wzt_ref, wqt_ref, wkt_ref, wvt_ref, wft_ref, wgt_ref,
                   bf_ref, gsgu_ref, bsgu_ref, tri_ref, expand_ref, expand_t_ref,
                   head_rows_ref, piece_rows_ref, *rest, n_cast):
    cast_in = rest[:n_cast]
    (u_ref, vn_ref, qt_ref, k_ref, vt_ref, gates_ref, qat_ref, ka_ref,
     stats_ref) = rest[n_cast:n_cast + 9]
    cast_out = rest[n_cast + 9:2 * n_cast + 9]
    carry_ref, = rest[2 * n_cast + 9:]
    for src, dst in zip(cast_in, cast_out):
        dst[...] = src[...].astype(_BF16)

    @pl.when(pl.program_id(1) == 0)
    def _():
        carry_ref[...] = jnp.zeros_like(carry_ref)
        stats_ref[...] = jnp.zeros_like(stats_ref)

    x = x_ref[0]
    tm = x.shape[0]
    ms = jnp.mean(x * x, axis=-1, keepdims=True)
    xb = ((x * lax.rsqrt(ms + EPS)) * gpre_ref[...]).astype(_BF16)

    lane = lax.broadcasted_iota(jnp.int32, (tm, LANES), 1)

    def pack_pieces(v):
        hi, mid, lo = _split_bf16(v)
        zero = jnp.zeros((), _BF16)
        return jnp.where(lane < N_HEADS, hi,
                         jnp.where(lane < 2 * N_HEADS, mid,
                                   jnp.where(lane < N_SPLIT * N_HEADS, lo, zero)))

    f = _dot_nt(xb, wft_ref[...]) + bf_ref[...]

    z = jax.nn.gelu(_dot_nt(xb, wzt_ref[...]), approximate=True)
    u_ref[0] = z[:, :SGU_WIDTH].astype(_BF16)
    v = z[:, SGU_WIDTH:]
    mu = jnp.mean(v, axis=-1, keepdims=True)
    vc = v - mu
    var = jnp.mean(vc * vc, axis=-1, keepdims=True)
    vn_ref[0] = ((vc * lax.rsqrt(var + EPS)) * gsgu_ref[...] + bsgu_ref[...]).astype(_BF16)

    log_f = jnp.minimum(f, 0.0) - jnp.log(1.0 + jnp.exp(-jnp.abs(f)))
    sums = _dot(tri_ref[...], pack_pieces(log_f))

    qt = _dot_nt(wqt_ref[...], xb) * (HEAD_DIM ** -0.5 * LOG2E)
    qt_ref[0] = qt.astype(_BF16)
    k = _dot_nt(xb, wkt_ref[...])
    k_ref[0] = k.astype(_BF16)
    qn2 = _dot(head_rows_ref[...], (qt * qt).astype(_BF16))
    kn2 = _dot_nt(head_rows_ref[...], (k * k).astype(_BF16))

    total = sums
    for shift in (N_HEADS, 2 * N_HEADS, LANES - N_HEADS, LANES - 2 * N_HEADS):
        total = total + pltpu.roll(sums, shift, 1)
    cum = carry_ref[0:1, :] + total
    carry_ref[0:1, :] = cum[tm - 1:tm, :]
    pieces = pack_pieces(cum * LOG2E)
    spread = _dot(pieces, expand_ref[...])
    spread_t = _dot_nt(expand_t_ref[...], pieces)

    cum_t = _dot_nt(piece_rows_ref[...], pieces)
    tile_lane = lax.broadcasted_iota(jnp.int32, (2 * N_HEADS, LANES), 1)
    first_tile = pl.program_id(1) * (tm // TK)
    stats = [stats_ref[0, s] for s in range(N_STATS)]
    for j in range(tm // TK):
        lo, hi = j * TK, (j + 1) * TK
        cols = (cum_t[:, lo:lo + 1], cum_t[:, hi - 1:hi],
                jnp.max(qn2[:, lo:hi], axis=1, keepdims=True),
                jnp.max(kn2[:, lo:hi], axis=1, keepdims=True))
        stats = [jnp.where(tile_lane == first_tile + j, c, s) for c, s in zip(cols, stats)]
    for s in range(N_STATS):
        stats_ref[0, s] = stats[s]

    vt = _dot_nt(wvt_ref[...], xb).astype(_BF16)
    for h in range(N_HEADS):
        vt_ref[0, h, :HEAD_DIM, :] = vt[h * HEAD_DIM:(h + 1) * HEAD_DIM, :]
        vt_ref[0, h, HEAD_DIM:, :] = jnp.ones((PAIR - HEAD_DIM, tm), _BF16)

    gates_ref[0] = _dot_nt(xb, wgt_ref[...]).astype(_BF16)

    in_a = lane < BIAS_COLS
    in_b = jnp.logical_and(lane >= BIAS_COLS, lane < 2 * BIAS_COLS)
    ka_ref[0] = jnp.where(in_a, -spread, jnp.where(in_b, 1.0, 0.0)).astype(_BF16)
    row = lax.broadcasted_iota(jnp.int32, (LANES, tm), 0)
    in_a = row < BIAS_COLS
    in_b = jnp.logical_and(row >= BIAS_COLS, row < 2 * BIAS_COLS)
    qat_ref[0] = jnp.where(in_a, 1.0, jnp.where(in_b, spread_t, 0.0)).astype(_BF16)


def _head_rows():
    r = np.zeros((2 * N_HEADS, ATTN_WIDTH), np.float32)
    for h in range(N_HEADS):
        r[h, h * HEAD_DIM:(h + 1) * HEAD_DIM] = 1.0
    return r


def _piece_rows():
    r = np.zeros((2 * N_HEADS, LANES), np.float32)
    for h in range(N_HEADS):
        for i in range(N_SPLIT):
            r[h, N_HEADS * i + h] = 1.0
    return r


def _expand_matrix():
    e = np.zeros((LANES, LANES), np.float32)
    for h in range(N_HEADS):
        for i in range(N_SPLIT):
            e[N_HEADS * i + h, N_SPLIT * h + i] = 1.0
            e[N_HEADS * i + h, BIAS_COLS + N_SPLIT * h + i] = 1.0
    return e


_IN_OFFSETS = tuple(int(v) for v in np.cumsum(
    (0, 2 * SGU_WIDTH, ATTN_WIDTH, ATTN_WIDTH, ATTN_WIDTH, N_HEADS, 2 * D_MODEL)))


def _heads_in_order(w, head_order):
    return w.reshape(N_HEADS, HEAD_DIM, -1)[head_order].reshape(w.shape)


def _in_projection(x, g_pre, w_in, b_forget, g_sgu, b_sgu, later_weights, head_order):
    B, S, D = x.shape
    tm = ROW_TILE
    n_steps = B * (S // tm)
    o = _IN_OFFSETS
    wt = jnp.swapaxes(w_in, 0, 1)
    wzt, wqt, wkt, wvt, wgt = (wt[o[i]:o[i + 1]].astype(_BF16) for i in (0, 1, 2, 3, 5))
    wqt, wkt, wvt = (_heads_in_order(w, head_order) for w in (wqt, wkt, wvt))
    row_pad = ((0, LANES - N_SPLIT * N_HEADS), (0, 0))
    wft = jnp.pad(jnp.tile(wt[o[4]:o[5]][head_order], (N_SPLIT, 1)), row_pad).astype(_BF16)
    lane_pad = ((0, 0), (0, LANES - N_SPLIT * N_HEADS))
    bf = jnp.pad(jnp.tile(b_forget[head_order].reshape(1, N_HEADS), (1, N_SPLIT)), lane_pad)
    tri = jnp.asarray(np.tril(np.ones((tm, tm), np.float32)), _BF16)
    expand = jnp.asarray(_expand_matrix(), _BF16)
    expand_t = jnp.asarray(_expand_matrix().T, _BF16)
    head_rows = jnp.asarray(_head_rows(), _BF16)
    piece_rows = jnp.asarray(_piece_rows(), _BF16)

    def const(shape):
        return pl.BlockSpec(shape, lambda b, i: (0,) * len(shape))

    def rows(width):
        return pl.BlockSpec((1, tm, width), lambda b, i: (b, i, 0))

    out_shape = (
        jax.ShapeDtypeStruct((B, S, SGU_WIDTH), _BF16),
        jax.ShapeDtypeStruct((B, S, SGU_WIDTH), _BF16),
        jax.ShapeDtypeStruct((B, ATTN_WIDTH, S), _BF16),
        jax.ShapeDtypeStruct((B, S, ATTN_WIDTH), _BF16),
        jax.ShapeDtypeStruct((B, N_HEADS, PAIR, S), _BF16),
        jax.ShapeDtypeStruct((B, S, 2 * D_MODEL), _BF16),
        jax.ShapeDtypeStruct((B, LANES, S), _BF16),
        jax.ShapeDtypeStruct((B, S, LANES), _BF16),
        jax.ShapeDtypeStruct((B, N_STATS, 2 * N_HEADS, LANES), _F32),
    )

    def cols(height):
        return pl.BlockSpec((1, height, tm), lambda b, i: (b, 0, i))

    def row_block(w):
        assert w.shape[0] % (n_steps * BF16_SUBLANES) == 0
        return pl.BlockSpec((w.shape[0] // n_steps, w.shape[1]),
                            lambda b, i: (b * (S // tm) + i, 0))

    out_specs = (
        rows(SGU_WIDTH), rows(SGU_WIDTH), cols(ATTN_WIDTH), rows(ATTN_WIDTH),
        pl.BlockSpec((1, N_HEADS, PAIR, tm), lambda b, i: (b, 0, 0, i)),
        rows(2 * D_MODEL), cols(LANES), rows(LANES),
        pl.BlockSpec((1, N_STATS, 2 * N_HEADS, LANES), lambda b, i: (b, 0, 0, 0)),
    ) + tuple(row_block(w) for w in later_weights)
    out_shape += tuple(jax.ShapeDtypeStruct(w.shape, _BF16) for w in later_weights)
    return pl.pallas_call(
        functools.partial(_inproj_kernel, n_cast=len(later_weights)),
        grid=(B, S // tm),
        in_specs=[
            rows(D), const((1, D)),
            const(wzt.shape), const(wqt.shape), const(wkt.shape), const(wvt.shape),
            const(wft.shape), const(wgt.shape),
            const((1, LANES)), const((1, SGU_WIDTH)), const((1, SGU_WIDTH)),
            const(tri.shape), const(expand.shape), const(expand_t.shape),
            const(head_rows.shape), const(piece_rows.shape),
        ] + [row_block(w) for w in later_weights],
        out_specs=out_specs,
        out_shape=out_shape,
        scratch_shapes=[pltpu.VMEM((8, LANES), _F32)],
        compiler_params=pltpu.CompilerParams(
            dimension_semantics=("arbitrary", "arbitrary"),
            vmem_limit_bytes=VMEM_LIMIT),
        name="in_projection",
    )(x, g_pre.reshape(1, D), wzt, wqt, wkt, wvt, wft, wgt, bf,
      g_sgu.reshape(1, SGU_WIDTH), b_sgu.reshape(1, SGU_WIDTH), tri, expand, expand_t,
      head_rows, piece_rows, *later_weights)


def _attn_kernel(plan_ref, k_ref, ka_ref, vt_ref, qt_ref, qat_ref, o_ref,
                 qf_ref, st_a, st_b, st_c, m_ref, acc_ref):
    batch = pl.program_id(0)
    pair = pl.program_id(1)
    S = k_ref.shape[1]
    row = lax.broadcasted_iota(jnp.int32, (PAIR, 1), 0)
    all_tiles = [(e, t) for t in range(NQ) for e in range(2)]

    def key_tile(k0):
        return jnp.concatenate([k_ref[0, pl.ds(k0, TK), :], ka_ref[0, pl.ds(k0, TK), :]], axis=1)

    def offsets(g, j):
        n, skew = plan_ref[batch, pair, g, 0], plan_ref[batch, pair, g, 1]
        first = (g * NQ - n - skew + j) * TK
        return [pl.multiple_of(first + t * skew * TK, TK) for t in range(NQ)]

    def update(e, t, st, k0, masked):
        cs = slice(t * TQ, (t + 1) * TQ)
        if masked:
            key_i = lax.broadcasted_iota(jnp.int32, (TK, TQ), 0)
            qry_i = lax.broadcasted_iota(jnp.int32, (TK, TQ), 1)
            st = jnp.where(key_i <= qry_i, st, -jnp.inf)
        m = m_ref[e, :, cs]
        m_new = jnp.maximum(m, jnp.max(st, axis=0, keepdims=True))
        p = jnp.exp2(st - m_new)
        alpha = jnp.exp2(m - m_new)
        m_ref[e, :, cs] = m_new
        acc_ref[e, :, cs] = alpha * acc_ref[e, :, cs] + _dot(
            vt_ref[0, e, :, pl.ds(k0, TK)], p.astype(_BF16))

    def step(cur, cur_tiles, cur_offs, nxt, nxt_tiles, nxt_offs):
        for n in range(max(len(cur_tiles), len(nxt_tiles))):
            if n < len(nxt_tiles):
                e, t = nxt_tiles[n]
                nxt[e, t] = _dot(key_tile(nxt_offs[t]), qf_ref[e, :, t * TQ:(t + 1) * TQ])
            if n < len(cur_tiles):
                e, t, masked = cur_tiles[n]
                update(e, t, cur[e, t], cur_offs[t], masked)

    def load_queries(g):
        q0 = pl.multiple_of(g * QG, QG)
        qt = qt_ref[0, :, pl.ds(q0, QG)].astype(_F32)
        qat = qat_ref[0, :, pl.ds(q0, QG)].astype(_F32)
        for e in range(2):
            a0 = N_SPLIT * (2 * pair + e)
            q_mask = jnp.logical_and(row >= HEAD_DIM * e, row < HEAD_DIM * (e + 1))
            a_mask = jnp.logical_or(
                jnp.logical_and(row >= a0, row < a0 + N_SPLIT),
                jnp.logical_and(row >= BIAS_COLS + a0, row < BIAS_COLS + a0 + N_SPLIT))
            qf_ref[e, :PAIR, :] = jnp.where(q_mask, qt, 0.0).astype(_BF16)
            qf_ref[e, PAIR:, :] = jnp.where(a_mask, qat, 0.0).astype(_BF16)

    def reset_state():
        m_ref[...] = jnp.full(m_ref.shape, -jnp.inf, _F32)
        acc_ref[...] = jnp.zeros(acc_ref.shape, _F32)

    full = [(e, t, False) for e, t in all_tiles]
    bufs = (st_a, st_b)
    n_groups = S // QG

    def finish(g):
        out = [acc_ref[e, :HEAD_DIM, :] * (1.0 / acc_ref[e, HEAD_DIM:HEAD_DIM + 1, :])
               for e in range(2)]
        o_ref[0, pl.ds(pl.multiple_of(g * QG, QG), QG), :] = (
            jnp.concatenate(out, axis=0).T.astype(_BF16))
        reset_state()

    def next_group(g):
        return jnp.minimum(g + 1, n_groups - 1)

    def triangle(g):
        q0 = pl.multiple_of(g * QG, QG)
        for i in range(NQ):
            cur_tiles = [(e, t, t == i) for t in range(i, NQ) for e in range(2)]
            cur_offs = [q0 + i * TK] * NQ
            if i + 1 < NQ:
                nxt_tiles = [(e, t) for t in range(i + 1, NQ) for e in range(2)]
                nxt_offs = [q0 + (i + 1) * TK] * NQ
            else:
                load_queries(next_group(g))
                nxt_tiles, nxt_offs = all_tiles, offsets(next_group(g), 0)
            step(bufs[i % 2], cur_tiles, cur_offs, bufs[(i + 1) % 2], nxt_tiles, nxt_offs)

    load_queries(0)
    reset_state()
    step(None, [], None, st_a, all_tiles, [0] * NQ)
    triangle(0)

    def q_group(g, _):
        n = plan_ref[batch, pair, g, 0]
        skew = plan_ref[batch, pair, g, 1]

        def full_steps(i, _):
            for u in range(KU):
                j = i * KU + u
                step(bufs[u % 2], full, offsets(g, j), bufs[(u + 1) % 2], all_tiles,
                     offsets(g, j + 1))
            return 0

        def skew_tail(j):
            step(st_a, full, offsets(g, j), st_b, all_tiles, offsets(g, j + 1))
            load_queries(next_group(g))
            step(st_b, [(e, t, True) for e, t in all_tiles], offsets(g, j + 1),
                 st_a, all_tiles, offsets(next_group(g), 0))

        unrolled = jnp.logical_and(skew == 1, functools.reduce(
            jnp.logical_or, [n == n_static for n_static in UNROLLED_SKEW_STEPS]))
        for n_static in UNROLLED_SKEW_STEPS:
            @pl.when(jnp.logical_and(skew == 1, n == n_static))
            def _(n_static=n_static):
                last = n_static + 1
                seq = [bufs[j % 2] for j in range(last)] + [st_c if last % 2 == 0 else st_b,
                                                            st_a]
                finish(g - 1)
                for j in range(last):
                    step(seq[j], full, offsets(g, j), seq[j + 1], all_tiles, offsets(g, j + 1))
                load_queries(next_group(g))
                step(seq[last], [(e, t, True) for e, t in all_tiles], offsets(g, last),
                     st_a, all_tiles, offsets(next_group(g), 0))

        @pl.when(jnp.logical_not(unrolled))
        def _():
            finish(g - 1)
            lax.fori_loop(0, n // KU, full_steps, 0)

        @pl.when(jnp.logical_and(skew == 1, jnp.logical_not(unrolled)))
        def _():
            skew_tail(n)

        @pl.when(skew == 0)
        def _():
            triangle(g)

        return 0

    lax.fori_loop(1, n_groups, q_group, 0)
    finish(n_groups - 1)


def _sweep_plan(stats, n_tiles):
    cum_first, cum_last, qn2, kn2 = (stats[:, s, :N_HEADS, :n_tiles] for s in range(N_STATS))
    B = stats.shape[0]
    n_groups = n_tiles // NQ
    k_norm = jnp.sqrt(jnp.max(kn2, axis=-1, keepdims=True))
    reach = NORM_SLACK * jnp.sqrt(qn2) * k_norm + cum_first
    bound = reach[..., None] - cum_last[:, :, None, :]
    tile = jnp.arange(n_tiles)
    needed = jnp.logical_and(tile[None, :] < tile[:, None], bound >= -ZERO_EXP2)
    w = jnp.sum(needed, axis=-1).astype(jnp.int32)
    w = jnp.max(w.reshape(B, N_PAIRS, 2, n_groups, NQ), axis=2)
    start = jnp.arange(n_groups, dtype=jnp.int32) * NQ
    n_flat = jnp.max(jnp.maximum(w - jnp.arange(NQ, dtype=jnp.int32), 0), axis=-1)
    n_flat = jnp.minimum(((n_flat + KU - 1) // KU) * KU, start)
    n_skew = jnp.maximum(jnp.max(w, axis=-1), 1)
    exact = functools.reduce(jnp.logical_or, [n_skew == s + 1 for s in UNROLLED_SKEW_STEPS])
    n_skew = jnp.where(exact, n_skew, n_skew + (n_skew + 1) % 2)
    units_flat = NQ * n_flat + NQ * (NQ + 1) // 2
    units_skew = NQ * n_skew + NQ
    skew = jnp.logical_and(n_skew < start, units_skew < units_flat)
    n = jnp.where(skew, n_skew - 1, n_flat)
    return jnp.stack([n, skew.astype(jnp.int32)], axis=-1)


def _attention(k, ka, vt, qt, qat, stats):
    B, S, _ = k.shape
    assert TQ == TK and KU == 2 and NQ % KU == 0 and S % QG == 0
    assert S // TK <= LANES
    plan = _sweep_plan(stats, S // TK)
    pair_rows = pl.BlockSpec((1, S, PAIR), lambda b, j, nb: (b, 0, j))
    return pl.pallas_call(
        _attn_kernel,
        grid_spec=pltpu.PrefetchScalarGridSpec(
            num_scalar_prefetch=1,
            grid=(B, N_PAIRS),
            in_specs=[pair_rows, pl.BlockSpec((1, S, LANES), lambda b, j, nb: (b, 0, 0)),
                      pl.BlockSpec((1, 2, PAIR, S), lambda b, j, nb: (b, j, 0, 0)),
                      pl.BlockSpec((1, PAIR, S), lambda b, j, nb: (b, j, 0)),
                      pl.BlockSpec((1, LANES, S), lambda b, j, nb: (b, 0, 0))],
            out_specs=pair_rows,
            scratch_shapes=[pltpu.VMEM((2, PAIR + LANES, QG), _BF16),
                            pltpu.VMEM((2, NQ, TK, TQ), _F32), pltpu.VMEM((2, NQ, TK, TQ), _F32),
                            pltpu.VMEM((2, NQ, TK, TQ), _F32),
                            pltpu.VMEM((2, 1, QG), _F32), pltpu.VMEM((2, PAIR, QG), _F32)]),
        out_shape=jax.ShapeDtypeStruct((B, S, ATTN_WIDTH), _BF16),
        compiler_params=pltpu.CompilerParams(
            dimension_semantics=("parallel", "parallel"),
            vmem_limit_bytes=VMEM_LIMIT),
        name="forgetting_attention",
    )(plan, k, ka, vt, qt, qat)


def _rms_scale(v, g):
    ms = jnp.mean(v * v, axis=-1, keepdims=True)
    return (v * lax.rsqrt(ms + EPS)) * g


def _merge_mlp_kernel(x_ref, u_ref, vn_ref, ya_ref, gates_ref, wsp_ref, bsp_ref,
                      wbs_ref, wba_ref, wout_ref, gpost_ref,
                      gpre2_ref, wup_ref, wdown_ref, gpost2_ref, o_ref, ysgu_ref):
    tm = x_ref.shape[0]
    row = lax.broadcasted_iota(jnp.int32, (CHUNK, CHUNK), 0)
    col = lax.broadcasted_iota(jnp.int32, (CHUNK, CHUNK), 1)
    lane = lax.broadcasted_iota(jnp.int32, (CHUNK, PAIR), 1)
    ws = [jnp.where(row >= col, wsp_ref[g], 0.0).astype(_BF16) for g in range(N_GROUPS)]
    for c in range(tm // CHUNK):
        r = slice(c * CHUNK, (c + 1) * CHUNK)
        for j in range(N_GROUPS // 2):
            cs = slice(j * PAIR, (j + 1) * PAIR)
            vp = vn_ref[r, cs]
            s = jnp.where(lane < HEAD_DIM, _dot(ws[2 * j], vp), _dot(ws[2 * j + 1], vp))
            s = s + bsp_ref[:, cs]
            ysgu_ref[r, cs] = (u_ref[r, cs].astype(_F32) * s).astype(_BF16)

    blocks = [slice(n * tm // N_STREAMS, (n + 1) * tm // N_STREAMS) for n in range(N_STREAMS)]

    def mix(r):
        a = _dot(ysgu_ref[r, :], wbs_ref[...])
        b = _dot(ya_ref[r, :], wba_ref[...])
        gate_a = jax.nn.sigmoid(gates_ref[r, :D_MODEL].astype(_F32))
        gate_b = jax.nn.sigmoid(gates_ref[r, D_MODEL:].astype(_F32))
        merged = (gate_a * a + gate_b * b).astype(_BF16)
        return _dot(merged, wout_ref[...])

    def mlp(xb):
        acc = jnp.zeros(xb.shape, _F32)
        for c in range(D_FF // FF_TILE):
            cs = slice(c * FF_TILE, (c + 1) * FF_TILE)
            hid = jnp.square(jnp.maximum(_dot(xb, wup_ref[:, cs]), 0.0)).astype(_BF16)
            acc = acc + _dot(hid, wdown_ref[cs, :])
        return acc

    mixed = [mix(r) for r in blocks]
    h1 = [x_ref[r, :] + _rms_scale(o, gpost_ref[...]) for r, o in zip(blocks, mixed)]
    xb = [_rms_scale(h, gpre2_ref[...]).astype(_BF16) for h in h1]
    ff = [mlp(v) for v in xb]
    for r, h, f in zip(blocks, h1, ff):
        o_ref[r, :] = h + _rms_scale(f, gpost2_ref[...])


def _merge_mlp(x2, u2, vn2, ya2, gates2, w_spatial, b_spatial, w_bs, w_ba, w_out, g_post,
               g_pre2, w_up, w_down, g_post2):
    R, D = x2.shape
    tm = ROW_TILE
    bsp = jnp.repeat(b_spatial.T, SGU_WIDTH // N_GROUPS, axis=1)

    def const(shape):
        return pl.BlockSpec(shape, lambda i: (0,) * len(shape), pipeline_mode=pl.Buffered(1))

    def rows(width):
        return pl.BlockSpec((tm, width), lambda i: (i, 0))

    return pl.pallas_call(
        _merge_mlp_kernel,
        grid=(R // tm,),
        in_specs=[rows(D), rows(SGU_WIDTH), rows(SGU_WIDTH), rows(ATTN_WIDTH), rows(2 * D),
                  const(w_spatial.shape), const(bsp.shape),
                  const(w_bs.shape), const(w_ba.shape), const(w_out.shape), const((1, D)),
                  const((1, D)), const(w_up.shape), const(w_down.shape), const((1, D))],
        out_specs=rows(D),
        out_shape=jax.ShapeDtypeStruct((R, D), _F32),
        scratch_shapes=[pltpu.VMEM((tm, SGU_WIDTH), _BF16)],
        compiler_params=pltpu.CompilerParams(
            dimension_semantics=("parallel",), vmem_limit_bytes=VMEM_LIMIT),
        name="merge_mlp",
    )(x2, u2, vn2, ya2, gates2, w_spatial, bsp,
      w_bs, w_ba, w_out, g_post.reshape(1, D),
      g_pre2.reshape(1, D), w_up, w_down, g_post2.reshape(1, D))


def kernel(x, g_mix_pre, w_in, b_forget, g_sgu, b_sgu, w_spatial, b_spatial, w_branch_sgu,
           w_branch_attn, w_out, g_mix_post, g_ffn_pre, w_up, w_down, g_ffn_post):
    B, S, D = x.shape
    h = x
    for l in range(g_mix_pre.shape[0]):
        head_order = jnp.argsort(b_forget[l])
        later = (w_branch_sgu[l], _heads_in_order(w_branch_attn[l], head_order), w_out[l],
                 w_up[l], w_down[l])
        u, vn, qt, k, vt, gates, qat, ka, stats, w_bs, w_ba, w_o, w_u, w_d = _in_projection(
            h, g_mix_pre[l], w_in[l], b_forget[l], g_sgu[l], b_sgu[l], later, head_order)
        y_attn = _attention(k, ka, vt, qt, qat, stats)
        h = _merge_mlp(h.reshape(B * S, D), u.reshape(B * S, -1), vn.reshape(B * S, -1),
                       y_attn.reshape(B * S, -1), gates.reshape(B * S, -1),
                       w_spatial[l], b_spatial[l], w_bs, w_ba, w_o,
                       g_mix_post[l], g_ffn_pre[l], w_u, w_d,
                       g_ffn_post[l]).reshape(B, S, D)
    return h
```

```python
import functools

import jax
import jax.numpy as jnp
import numpy as np
from jax import lax
from jax.experimental import pallas as pl
from jax.experimental.pallas import tpu as pltpu

D_MODEL = 1024
N_HEADS = 8
HEAD_DIM = 64
ATTN_WIDTH = N_HEADS * HEAD_DIM
N_GROUPS = 8
SGU_WIDTH = D_MODEL // 2
CHUNK = 128
D_FF = 4 * D_MODEL
EPS = 1e-6
LOG2E = 1.4426950408889634

LANES = 128
N_SPLIT = 3
BIAS_COLS = N_HEADS * N_SPLIT
PAIR = 2 * HEAD_DIM
N_PAIRS = N_HEADS // 2

ROW_TILE = 512
FF_TILE = 1024
N_STREAMS = 2
TQ = 256
TK = 256
NQ = 4
QG = NQ * TQ
KU = 2
UNROLLED_SKEW_STEPS = (1, 2, 3, 4, 5, 6)
N_STATS = 4
ZERO_EXP2 = 136.0
NORM_SLACK = 2.05
V7X_VMEM_BYTES = 64 * 1024 * 1024
VMEM_LIMIT = V7X_VMEM_BYTES * 7 // 8
BF16_SUBLANES = 16

_BF16 = jnp.bfloat16
_F32 = jnp.float32


def _split_bf16(x):
    parts = []
    r = x
    for _ in range(N_SPLIT):
        p = r.astype(_BF16)
        parts.append(p)
        r = r - p.astype(_F32)
    return parts


def _dot(a, b):
    return jnp.dot(a, b, preferred_element_type=_F32)


def _dot_nt(a, b):
    return lax.dot_general(a, b, (((1,), (1,)), ((), ())), preferred_element_type=_F32)


def _inproj_kernel(x_ref, gpre_ref, wzt_ref, wqt_ref, wkt_ref, wvt_ref, wft_ref, wgt_ref,
                   bf_ref, gsgu_ref, bsgu_ref, tri_ref, expand_ref, expand_t_ref,
                   head_rows_ref, piece_rows_ref, *rest, n_cast):
    cast_in = rest[:n_cast]
    (u_ref, vn_ref, qt_ref, k_ref, vt_ref, gates_ref, qat_ref, ka_ref,
     stats_ref) = rest[n_cast:n_cast + 9]
    cast_out = rest[n_cast + 9:2 * n_cast + 9]
    carry_ref, = rest[2 * n_cast + 9:]
    for src, dst in zip(cast_in, cast_out):
        dst[...] = src[...].astype(_BF16)

    @pl.when(pl.program_id(1) == 0)
    def _():
        carry_ref[...] = jnp.zeros_like(carry_ref)
        stats_ref[...] = jnp.zeros_like(stats_ref)

    x = x_ref[0]
    tm = x.shape[0]
    ms = jnp.mean(x * x, axis=-1, keepdims=True)
    xb = ((x * lax.rsqrt(ms + EPS)) * gpre_ref[...]).astype(_BF16)

    lane = lax.broadcasted_iota(jnp.int32, (tm, LANES), 1)

    def pack_pieces(v):
        hi, mid, lo = _split_bf16(v)
        zero = jnp.zeros((), _BF16)
        return jnp.where(lane < N_HEADS, hi,
                         jnp.where(lane < 2 * N_HEADS, mid,
                                   jnp.where(lane < N_SPLIT * N_HEADS, lo, zero)))

    f = _dot_nt(xb, wft_ref[...]) + bf_ref[...]

    z = jax.nn.gelu(_dot_nt(xb, wzt_ref[...]), approximate=True)
    u_ref[0] = z[:, :SGU_WIDTH].astype(_BF16)
    v = z[:, SGU_WIDTH:]
    mu = jnp.mean(v, axis=-1, keepdims=True)
    vc = v - mu
    var = jnp.mean(vc * vc, axis=-1, keepdims=True)
    vn_ref[0] = ((vc * lax.rsqrt(var + EPS)) * gsgu_ref[...] + bsgu_ref[...]).astype(_BF16)

    log_f = jnp.minimum(f, 0.0) - jnp.log(1.0 + jnp.exp(-jnp.abs(f)))
    sums = _dot(tri_ref[...], pack_pieces(log_f))

    qt = _dot_nt(wqt_ref[...], xb) * (HEAD_DIM ** -0.5 * LOG2E)
    qt_ref[0] = qt.astype(_BF16)
    k = _dot_nt(xb, wkt_ref[...])
    k_ref[0] = k.astype(_BF16)
    qn2 = _dot(head_rows_ref[...], (qt * qt).astype(_BF16))
    kn2 = _dot_nt(head_rows_ref[...], (k * k).astype(_BF16))

    total = sums
    for shift in (N_HEADS, 2 * N_HEADS, LANES - N_HEADS, LANES - 2 * N_HEADS):
        total = total + pltpu.roll(sums, shift, 1)
    cum = carry_ref[0:1, :] + total
    carry_ref[0:1, :] = cum[tm - 1:tm, :]
    pieces = pack_pieces(cum * LOG2E)
    spread = _dot(pieces, expand_ref[...])
    spread_t = _dot_nt(expand_t_ref[...], pieces)

    cum_t = _dot_nt(piece_rows_ref[...], pieces)
    tile_lane = lax.broadcasted_iota(jnp.int32, (2 * N_HEADS, LANES), 1)
    first_tile = pl.program_id(1) * (tm // TK)
    stats = [stats_ref[0, s] for s in range(N_STATS)]
    for j in range(tm // TK):
        lo, hi = j * TK, (j + 1) * TK
        cols = (cum_t[:, lo:lo + 1], cum_t[:, hi - 1:hi],
                jnp.max(qn2[:, lo:hi], axis=1, keepdims=True),
                jnp.max(kn2[:, lo:hi], axis=1, keepdims=True))
        stats = [jnp.where(tile_lane == first_tile + j, c, s) for c, s in zip(cols, stats)]
    for s in range(N_STATS):
        stats_ref[0, s] = stats[s]

    vt = _dot_nt(wvt_ref[...], xb).astype(_BF16)
    for h in range(N_HEADS):
        vt_ref[0, h, :HEAD_DIM, :] = vt[h * HEAD_DIM:(h + 1) * HEAD_DIM, :]
        vt_ref[0, h, HEAD_DIM:, :] = jnp.ones((PAIR - HEAD_DIM, tm), _BF16)

    gates_ref[0] = _dot_nt(xb, wgt_ref[...]).astype(_BF16)

    in_a = lane < BIAS_COLS
    in_b = jnp.logical_and(lane >= BIAS_COLS, lane < 2 * BIAS_COLS)
    ka_ref[0] = jnp.where(in_a, -spread, jnp.where(in_b, 1.0, 0.0)).astype(_BF16)
    row = lax.broadcasted_iota(jnp.int32, (LANES, tm), 0)
    in_a = row < BIAS_COLS
    in_b = jnp.logical_and(row >= BIAS_COLS, row < 2 * BIAS_COLS)
    qat_ref[0] = jnp.where(in_a, 1.0, jnp.where(in_b, spread_t, 0.0)).astype(_BF16)


def _head_rows():
    r = np.zeros((2 * N_HEADS, ATTN_WIDTH), np.float32)
    for h in range(N_HEADS):
        r[h, h * HEAD_DIM:(h + 1) * HEAD_DIM] = 1.0
    return r


def _piece_rows():
    r = np.zeros((2 * N_HEADS, LANES), np.float32)
    for h in range(N_HEADS):
        for i in range(N_SPLIT):
            r[h, N_HEADS * i + h] = 1.0
    return r


def _expand_matrix():
    e = np.zeros((LANES, LANES), np.float32)
    for h in range(N_HEADS):
        for i in range(N_SPLIT):
            e[N_HEADS * i + h, N_SPLIT * h + i] = 1.0
            e[N_HEADS * i + h, BIAS_COLS + N_SPLIT * h + i] = 1.0
    return e


_IN_OFFSETS = tuple(int(v) for v in np.cumsum(
    (0, 2 * SGU_WIDTH, ATTN_WIDTH, ATTN_WIDTH, ATTN_WIDTH, N_HEADS, 2 * D_MODEL)))


def _heads_in_order(w, head_order):
    return w.reshape(N_HEADS, HEAD_DIM, -1)[head_order].reshape(w.shape)


def _in_projection(x, g_pre, w_in, b_forget, g_sgu, b_sgu, later_weights, head_order):
    B, S, D = x.shape
    tm = ROW_TILE
    n_steps = B * (S // tm)
    o = _IN_OFFSETS
    wt = jnp.swapaxes(w_in, 0, 1)
    wtb = wt.astype(_BF16)
    wqt, wkt, wvt = (_heads_in_order(wtb[o[i]:o[i + 1]], head_order) for i in (1, 2, 3))
    row_pad = ((0, LANES - N_SPLIT * N_HEADS), (0, 0))
    wft = jnp.pad(jnp.tile(wt[o[4]:o[5]][head_order], (N_SPLIT, 1)), row_pad).astype(_BF16)
    lane_pad = ((0, 0), (0, LANES - N_SPLIT * N_HEADS))
    bf = jnp.pad(jnp.tile(b_forget[head_order].reshape(1, N_HEADS), (1, N_SPLIT)), lane_pad)
    tri = jnp.asarray(np.tril(np.ones((tm, tm), np.float32)), _BF16)
    expand = jnp.asarray(_expand_matrix(), _BF16)
    expand_t = jnp.asarray(_expand_matrix().T, _BF16)
    head_rows = jnp.asarray(_head_rows(), _BF16)
    piece_rows = jnp.asarray(_piece_rows(), _BF16)

    def const(shape):
        return pl.BlockSpec(shape, lambda b, i: (0,) * len(shape))

    def rows(width):
        return pl.BlockSpec((1, tm, width), lambda b, i: (b, i, 0))

    out_shape = (
        jax.ShapeDtypeStruct((B, S, SGU_WIDTH), _BF16),
        jax.ShapeDtypeStruct((B, S, SGU_WIDTH), _BF16),
        jax.ShapeDtypeStruct((B, ATTN_WIDTH, S), _BF16),
        jax.ShapeDtypeStruct((B, S, ATTN_WIDTH), _BF16),
        jax.ShapeDtypeStruct((B, N_HEADS, PAIR, S), _BF16),
        jax.ShapeDtypeStruct((B, S, 2 * D_MODEL), _BF16),
        jax.ShapeDtypeStruct((B, LANES, S), _BF16),
        jax.ShapeDtypeStruct((B, S, LANES), _BF16),
        jax.ShapeDtypeStruct((B, N_STATS, 2 * N_HEADS, LANES), _F32),
    )

    def cols(height):
        return pl.BlockSpec((1, height, tm), lambda b, i: (b, 0, i))

    def row_block(w):
        assert w.shape[0] % (n_steps * BF16_SUBLANES) == 0
        return pl.BlockSpec((w.shape[0] // n_steps, w.shape[1]),
                            lambda b, i: (b * (S // tm) + i, 0))

    out_specs = (
        rows(SGU_WIDTH), rows(SGU_WIDTH), cols(ATTN_WIDTH), rows(ATTN_WIDTH),
        pl.BlockSpec((1, N_HEADS, PAIR, tm), lambda b, i: (b, 0, 0, i)),
        rows(2 * D_MODEL), cols(LANES), rows(LANES),
        pl.BlockSpec((1, N_STATS, 2 * N_HEADS, LANES), lambda b, i: (b, 0, 0, 0)),
    ) + tuple(row_block(w) for w in later_weights)
    out_shape += tuple(jax.ShapeDtypeStruct(w.shape, _BF16) for w in later_weights)
    return pl.pallas_call(
        functools.partial(_inproj_kernel, n_cast=len(later_weights)),
        grid=(B, S // tm),
        in_specs=[
            rows(D), const((1, D)),
            pl.BlockSpec((o[1] - o[0], D), lambda b, i: (0, 0)),
            const(wqt.shape), const(wkt.shape), const(wvt.shape), const(wft.shape),
            pl.BlockSpec((pl.Element(o[6] - o[5]), pl.Element(D)), lambda b, i: (o[5], 0)),
            const((1, LANES)), const((1, SGU_WIDTH)), const((1, SGU_WIDTH)),
            const(tri.shape), const(expand.shape), const(expand_t.shape),
            const(head_rows.shape), const(piece_rows.shape),
        ] + [row_block(w) for w in later_weights],
        out_specs=out_specs,
        out_shape=out_shape,
        scratch_shapes=[pltpu.VMEM((8, LANES), _F32)],
        compiler_params=pltpu.CompilerParams(
            dimension_semantics=("arbitrary", "arbitrary"),
            vmem_limit_bytes=VMEM_LIMIT),
        name="in_projection",
    )(x, g_pre.reshape(1, D), wtb, wqt, wkt, wvt, wft, wtb, bf,
      g_sgu.reshape(1, SGU_WIDTH), b_sgu.reshape(1, SGU_WIDTH), tri, expand, expand_t,
      head_rows, piece_rows, *later_weights)


def _attn_kernel(plan_ref, k_ref, ka_ref, vt_ref, qt_ref, qat_ref, o_ref,
                 qf_ref, st_a, st_b, st_c, m_ref, acc_ref):
    batch = pl.program_id(0)
    pair = pl.program_id(1)
    S = k_ref.shape[1]
    row = lax.broadcasted_iota(jnp.int32, (PAIR, 1), 0)
    all_tiles = [(e, t) for t in range(NQ) for e in range(2)]

    def key_tile(k0):
        return jnp.concatenate([k_ref[0, pl.ds(k0, TK), :], ka_ref[0, pl.ds(k0, TK), :]], axis=1)

    def offsets(g, j):
        n, skew = plan_ref[batch, pair, g, 0], plan_ref[batch, pair, g, 1]
        first = (g * NQ - n - skew + j) * TK
        return [pl.multiple_of(first + t * skew * TK, TK) for t in range(NQ)]

    def update(e, t, st, k0, masked):
        cs = slice(t * TQ, (t + 1) * TQ)
        if masked:
            key_i = lax.broadcasted_iota(jnp.int32, (TK, TQ), 0)
            qry_i = lax.broadcasted_iota(jnp.int32, (TK, TQ), 1)
            st = jnp.where(key_i <= qry_i, st, -jnp.inf)
        m = m_ref[e, :, cs]
        m_new = jnp.maximum(m, jnp.max(st, axis=0, keepdims=True))
        p = jnp.exp2(st - m_new)
        alpha = jnp.exp2(m - m_new)
        m_ref[e, :, cs] = m_new
        acc_ref[e, :, cs] = alpha * acc_ref[e, :, cs] + _dot(
            vt_ref[0, e, :, pl.ds(k0, TK)], p.astype(_BF16))

    def step(cur, cur_tiles, cur_offs, nxt, nxt_tiles, nxt_offs):
        for n in range(max(len(cur_tiles), len(nxt_tiles))):
            if n < len(nxt_tiles):
                e, t = nxt_tiles[n]
                nxt[e, t] = _dot(key_tile(nxt_offs[t]), qf_ref[e, :, t * TQ:(t + 1) * TQ])
            if n < len(cur_tiles):
                e, t, masked = cur_tiles[n]
                update(e, t, cur[e, t], cur_offs[t], masked)

    def load_queries(g):
        q0 = pl.multiple_of(g * QG, QG)
        qt = qt_ref[0, :, pl.ds(q0, QG)].astype(_F32)
        qat = qat_ref[0, :, pl.ds(q0, QG)].astype(_F32)
        for e in range(2):
            a0 = N_SPLIT * (2 * pair + e)
            q_mask = jnp.logical_and(row >= HEAD_DIM * e, row < HEAD_DIM * (e + 1))
            a_mask = jnp.logical_or(
                jnp.logical_and(row >= a0, row < a0 + N_SPLIT),
                jnp.logical_and(row >= BIAS_COLS + a0, row < BIAS_COLS + a0 + N_SPLIT))
            qf_ref[e, :PAIR, :] = jnp.where(q_mask, qt, 0.0).astype(_BF16)
            qf_ref[e, PAIR:, :] = jnp.where(a_mask, qat, 0.0).astype(_BF16)

    def reset_state():
        m_ref[...] = jnp.full(m_ref.shape, -jnp.inf, _F32)
        acc_ref[...] = jnp.zeros(acc_ref.shape, _F32)

    full = [(e, t, False) for e, t in all_tiles]
    bufs = (st_a, st_b)
    n_groups = S // QG

    def finish(g):
        out = [acc_ref[e, :HEAD_DIM, :] * (1.0 / acc_ref[e, HEAD_DIM:HEAD_DIM + 1, :])
               for e in range(2)]
        o_ref[0, pl.ds(pl.multiple_of(g * QG, QG), QG), :] = (
            jnp.concatenate(out, axis=0).T.astype(_BF16))
        reset_state()

    def next_group(g):
        return jnp.minimum(g + 1, n_groups - 1)

    def triangle(g):
        q0 = pl.multiple_of(g * QG, QG)
        for i in range(NQ):
            cur_tiles = [(e, t, t == i) for t in range(i, NQ) for e in range(2)]
            cur_offs = [q0 + i * TK] * NQ
            if i + 1 < NQ:
                nxt_tiles = [(e, t) for t in range(i + 1, NQ) for e in range(2)]
                nxt_offs = [q0 + (i + 1) * TK] * NQ
            else:
                load_queries(next_group(g))
                nxt_tiles, nxt_offs = all_tiles, offsets(next_group(g), 0)
            step(bufs[i % 2], cur_tiles, cur_offs, bufs[(i + 1) % 2], nxt_tiles, nxt_offs)

    load_queries(0)
    reset_state()
    step(None, [], None, st_a, all_tiles, [0] * NQ)
    triangle(0)

    def q_group(g, _):
        n = plan_ref[batch, pair, g, 0]
        skew = plan_ref[batch, pair, g, 1]

        def full_steps(i, _):
            for u in range(KU):
                j = i * KU + u
                step(bufs[u % 2], full, offsets(g, j), bufs[(u + 1) % 2], all_tiles,
                     offsets(g, j + 1))
            return 0

        def skew_tail(j):
            step(st_a, full, offsets(g, j), st_b, all_tiles, offsets(g, j + 1))
            load_queries(next_group(g))
            step(st_b, [(e, t, True) for e, t in all_tiles], offsets(g, j + 1),
                 st_a, all_tiles, offsets(next_group(g), 0))

        unrolled = jnp.logical_and(skew == 1, functools.reduce(
            jnp.logical_or, [n == n_static for n_static in UNROLLED_SKEW_STEPS]))
        for n_static in UNROLLED_SKEW_STEPS:
            @pl.when(jnp.logical_and(skew == 1, n == n_static))
            def _(n_static=n_static):
                last = n_static + 1
                seq = [bufs[j % 2] for j in range(last)] + [st_c if last % 2 == 0 else st_b,
                                                            st_a]
                finish(g - 1)
                for j in range(last):
                    step(seq[j], full, offsets(g, j), seq[j + 1], all_tiles, offsets(g, j + 1))
                load_queries(next_group(g))
                step(seq[last], [(e, t, True) for e, t in all_tiles], offsets(g, last),
                     st_a, all_tiles, offsets(next_group(g), 0))

        @pl.when(jnp.logical_not(unrolled))
        def _():
            finish(g - 1)
            lax.fori_loop(0, n // KU, full_steps, 0)

        @pl.when(jnp.logical_and(skew == 1, jnp.logical_not(unrolled)))
        def _():
            skew_tail(n)

        @pl.when(skew == 0)
        def _():
            triangle(g)

        return 0

    lax.fori_loop(1, n_groups, q_group, 0)
    finish(n_groups - 1)


def _sweep_plan(stats, n_tiles):
    cum_first, cum_last, qn2, kn2 = (stats[:, s, :N_HEADS, :n_tiles] for s in range(N_STATS))
    B = stats.shape[0]
    n_groups = n_tiles // NQ
    k_norm = jnp.sqrt(jnp.max(kn2, axis=-1, keepdims=True))
    reach = NORM_SLACK * jnp.sqrt(qn2) * k_norm + cum_first
    bound = reach[..., None] - cum_last[:, :, None, :]
    tile = jnp.arange(n_tiles)
    needed = jnp.logical_and(tile[None, :] < tile[:, None], bound >= -ZERO_EXP2)
    w = jnp.sum(needed, axis=-1).astype(jnp.int32)
    w = jnp.max(w.reshape(B, N_PAIRS, 2, n_groups, NQ), axis=2)
    start = jnp.arange(n_groups, dtype=jnp.int32) * NQ
    n_flat = jnp.max(jnp.maximum(w - jnp.arange(NQ, dtype=jnp.int32), 0), axis=-1)
    n_flat = jnp.minimum(((n_flat + KU - 1) // KU) * KU, start)
    n_skew = jnp.maximum(jnp.max(w, axis=-1), 1)
    exact = functools.reduce(jnp.logical_or, [n_skew == s + 1 for s in UNROLLED_SKEW_STEPS])
    n_skew = jnp.where(exact, n_skew, n_skew + (n_skew + 1) % 2)
    units_flat = NQ * n_flat + NQ * (NQ + 1) // 2
    units_skew = NQ * n_skew + NQ
    skew = jnp.logical_and(n_skew < start, units_skew < units_flat)
    n = jnp.where(skew, n_skew - 1, n_flat)
    return jnp.stack([n, skew.astype(jnp.int32)], axis=-1)


def _attention(k, ka, vt, qt, qat, stats):
    B, S, _ = k.shape
    assert TQ == TK and KU == 2 and NQ % KU == 0 and S % QG == 0
    assert S // TK <= LANES
    plan = _sweep_plan(stats, S // TK)
    pair_rows = pl.BlockSpec((1, S, PAIR), lambda b, j, nb: (b, 0, j))
    return pl.pallas_call(
        _attn_kernel,
        grid_spec=pltpu.PrefetchScalarGridSpec(
            num_scalar_prefetch=1,
            grid=(B, N_PAIRS),
            in_specs=[pair_rows, pl.BlockSpec((1, S, LANES), lambda b, j, nb: (b, 0, 0)),
                      pl.BlockSpec((1, 2, PAIR, S), lambda b, j, nb: (b, j, 0, 0)),
                      pl.BlockSpec((1, PAIR, S), lambda b, j, nb: (b, j, 0)),
                      pl.BlockSpec((1, LANES, S), lambda b, j, nb: (b, 0, 0))],
            out_specs=pair_rows,
            scratch_shapes=[pltpu.VMEM((2, PAIR + LANES, QG), _BF16),
                            pltpu.VMEM((2, NQ, TK, TQ), _F32), pltpu.VMEM((2, NQ, TK, TQ), _F32),
                            pltpu.VMEM((2, NQ, TK, TQ), _F32),
                            pltpu.VMEM((2, 1, QG), _F32), pltpu.VMEM((2, PAIR, QG), _F32)]),
        out_shape=jax.ShapeDtypeStruct((B, S, ATTN_WIDTH), _BF16),
        compiler_params=pltpu.CompilerParams(
            dimension_semantics=("parallel", "parallel"),
            vmem_limit_bytes=VMEM_LIMIT),
        name="forgetting_attention",
    )(plan, k, ka, vt, qt, qat)


def _rms_scale(v, g):
    ms = jnp.mean(v * v, axis=-1, keepdims=True)
    return (v * lax.rsqrt(ms + EPS)) * g


def _merge_mlp_kernel(x_ref, u_ref, vn_ref, ya_ref, gates_ref, wsp_ref, bsp_ref,
                      wbs_ref, wba_ref, wout_ref, gpost_ref,
                      gpre2_ref, wup_ref, wdown_ref, gpost2_ref, o_ref, ysgu_ref):
    tm = x_ref.shape[0]
    row = lax.broadcasted_iota(jnp.int32, (CHUNK, CHUNK), 0)
    col = lax.broadcasted_iota(jnp.int32, (CHUNK, CHUNK), 1)
    lane = lax.broadcasted_iota(jnp.int32, (CHUNK, PAIR), 1)
    ws = [jnp.where(row >= col, wsp_ref[g], 0.0).astype(_BF16) for g in range(N_GROUPS)]
    for c in range(tm // CHUNK):
        r = slice(c * CHUNK, (c + 1) * CHUNK)
        for j in range(N_GROUPS // 2):
            cs = slice(j * PAIR, (j + 1) * PAIR)
            vp = vn_ref[r, cs]
            s = jnp.where(lane < HEAD_DIM, _dot(ws[2 * j], vp), _dot(ws[2 * j + 1], vp))
            s = s + bsp_ref[:, cs]
            ysgu_ref[r, cs] = (u_ref[r, cs].astype(_F32) * s).astype(_BF16)

    blocks = [slice(n * tm // N_STREAMS, (n + 1) * tm // N_STREAMS) for n in range(N_STREAMS)]

    def mix(r):
        a = _dot(ysgu_ref[r, :], wbs_ref[...])
        b = _dot(ya_ref[r, :], wba_ref[...])
        gate_a = jax.nn.sigmoid(gates_ref[r, :D_MODEL].astype(_F32))
        gate_b = jax.nn.sigmoid(gates_ref[r, D_MODEL:].astype(_F32))
        merged = (gate_a * a + gate_b * b).astype(_BF16)
        return _dot(merged, wout_ref[...])

    def mlp(xb):
        acc = jnp.zeros(xb.shape, _F32)
        for c in range(D_FF // FF_TILE):
            cs = slice(c * FF_TILE, (c + 1) * FF_TILE)
            hid = jnp.square(jnp.maximum(_dot(xb, wup_ref[:, cs]), 0.0)).astype(_BF16)
            acc = acc + _dot(hid, wdown_ref[cs, :])
        return acc

    mixed = [mix(r) for r in blocks]
    h1 = [x_ref[r, :] + _rms_scale(o, gpost_ref[...]) for r, o in zip(blocks, mixed)]
    xb = [_rms_scale(h, gpre2_ref[...]).astype(_BF16) for h in h1]
    ff = [mlp(v) for v in xb]
    for r, h, f in zip(blocks, h1, ff):
        o_ref[r, :] = h + _rms_scale(f, gpost2_ref[...])


def _merge_mlp(x2, u2, vn2, ya2, gates2, w_spatial, b_spatial, w_bs, w_ba, w_out, g_post,
               g_pre2, w_up, w_down, g_post2):
    R, D = x2.shape
    tm = ROW_TILE
    bsp = jnp.repeat(b_spatial.T, SGU_WIDTH // N_GROUPS, axis=1)

    def const(shape):
        return pl.BlockSpec(shape, lambda i: (0,) * len(shape), pipeline_mode=pl.Buffered(1))

    def rows(width):
        return pl.BlockSpec((tm, width), lambda i: (i, 0))

    return pl.pallas_call(
        _merge_mlp_kernel,
        grid=(R // tm,),
        in_specs=[rows(D), rows(SGU_WIDTH), rows(SGU_WIDTH), rows(ATTN_WIDTH), rows(2 * D),
                  const(w_spatial.shape), const(bsp.shape),
                  const(w_bs.shape), const(w_ba.shape), const(w_out.shape), const((1, D)),
                  const((1, D)), const(w_up.shape), const(w_down.shape), const((1, D))],
        out_specs=rows(D),
        out_shape=jax.ShapeDtypeStruct((R, D), _F32),
        scratch_shapes=[pltpu.VMEM((tm, SGU_WIDTH), _BF16)],
        compiler_params=pltpu.CompilerParams(
            dimension_semantics=("parallel",), vmem_limit_bytes=VMEM_LIMIT),
        name="merge_mlp",
    )(x2, u2, vn2, ya2, gates2, w_spatial, bsp,
      w_bs, w_ba, w_out, g_post.reshape(1, D),
      g_pre2.reshape(1, D), w_up, w_down, g_post2.reshape(1, D))


def kernel(x, g_mix_pre, w_in, b_forget, g_sgu, b_sgu, w_spatial, b_spatial, w_branch_sgu,
           w_branch_attn, w_out, g_mix_post, g_ffn_pre, w_up, w_down, g_ffn_post):
    B, S, D = x.shape
    h = x
    for l in range(g_mix_pre.shape[0]):
        head_order = jnp.argsort(b_forget[l])
        later = (w_branch_sgu[l], _heads_in_order(w_branch_attn[l], head_order), w_out[l],
                 w_up[l], w_down[l])
        u, vn, qt, k, vt, gates, qat, ka, stats, w_bs, w_ba, w_o, w_u, w_d = _in_projection(
            h, g_mix_pre[l], w_in[l], b_forget[l], g_sgu[l], b_sgu[l], later, head_order)
        y_attn = _attention(k, ka, vt, qt, qat, stats)
        h = _merge_mlp(h.reshape(B * S, D), u.reshape(B * S, -1), vn.reshape(B * S, -1),
                       y_attn.reshape(B * S, -1), gates.reshape(B * S, -1),
                       w_spatial[l], b_spatial[l], w_bs, w_ba, w_o,
                       g_mix_post[l], g_ffn_pre[l], w_u, w_d,
                       g_ffn_post[l]).reshape(B, S, D)
    return h
```

```python
import functools

import jax
import jax.numpy as jnp
import numpy as np
from jax import lax
from jax.experimental import pallas as pl
from jax.experimental.pallas import tpu as pltpu

D_MODEL = 1024
N_HEADS = 8
HEAD_DIM = 64
ATTN_WIDTH = N_HEADS * HEAD_DIM
N_GROUPS = 8
SGU_WIDTH = D_MODEL // 2
CHUNK = 128
D_FF = 4 * D_MODEL
EPS = 1e-6
LOG2E = 1.4426950408889634

LANES = 128
N_SPLIT = 3
BIAS_COLS = N_HEADS * N_SPLIT
PAIR = 2 * HEAD_DIM
N_PAIRS = N_HEADS // 2

ROW_TILE = 512
FF_TILE = 1024
N_STREAMS = 2
TQ = 256
TK = 256
NQ = 4
QG = NQ * TQ
KU = 2
UNROLLED_SKEW_STEPS = (1, 2, 3, 4, 5, 6)
N_STATS = 4
ZERO_EXP2 = 136.0
NORM_SLACK = 2.05
V7X_VMEM_BYTES = 64 * 1024 * 1024
VMEM_LIMIT = V7X_VMEM_BYTES * 7 // 8
BF16_SUBLANES = 16

_BF16 = jnp.bfloat16
_F32 = jnp.float32


def _split_bf16(x):
    parts = []
    r = x
    for _ in range(N_SPLIT):
        p = r.astype(_BF16)
        parts.append(p)
        r = r - p.astype(_F32)
    return parts


def _dot(a, b):
    return jnp.dot(a, b, preferred_element_type=_F32)


def _dot_nt(a, b):
    return lax.dot_general(a, b, (((1,), (1,)), ((), ())), preferred_element_type=_F32)


def _inproj_kernel(x_ref, gpre_ref, wzt_ref, wqkvt_ref, wft_ref, wgt_ref,
                   bf_ref, gsgu_ref, bsgu_ref, tri_ref, expand_ref, expand_t_ref,
                   head_rows_ref, piece_rows_ref, *rest, n_cast):
    cast_in = rest[:n_cast]
    (u_ref, vn_ref, qt_ref, k_ref, vt_ref, gates_ref, qat_ref, ka_ref,
     stats_ref) = rest[n_cast:n_cast + 9]
    cast_out = rest[n_cast + 9:2 * n_cast + 9]
    carry_ref, = rest[2 * n_cast + 9:]
    for src, dst in zip(cast_in, cast_out):
        dst[...] = src[...].astype(_BF16)

    @pl.when(pl.program_id(1) == 0)
    def _():
        carry_ref[...] = jnp.zeros_like(carry_ref)
        stats_ref[...] = jnp.zeros_like(stats_ref)

    x = x_ref[0]
    tm = x.shape[0]
    ms = jnp.mean(x * x, axis=-1, keepdims=True)
    xb = ((x * lax.rsqrt(ms + EPS)) * gpre_ref[...]).astype(_BF16)

    lane = lax.broadcasted_iota(jnp.int32, (tm, LANES), 1)

    def pack_pieces(v):
        hi, mid, lo = _split_bf16(v)
        zero = jnp.zeros((), _BF16)
        return jnp.where(lane < N_HEADS, hi,
                         jnp.where(lane < 2 * N_HEADS, mid,
                                   jnp.where(lane < N_SPLIT * N_HEADS, lo, zero)))

    f = _dot_nt(xb, wft_ref[...]) + bf_ref[...]

    z = jax.nn.gelu(_dot_nt(xb, wzt_ref[...]), approximate=True)
    u_ref[0] = z[:, :SGU_WIDTH].astype(_BF16)
    v = z[:, SGU_WIDTH:]
    mu = jnp.mean(v, axis=-1, keepdims=True)
    vc = v - mu
    var = jnp.mean(vc * vc, axis=-1, keepdims=True)
    vn_ref[0] = ((vc * lax.rsqrt(var + EPS)) * gsgu_ref[...] + bsgu_ref[...]).astype(_BF16)

    log_f = jnp.minimum(f, 0.0) - jnp.log(1.0 + jnp.exp(-jnp.abs(f)))
    sums = _dot(tri_ref[...], pack_pieces(log_f))

    qt = _dot_nt(wqkvt_ref[0], xb) * (HEAD_DIM ** -0.5 * LOG2E)
    qt_ref[0] = qt.astype(_BF16)
    k = _dot_nt(xb, wqkvt_ref[1])
    k_ref[0] = k.astype(_BF16)
    qn2 = _dot(head_rows_ref[...], (qt * qt).astype(_BF16))
    kn2 = _dot_nt(head_rows_ref[...], (k * k).astype(_BF16))

    total = sums
    for shift in (N_HEADS, 2 * N_HEADS, LANES - N_HEADS, LANES - 2 * N_HEADS):
        total = total + pltpu.roll(sums, shift, 1)
    cum = carry_ref[0:1, :] + total
    carry_ref[0:1, :] = cum[tm - 1:tm, :]
    pieces = pack_pieces(cum * LOG2E)
    spread = _dot(pieces, expand_ref[...])
    spread_t = _dot_nt(expand_t_ref[...], pieces)

    cum_t = _dot_nt(piece_rows_ref[...], pieces)
    tile_lane = lax.broadcasted_iota(jnp.int32, (2 * N_HEADS, LANES), 1)
    first_tile = pl.program_id(1) * (tm // TK)
    stats = [stats_ref[0, s] for s in range(N_STATS)]
    for j in range(tm // TK):
        lo, hi = j * TK, (j + 1) * TK
        cols = (cum_t[:, lo:lo + 1], cum_t[:, hi - 1:hi],
                jnp.max(qn2[:, lo:hi], axis=1, keepdims=True),
                jnp.max(kn2[:, lo:hi], axis=1, keepdims=True))
        stats = [jnp.where(tile_lane == first_tile + j, c, s) for c, s in zip(cols, stats)]
    for s in range(N_STATS):
        stats_ref[0, s] = stats[s]

    vt = _dot_nt(wqkvt_ref[2], xb).astype(_BF16)
    for h in range(N_HEADS):
        vt_ref[0, h, :HEAD_DIM, :] = vt[h * HEAD_DIM:(h + 1) * HEAD_DIM, :]
        vt_ref[0, h, HEAD_DIM:, :] = jnp.ones((PAIR - HEAD_DIM, tm), _BF16)

    gates_ref[0] = _dot_nt(xb, wgt_ref[...]).astype(_BF16)

    in_a = lane < BIAS_COLS
    in_b = jnp.logical_and(lane >= BIAS_COLS, lane < 2 * BIAS_COLS)
    ka_ref[0] = jnp.where(in_a, -spread, jnp.where(in_b, 1.0, 0.0)).astype(_BF16)
    row = lax.broadcasted_iota(jnp.int32, (LANES, tm), 0)
    in_a = row < BIAS_COLS
    in_b = jnp.logical_and(row >= BIAS_COLS, row < 2 * BIAS_COLS)
    qat_ref[0] = jnp.where(in_a, 1.0, jnp.where(in_b, spread_t, 0.0)).astype(_BF16)


def _head_rows():
    r = np.zeros((2 * N_HEADS, ATTN_WIDTH), np.float32)
    for h in range(N_HEADS):
        r[h, h * HEAD_DIM:(h + 1) * HEAD_DIM] = 1.0
    return r


def _piece_rows():
    r = np.zeros((2 * N_HEADS, LANES), np.float32)
    for h in range(N_HEADS):
        for i in range(N_SPLIT):
            r[h, N_HEADS * i + h] = 1.0
    return r


def _expand_matrix():
    e = np.zeros((LANES, LANES), np.float32)
    for h in range(N_HEADS):
        for i in range(N_SPLIT):
            e[N_HEADS * i + h, N_SPLIT * h + i] = 1.0
            e[N_HEADS * i + h, BIAS_COLS + N_SPLIT * h + i] = 1.0
    return e


_IN_OFFSETS = tuple(int(v) for v in np.cumsum(
    (0, 2 * SGU_WIDTH, ATTN_WIDTH, ATTN_WIDTH, ATTN_WIDTH, N_HEADS, 2 * D_MODEL)))


def _heads_in_order(w, head_order):
    return w.reshape(N_HEADS, HEAD_DIM, -1)[head_order].reshape(w.shape)


def _in_projection(x, g_pre, w_in, b_forget, g_sgu, b_sgu, later_weights, head_order):
    B, S, D = x.shape
    tm = ROW_TILE
    n_steps = B * (S // tm)
    o = _IN_OFFSETS
    wt = jnp.swapaxes(w_in, 0, 1)
    wtb = wt.astype(_BF16)
    wqkvt = wtb[o[1]:o[4]].reshape(3, N_HEADS, HEAD_DIM, D)[:, head_order].reshape(
        3, ATTN_WIDTH, D)
    row_pad = ((0, LANES - N_SPLIT * N_HEADS), (0, 0))
    wft = jnp.pad(jnp.tile(wt[o[4]:o[5]][head_order], (N_SPLIT, 1)), row_pad).astype(_BF16)
    lane_pad = ((0, 0), (0, LANES - N_SPLIT * N_HEADS))
    bf = jnp.pad(jnp.tile(b_forget[head_order].reshape(1, N_HEADS), (1, N_SPLIT)), lane_pad)
    tri = jnp.asarray(np.tril(np.ones((tm, tm), np.float32)), _BF16)
    expand = jnp.asarray(_expand_matrix(), _BF16)
    expand_t = jnp.asarray(_expand_matrix().T, _BF16)
    head_rows = jnp.asarray(_head_rows(), _BF16)
    piece_rows = jnp.asarray(_piece_rows(), _BF16)

    def const(shape):
        return pl.BlockSpec(shape, lambda b, i: (0,) * len(shape))

    def rows(width):
        return pl.BlockSpec((1, tm, width), lambda b, i: (b, i, 0))

    out_shape = (
        jax.ShapeDtypeStruct((B, S, SGU_WIDTH), _BF16),
        jax.ShapeDtypeStruct((B, S, SGU_WIDTH), _BF16),
        jax.ShapeDtypeStruct((B, ATTN_WIDTH, S), _BF16),
        jax.ShapeDtypeStruct((B, S, ATTN_WIDTH), _BF16),
        jax.ShapeDtypeStruct((B, N_HEADS, PAIR, S), _BF16),
        jax.ShapeDtypeStruct((B, S, 2 * D_MODEL), _BF16),
        jax.ShapeDtypeStruct((B, LANES, S), _BF16),
        jax.ShapeDtypeStruct((B, S, LANES), _BF16),
        jax.ShapeDtypeStruct((B, N_STATS, 2 * N_HEADS, LANES), _F32),
    )

    def cols(height):
        return pl.BlockSpec((1, height, tm), lambda b, i: (b, 0, i))

    def row_block(w):
        assert w.shape[0] % (n_steps * BF16_SUBLANES) == 0
        return pl.BlockSpec((w.shape[0] // n_steps, w.shape[1]),
                            lambda b, i: (b * (S // tm) + i, 0))

    out_specs = (
        rows(SGU_WIDTH), rows(SGU_WIDTH), cols(ATTN_WIDTH), rows(ATTN_WIDTH),
        pl.BlockSpec((1, N_HEADS, PAIR, tm), lambda b, i: (b, 0, 0, i)),
        rows(2 * D_MODEL), cols(LANES), rows(LANES),
        pl.BlockSpec((1, N_STATS, 2 * N_HEADS, LANES), lambda b, i: (b, 0, 0, 0)),
    ) + tuple(row_block(w) for w in later_weights)
    out_shape += tuple(jax.ShapeDtypeStruct(w.shape, _BF16) for w in later_weights)
    return pl.pallas_call(
        functools.partial(_inproj_kernel, n_cast=len(later_weights)),
        grid=(B, S // tm),
        in_specs=[
            rows(D), const((1, D)),
            pl.BlockSpec((o[1] - o[0], D), lambda b, i: (0, 0)),
            const(wqkvt.shape), const(wft.shape),
            pl.BlockSpec((pl.Element(o[6] - o[5]), pl.Element(D)), lambda b, i: (o[5], 0)),
            const((1, LANES)), const((1, SGU_WIDTH)), const((1, SGU_WIDTH)),
            const(tri.shape), const(expand.shape), const(expand_t.shape),
            const(head_rows.shape), const(piece_rows.shape),
        ] + [row_block(w) for w in later_weights],
        out_specs=out_specs,
        out_shape=out_shape,
        scratch_shapes=[pltpu.VMEM((8, LANES), _F32)],
        compiler_params=pltpu.CompilerParams(
            dimension_semantics=("arbitrary", "arbitrary"),
            vmem_limit_bytes=VMEM_LIMIT),
        name="in_projection",
    )(x, g_pre.reshape(1, D), wtb, wqkvt, wft, wtb, bf,
      g_sgu.reshape(1, SGU_WIDTH), b_sgu.reshape(1, SGU_WIDTH), tri, expand, expand_t,
      head_rows, piece_rows, *later_weights)


def _attn_kernel(plan_ref, k_ref, ka_ref, vt_ref, qt_ref, qat_ref, o_ref,
                 qf_ref, st_a, st_b, st_c, m_ref, acc_ref):
    batch = pl.program_id(0)
    pair = pl.program_id(1)
    S = k_ref.shape[1]
    row = lax.broadcasted_iota(jnp.int32, (PAIR, 1), 0)
    all_tiles = [(e, t) for t in range(NQ) for e in range(2)]

    def key_tile(k0):
        return jnp.concatenate([k_ref[0, pl.ds(k0, TK), :], ka_ref[0, pl.ds(k0, TK), :]], axis=1)

    def offsets(g, j):
        n, skew = plan_ref[batch, pair, g, 0], plan_ref[batch, pair, g, 1]
        first = (g * NQ - n - skew + j) * TK
        return [pl.multiple_of(first + t * skew * TK, TK) for t in range(NQ)]

    def update(e, t, st, k0, masked):
        cs = slice(t * TQ, (t + 1) * TQ)
        if masked:
            key_i = lax.broadcasted_iota(jnp.int32, (TK, TQ), 0)
            qry_i = lax.broadcasted_iota(jnp.int32, (TK, TQ), 1)
            st = jnp.where(key_i <= qry_i, st, -jnp.inf)
        m = m_ref[e, :, cs]
        m_new = jnp.maximum(m, jnp.max(st, axis=0, keepdims=True))
        p = jnp.exp2(st - m_new)
        alpha = jnp.exp2(m - m_new)
        m_ref[e, :, cs] = m_new
        acc_ref[e, :, cs] = alpha * acc_ref[e, :, cs] + _dot(
            vt_ref[0, e, :, pl.ds(k0, TK)], p.astype(_BF16))

    def step(cur, cur_tiles, cur_offs, nxt, nxt_tiles, nxt_offs):
        for n in range(max(len(cur_tiles), len(nxt_tiles))):
            if n < len(nxt_tiles):
                e, t = nxt_tiles[n]
                nxt[e, t] = _dot(key_tile(nxt_offs[t]), qf_ref[e, :, t * TQ:(t + 1) * TQ])
            if n < len(cur_tiles):
                e, t, masked = cur_tiles[n]
                update(e, t, cur[e, t], cur_offs[t], masked)

    def load_queries(g):
        q0 = pl.multiple_of(g * QG, QG)
        qt = qt_ref[0, :, pl.ds(q0, QG)].astype(_F32)
        qat = qat_ref[0, :, pl.ds(q0, QG)].astype(_F32)
        for e in range(2):
            a0 = N_SPLIT * (2 * pair + e)
            q_mask = jnp.logical_and(row >= HEAD_DIM * e, row < HEAD_DIM * (e + 1))
            a_mask = jnp.logical_or(
                jnp.logical_and(row >= a0, row < a0 + N_SPLIT),
                jnp.logical_and(row >= BIAS_COLS + a0, row < BIAS_COLS + a0 + N_SPLIT))
            qf_ref[e, :PAIR, :] = jnp.where(q_mask, qt, 0.0).astype(_BF16)
            qf_ref[e, PAIR:, :] = jnp.where(a_mask, qat, 0.0).astype(_BF16)

    def reset_state():
        m_ref[...] = jnp.full(m_ref.shape, -jnp.inf, _F32)
        acc_ref[...] = jnp.zeros(acc_ref.shape, _F32)

    full = [(e, t, False) for e, t in all_tiles]
    bufs = (st_a, st_b)
    n_groups = S // QG

    def finish(g):
        out = [acc_ref[e, :HEAD_DIM, :] * (1.0 / acc_ref[e, HEAD_DIM:HEAD_DIM + 1, :])
               for e in range(2)]
        o_ref[0, pl.ds(pl.multiple_of(g * QG, QG), QG), :] = (
            jnp.concatenate(out, axis=0).T.astype(_BF16))
        reset_state()

    def next_group(g):
        return jnp.minimum(g + 1, n_groups - 1)

    def triangle(g):
        q0 = pl.multiple_of(g * QG, QG)
        for i in range(NQ):
            cur_tiles = [(e, t, t == i) for t in range(i, NQ) for e in range(2)]
            cur_offs = [q0 + i * TK] * NQ
            if i + 1 < NQ:
                nxt_tiles = [(e, t) for t in range(i + 1, NQ) for e in range(2)]
                nxt_offs = [q0 + (i + 1) * TK] * NQ
            else:
                load_queries(next_group(g))
                nxt_tiles, nxt_offs = all_tiles, offsets(next_group(g), 0)
            step(bufs[i % 2], cur_tiles, cur_offs, bufs[(i + 1) % 2], nxt_tiles, nxt_offs)

    load_queries(0)
    reset_state()
    step(None, [], None, st_a, all_tiles, [0] * NQ)
    triangle(0)

    def q_group(g, _):
        n = plan_ref[batch, pair, g, 0]
        skew = plan_ref[batch, pair, g, 1]

        def full_steps(i, _):
            for u in range(KU):
                j = i * KU + u
                step(bufs[u % 2], full, offsets(g, j), bufs[(u + 1) % 2], all_tiles,
                     offsets(g, j + 1))
            return 0

        def skew_tail(j):
            step(st_a, full, offsets(g, j), st_b, all_tiles, offsets(g, j + 1))
            load_queries(next_group(g))
            step(st_b, [(e, t, True) for e, t in all_tiles], offsets(g, j + 1),
                 st_a, all_tiles, offsets(next_group(g), 0))

        unrolled = jnp.logical_and(skew == 1, functools.reduce(
            jnp.logical_or, [n == n_static for n_static in UNROLLED_SKEW_STEPS]))
        for n_static in UNROLLED_SKEW_STEPS:
            @pl.when(jnp.logical_and(skew == 1, n == n_static))
            def _(n_static=n_static):
                last = n_static + 1
                seq = [bufs[j % 2] for j in range(last)] + [st_c if last % 2 == 0 else st_b,
                                                            st_a]
                finish(g - 1)
                for j in range(last):
                    step(seq[j], full, offsets(g, j), seq[j + 1], all_tiles, offsets(g, j + 1))
                load_queries(next_group(g))
                step(seq[last], [(e, t, True) for e, t in all_tiles], offsets(g, last),
                     st_a, all_tiles, offsets(next_group(g), 0))

        @pl.when(jnp.logical_not(unrolled))
        def _():
            finish(g - 1)
            lax.fori_loop(0, n // KU, full_steps, 0)

        @pl.when(jnp.logical_and(skew == 1, jnp.logical_not(unrolled)))
        def _():
            skew_tail(n)

        @pl.when(skew == 0)
        def _():
            triangle(g)

        return 0

    lax.fori_loop(1, n_groups, q_group, 0)
    finish(n_groups - 1)


def _sweep_plan(stats, n_tiles):
    cum_first, cum_last, qn2, kn2 = (stats[:, s, :N_HEADS, :n_tiles] for s in range(N_STATS))
    B = stats.shape[0]
    n_groups = n_tiles // NQ
    k_norm = jnp.sqrt(jnp.max(kn2, axis=-1, keepdims=True))
    reach = NORM_SLACK * jnp.sqrt(qn2) * k_norm + cum_first
    bound = reach[..., None] - cum_last[:, :, None, :]
    tile = jnp.arange(n_tiles)
    needed = jnp.logical_and(tile[None, :] < tile[:, None], bound >= -ZERO_EXP2)
    w = jnp.sum(needed, axis=-1).astype(jnp.int32)
    w = jnp.max(w.reshape(B, N_PAIRS, 2, n_groups, NQ), axis=2)
    start = jnp.arange(n_groups, dtype=jnp.int32) * NQ
    n_flat = jnp.max(jnp.maximum(w - jnp.arange(NQ, dtype=jnp.int32), 0), axis=-1)
    n_flat = jnp.minimum(((n_flat + KU - 1) // KU) * KU, start)
    n_skew = jnp.maximum(jnp.max(w, axis=-1), 1)
    exact = functools.reduce(jnp.logical_or, [n_skew == s + 1 for s in UNROLLED_SKEW_STEPS])
    n_skew = jnp.where(exact, n_skew, n_skew + (n_skew + 1) % 2)
    units_flat = NQ * n_flat + NQ * (NQ + 1) // 2
    units_skew = NQ * n_skew + NQ
    skew = jnp.logical_and(n_skew < start, units_skew < units_flat)
    n = jnp.where(skew, n_skew - 1, n_flat)
    return jnp.stack([n, skew.astype(jnp.int32)], axis=-1)


def _attention(k, ka, vt, qt, qat, stats):
    B, S, _ = k.shape
    assert TQ == TK and KU == 2 and NQ % KU == 0 and S % QG == 0
    assert S // TK <= LANES
    plan = _sweep_plan(stats, S // TK)
    pair_rows = pl.BlockSpec((1, S, PAIR), lambda b, j, nb: (b, 0, j))
    return pl.pallas_call(
        _attn_kernel,
        grid_spec=pltpu.PrefetchScalarGridSpec(
            num_scalar_prefetch=1,
            grid=(B, N_PAIRS),
            in_specs=[pair_rows, pl.BlockSpec((1, S, LANES), lambda b, j, nb: (b, 0, 0)),
                      pl.BlockSpec((1, 2, PAIR, S), lambda b, j, nb: (b, j, 0, 0)),
                      pl.BlockSpec((1, PAIR, S), lambda b, j, nb: (b, j, 0)),
                      pl.BlockSpec((1, LANES, S), lambda b, j, nb: (b, 0, 0))],
            out_specs=pair_rows,
            scratch_shapes=[pltpu.VMEM((2, PAIR + LANES, QG), _BF16),
                            pltpu.VMEM((2, NQ, TK, TQ), _F32), pltpu.VMEM((2, NQ, TK, TQ), _F32),
                            pltpu.VMEM((2, NQ, TK, TQ), _F32),
                            pltpu.VMEM((2, 1, QG), _F32), pltpu.VMEM((2, PAIR, QG), _F32)]),
        out_shape=jax.ShapeDtypeStruct((B, S, ATTN_WIDTH), _BF16),
        compiler_params=pltpu.CompilerParams(
            dimension_semantics=("parallel", "parallel"),
            vmem_limit_bytes=VMEM_LIMIT),
        name="forgetting_attention",
    )(plan, k, ka, vt, qt, qat)


def _rms_scale(v, g):
    ms = jnp.mean(v * v, axis=-1, keepdims=True)
    return (v * lax.rsqrt(ms + EPS)) * g


def _merge_mlp_kernel(x_ref, u_ref, vn_ref, ya_ref, gates_ref, wsp_ref, bsp_ref,
                      wbs_ref, wba_ref, wout_ref, gpost_ref,
                      gpre2_ref, wup_ref, wdown_ref, gpost2_ref, o_ref, ysgu_ref):
    tm = x_ref.shape[0]
    row = lax.broadcasted_iota(jnp.int32, (CHUNK, CHUNK), 0)
    col = lax.broadcasted_iota(jnp.int32, (CHUNK, CHUNK), 1)
    lane = lax.broadcasted_iota(jnp.int32, (CHUNK, PAIR), 1)
    ws = [jnp.where(row >= col, wsp_ref[g], 0.0).astype(_BF16) for g in range(N_GROUPS)]
    for c in range(tm // CHUNK):
        r = slice(c * CHUNK, (c + 1) * CHUNK)
        for j in range(N_GROUPS // 2):
            cs = slice(j * PAIR, (j + 1) * PAIR)
            vp = vn_ref[r, cs]
            s = jnp.where(lane < HEAD_DIM, _dot(ws[2 * j], vp), _dot(ws[2 * j + 1], vp))
            s = s + bsp_ref[:, cs]
            ysgu_ref[r, cs] = (u_ref[r, cs].astype(_F32) * s).astype(_BF16)

    blocks = [slice(n * tm // N_STREAMS, (n + 1) * tm // N_STREAMS) for n in range(N_STREAMS)]

    def mix(r):
        a = _dot(ysgu_ref[r, :], wbs_ref[...])
        b = _dot(ya_ref[r, :], wba_ref[...])
        gate_a = jax.nn.sigmoid(gates_ref[r, :D_MODEL].astype(_F32))
        gate_b = jax.nn.sigmoid(gates_ref[r, D_MODEL:].astype(_F32))
        merged = (gate_a * a + gate_b * b).astype(_BF16)
        return _dot(merged, wout_ref[...])

    def mlp(xb):
        acc = jnp.zeros(xb.shape, _F32)
        for c in range(D_FF // FF_TILE):
            cs = slice(c * FF_TILE, (c + 1) * FF_TILE)
            hid = jnp.square(jnp.maximum(_dot(xb, wup_ref[:, cs]), 0.0)).astype(_BF16)
            acc = acc + _dot(hid, wdown_ref[cs, :])
        return acc

    mixed = [mix(r) for r in blocks]
    h1 = [x_ref[r, :] + _rms_scale(o, gpost_ref[...]) for r, o in zip(blocks, mixed)]
    xb = [_rms_scale(h, gpre2_ref[...]).astype(_BF16) for h in h1]
    ff = [mlp(v) for v in xb]
    for r, h, f in zip(blocks, h1, ff):
        o_ref[r, :] = h + _rms_scale(f, gpost2_ref[...])


def _merge_mlp(x2, u2, vn2, ya2, gates2, w_spatial, b_spatial, w_bs, w_ba, w_out, g_post,
               g_pre2, w_up, w_down, g_post2):
    R, D = x2.shape
    tm = ROW_TILE
    bsp = jnp.repeat(b_spatial.T, SGU_WIDTH // N_GROUPS, axis=1)

    def const(shape):
        return pl.BlockSpec(shape, lambda i: (0,) * len(shape), pipeline_mode=pl.Buffered(1))

    def rows(width):
        return pl.BlockSpec((tm, width), lambda i: (i, 0))

    return pl.pallas_call(
        _merge_mlp_kernel,
        grid=(R // tm,),
        in_specs=[rows(D), rows(SGU_WIDTH), rows(SGU_WIDTH), rows(ATTN_WIDTH), rows(2 * D),
                  const(w_spatial.shape), const(bsp.shape),
                  const(w_bs.shape), const(w_ba.shape), const(w_out.shape), const((1, D)),
                  const((1, D)), const(w_up.shape), const(w_down.shape), const((1, D))],
        out_specs=rows(D),
        out_shape=jax.ShapeDtypeStruct((R, D), _F32),
        scratch_shapes=[pltpu.VMEM((tm, SGU_WIDTH), _BF16)],
        compiler_params=pltpu.CompilerParams(
            dimension_semantics=("parallel",), vmem_limit_bytes=VMEM_LIMIT),
        name="merge_mlp",
    )(x2, u2, vn2, ya2, gates2, w_spatial, bsp,
      w_bs, w_ba, w_out, g_post.reshape(1, D),
      g_pre2.reshape(1, D), w_up, w_down, g_post2.reshape(1, D))


def kernel(x, g_mix_pre, w_in, b_forget, g_sgu, b_sgu, w_spatial, b_spatial, w_branch_sgu,
           w_branch_attn, w_out, g_mix_post, g_ffn_pre, w_up, w_down, g_ffn_post):
    B, S, D = x.shape
    h = x
    for l in range(g_mix_pre.shape[0]):
        head_order = jnp.argsort(b_forget[l])
        later = (w_branch_sgu[l], _heads_in_order(w_branch_attn[l], head_order), w_out[l],
                 w_up[l], w_down[l])
        u, vn, qt, k, vt, gates, qat, ka, stats, w_bs, w_ba, w_o, w_u, w_d = _in_projection(
            h, g_mix_pre[l], w_in[l], b_forget[l], g_sgu[l], b_sgu[l], later, head_order)
        y_attn = _attention(k, ka, vt, qt, qat, stats)
        h = _merge_mlp(h.reshape(B * S, D), u.reshape(B * S, -1), vn.reshape(B * S, -1),
                       y_attn.reshape(B * S, -1), gates.reshape(B * S, -1),
                       w_spatial[l], b_spatial[l], w_bs, w_ba, w_o,
                       g_mix_post[l], g_ffn_pre[l], w_u, w_d,
                       g_ffn_post[l]).reshape(B, S, D)
    return h
```

```python
import functools

import jax
import jax.numpy as jnp
import numpy as np
from jax import lax
from jax.experimental import pallas as pl
from jax.experimental.pallas import tpu as pltpu

D_MODEL = 1024
N_HEADS = 8
HEAD_DIM = 64
ATTN_WIDTH = N_HEADS * HEAD_DIM
N_GROUPS = 8
SGU_WIDTH = D_MODEL // 2
CHUNK = 128
D_FF = 4 * D_MODEL
EPS = 1e-6
LOG2E = 1.4426950408889634

LANES = 128
N_SPLIT = 3
BIAS_COLS = N_HEADS * N_SPLIT
PAIR = 2 * HEAD_DIM
N_PAIRS = N_HEADS // 2

ROW_TILE = 512
FF_TILE = 1024
N_STREAMS = 2
IN_STREAMS = 2
TQ = 256
TK = 256
NQ = 4
QG = NQ * TQ
KU = 2
UNROLLED_SKEW_STEPS = (1, 2, 3, 4, 5, 6)
N_STATS = 4
ZERO_EXP2 = 136.0
NORM_SLACK = 2.05
V7X_VMEM_BYTES = 64 * 1024 * 1024
VMEM_LIMIT = V7X_VMEM_BYTES * 7 // 8
BF16_SUBLANES = 16

_BF16 = jnp.bfloat16
_F32 = jnp.float32


def _split_bf16(x):
    parts = []
    r = x
    for _ in range(N_SPLIT):
        p = r.astype(_BF16)
        parts.append(p)
        r = r - p.astype(_F32)
    return parts


def _dot(a, b):
    return jnp.dot(a, b, preferred_element_type=_F32)


def _dot_nt(a, b):
    return lax.dot_general(a, b, (((1,), (1,)), ((), ())), preferred_element_type=_F32)


def _inproj_kernel(x_ref, gpre_ref, wzt_ref, wqkvt_ref, wft_ref, wgt_ref,
                   bf_ref, gsgu_ref, bsgu_ref, tri_ref, expand_ref, expand_t_ref,
                   head_rows_ref, piece_rows_ref, *rest, n_cast):
    cast_in = rest[:n_cast]
    (u_ref, vn_ref, qt_ref, k_ref, vt_ref, gates_ref, qat_ref, ka_ref,
     stats_ref) = rest[n_cast:n_cast + 9]
    cast_out = rest[n_cast + 9:2 * n_cast + 9]
    carry_ref, = rest[2 * n_cast + 9:]
    for src, dst in zip(cast_in, cast_out):
        dst[...] = src[...].astype(_BF16)

    @pl.when(pl.program_id(1) == 0)
    def _():
        carry_ref[...] = jnp.zeros_like(carry_ref)
        stats_ref[...] = jnp.zeros_like(stats_ref)

    tm = x_ref.shape[1] // IN_STREAMS
    blocks = [slice(s * tm, (s + 1) * tm) for s in range(IN_STREAMS)]

    def normed(rs):
        x = x_ref[0, rs, :]
        ms = jnp.mean(x * x, axis=-1, keepdims=True)
        return ((x * lax.rsqrt(ms + EPS)) * gpre_ref[...]).astype(_BF16)

    xbs = [normed(rs) for rs in blocks]
    lane = lax.broadcasted_iota(jnp.int32, (tm, LANES), 1)

    def pack_pieces(v):
        hi, mid, lo = _split_bf16(v)
        zero = jnp.zeros((), _BF16)
        return jnp.where(lane < N_HEADS, hi,
                         jnp.where(lane < 2 * N_HEADS, mid,
                                   jnp.where(lane < N_SPLIT * N_HEADS, lo, zero)))

    def project(s, rs, xb):
        f = _dot_nt(xb, wft_ref[...]) + bf_ref[...]

        z = jax.nn.gelu(_dot_nt(xb, wzt_ref[...]), approximate=True)
        u_ref[0, rs, :] = z[:, :SGU_WIDTH].astype(_BF16)
        v = z[:, SGU_WIDTH:]
        mu = jnp.mean(v, axis=-1, keepdims=True)
        vc = v - mu
        var = jnp.mean(vc * vc, axis=-1, keepdims=True)
        vn_ref[0, rs, :] = ((vc * lax.rsqrt(var + EPS)) * gsgu_ref[...]
                            + bsgu_ref[...]).astype(_BF16)

        log_f = jnp.minimum(f, 0.0) - jnp.log(1.0 + jnp.exp(-jnp.abs(f)))
        sums = _dot(tri_ref[...], pack_pieces(log_f))

        qt = _dot_nt(wqkvt_ref[0], xb) * (HEAD_DIM ** -0.5 * LOG2E)
        qt_ref[0, :, rs] = qt.astype(_BF16)
        k = _dot_nt(xb, wqkvt_ref[1])
        k_ref[0, rs, :] = k.astype(_BF16)
        qn2 = _dot(head_rows_ref[...], (qt * qt).astype(_BF16))
        kn2 = _dot_nt(head_rows_ref[...], (k * k).astype(_BF16))

        total = sums
        for shift in (N_HEADS, 2 * N_HEADS, LANES - N_HEADS, LANES - 2 * N_HEADS):
            total = total + pltpu.roll(sums, shift, 1)
        cum = carry_ref[0:1, :] + total
        carry_ref[0:1, :] = cum[tm - 1:tm, :]
        pieces = pack_pieces(cum * LOG2E)
        spread = _dot(pieces, expand_ref[...])
        spread_t = _dot_nt(expand_t_ref[...], pieces)

        cum_t = _dot_nt(piece_rows_ref[...], pieces)
        tile_lane = lax.broadcasted_iota(jnp.int32, (2 * N_HEADS, LANES), 1)
        first_tile = (pl.program_id(1) * IN_STREAMS + s) * (tm // TK)
        stats = [stats_ref[0, n] for n in range(N_STATS)]
        for j in range(tm // TK):
            lo, hi = j * TK, (j + 1) * TK
            cols = (cum_t[:, lo:lo + 1], cum_t[:, hi - 1:hi],
                    jnp.max(qn2[:, lo:hi], axis=1, keepdims=True),
                    jnp.max(kn2[:, lo:hi], axis=1, keepdims=True))
            stats = [jnp.where(tile_lane == first_tile + j, c, st) for c, st in zip(cols, stats)]
        for n in range(N_STATS):
            stats_ref[0, n] = stats[n]

        vt = _dot_nt(wqkvt_ref[2], xb).astype(_BF16)
        for h in range(N_HEADS):
            vt_ref[0, h, :HEAD_DIM, rs] = vt[h * HEAD_DIM:(h + 1) * HEAD_DIM, :]
            vt_ref[0, h, HEAD_DIM:, rs] = jnp.ones((PAIR - HEAD_DIM, tm), _BF16)

        gates_ref[0, rs, :] = _dot_nt(xb, wgt_ref[...]).astype(_BF16)

        in_a = lane < BIAS_COLS
        in_b = jnp.logical_and(lane >= BIAS_COLS, lane < 2 * BIAS_COLS)
        ka_ref[0, rs, :] = jnp.where(in_a, -spread, jnp.where(in_b, 1.0, 0.0)).astype(_BF16)
        row = lax.broadcasted_iota(jnp.int32, (LANES, tm), 0)
        in_a = row < BIAS_COLS
        in_b = jnp.logical_and(row >= BIAS_COLS, row < 2 * BIAS_COLS)
        qat_ref[0, :, rs] = jnp.where(in_a, 1.0, jnp.where(in_b, spread_t, 0.0)).astype(_BF16)

    for s, (rs, xb) in enumerate(zip(blocks, xbs)):
        project(s, rs, xb)


def _head_rows():
    r = np.zeros((2 * N_HEADS, ATTN_WIDTH), np.float32)
    for h in range(N_HEADS):
        r[h, h * HEAD_DIM:(h + 1) * HEAD_DIM] = 1.0
    return r


def _piece_rows():
    r = np.zeros((2 * N_HEADS, LANES), np.float32)
    for h in range(N_HEADS):
        for i in range(N_SPLIT):
            r[h, N_HEADS * i + h] = 1.0
    return r


def _expand_matrix():
    e = np.zeros((LANES, LANES), np.float32)
    for h in range(N_HEADS):
        for i in range(N_SPLIT):
            e[N_HEADS * i + h, N_SPLIT * h + i] = 1.0
            e[N_HEADS * i + h, BIAS_COLS + N_SPLIT * h + i] = 1.0
    return e


_IN_OFFSETS = tuple(int(v) for v in np.cumsum(
    (0, 2 * SGU_WIDTH, ATTN_WIDTH, ATTN_WIDTH, ATTN_WIDTH, N_HEADS, 2 * D_MODEL)))


def _heads_in_order(w, head_order):
    return w.reshape(N_HEADS, HEAD_DIM, -1)[head_order].reshape(w.shape)


def _in_projection(x, g_pre, w_in, b_forget, g_sgu, b_sgu, later_weights, head_order):
    B, S, D = x.shape
    tm = IN_STREAMS * ROW_TILE
    n_steps = B * (S // tm)
    o = _IN_OFFSETS
    wt = jnp.swapaxes(w_in, 0, 1)
    wtb = wt.astype(_BF16)
    wqkvt = wtb[o[1]:o[4]].reshape(3, N_HEADS, HEAD_DIM, D)[:, head_order].reshape(
        3, ATTN_WIDTH, D)
    row_pad = ((0, LANES - N_SPLIT * N_HEADS), (0, 0))
    wft = jnp.pad(jnp.tile(wt[o[4]:o[5]][head_order], (N_SPLIT, 1)), row_pad).astype(_BF16)
    lane_pad = ((0, 0), (0, LANES - N_SPLIT * N_HEADS))
    bf = jnp.pad(jnp.tile(b_forget[head_order].reshape(1, N_HEADS), (1, N_SPLIT)), lane_pad)
    tri = jnp.asarray(np.tril(np.ones((ROW_TILE, ROW_TILE), np.float32)), _BF16)
    expand = jnp.asarray(_expand_matrix(), _BF16)
    expand_t = jnp.asarray(_expand_matrix().T, _BF16)
    head_rows = jnp.asarray(_head_rows(), _BF16)
    piece_rows = jnp.asarray(_piece_rows(), _BF16)

    once = pl.Buffered(1)

    def const(shape):
        return pl.BlockSpec(shape, lambda b, i: (0,) * len(shape), pipeline_mode=once)

    def rows(width):
        return pl.BlockSpec((1, tm, width), lambda b, i: (b, i, 0))

    out_shape = (
        jax.ShapeDtypeStruct((B, S, SGU_WIDTH), _BF16),
        jax.ShapeDtypeStruct((B, S, SGU_WIDTH), _BF16),
        jax.ShapeDtypeStruct((B, ATTN_WIDTH, S), _BF16),
        jax.ShapeDtypeStruct((B, S, ATTN_WIDTH), _BF16),
        jax.ShapeDtypeStruct((B, N_HEADS, PAIR, S), _BF16),
        jax.ShapeDtypeStruct((B, S, 2 * D_MODEL), _BF16),
        jax.ShapeDtypeStruct((B, LANES, S), _BF16),
        jax.ShapeDtypeStruct((B, S, LANES), _BF16),
        jax.ShapeDtypeStruct((B, N_STATS, 2 * N_HEADS, LANES), _F32),
    )

    def cols(height):
        return pl.BlockSpec((1, height, tm), lambda b, i: (b, 0, i))

    def row_block(w):
        assert w.shape[0] % (n_steps * BF16_SUBLANES) == 0
        return pl.BlockSpec((w.shape[0] // n_steps, w.shape[1]),
                            lambda b, i: (b * (S // tm) + i, 0))

    out_specs = (
        rows(SGU_WIDTH), rows(SGU_WIDTH), cols(ATTN_WIDTH), rows(ATTN_WIDTH),
        pl.BlockSpec((1, N_HEADS, PAIR, tm), lambda b, i: (b, 0, 0, i)),
        rows(2 * D_MODEL), cols(LANES), rows(LANES),
        pl.BlockSpec((1, N_STATS, 2 * N_HEADS, LANES), lambda b, i: (b, 0, 0, 0)),
    ) + tuple(row_block(w) for w in later_weights)
    out_shape += tuple(jax.ShapeDtypeStruct(w.shape, _BF16) for w in later_weights)
    return pl.pallas_call(
        functools.partial(_inproj_kernel, n_cast=len(later_weights)),
        grid=(B, S // tm),
        in_specs=[
            rows(D), const((1, D)),
            pl.BlockSpec((o[1] - o[0], D), lambda b, i: (0, 0), pipeline_mode=once),
            const(wqkvt.shape), const(wft.shape),
            pl.BlockSpec((pl.Element(o[6] - o[5]), pl.Element(D)), lambda b, i: (o[5], 0),
                         pipeline_mode=once),
            const((1, LANES)), const((1, SGU_WIDTH)), const((1, SGU_WIDTH)),
            const(tri.shape), const(expand.shape), const(expand_t.shape),
            const(head_rows.shape), const(piece_rows.shape),
        ] + [row_block(w) for w in later_weights],
        out_specs=out_specs,
        out_shape=out_shape,
        scratch_shapes=[pltpu.VMEM((8, LANES), _F32)],
        compiler_params=pltpu.CompilerParams(
            dimension_semantics=("arbitrary", "arbitrary"),
            vmem_limit_bytes=VMEM_LIMIT),
        name="in_projection",
    )(x, g_pre.reshape(1, D), wtb, wqkvt, wft, wtb, bf,
      g_sgu.reshape(1, SGU_WIDTH), b_sgu.reshape(1, SGU_WIDTH), tri, expand, expand_t,
      head_rows, piece_rows, *later_weights)


def _attn_kernel(plan_ref, k_ref, ka_ref, vt_ref, qt_ref, qat_ref, o_ref,
                 qf_ref, st_a, st_b, st_c, m_ref, acc_ref):
    batch = pl.program_id(0)
    pair = pl.program_id(1)
    S = k_ref.shape[1]
    row = lax.broadcasted_iota(jnp.int32, (PAIR, 1), 0)
    all_tiles = [(e, t) for t in range(NQ) for e in range(2)]

    def key_tile(k0):
        return jnp.concatenate([k_ref[0, pl.ds(k0, TK), :], ka_ref[0, pl.ds(k0, TK), :]], axis=1)

    def offsets(g, j):
        n, skew = plan_ref[batch, pair, g, 0], plan_ref[batch, pair, g, 1]
        first = (g * NQ - n - skew + j) * TK
        return [pl.multiple_of(first + t * skew * TK, TK) for t in range(NQ)]

    def update(e, t, st, k0, masked):
        cs = slice(t * TQ, (t + 1) * TQ)
        if masked:
            key_i = lax.broadcasted_iota(jnp.int32, (TK, TQ), 0)
            qry_i = lax.broadcasted_iota(jnp.int32, (TK, TQ), 1)
            st = jnp.where(key_i <= qry_i, st, -jnp.inf)
        m = m_ref[e, :, cs]
        m_new = jnp.maximum(m, jnp.max(st, axis=0, keepdims=True))
        p = jnp.exp2(st - m_new)
        alpha = jnp.exp2(m - m_new)
        m_ref[e, :, cs] = m_new
        acc_ref[e, :, cs] = alpha * acc_ref[e, :, cs] + _dot(
            vt_ref[0, e, :, pl.ds(k0, TK)], p.astype(_BF16))

    def step(cur, cur_tiles, cur_offs, nxt, nxt_tiles, nxt_offs):
        for n in range(max(len(cur_tiles), len(nxt_tiles))):
            if n < len(nxt_tiles):
                e, t = nxt_tiles[n]
                nxt[e, t] = _dot(key_tile(nxt_offs[t]), qf_ref[e, :, t * TQ:(t + 1) * TQ])
            if n < len(cur_tiles):
                e, t, masked = cur_tiles[n]
                update(e, t, cur[e, t], cur_offs[t], masked)

    def load_queries(g):
        q0 = pl.multiple_of(g * QG, QG)
        qt = qt_ref[0, :, pl.ds(q0, QG)].astype(_F32)
        qat = qat_ref[0, :, pl.ds(q0, QG)].astype(_F32)
        for e in range(2):
            a0 = N_SPLIT * (2 * pair + e)
            q_mask = jnp.logical_and(row >= HEAD_DIM * e, row < HEAD_DIM * (e + 1))
            a_mask = jnp.logical_or(
                jnp.logical_and(row >= a0, row < a0 + N_SPLIT),
                jnp.logical_and(row >= BIAS_COLS + a0, row < BIAS_COLS + a0 + N_SPLIT))
            qf_ref[e, :PAIR, :] = jnp.where(q_mask, qt, 0.0).astype(_BF16)
            qf_ref[e, PAIR:, :] = jnp.where(a_mask, qat, 0.0).astype(_BF16)

    def reset_state():
        m_ref[...] = jnp.full(m_ref.shape, -jnp.inf, _F32)
        acc_ref[...] = jnp.zeros(acc_ref.shape, _F32)

    full = [(e, t, False) for e, t in all_tiles]
    bufs = (st_a, st_b)
    n_groups = S // QG

    def finish(g):
        out = [acc_ref[e, :HEAD_DIM, :] * (1.0 / acc_ref[e, HEAD_DIM:HEAD_DIM + 1, :])
               for e in range(2)]
        o_ref[0, pl.ds(pl.multiple_of(g * QG, QG), QG), :] = (
            jnp.concatenate(out, axis=0).T.astype(_BF16))
        reset_state()

    def next_group(g):
        return jnp.minimum(g + 1, n_groups - 1)

    def triangle(g):
        q0 = pl.multiple_of(g * QG, QG)
        for i in range(NQ):
            cur_tiles = [(e, t, t == i) for t in range(i, NQ) for e in range(2)]
            cur_offs = [q0 + i * TK] * NQ
            if i + 1 < NQ:
                nxt_tiles = [(e, t) for t in range(i + 1, NQ) for e in range(2)]
                nxt_offs = [q0 + (i + 1) * TK] * NQ
            else:
                load_queries(next_group(g))
                nxt_tiles, nxt_offs = all_tiles, offsets(next_group(g), 0)
            step(bufs[i % 2], cur_tiles, cur_offs, bufs[(i + 1) % 2], nxt_tiles, nxt_offs)

    load_queries(0)
    reset_state()
    step(None, [], None, st_a, all_tiles, [0] * NQ)
    triangle(0)

    def q_group(g, _):
        n = plan_ref[batch, pair, g, 0]
        skew = plan_ref[batch, pair, g, 1]

        def full_steps(i, _):
            for u in range(KU):
                j = i * KU + u
                step(bufs[u % 2], full, offsets(g, j), bufs[(u + 1) % 2], all_tiles,
                     offsets(g, j + 1))
            return 0

        def skew_tail(j):
            step(st_a, full, offsets(g, j), st_b, all_tiles, offsets(g, j + 1))
            load_queries(next_group(g))
            step(st_b, [(e, t, True) for e, t in all_tiles], offsets(g, j + 1),
                 st_a, all_tiles, offsets(next_group(g), 0))

        unrolled = jnp.logical_and(skew == 1, functools.reduce(
            jnp.logical_or, [n == n_static for n_static in UNROLLED_SKEW_STEPS]))
        for n_static in UNROLLED_SKEW_STEPS:
            @pl.when(jnp.logical_and(skew == 1, n == n_static))
            def _(n_static=n_static):
                last = n_static + 1
                seq = [bufs[j % 2] for j in range(last)] + [st_c if last % 2 == 0 else st_b,
                                                            st_a]
                finish(g - 1)
                for j in range(last):
                    step(seq[j], full, offsets(g, j), seq[j + 1], all_tiles, offsets(g, j + 1))
                load_queries(next_group(g))
                step(seq[last], [(e, t, True) for e, t in all_tiles], offsets(g, last),
                     st_a, all_tiles, offsets(next_group(g), 0))

        @pl.when(jnp.logical_not(unrolled))
        def _():
            finish(g - 1)
            lax.fori_loop(0, n // KU, full_steps, 0)

        @pl.when(jnp.logical_and(skew == 1, jnp.logical_not(unrolled)))
        def _():
            skew_tail(n)

        @pl.when(skew == 0)
        def _():
            triangle(g)

        return 0

    lax.fori_loop(1, n_groups, q_group, 0)
    finish(n_groups - 1)


def _sweep_plan(stats, n_tiles):
    cum_first, cum_last, qn2, kn2 = (stats[:, s, :N_HEADS, :n_tiles] for s in range(N_STATS))
    B = stats.shape[0]
    n_groups = n_tiles // NQ
    k_norm = jnp.sqrt(jnp.max(kn2, axis=-1, keepdims=True))
    reach = NORM_SLACK * jnp.sqrt(qn2) * k_norm + cum_first
    bound = reach[..., None] - cum_last[:, :, None, :]
    tile = jnp.arange(n_tiles)
    needed = jnp.logical_and(tile[None, :] < tile[:, None], bound >= -ZERO_EXP2)
    w = jnp.sum(needed, axis=-1).astype(jnp.int32)
    w = jnp.max(w.reshape(B, N_PAIRS, 2, n_groups, NQ), axis=2)
    start = jnp.arange(n_groups, dtype=jnp.int32) * NQ
    n_flat = jnp.max(jnp.maximum(w - jnp.arange(NQ, dtype=jnp.int32), 0), axis=-1)
    n_flat = jnp.minimum(((n_flat + KU - 1) // KU) * KU, start)
    n_skew = jnp.maximum(jnp.max(w, axis=-1), 1)
    exact = functools.reduce(jnp.logical_or, [n_skew == s + 1 for s in UNROLLED_SKEW_STEPS])
    n_skew = jnp.where(exact, n_skew, n_skew + (n_skew + 1) % 2)
    units_flat = NQ * n_flat + NQ * (NQ + 1) // 2
    units_skew = NQ * n_skew + NQ
    skew = jnp.logical_and(n_skew < start, units_skew < units_flat)
    n = jnp.where(skew, n_skew - 1, n_flat)
    return jnp.stack([n, skew.astype(jnp.int32)], axis=-1)


def _attention(k, ka, vt, qt, qat, stats):
    B, S, _ = k.shape
    assert TQ == TK and KU == 2 and NQ % KU == 0 and S % QG == 0
    assert S // TK <= LANES
    plan = _sweep_plan(stats, S // TK)
    pair_rows = pl.BlockSpec((1, S, PAIR), lambda b, j, nb: (b, 0, j))
    return pl.pallas_call(
        _attn_kernel,
        grid_spec=pltpu.PrefetchScalarGridSpec(
            num_scalar_prefetch=1,
            grid=(B, N_PAIRS),
            in_specs=[pair_rows, pl.BlockSpec((1, S, LANES), lambda b, j, nb: (b, 0, 0)),
                      pl.BlockSpec((1, 2, PAIR, S), lambda b, j, nb: (b, j, 0, 0)),
                      pl.BlockSpec((1, PAIR, S), lambda b, j, nb: (b, j, 0)),
                      pl.BlockSpec((1, LANES, S), lambda b, j, nb: (b, 0, 0))],
            out_specs=pair_rows,
            scratch_shapes=[pltpu.VMEM((2, PAIR + LANES, QG), _BF16),
                            pltpu.VMEM((2, NQ, TK, TQ), _F32), pltpu.VMEM((2, NQ, TK, TQ), _F32),
                            pltpu.VMEM((2, NQ, TK, TQ), _F32),
                            pltpu.VMEM((2, 1, QG), _F32), pltpu.VMEM((2, PAIR, QG), _F32)]),
        out_shape=jax.ShapeDtypeStruct((B, S, ATTN_WIDTH), _BF16),
        compiler_params=pltpu.CompilerParams(
            dimension_semantics=("parallel", "parallel"),
            vmem_limit_bytes=VMEM_LIMIT),
        name="forgetting_attention",
    )(plan, k, ka, vt, qt, qat)


def _rms_scale(v, g):
    ms = jnp.mean(v * v, axis=-1, keepdims=True)
    return (v * lax.rsqrt(ms + EPS)) * g


def _merge_mlp_kernel(x_ref, u_ref, vn_ref, ya_ref, gates_ref, wsp_ref, bsp_ref,
                      wbs_ref, wba_ref, wout_ref, gpost_ref,
                      gpre2_ref, wup_ref, wdown_ref, gpost2_ref, o_ref, ysgu_ref):
    tm = x_ref.shape[0]
    row = lax.broadcasted_iota(jnp.int32, (CHUNK, CHUNK), 0)
    col = lax.broadcasted_iota(jnp.int32, (CHUNK, CHUNK), 1)
    lane = lax.broadcasted_iota(jnp.int32, (CHUNK, PAIR), 1)
    ws = [jnp.where(row >= col, wsp_ref[g], 0.0).astype(_BF16) for g in range(N_GROUPS)]
    for c in range(tm // CHUNK):
        r = slice(c * CHUNK, (c + 1) * CHUNK)
        for j in range(N_GROUPS // 2):
            cs = slice(j * PAIR, (j + 1) * PAIR)
            vp = vn_ref[r, cs]
            s = jnp.where(lane < HEAD_DIM, _dot(ws[2 * j], vp), _dot(ws[2 * j + 1], vp))
            s = s + bsp_ref[:, cs]
            ysgu_ref[r, cs] = (u_ref[r, cs].astype(_F32) * s).astype(_BF16)

    blocks = [slice(n * tm // N_STREAMS, (n + 1) * tm // N_STREAMS) for n in range(N_STREAMS)]

    def mix(r):
        a = _dot(ysgu_ref[r, :], wbs_ref[...])
        b = _dot(ya_ref[r, :], wba_ref[...])
        gate_a = jax.nn.sigmoid(gates_ref[r, :D_MODEL].astype(_F32))
        gate_b = jax.nn.sigmoid(gates_ref[r, D_MODEL:].astype(_F32))
        merged = (gate_a * a + gate_b * b).astype(_BF16)
        return _dot(merged, wout_ref[...])

    def mlp(xb):
        acc = jnp.zeros(xb.shape, _F32)
        for c in range(D_FF // FF_TILE):
            cs = slice(c * FF_TILE, (c + 1) * FF_TILE)
            hid = jnp.square(jnp.maximum(_dot(xb, wup_ref[:, cs]), 0.0)).astype(_BF16)
            acc = acc + _dot(hid, wdown_ref[cs, :])
        return acc

    mixed = [mix(r) for r in blocks]
    h1 = [x_ref[r, :] + _rms_scale(o, gpost_ref[...]) for r, o in zip(blocks, mixed)]
    xb = [_rms_scale(h, gpre2_ref[...]).astype(_BF16) for h in h1]
    ff = [mlp(v) for v in xb]
    for r, h, f in zip(blocks, h1, ff):
        o_ref[r, :] = h + _rms_scale(f, gpost2_ref[...])


def _merge_mlp(x2, u2, vn2, ya2, gates2, w_spatial, b_spatial, w_bs, w_ba, w_out, g_post,
               g_pre2, w_up, w_down, g_post2):
    R, D = x2.shape
    tm = ROW_TILE
    bsp = jnp.repeat(b_spatial.T, SGU_WIDTH // N_GROUPS, axis=1)

    def const(shape):
        return pl.BlockSpec(shape, lambda i: (0,) * len(shape), pipeline_mode=pl.Buffered(1))

    def rows(width):
        return pl.BlockSpec((tm, width), lambda i: (i, 0))

    return pl.pallas_call(
        _merge_mlp_kernel,
        grid=(R // tm,),
        in_specs=[rows(D), rows(SGU_WIDTH), rows(SGU_WIDTH), rows(ATTN_WIDTH), rows(2 * D),
                  const(w_spatial.shape), const(bsp.shape),
                  const(w_bs.shape), const(w_ba.shape), const(w_out.shape), const((1, D)),
                  const((1, D)), const(w_up.shape), const(w_down.shape), const((1, D))],
        out_specs=rows(D),
        out_shape=jax.ShapeDtypeStruct((R, D), _F32),
        scratch_shapes=[pltpu.VMEM((tm, SGU_WIDTH), _BF16)],
        compiler_params=pltpu.CompilerParams(
            dimension_semantics=("parallel",), vmem_limit_bytes=VMEM_LIMIT),
        name="merge_mlp",
    )(x2, u2, vn2, ya2, gates2, w_spatial, bsp,
      w_bs, w_ba, w_out, g_post.reshape(1, D),
      g_pre2.reshape(1, D), w_up, w_down, g_post2.reshape(1, D))


def kernel(x, g_mix_pre, w_in, b_forget, g_sgu, b_sgu, w_spatial, b_spatial, w_branch_sgu,
           w_branch_attn, w_out, g_mix_post, g_ffn_pre, w_up, w_down, g_ffn_post):
    B, S, D = x.shape
    h = x
    for l in range(g_mix_pre.shape[0]):
        head_order = jnp.argsort(b_forget[l])
        later = (w_branch_sgu[l], _heads_in_order(w_branch_attn[l], head_order), w_out[l],
                 w_up[l], w_down[l])
        u, vn, qt, k, vt, gates, qat, ka, stats, w_bs, w_ba, w_o, w_u, w_d = _in_projection(
            h, g_mix_pre[l], w_in[l], b_forget[l], g_sgu[l], b_sgu[l], later, head_order)
        y_attn = _attention(k, ka, vt, qt, qat, stats)
        h = _merge_mlp(h.reshape(B * S, D), u.reshape(B * S, -1), vn.reshape(B * S, -1),
                       y_attn.reshape(B * S, -1), gates.reshape(B * S, -1),
                       w_spatial[l], b_spatial[l], w_bs, w_ba, w_o,
                       g_mix_post[l], g_ffn_pre[l], w_u, w_d,
                       g_ffn_post[l]).reshape(B, S, D)
    return h
```

```python
import functools

import jax
import jax.numpy as jnp
import numpy as np
from jax import lax
from jax.experimental import pallas as pl
from jax.experimental.pallas import tpu as pltpu

D_MODEL = 1024
N_HEADS = 8
HEAD_DIM = 64
ATTN_WIDTH = N_HEADS * HEAD_DIM
N_GROUPS = 8
SGU_WIDTH = D_MODEL // 2
CHUNK = 128
D_FF = 4 * D_MODEL
EPS = 1e-6
LOG2E = 1.4426950408889634

LANES = 128
N_SPLIT = 3
BIAS_COLS = N_HEADS * N_SPLIT
PAIR = 2 * HEAD_DIM
N_PAIRS = N_HEADS // 2

ROW_TILE = 512
FF_TILE = 1024
N_STREAMS = 2
IN_STREAMS = 2
TQ = 256
TK = 256
NQ = 4
QG = NQ * TQ
KU = 2
UNROLLED_SKEW_STEPS = (1, 2, 3, 4, 5, 6)
N_STATS = 4
ZERO_EXP2 = 136.0
NORM_SLACK = 2.05
V7X_VMEM_BYTES = 64 * 1024 * 1024
VMEM_LIMIT = V7X_VMEM_BYTES * 7 // 8
BF16_SUBLANES = 16

_BF16 = jnp.bfloat16
_F32 = jnp.float32


def _split_bf16(x):
    parts = []
    r = x
    for _ in range(N_SPLIT):
        p = r.astype(_BF16)
        parts.append(p)
        r = r - p.astype(_F32)
    return parts


def _dot(a, b):
    return jnp.dot(a, b, preferred_element_type=_F32)


def _dot_nt(a, b):
    return lax.dot_general(a, b, (((1,), (1,)), ((), ())), preferred_element_type=_F32)


def _inproj_kernel(x_ref, gpre_ref, wzt_ref, wqkvt_ref, wft_ref, wgt_ref,
                   bf_ref, gsgu_ref, bsgu_ref, tri_ref, expand_ref, expand_t_ref,
                   head_rows_ref, piece_rows_ref, *rest, n_cast):
    cast_in = rest[:n_cast]
    (u_ref, vn_ref, qt_ref, k_ref, vt_ref, gates_ref, qat_ref, ka_ref,
     stats_ref) = rest[n_cast:n_cast + 9]
    cast_out = rest[n_cast + 9:2 * n_cast + 9]
    carry_ref, = rest[2 * n_cast + 9:]
    for src, dst in zip(cast_in, cast_out):
        dst[...] = src[...].astype(_BF16)

    @pl.when(pl.program_id(1) == 0)
    def _():
        carry_ref[...] = jnp.zeros_like(carry_ref)
        stats_ref[...] = jnp.zeros_like(stats_ref)

    tm = x_ref.shape[1] // IN_STREAMS
    blocks = [slice(s * tm, (s + 1) * tm) for s in range(IN_STREAMS)]

    def normed(rs):
        x = x_ref[0, rs, :]
        ms = jnp.mean(x * x, axis=-1, keepdims=True)
        return ((x * lax.rsqrt(ms + EPS)) * gpre_ref[...]).astype(_BF16)

    xbs = [normed(rs) for rs in blocks]
    lane = lax.broadcasted_iota(jnp.int32, (tm, LANES), 1)

    def pack_pieces(v):
        hi, mid, lo = _split_bf16(v)
        zero = jnp.zeros((), _BF16)
        return jnp.where(lane < N_HEADS, hi,
                         jnp.where(lane < 2 * N_HEADS, mid,
                                   jnp.where(lane < N_SPLIT * N_HEADS, lo, zero)))

    def project(s, rs, xb):
        f = _dot_nt(xb, wft_ref[...]) + bf_ref[...]

        z = jax.nn.gelu(_dot_nt(xb, wzt_ref[...]), approximate=True)
        u_ref[0, rs, :] = z[:, :SGU_WIDTH].astype(_BF16)
        v = z[:, SGU_WIDTH:]
        mu = jnp.mean(v, axis=-1, keepdims=True)
        vc = v - mu
        var = jnp.mean(vc * vc, axis=-1, keepdims=True)
        vn_ref[0, rs, :] = ((vc * lax.rsqrt(var + EPS)) * gsgu_ref[...]
                            + bsgu_ref[...]).astype(_BF16)

        log_f = jnp.minimum(f, 0.0) - jnp.log(1.0 + jnp.exp(-jnp.abs(f)))
        sums = _dot(tri_ref[...], pack_pieces(log_f))

        qt = _dot_nt(wqkvt_ref[0], xb) * (HEAD_DIM ** -0.5 * LOG2E)
        qt_ref[0, :, rs] = qt.astype(_BF16)
        k = _dot_nt(xb, wqkvt_ref[1])
        k_ref[0, rs, :] = k.astype(_BF16)
        qn2 = _dot(head_rows_ref[...], (qt * qt).astype(_BF16))
        kn2 = _dot_nt(head_rows_ref[...], (k * k).astype(_BF16))

        total = sums
        for shift in (N_HEADS, 2 * N_HEADS, LANES - N_HEADS, LANES - 2 * N_HEADS):
            total = total + pltpu.roll(sums, shift, 1)
        cum = carry_ref[0:1, :] + total
        carry_ref[0:1, :] = cum[tm - 1:tm, :]
        pieces = pack_pieces(cum * LOG2E)
        spread = _dot(pieces, expand_ref[...])
        spread_t = _dot_nt(expand_t_ref[...], pieces)

        cum_t = _dot_nt(piece_rows_ref[...], pieces)
        tile_lane = lax.broadcasted_iota(jnp.int32, (2 * N_HEADS, LANES), 1)
        first_tile = (pl.program_id(1) * IN_STREAMS + s) * (tm // TK)
        stats = [stats_ref[0, n] for n in range(N_STATS)]
        for j in range(tm // TK):
            lo, hi = j * TK, (j + 1) * TK
            cols = (cum_t[:, lo:lo + 1], cum_t[:, hi - 1:hi],
                    jnp.max(qn2[:, lo:hi], axis=1, keepdims=True),
                    jnp.max(kn2[:, lo:hi], axis=1, keepdims=True))
            stats = [jnp.where(tile_lane == first_tile + j, c, st) for c, st in zip(cols, stats)]
        for n in range(N_STATS):
            stats_ref[0, n] = stats[n]

        vt = _dot_nt(wqkvt_ref[2], xb).astype(_BF16)
        for h in range(N_HEADS):
            vt_ref[0, h, :, rs] = vt[h * HEAD_DIM:(h + 1) * HEAD_DIM, :]

        gates_ref[0, rs, :] = _dot_nt(xb, wgt_ref[...]).astype(_BF16)

        in_a = lane < BIAS_COLS
        in_b = jnp.logical_and(lane >= BIAS_COLS, lane < 2 * BIAS_COLS)
        ka_ref[0, rs, :] = jnp.where(in_a, -spread, jnp.where(in_b, 1.0, 0.0)).astype(_BF16)
        row = lax.broadcasted_iota(jnp.int32, (LANES, tm), 0)
        in_a = row < BIAS_COLS
        in_b = jnp.logical_and(row >= BIAS_COLS, row < 2 * BIAS_COLS)
        qat_ref[0, :, rs] = jnp.where(in_a, 1.0, jnp.where(in_b, spread_t, 0.0)).astype(_BF16)

    for s, (rs, xb) in enumerate(zip(blocks, xbs)):
        project(s, rs, xb)


def _head_rows():
    r = np.zeros((2 * N_HEADS, ATTN_WIDTH), np.float32)
    for h in range(N_HEADS):
        r[h, h * HEAD_DIM:(h + 1) * HEAD_DIM] = 1.0
    return r


def _piece_rows():
    r = np.zeros((2 * N_HEADS, LANES), np.float32)
    for h in range(N_HEADS):
        for i in range(N_SPLIT):
            r[h, N_HEADS * i + h] = 1.0
    return r


def _expand_matrix():
    e = np.zeros((LANES, LANES), np.float32)
    for h in range(N_HEADS):
        for i in range(N_SPLIT):
            e[N_HEADS * i + h, N_SPLIT * h + i] = 1.0
            e[N_HEADS * i + h, BIAS_COLS + N_SPLIT * h + i] = 1.0
    return e


_IN_OFFSETS = tuple(int(v) for v in np.cumsum(
    (0, 2 * SGU_WIDTH, ATTN_WIDTH, ATTN_WIDTH, ATTN_WIDTH, N_HEADS, 2 * D_MODEL)))


def _heads_in_order(w, head_order):
    return w.reshape(N_HEADS, HEAD_DIM, -1)[head_order].reshape(w.shape)


def _in_projection(x, g_pre, w_in, b_forget, g_sgu, b_sgu, later_weights, head_order):
    B, S, D = x.shape
    tm = IN_STREAMS * ROW_TILE
    n_steps = B * (S // tm)
    o = _IN_OFFSETS
    wt = jnp.swapaxes(w_in, 0, 1)
    wtb = wt.astype(_BF16)
    wqkvt = wtb[o[1]:o[4]].reshape(3, N_HEADS, HEAD_DIM, D)[:, head_order].reshape(
        3, ATTN_WIDTH, D)
    row_pad = ((0, LANES - N_SPLIT * N_HEADS), (0, 0))
    wft = jnp.pad(jnp.tile(wt[o[4]:o[5]][head_order], (N_SPLIT, 1)), row_pad).astype(_BF16)
    lane_pad = ((0, 0), (0, LANES - N_SPLIT * N_HEADS))
    bf = jnp.pad(jnp.tile(b_forget[head_order].reshape(1, N_HEADS), (1, N_SPLIT)), lane_pad)
    tri = jnp.asarray(np.tril(np.ones((ROW_TILE, ROW_TILE), np.float32)), _BF16)
    expand = jnp.asarray(_expand_matrix(), _BF16)
    expand_t = jnp.asarray(_expand_matrix().T, _BF16)
    head_rows = jnp.asarray(_head_rows(), _BF16)
    piece_rows = jnp.asarray(_piece_rows(), _BF16)

    once = pl.Buffered(1)

    def const(shape):
        return pl.BlockSpec(shape, lambda b, i: (0,) * len(shape), pipeline_mode=once)

    def rows(width):
        return pl.BlockSpec((1, tm, width), lambda b, i: (b, i, 0))

    out_shape = (
        jax.ShapeDtypeStruct((B, S, SGU_WIDTH), _BF16),
        jax.ShapeDtypeStruct((B, S, SGU_WIDTH), _BF16),
        jax.ShapeDtypeStruct((B, ATTN_WIDTH, S), _BF16),
        jax.ShapeDtypeStruct((B, S, ATTN_WIDTH), _BF16),
        jax.ShapeDtypeStruct((B, N_HEADS, HEAD_DIM, S), _BF16),
        jax.ShapeDtypeStruct((B, S, 2 * D_MODEL), _BF16),
        jax.ShapeDtypeStruct((B, LANES, S), _BF16),
        jax.ShapeDtypeStruct((B, S, LANES), _BF16),
        jax.ShapeDtypeStruct((B, N_STATS, 2 * N_HEADS, LANES), _F32),
    )

    def cols(height):
        return pl.BlockSpec((1, height, tm), lambda b, i: (b, 0, i))

    def row_block(w):
        assert w.shape[0] % (n_steps * BF16_SUBLANES) == 0
        return pl.BlockSpec((w.shape[0] // n_steps, w.shape[1]),
                            lambda b, i: (b * (S // tm) + i, 0))

    out_specs = (
        rows(SGU_WIDTH), rows(SGU_WIDTH), cols(ATTN_WIDTH), rows(ATTN_WIDTH),
        pl.BlockSpec((1, N_HEADS, HEAD_DIM, tm), lambda b, i: (b, 0, 0, i)),
        rows(2 * D_MODEL), cols(LANES), rows(LANES),
        pl.BlockSpec((1, N_STATS, 2 * N_HEADS, LANES), lambda b, i: (b, 0, 0, 0)),
    ) + tuple(row_block(w) for w in later_weights)
    out_shape += tuple(jax.ShapeDtypeStruct(w.shape, _BF16) for w in later_weights)
    return pl.pallas_call(
        functools.partial(_inproj_kernel, n_cast=len(later_weights)),
        grid=(B, S // tm),
        in_specs=[
            rows(D), const((1, D)),
            pl.BlockSpec((o[1] - o[0], D), lambda b, i: (0, 0), pipeline_mode=once),
            const(wqkvt.shape), const(wft.shape),
            pl.BlockSpec((pl.Element(o[6] - o[5]), pl.Element(D)), lambda b, i: (o[5], 0),
                         pipeline_mode=once),
            const((1, LANES)), const((1, SGU_WIDTH)), const((1, SGU_WIDTH)),
            const(tri.shape), const(expand.shape), const(expand_t.shape),
            const(head_rows.shape), const(piece_rows.shape),
        ] + [row_block(w) for w in later_weights],
        out_specs=out_specs,
        out_shape=out_shape,
        scratch_shapes=[pltpu.VMEM((8, LANES), _F32)],
        compiler_params=pltpu.CompilerParams(
            dimension_semantics=("arbitrary", "arbitrary"),
            vmem_limit_bytes=VMEM_LIMIT),
        name="in_projection",
    )(x, g_pre.reshape(1, D), wtb, wqkvt, wft, wtb, bf,
      g_sgu.reshape(1, SGU_WIDTH), b_sgu.reshape(1, SGU_WIDTH), tri, expand, expand_t,
      head_rows, piece_rows, *later_weights)


def _attn_kernel(plan_ref, k_ref, ka_ref, vt_ref, qt_ref, qat_ref, o_ref,
                 qf_ref, st_a, st_b, st_c, m_ref, acc_ref):
    batch = pl.program_id(0)
    pair = pl.program_id(1)
    S = k_ref.shape[1]
    row = lax.broadcasted_iota(jnp.int32, (PAIR, 1), 0)
    all_tiles = [(e, t) for t in range(NQ) for e in range(2)]

    def key_tile(k0):
        return jnp.concatenate([k_ref[0, pl.ds(k0, TK), :], ka_ref[0, pl.ds(k0, TK), :]], axis=1)

    def offsets(g, j):
        n, skew = plan_ref[batch, pair, g, 0], plan_ref[batch, pair, g, 1]
        first = (g * NQ - n - skew + j) * TK
        return [pl.multiple_of(first + t * skew * TK, TK) for t in range(NQ)]

    def update(e, t, st, k0, masked):
        cs = slice(t * TQ, (t + 1) * TQ)
        if masked:
            key_i = lax.broadcasted_iota(jnp.int32, (TK, TQ), 0)
            qry_i = lax.broadcasted_iota(jnp.int32, (TK, TQ), 1)
            st = jnp.where(key_i <= qry_i, st, -jnp.inf)
        m = m_ref[e, :, cs]
        m_new = jnp.maximum(m, jnp.max(st, axis=0, keepdims=True))
        p = jnp.exp2(st - m_new)
        alpha = jnp.exp2(m - m_new)
        m_ref[e, :, cs] = m_new
        values = jnp.concatenate([vt_ref[0, e, :, pl.ds(k0, TK)],
                                  jnp.ones((PAIR - HEAD_DIM, TK), _BF16)], axis=0)
        acc_ref[e, :, cs] = alpha * acc_ref[e, :, cs] + _dot(values, p.astype(_BF16))

    def step(cur, cur_tiles, cur_offs, nxt, nxt_tiles, nxt_offs):
        for n in range(max(len(cur_tiles), len(nxt_tiles))):
            if n < len(nxt_tiles):
                e, t = nxt_tiles[n]
                nxt[e, t] = _dot(key_tile(nxt_offs[t]), qf_ref[e, :, t * TQ:(t + 1) * TQ])
            if n < len(cur_tiles):
                e, t, masked = cur_tiles[n]
                update(e, t, cur[e, t], cur_offs[t], masked)

    def load_queries(g):
        q0 = pl.multiple_of(g * QG, QG)
        qt = qt_ref[0, :, pl.ds(q0, QG)].astype(_F32)
        qat = qat_ref[0, :, pl.ds(q0, QG)].astype(_F32)
        for e in range(2):
            a0 = N_SPLIT * (2 * pair + e)
            q_mask = jnp.logical_and(row >= HEAD_DIM * e, row < HEAD_DIM * (e + 1))
            a_mask = jnp.logical_or(
                jnp.logical_and(row >= a0, row < a0 + N_SPLIT),
                jnp.logical_and(row >= BIAS_COLS + a0, row < BIAS_COLS + a0 + N_SPLIT))
            qf_ref[e, :PAIR, :] = jnp.where(q_mask, qt, 0.0).astype(_BF16)
            qf_ref[e, PAIR:, :] = jnp.where(a_mask, qat, 0.0).astype(_BF16)

    def reset_state():
        m_ref[...] = jnp.full(m_ref.shape, -jnp.inf, _F32)
        acc_ref[...] = jnp.zeros(acc_ref.shape, _F32)

    full = [(e, t, False) for e, t in all_tiles]
    bufs = (st_a, st_b)
    n_groups = S // QG

    def finish(g):
        out = [acc_ref[e, :HEAD_DIM, :] * (1.0 / acc_ref[e, HEAD_DIM:HEAD_DIM + 1, :])
               for e in range(2)]
        o_ref[0, pl.ds(pl.multiple_of(g * QG, QG), QG), :] = (
            jnp.concatenate(out, axis=0).T.astype(_BF16))
        reset_state()

    def next_group(g):
        return jnp.minimum(g + 1, n_groups - 1)

    def triangle(g):
        q0 = pl.multiple_of(g * QG, QG)
        for i in range(NQ):
            cur_tiles = [(e, t, t == i) for t in range(i, NQ) for e in range(2)]
            cur_offs = [q0 + i * TK] * NQ
            if i + 1 < NQ:
                nxt_tiles = [(e, t) for t in range(i + 1, NQ) for e in range(2)]
                nxt_offs = [q0 + (i + 1) * TK] * NQ
            else:
                load_queries(next_group(g))
                nxt_tiles, nxt_offs = all_tiles, offsets(next_group(g), 0)
            step(bufs[i % 2], cur_tiles, cur_offs, bufs[(i + 1) % 2], nxt_tiles, nxt_offs)

    load_queries(0)
    reset_state()
    step(None, [], None, st_a, all_tiles, [0] * NQ)
    triangle(0)

    def q_group(g, _):
        n = plan_ref[batch, pair, g, 0]
        skew = plan_ref[batch, pair, g, 1]

        def full_steps(i, _):
            for u in range(KU):
                j = i * KU + u
                step(bufs[u % 2], full, offsets(g, j), bufs[(u + 1) % 2], all_tiles,
                     offsets(g, j + 1))
            return 0

        def skew_tail(j):
            step(st_a, full, offsets(g, j), st_b, all_tiles, offsets(g, j + 1))
            load_queries(next_group(g))
            step(st_b, [(e, t, True) for e, t in all_tiles], offsets(g, j + 1),
                 st_a, all_tiles, offsets(next_group(g), 0))

        unrolled = jnp.logical_and(skew == 1, functools.reduce(
            jnp.logical_or, [n == n_static for n_static in UNROLLED_SKEW_STEPS]))
        for n_static in UNROLLED_SKEW_STEPS:
            @pl.when(jnp.logical_and(skew == 1, n == n_static))
            def _(n_static=n_static):
                last = n_static + 1
                seq = [bufs[j % 2] for j in range(last)] + [st_c if last % 2 == 0 else st_b,
                                                            st_a]
                finish(g - 1)
                for j in range(last):
                    step(seq[j], full, offsets(g, j), seq[j + 1], all_tiles, offsets(g, j + 1))
                load_queries(next_group(g))
                step(seq[last], [(e, t, True) for e, t in all_tiles], offsets(g, last),
                     st_a, all_tiles, offsets(next_group(g), 0))

        @pl.when(jnp.logical_not(unrolled))
        def _():
            finish(g - 1)
            lax.fori_loop(0, n // KU, full_steps, 0)

        @pl.when(jnp.logical_and(skew == 1, jnp.logical_not(unrolled)))
        def _():
            skew_tail(n)

        @pl.when(skew == 0)
        def _():
            triangle(g)

        return 0

    lax.fori_loop(1, n_groups, q_group, 0)
    finish(n_groups - 1)


def _sweep_plan(stats, n_tiles):
    cum_first, cum_last, qn2, kn2 = (stats[:, s, :N_HEADS, :n_tiles] for s in range(N_STATS))
    B = stats.shape[0]
    n_groups = n_tiles // NQ
    k_norm = jnp.sqrt(jnp.max(kn2, axis=-1, keepdims=True))
    reach = NORM_SLACK * jnp.sqrt(qn2) * k_norm + cum_first
    bound = reach[..., None] - cum_last[:, :, None, :]
    tile = jnp.arange(n_tiles)
    needed = jnp.logical_and(tile[None, :] < tile[:, None], bound >= -ZERO_EXP2)
    w = jnp.sum(needed, axis=-1).astype(jnp.int32)
    w = jnp.max(w.reshape(B, N_PAIRS, 2, n_groups, NQ), axis=2)
    start = jnp.arange(n_groups, dtype=jnp.int32) * NQ
    n_flat = jnp.max(jnp.maximum(w - jnp.arange(NQ, dtype=jnp.int32), 0), axis=-1)
    n_flat = jnp.minimum(((n_flat + KU - 1) // KU) * KU, start)
    n_skew = jnp.maximum(jnp.max(w, axis=-1), 1)
    exact = functools.reduce(jnp.logical_or, [n_skew == s + 1 for s in UNROLLED_SKEW_STEPS])
    n_skew = jnp.where(exact, n_skew, n_skew + (n_skew + 1) % 2)
    units_flat = NQ * n_flat + NQ * (NQ + 1) // 2
    units_skew = NQ * n_skew + NQ
    skew = jnp.logical_and(n_skew < start, units_skew < units_flat)
    n = jnp.where(skew, n_skew - 1, n_flat)
    return jnp.stack([n, skew.astype(jnp.int32)], axis=-1)


def _attention(k, ka, vt, qt, qat, stats):
    B, S, _ = k.shape
    assert TQ == TK and KU == 2 and NQ % KU == 0 and S % QG == 0
    assert S // TK <= LANES
    plan = _sweep_plan(stats, S // TK)
    pair_rows = pl.BlockSpec((1, S, PAIR), lambda b, j, nb: (b, 0, j))
    return pl.pallas_call(
        _attn_kernel,
        grid_spec=pltpu.PrefetchScalarGridSpec(
            num_scalar_prefetch=1,
            grid=(B, N_PAIRS),
            in_specs=[pair_rows, pl.BlockSpec((1, S, LANES), lambda b, j, nb: (b, 0, 0)),
                      pl.BlockSpec((1, 2, HEAD_DIM, S), lambda b, j, nb: (b, j, 0, 0)),
                      pl.BlockSpec((1, PAIR, S), lambda b, j, nb: (b, j, 0)),
                      pl.BlockSpec((1, LANES, S), lambda b, j, nb: (b, 0, 0))],
            out_specs=pair_rows,
            scratch_shapes=[pltpu.VMEM((2, PAIR + LANES, QG), _BF16),
                            pltpu.VMEM((2, NQ, TK, TQ), _F32), pltpu.VMEM((2, NQ, TK, TQ), _F32),
                            pltpu.VMEM((2, NQ, TK, TQ), _F32),
                            pltpu.VMEM((2, 1, QG), _F32), pltpu.VMEM((2, PAIR, QG), _F32)]),
        out_shape=jax.ShapeDtypeStruct((B, S, ATTN_WIDTH), _BF16),
        compiler_params=pltpu.CompilerParams(
            dimension_semantics=("parallel", "parallel"),
            vmem_limit_bytes=VMEM_LIMIT),
        name="forgetting_attention",
    )(plan, k, ka, vt, qt, qat)


def _rms_scale(v, g):
    ms = jnp.mean(v * v, axis=-1, keepdims=True)
    return (v * lax.rsqrt(ms + EPS)) * g


def _merge_mlp_kernel(x_ref, u_ref, vn_ref, ya_ref, gates_ref, wsp_ref, bsp_ref,
                      wbs_ref, wba_ref, wout_ref, gpost_ref,
                      gpre2_ref, wup_ref, wdown_ref, gpost2_ref, o_ref, ysgu_ref):
    tm = x_ref.shape[0]
    row = lax.broadcasted_iota(jnp.int32, (CHUNK, CHUNK), 0)
    col = lax.broadcasted_iota(jnp.int32, (CHUNK, CHUNK), 1)
    lane = lax.broadcasted_iota(jnp.int32, (CHUNK, PAIR), 1)
    ws = [jnp.where(row >= col, wsp_ref[g], 0.0).astype(_BF16) for g in range(N_GROUPS)]
    for c in range(tm // CHUNK):
        r = slice(c * CHUNK, (c + 1) * CHUNK)
        for j in range(N_GROUPS // 2):
            cs = slice(j * PAIR, (j + 1) * PAIR)
            vp = vn_ref[r, cs]
            s = jnp.where(lane < HEAD_DIM, _dot(ws[2 * j], vp), _dot(ws[2 * j + 1], vp))
            s = s + bsp_ref[:, cs]
            ysgu_ref[r, cs] = (u_ref[r, cs].astype(_F32) * s).astype(_BF16)

    blocks = [slice(n * tm // N_STREAMS, (n + 1) * tm // N_STREAMS) for n in range(N_STREAMS)]

    def mix(r):
        a = _dot(ysgu_ref[r, :], wbs_ref[...])
        b = _dot(ya_ref[r, :], wba_ref[...])
        gate_a = jax.nn.sigmoid(gates_ref[r, :D_MODEL].astype(_F32))
        gate_b = jax.nn.sigmoid(gates_ref[r, D_MODEL:].astype(_F32))
        merged = (gate_a * a + gate_b * b).astype(_BF16)
        return _dot(merged, wout_ref[...])

    def mlp(xb):
        acc = jnp.zeros(xb.shape, _F32)
        for c in range(D_FF // FF_TILE):
            cs = slice(c * FF_TILE, (c + 1) * FF_TILE)
            hid = jnp.square(jnp.maximum(_dot(xb, wup_ref[:, cs]), 0.0)).astype(_BF16)
            acc = acc + _dot(hid, wdown_ref[cs, :])
        return acc

    mixed = [mix(r) for r in blocks]
    h1 = [x_ref[r, :] + _rms_scale(o, gpost_ref[...]) for r, o in zip(blocks, mixed)]
    xb = [_rms_scale(h, gpre2_ref[...]).astype(_BF16) for h in h1]
    ff = [mlp(v) for v in xb]
    for r, h, f in zip(blocks, h1, ff):
        o_ref[r, :] = h + _rms_scale(f, gpost2_ref[...])


def _merge_mlp(x2, u2, vn2, ya2, gates2, w_spatial, b_spatial, w_bs, w_ba, w_out, g_post,
               g_pre2, w_up, w_down, g_post2):
    R, D = x2.shape
    tm = ROW_TILE
    bsp = jnp.repeat(b_spatial.T, SGU_WIDTH // N_GROUPS, axis=1)

    def const(shape):
        return pl.BlockSpec(shape, lambda i: (0,) * len(shape), pipeline_mode=pl.Buffered(1))

    def rows(width):
        return pl.BlockSpec((tm, width), lambda i: (i, 0))

    return pl.pallas_call(
        _merge_mlp_kernel,
        grid=(R // tm,),
        in_specs=[rows(D), rows(SGU_WIDTH), rows(SGU_WIDTH), rows(ATTN_WIDTH), rows(2 * D),
                  const(w_spatial.shape), const(bsp.shape),
                  const(w_bs.shape), const(w_ba.shape), const(w_out.shape), const((1, D)),
                  const((1, D)), const(w_up.shape), const(w_down.shape), const((1, D))],
        out_specs=rows(D),
        out_shape=jax.ShapeDtypeStruct((R, D), _F32),
        scratch_shapes=[pltpu.VMEM((tm, SGU_WIDTH), _BF16)],
        compiler_params=pltpu.CompilerParams(
            dimension_semantics=("parallel",), vmem_limit_bytes=VMEM_LIMIT),
        name="merge_mlp",
    )(x2, u2, vn2, ya2, gates2, w_spatial, bsp,
      w_bs, w_ba, w_out, g_post.reshape(1, D),
      g_pre2.reshape(1, D), w_up, w_down, g_post2.reshape(1, D))


def kernel(x, g_mix_pre, w_in, b_forget, g_sgu, b_sgu, w_spatial, b_spatial, w_branch_sgu,
           w_branch_attn, w_out, g_mix_post, g_ffn_pre, w_up, w_down, g_ffn_post):
    B, S, D = x.shape
    h = x
    for l in range(g_mix_pre.shape[0]):
        head_order = jnp.argsort(b_forget[l])
        later = (w_branch_sgu[l], _heads_in_order(w_branch_attn[l], head_order), w_out[l],
                 w_up[l], w_down[l])
        u, vn, qt, k, vt, gates, qat, ka, stats, w_bs, w_ba, w_o, w_u, w_d = _in_projection(
            h, g_mix_pre[l], w_in[l], b_forget[l], g_sgu[l], b_sgu[l], later, head_order)
        y_attn = _attention(k, ka, vt, qt, qat, stats)
        h = _merge_mlp(h.reshape(B * S, D), u.reshape(B * S, -1), vn.reshape(B * S, -1),
                       y_attn.reshape(B * S, -1), gates.reshape(B * S, -1),
                       w_spatial[l], b_spatial[l], w_bs, w_ba, w_o,
                       g_mix_post[l], g_ffn_pre[l], w_u, w_d,
                       g_ffn_post[l]).reshape(B, S, D)
    return h
```

```python
import functools

import jax
import jax.numpy as jnp
import numpy as np
from jax import lax
from jax.experimental import pallas as pl
from jax.experimental.pallas import tpu as pltpu

D_MODEL = 1024
N_HEADS = 8
HEAD_DIM = 64
ATTN_WIDTH = N_HEADS * HEAD_DIM
N_GROUPS = 8
SGU_WIDTH = D_MODEL // 2
CHUNK = 128
D_FF = 4 * D_MODEL
EPS = 1e-6
LOG2E = 1.4426950408889634

LANES = 128
N_SPLIT = 3
BIAS_COLS = N_HEADS * N_SPLIT
PAIR = 2 * HEAD_DIM
N_PAIRS = N_HEADS // 2

ROW_TILE = 512
FF_TILE = 1024
N_STREAMS = 2
IN_STREAMS = 2
TQ = 256
TK = 256
NQ = 4
QG = NQ * TQ
KU = 2
UNROLLED_SKEW_STEPS = (1, 2, 3, 4, 5, 6)
N_STATS = 4
ZERO_EXP2 = 136.0
NORM_SLACK = 2.05
V7X_VMEM_BYTES = 64 * 1024 * 1024
VMEM_LIMIT = V7X_VMEM_BYTES * 7 // 8
BF16_SUBLANES = 16

_BF16 = jnp.bfloat16
_F32 = jnp.float32


def _split_bf16(x):
    parts = []
    r = x
    for _ in range(N_SPLIT):
        p = r.astype(_BF16)
        parts.append(p)
        r = r - p.astype(_F32)
    return parts


def _dot(a, b):
    return jnp.dot(a, b, preferred_element_type=_F32)


def _dot_nt(a, b):
    return lax.dot_general(a, b, (((1,), (1,)), ((), ())), preferred_element_type=_F32)


def _inproj_kernel(x_ref, gpre_ref, wzt_ref, wqkvt_ref, wft_ref, wgt_ref,
                   bf_ref, gsgu_ref, bsgu_ref, tri_ref, expand_ref, expand_t_ref,
                   head_rows_ref, piece_rows_ref, *rest, n_cast):
    cast_in = rest[:n_cast]
    (u_ref, vn_ref, qt_ref, k_ref, vt_ref, gates_ref, qat_ref, ka_ref,
     stats_ref) = rest[n_cast:n_cast + 9]
    cast_out = rest[n_cast + 9:2 * n_cast + 9]
    carry_ref, = rest[2 * n_cast + 9:]
    for src, dst in zip(cast_in, cast_out):
        dst[...] = src[...].astype(_BF16)

    @pl.when(pl.program_id(1) == 0)
    def _():
        carry_ref[...] = jnp.zeros_like(carry_ref)
        stats_ref[...] = jnp.zeros_like(stats_ref)

    tm = x_ref.shape[1] // IN_STREAMS
    blocks = [slice(s * tm, (s + 1) * tm) for s in range(IN_STREAMS)]

    def normed(rs):
        x = x_ref[0, rs, :]
        ms = jnp.mean(x * x, axis=-1, keepdims=True)
        return ((x * lax.rsqrt(ms + EPS)) * gpre_ref[...]).astype(_BF16)

    xbs = [normed(rs) for rs in blocks]
    lane = lax.broadcasted_iota(jnp.int32, (tm, LANES), 1)

    def pack_pieces(v):
        hi, mid, lo = _split_bf16(v)
        zero = jnp.zeros((), _BF16)
        return jnp.where(lane < N_HEADS, hi,
                         jnp.where(lane < 2 * N_HEADS, mid,
                                   jnp.where(lane < N_SPLIT * N_HEADS, lo, zero)))

    def project(s, rs, xb):
        f = _dot_nt(xb, wft_ref[...]) + bf_ref[...]

        z = jax.nn.gelu(_dot_nt(xb, wzt_ref[...]), approximate=True)
        u_ref[0, rs, :] = z[:, :SGU_WIDTH].astype(_BF16)
        v = z[:, SGU_WIDTH:]
        mu = jnp.mean(v, axis=-1, keepdims=True)
        vc = v - mu
        var = jnp.mean(vc * vc, axis=-1, keepdims=True)
        vn_ref[0, rs, :] = ((vc * lax.rsqrt(var + EPS)) * gsgu_ref[...]
                            + bsgu_ref[...]).astype(_BF16)

        log_f = jnp.minimum(f, 0.0) - jnp.log(1.0 + jnp.exp(-jnp.abs(f)))
        sums = _dot(tri_ref[...], pack_pieces(log_f))

        qt = _dot_nt(wqkvt_ref[0], xb) * (HEAD_DIM ** -0.5 * LOG2E)
        qt_ref[0, :, rs] = qt.astype(_BF16)
        k = _dot_nt(xb, wqkvt_ref[1])
        k_ref[0, rs, :] = k.astype(_BF16)
        qn2 = _dot(head_rows_ref[...], (qt * qt).astype(_BF16))
        kn2 = _dot_nt(head_rows_ref[...], (k * k).astype(_BF16))

        total = sums
        for shift in (N_HEADS, 2 * N_HEADS, LANES - N_HEADS, LANES - 2 * N_HEADS):
            total = total + pltpu.roll(sums, shift, 1)
        cum = carry_ref[0:1, :] + total
        carry_ref[0:1, :] = cum[tm - 1:tm, :]
        pieces = pack_pieces(cum * LOG2E)
        spread = _dot(pieces, expand_ref[...])
        spread_t = _dot_nt(expand_t_ref[...], pieces)

        cum_t = _dot_nt(piece_rows_ref[...], pieces)
        tile_lane = lax.broadcasted_iota(jnp.int32, (2 * N_HEADS, LANES), 1)
        first_tile = (pl.program_id(1) * IN_STREAMS + s) * (tm // TK)
        stats = [stats_ref[0, n] for n in range(N_STATS)]
        for j in range(tm // TK):
            lo, hi = j * TK, (j + 1) * TK
            cols = (cum_t[:, lo:lo + 1], cum_t[:, hi - 1:hi],
                    jnp.max(qn2[:, lo:hi], axis=1, keepdims=True),
                    jnp.max(kn2[:, lo:hi], axis=1, keepdims=True))
            stats = [jnp.where(tile_lane == first_tile + j, c, st) for c, st in zip(cols, stats)]
        for n in range(N_STATS):
            stats_ref[0, n] = stats[n]

        vt = _dot_nt(wqkvt_ref[2], xb).astype(_BF16)
        for h in range(N_HEADS):
            vt_ref[0, h, :, rs] = vt[h * HEAD_DIM:(h + 1) * HEAD_DIM, :]

        gates_ref[0, rs, :] = _dot_nt(xb, wgt_ref[...]).astype(_BF16)

        in_a = lane < BIAS_COLS
        in_b = jnp.logical_and(lane >= BIAS_COLS, lane < 2 * BIAS_COLS)
        ka_ref[0, rs, :] = jnp.where(in_a, -spread, jnp.where(in_b, 1.0, 0.0)).astype(_BF16)
        row = lax.broadcasted_iota(jnp.int32, (LANES, tm), 0)
        in_a = row < BIAS_COLS
        in_b = jnp.logical_and(row >= BIAS_COLS, row < 2 * BIAS_COLS)
        qat_ref[0, :, rs] = jnp.where(in_a, 1.0, jnp.where(in_b, spread_t, 0.0)).astype(_BF16)

    for s, (rs, xb) in enumerate(zip(blocks, xbs)):
        project(s, rs, xb)


def _head_rows():
    r = np.zeros((2 * N_HEADS, ATTN_WIDTH), np.float32)
    for h in range(N_HEADS):
        r[h, h * HEAD_DIM:(h + 1) * HEAD_DIM] = 1.0
    return r


def _piece_rows():
    r = np.zeros((2 * N_HEADS, LANES), np.float32)
    for h in range(N_HEADS):
        for i in range(N_SPLIT):
            r[h, N_HEADS * i + h] = 1.0
    return r


def _expand_matrix():
    e = np.zeros((LANES, LANES), np.float32)
    for h in range(N_HEADS):
        for i in range(N_SPLIT):
            e[N_HEADS * i + h, N_SPLIT * h + i] = 1.0
            e[N_HEADS * i + h, BIAS_COLS + N_SPLIT * h + i] = 1.0
    return e


_IN_OFFSETS = tuple(int(v) for v in np.cumsum(
    (0, 2 * SGU_WIDTH, ATTN_WIDTH, ATTN_WIDTH, ATTN_WIDTH, N_HEADS, 2 * D_MODEL)))


def _heads_in_order(w, head_order):
    return w.reshape(N_HEADS, HEAD_DIM, -1)[head_order].reshape(w.shape)


def _in_projection(x, g_pre, w_in, b_forget, g_sgu, b_sgu, later_weights, head_order):
    B, S, D = x.shape
    tm = IN_STREAMS * ROW_TILE
    n_steps = B * (S // tm)
    o = _IN_OFFSETS
    wt = jnp.swapaxes(w_in, 0, 1)
    wtb = wt.astype(_BF16)
    wqkvt = wtb[o[1]:o[4]].reshape(3, N_HEADS, HEAD_DIM, D)[:, head_order].reshape(
        3, ATTN_WIDTH, D)
    row_pad = ((0, LANES - N_SPLIT * N_HEADS), (0, 0))
    wft = jnp.pad(jnp.tile(wt[o[4]:o[5]][head_order], (N_SPLIT, 1)), row_pad).astype(_BF16)
    lane_pad = ((0, 0), (0, LANES - N_SPLIT * N_HEADS))
    bf = jnp.pad(jnp.tile(b_forget[head_order].reshape(1, N_HEADS), (1, N_SPLIT)), lane_pad)
    tri = jnp.asarray(np.tril(np.ones((ROW_TILE, ROW_TILE), np.float32)), _BF16)
    expand = jnp.asarray(_expand_matrix(), _BF16)
    expand_t = jnp.asarray(_expand_matrix().T, _BF16)
    head_rows = jnp.asarray(_head_rows(), _BF16)
    piece_rows = jnp.asarray(_piece_rows(), _BF16)

    once = pl.Buffered(1)

    def const(shape):
        return pl.BlockSpec(shape, lambda b, i: (0,) * len(shape), pipeline_mode=once)

    def rows(width):
        return pl.BlockSpec((1, tm, width), lambda b, i: (b, i, 0))

    out_shape = (
        jax.ShapeDtypeStruct((B, S, SGU_WIDTH), _BF16),
        jax.ShapeDtypeStruct((B, S, SGU_WIDTH), _BF16),
        jax.ShapeDtypeStruct((B, ATTN_WIDTH, S), _BF16),
        jax.ShapeDtypeStruct((B, S, ATTN_WIDTH), _BF16),
        jax.ShapeDtypeStruct((B, N_HEADS, HEAD_DIM, S), _BF16),
        jax.ShapeDtypeStruct((B, S, 2 * D_MODEL), _BF16),
        jax.ShapeDtypeStruct((B, LANES, S), _BF16),
        jax.ShapeDtypeStruct((B, S, LANES), _BF16),
        jax.ShapeDtypeStruct((B, N_STATS, 2 * N_HEADS, LANES), _F32),
    )

    def cols(height):
        return pl.BlockSpec((1, height, tm), lambda b, i: (b, 0, i))

    def row_block(w):
        assert w.shape[0] % (n_steps * BF16_SUBLANES) == 0
        return pl.BlockSpec((w.shape[0] // n_steps, w.shape[1]),
                            lambda b, i: (b * (S // tm) + i, 0))

    out_specs = (
        rows(SGU_WIDTH), rows(SGU_WIDTH), cols(ATTN_WIDTH), rows(ATTN_WIDTH),
        pl.BlockSpec((1, N_HEADS, HEAD_DIM, tm), lambda b, i: (b, 0, 0, i)),
        rows(2 * D_MODEL), cols(LANES), rows(LANES),
        pl.BlockSpec((1, N_STATS, 2 * N_HEADS, LANES), lambda b, i: (b, 0, 0, 0)),
    ) + tuple(row_block(w) for w in later_weights)
    out_shape += tuple(jax.ShapeDtypeStruct(w.shape, _BF16) for w in later_weights)
    return pl.pallas_call(
        functools.partial(_inproj_kernel, n_cast=len(later_weights)),
        grid=(B, S // tm),
        in_specs=[
            rows(D), const((1, D)),
            pl.BlockSpec((o[1] - o[0], D), lambda b, i: (0, 0), pipeline_mode=once),
            const(wqkvt.shape), const(wft.shape),
            pl.BlockSpec((pl.Element(o[6] - o[5]), pl.Element(D)), lambda b, i: (o[5], 0),
                         pipeline_mode=once),
            const((1, LANES)), const((1, SGU_WIDTH)), const((1, SGU_WIDTH)),
            const(tri.shape), const(expand.shape), const(expand_t.shape),
            const(head_rows.shape), const(piece_rows.shape),
        ] + [row_block(w) for w in later_weights],
        out_specs=out_specs,
        out_shape=out_shape,
        scratch_shapes=[pltpu.VMEM((8, LANES), _F32)],
        compiler_params=pltpu.CompilerParams(
            dimension_semantics=("arbitrary", "arbitrary"),
            vmem_limit_bytes=VMEM_LIMIT),
        name="in_projection",
    )(x, g_pre.reshape(1, D), wtb, wqkvt, wft, wtb, bf,
      g_sgu.reshape(1, SGU_WIDTH), b_sgu.reshape(1, SGU_WIDTH), tri, expand, expand_t,
      head_rows, piece_rows, *later_weights)


def _attn_kernel(plan_ref, k_ref, ka_ref, vt_ref, qt_ref, qat_ref, o_ref,
                 qf_ref, st_a, st_b, st_c, m_ref, acc_ref):
    batch = pl.program_id(0)
    pair = pl.program_id(1)
    S = k_ref.shape[1]
    row = lax.broadcasted_iota(jnp.int32, (PAIR, 1), 0)
    all_tiles = [(e, t) for t in range(NQ) for e in range(2)]

    def key_tile(k0):
        return jnp.concatenate([k_ref[0, pl.ds(k0, TK), :], ka_ref[0, pl.ds(k0, TK), :]], axis=1)

    def offsets(g, j):
        n, skew = plan_ref[batch, pair, g, 0], plan_ref[batch, pair, g, 1]
        first = (g * NQ - n - skew + j) * TK
        return [pl.multiple_of(first + t * skew * TK, TK) for t in range(NQ)]

    def update(e, t, st, k0, masked):
        cs = slice(t * TQ, (t + 1) * TQ)
        if masked:
            key_i = lax.broadcasted_iota(jnp.int32, (TK, TQ), 0)
            qry_i = lax.broadcasted_iota(jnp.int32, (TK, TQ), 1)
            st = jnp.where(key_i <= qry_i, st, -jnp.inf)
        m = m_ref[e, :, cs]
        m_new = jnp.maximum(m, jnp.max(st, axis=0, keepdims=True))
        p = jnp.exp2(st - m_new)
        alpha = jnp.exp2(m - m_new)
        m_ref[e, :, cs] = m_new
        values = jnp.concatenate([vt_ref[0, e, :, pl.ds(k0, TK)],
                                  jnp.ones((PAIR - HEAD_DIM, TK), _BF16)], axis=0)
        acc_ref[e, :, cs] = alpha * acc_ref[e, :, cs] + _dot(values, p.astype(_BF16))

    def step(cur, cur_tiles, cur_offs, nxt, nxt_tiles, nxt_offs):
        for n in range(max(len(cur_tiles), len(nxt_tiles))):
            if n < len(nxt_tiles):
                e, t = nxt_tiles[n]
                nxt[e, t] = _dot(key_tile(nxt_offs[t]), qf_ref[e, :, t * TQ:(t + 1) * TQ])
            if n < len(cur_tiles):
                e, t, masked = cur_tiles[n]
                update(e, t, cur[e, t], cur_offs[t], masked)

    def load_queries(g):
        q0 = pl.multiple_of(g * QG, QG)
        qt = qt_ref[0, :, pl.ds(q0, QG)].astype(_F32)
        qat = qat_ref[0, :, pl.ds(q0, QG)].astype(_F32)
        for e in range(2):
            a0 = N_SPLIT * (2 * pair + e)
            q_mask = jnp.logical_and(row >= HEAD_DIM * e, row < HEAD_DIM * (e + 1))
            a_mask = jnp.logical_or(
                jnp.logical_and(row >= a0, row < a0 + N_SPLIT),
                jnp.logical_and(row >= BIAS_COLS + a0, row < BIAS_COLS + a0 + N_SPLIT))
            qf_ref[e, :PAIR, :] = jnp.where(q_mask, qt, 0.0).astype(_BF16)
            qf_ref[e, PAIR:, :] = jnp.where(a_mask, qat, 0.0).astype(_BF16)

    def reset_state(clear_acc=False):
        m_ref[...] = jnp.full(m_ref.shape, -jnp.inf, _F32)
        if clear_acc:
            acc_ref[...] = jnp.zeros(acc_ref.shape, _F32)

    full = [(e, t, False) for e, t in all_tiles]
    bufs = (st_a, st_b)
    n_groups = S // QG

    def finish(g):
        out = [acc_ref[e, :HEAD_DIM, :] * (1.0 / acc_ref[e, HEAD_DIM:HEAD_DIM + 1, :])
               for e in range(2)]
        o_ref[0, pl.ds(pl.multiple_of(g * QG, QG), QG), :] = (
            jnp.concatenate(out, axis=0).T.astype(_BF16))
        reset_state()

    def next_group(g):
        return jnp.minimum(g + 1, n_groups - 1)

    def triangle(g):
        q0 = pl.multiple_of(g * QG, QG)
        for i in range(NQ):
            cur_tiles = [(e, t, t == i) for t in range(i, NQ) for e in range(2)]
            cur_offs = [q0 + i * TK] * NQ
            if i + 1 < NQ:
                nxt_tiles = [(e, t) for t in range(i + 1, NQ) for e in range(2)]
                nxt_offs = [q0 + (i + 1) * TK] * NQ
            else:
                load_queries(next_group(g))
                nxt_tiles, nxt_offs = all_tiles, offsets(next_group(g), 0)
            step(bufs[i % 2], cur_tiles, cur_offs, bufs[(i + 1) % 2], nxt_tiles, nxt_offs)

    load_queries(0)
    reset_state(clear_acc=True)
    step(None, [], None, st_a, all_tiles, [0] * NQ)
    triangle(0)

    def q_group(g, _):
        n = plan_ref[batch, pair, g, 0]
        skew = plan_ref[batch, pair, g, 1]

        def full_steps(i, _):
            for u in range(KU):
                j = i * KU + u
                step(bufs[u % 2], full, offsets(g, j), bufs[(u + 1) % 2], all_tiles,
                     offsets(g, j + 1))
            return 0

        def skew_tail(j):
            step(st_a, full, offsets(g, j), st_b, all_tiles, offsets(g, j + 1))
            load_queries(next_group(g))
            step(st_b, [(e, t, True) for e, t in all_tiles], offsets(g, j + 1),
                 st_a, all_tiles, offsets(next_group(g), 0))

        unrolled = jnp.logical_and(skew == 1, functools.reduce(
            jnp.logical_or, [n == n_static for n_static in UNROLLED_SKEW_STEPS]))
        for n_static in UNROLLED_SKEW_STEPS:
            @pl.when(jnp.logical_and(skew == 1, n == n_static))
            def _(n_static=n_static):
                last = n_static + 1
                seq = [bufs[j % 2] for j in range(last)] + [st_c if last % 2 == 0 else st_b,
                                                            st_a]
                finish(g - 1)
                for j in range(last):
                    step(seq[j], full, offsets(g, j), seq[j + 1], all_tiles, offsets(g, j + 1))
                load_queries(next_group(g))
                step(seq[last], [(e, t, True) for e, t in all_tiles], offsets(g, last),
                     st_a, all_tiles, offsets(next_group(g), 0))

        @pl.when(jnp.logical_not(unrolled))
        def _():
            finish(g - 1)
            lax.fori_loop(0, n // KU, full_steps, 0)

        @pl.when(jnp.logical_and(skew == 1, jnp.logical_not(unrolled)))
        def _():
            skew_tail(n)

        @pl.when(skew == 0)
        def _():
            triangle(g)

        return 0

    lax.fori_loop(1, n_groups, q_group, 0)
    finish(n_groups - 1)


def _sweep_plan(stats, n_tiles):
    cum_first, cum_last, qn2, kn2 = (stats[:, s, :N_HEADS, :n_tiles] for s in range(N_STATS))
    B = stats.shape[0]
    n_groups = n_tiles // NQ
    k_norm = jnp.sqrt(jnp.max(kn2, axis=-1, keepdims=True))
    reach = NORM_SLACK * jnp.sqrt(qn2) * k_norm + cum_first
    bound = reach[..., None] - cum_last[:, :, None, :]
    tile = jnp.arange(n_tiles)
    needed = jnp.logical_and(tile[None, :] < tile[:, None], bound >= -ZERO_EXP2)
    w = jnp.sum(needed, axis=-1).astype(jnp.int32)
    w = jnp.max(w.reshape(B, N_PAIRS, 2, n_groups, NQ), axis=2)
    start = jnp.arange(n_groups, dtype=jnp.int32) * NQ
    n_flat = jnp.max(jnp.maximum(w - jnp.arange(NQ, dtype=jnp.int32), 0), axis=-1)
    n_flat = jnp.minimum(((n_flat + KU - 1) // KU) * KU, start)
    n_skew = jnp.maximum(jnp.max(w, axis=-1), 1)
    exact = functools.reduce(jnp.logical_or, [n_skew == s + 1 for s in UNROLLED_SKEW_STEPS])
    n_skew = jnp.where(exact, n_skew, n_skew + (n_skew + 1) % 2)
    units_flat = NQ * n_flat + NQ * (NQ + 1) // 2
    units_skew = NQ * n_skew + NQ
    skew = jnp.logical_and(n_skew < start, units_skew < units_flat)
    n = jnp.where(skew, n_skew - 1, n_flat)
    return jnp.stack([n, skew.astype(jnp.int32)], axis=-1)


def _attention(k, ka, vt, qt, qat, stats):
    B, S, _ = k.shape
    assert TQ == TK and KU == 2 and NQ % KU == 0 and S % QG == 0
    assert S // TK <= LANES
    plan = _sweep_plan(stats, S // TK)
    pair_rows = pl.BlockSpec((1, S, PAIR), lambda b, j, nb: (b, 0, j))
    return pl.pallas_call(
        _attn_kernel,
        grid_spec=pltpu.PrefetchScalarGridSpec(
            num_scalar_prefetch=1,
            grid=(B, N_PAIRS),
            in_specs=[pair_rows, pl.BlockSpec((1, S, LANES), lambda b, j, nb: (b, 0, 0)),
                      pl.BlockSpec((1, 2, HEAD_DIM, S), lambda b, j, nb: (b, j, 0, 0)),
                      pl.BlockSpec((1, PAIR, S), lambda b, j, nb: (b, j, 0)),
                      pl.BlockSpec((1, LANES, S), lambda b, j, nb: (b, 0, 0))],
            out_specs=pair_rows,
            scratch_shapes=[pltpu.VMEM((2, PAIR + LANES, QG), _BF16),
                            pltpu.VMEM((2, NQ, TK, TQ), _F32), pltpu.VMEM((2, NQ, TK, TQ), _F32),
                            pltpu.VMEM((2, NQ, TK, TQ), _F32),
                            pltpu.VMEM((2, 1, QG), _F32), pltpu.VMEM((2, PAIR, QG), _F32)]),
        out_shape=jax.ShapeDtypeStruct((B, S, ATTN_WIDTH), _BF16),
        compiler_params=pltpu.CompilerParams(
            dimension_semantics=("parallel", "parallel"),
            vmem_limit_bytes=VMEM_LIMIT),
        name="forgetting_attention",
    )(plan, k, ka, vt, qt, qat)


def _rms_scale(v, g):
    ms = jnp.mean(v * v, axis=-1, keepdims=True)
    return (v * lax.rsqrt(ms + EPS)) * g


def _merge_mlp_kernel(x_ref, u_ref, vn_ref, ya_ref, gates_ref, wsp_ref, bsp_ref,
                      wbs_ref, wba_ref, wout_ref, gpost_ref,
                      gpre2_ref, wup_ref, wdown_ref, gpost2_ref, o_ref, ysgu_ref):
    tm = x_ref.shape[0]
    row = lax.broadcasted_iota(jnp.int32, (CHUNK, CHUNK), 0)
    col = lax.broadcasted_iota(jnp.int32, (CHUNK, CHUNK), 1)
    lane = lax.broadcasted_iota(jnp.int32, (CHUNK, PAIR), 1)
    ws = [jnp.where(row >= col, wsp_ref[g], 0.0).astype(_BF16) for g in range(N_GROUPS)]
    for c in range(tm // CHUNK):
        r = slice(c * CHUNK, (c + 1) * CHUNK)
        for j in range(N_GROUPS // 2):
            cs = slice(j * PAIR, (j + 1) * PAIR)
            vp = vn_ref[r, cs]
            s = jnp.where(lane < HEAD_DIM, _dot(ws[2 * j], vp), _dot(ws[2 * j + 1], vp))
            s = s + bsp_ref[:, cs]
            ysgu_ref[r, cs] = (u_ref[r, cs].astype(_F32) * s).astype(_BF16)

    blocks = [slice(n * tm // N_STREAMS, (n + 1) * tm // N_STREAMS) for n in range(N_STREAMS)]

    def mix(r):
        a = _dot(ysgu_ref[r, :], wbs_ref[...])
        b = _dot(ya_ref[r, :], wba_ref[...])
        gate_a = jax.nn.sigmoid(gates_ref[r, :D_MODEL].astype(_F32))
        gate_b = jax.nn.sigmoid(gates_ref[r, D_MODEL:].astype(_F32))
        merged = (gate_a * a + gate_b * b).astype(_BF16)
        return _dot(merged, wout_ref[...])

    def mlp(xb):
        acc = jnp.zeros(xb.shape, _F32)
        for c in range(D_FF // FF_TILE):
            cs = slice(c * FF_TILE, (c + 1) * FF_TILE)
            hid = jnp.square(jnp.maximum(_dot(xb, wup_ref[:, cs]), 0.0)).astype(_BF16)
            acc = acc + _dot(hid, wdown_ref[cs, :])
        return acc

    mixed = [mix(r) for r in blocks]
    h1 = [x_ref[r, :] + _rms_scale(o, gpost_ref[...]) for r, o in zip(blocks, mixed)]
    xb = [_rms_scale(h, gpre2_ref[...]).astype(_BF16) for h in h1]
    ff = [mlp(v) for v in xb]
    for r, h, f in zip(blocks, h1, ff):
        o_ref[r, :] = h + _rms_scale(f, gpost2_ref[...])


def _merge_mlp(x2, u2, vn2, ya2, gates2, w_spatial, b_spatial, w_bs, w_ba, w_out, g_post,
               g_pre2, w_up, w_down, g_post2):
    R, D = x2.shape
    tm = ROW_TILE
    bsp = jnp.repeat(b_spatial.T, SGU_WIDTH // N_GROUPS, axis=1)

    def const(shape):
        return pl.BlockSpec(shape, lambda i: (0,) * len(shape), pipeline_mode=pl.Buffered(1))

    def rows(width):
        return pl.BlockSpec((tm, width), lambda i: (i, 0))

    return pl.pallas_call(
        _merge_mlp_kernel,
        grid=(R // tm,),
        in_specs=[rows(D), rows(SGU_WIDTH), rows(SGU_WIDTH), rows(ATTN_WIDTH), rows(2 * D),
                  const(w_spatial.shape), const(bsp.shape),
                  const(w_bs.shape), const(w_ba.shape), const(w_out.shape), const((1, D)),
                  const((1, D)), const(w_up.shape), const(w_down.shape), const((1, D))],
        out_specs=rows(D),
        out_shape=jax.ShapeDtypeStruct((R, D), _F32),
        scratch_shapes=[pltpu.VMEM((tm, SGU_WIDTH), _BF16)],
        compiler_params=pltpu.CompilerParams(
            dimension_semantics=("parallel",), vmem_limit_bytes=VMEM_LIMIT),
        name="merge_mlp",
    )(x2, u2, vn2, ya2, gates2, w_spatial, bsp,
      w_bs, w_ba, w_out, g_post.reshape(1, D),
      g_pre2.reshape(1, D), w_up, w_down, g_post2.reshape(1, D))


def kernel(x, g_mix_pre, w_in, b_forget, g_sgu, b_sgu, w_spatial, b_spatial, w_branch_sgu,
           w_branch_attn, w_out, g_mix_post, g_ffn_pre, w_up, w_down, g_ffn_post):
    B, S, D = x.shape
    h = x
    for l in range(g_mix_pre.shape[0]):
        head_order = jnp.argsort(b_forget[l])
        later = (w_branch_sgu[l], _heads_in_order(w_branch_attn[l], head_order), w_out[l],
                 w_up[l], w_down[l])
        u, vn, qt, k, vt, gates, qat, ka, stats, w_bs, w_ba, w_o, w_u, w_d = _in_projection(
            h, g_mix_pre[l], w_in[l], b_forget[l], g_sgu[l], b_sgu[l], later, head_order)
        y_attn = _attention(k, ka, vt, qt, qat, stats)
        h = _merge_mlp(h.reshape(B * S, D), u.reshape(B * S, -1), vn.reshape(B * S, -1),
                       y_attn.reshape(B * S, -1), gates.reshape(B * S, -1),
                       w_spatial[l], b_spatial[l], w_bs, w_ba, w_o,
                       g_mix_post[l], g_ffn_pre[l], w_u, w_d,
                       g_ffn_post[l]).reshape(B, S, D)
    return h
```
